```python
import jax, jax.numpy as jnp
from jax import lax
import numpy as np

D_MODEL = 1024
BATCH = 4
SEQ = 4096
DEPTH = 1

N_SB_HEADS = 8
SB_HEAD_DIM = 64
SB_WIDTH = N_SB_HEADS * SB_HEAD_DIM
POOL_WINDOWS = (2, 4, 8, 16)
N_POOL_GROUPS = len(POOL_WINDOWS)
POOL_GROUP_DIM = 128
POOL_WIDTH = N_POOL_GROUPS * POOL_GROUP_DIM
MIX_WIDTH = SB_WIDTH + POOL_WIDTH
IN_PROJ_WIDTH = 3 * SB_WIDTH + POOL_WIDTH
Q_BLOCK = 128
N_EXPERT_GROUPS = 4
EXPERTS_PER_GROUP = 4
N_EXPERTS = N_EXPERT_GROUPS * EXPERTS_PER_GROUP
TOP_K_IN_GROUP = 2
EXPERT_HIDDEN = 512
DEEPNORM_ALPHA = (2.0 * DEPTH) ** 0.25
DEEPNORM_BETA = (8.0 * DEPTH) ** -0.25
LN_EPS = 1e-5
N_MOD = 6

kernel_name = "hybrid_stickbreak_pool_hmoe_deepnorm_adaln"


def layer_norm(x, gain=None, bias=None):
    xf = x.astype(jnp.float32)
    mu = jnp.mean(xf, axis=-1, keepdims=True)
    var = jnp.mean(jnp.square(xf - mu), axis=-1, keepdims=True)
    y = (xf - mu) * lax.rsqrt(var + LN_EPS)
    if gain is not None:
        y = y * gain.astype(jnp.float32) + bias.astype(jnp.float32)
    return y.astype(x.dtype)


def stick_breaking_attention(q, k, v):
    seq = q.shape[2]
    scale = SB_HEAD_DIM ** -0.5
    outs = []
    for blk in range(seq // Q_BLOCK):
        start = blk * Q_BLOCK
        end = start + Q_BLOCK
        qb = q[:, :, start:end]
        kb = k[:, :, :end]
        vb = v[:, :, :end]
        z = jnp.einsum('bhqd,bhkd->bhqk', qb, kb).astype(jnp.float32) * scale
        q_pos = start + jnp.arange(Q_BLOCK)[:, None]
        k_pos = jnp.arange(end)[None, :]
        causal = k_pos < q_pos
        log_beta = jnp.where(causal, jax.nn.log_sigmoid(z), -jnp.inf)
        log_not_beta = jnp.where(causal, jax.nn.log_sigmoid(-z), 0.0)
        after = lax.cumsum(log_not_beta, axis=3, reverse=True) - log_not_beta
        att = jnp.exp(log_beta + after)
        outs.append(jnp.einsum('bhqk,bhkd->bhqd', att.astype(vb.dtype), vb))
    return jnp.concatenate(outs, axis=2)


def causal_pool_minus_self(xg, window):
    seq = xg.shape[1]
    xf = xg.astype(jnp.float32)
    cs = jnp.cumsum(xf, axis=1)
    lag = jnp.pad(cs, ((0, 0), (window, 0), (0, 0)))[:, :seq]
    count = jnp.minimum(jnp.arange(seq) + 1, window).astype(jnp.float32)
    mean = (cs - lag) / count[None, :, None]
    return (mean - xf).astype(xg.dtype)


def multiscale_pool(p, w_pool, pool_scale):
    b, s, _ = p.shape
    groups = [causal_pool_minus_self(p[..., g * POOL_GROUP_DIM:(g + 1) * POOL_GROUP_DIM], w)
              for g, w in enumerate(POOL_WINDOWS)]
    pooled = jnp.stack(groups, axis=2)
    mixed = jnp.einsum('bsgc,gce->bsge', pooled, w_pool).reshape(b, s, POOL_WIDTH)
    return mixed * pool_scale


def hierarchical_moe(u, w_rg, b_rg, w_re, b_re, w_gate, w_up, w_down):
    t, d = u.shape
    g_logits = (u @ w_rg).astype(jnp.float32) + b_rg.astype(jnp.float32)
    g_prob = jax.nn.softmax(g_logits, axis=-1)
    g_p, g_idx = lax.top_k(g_prob, 1)
    e_logits = ((u @ w_re).astype(jnp.float32) + b_re.astype(jnp.float32)).reshape(
        t, N_EXPERT_GROUPS, EXPERTS_PER_GROUP)
    e_sel = jnp.take_along_axis(e_logits, g_idx[:, :, None], axis=1)[:, 0]
    e_top, e_idx = lax.top_k(e_sel, TOP_K_IN_GROUP)
    e_w = jax.nn.softmax(e_top, axis=-1) * g_p
    expert_id = g_idx * EXPERTS_PER_GROUP + e_idx
    combine = jnp.sum(jax.nn.one_hot(expert_id, N_EXPERTS, dtype=jnp.float32) * e_w[..., None],
                      axis=1)
    y = jnp.zeros((t, d), jnp.float32)
    for e in range(N_EXPERTS):
        h = jax.nn.silu(u @ w_gate[e]) * (u @ w_up[e])
        y = y + combine[:, e:e + 1] * (h @ w_down[e]).astype(jnp.float32)
    return y.astype(u.dtype)


def setup_inputs(seed: int = 0) -> dict:
    key = jax.random.key(seed)
    ks = jax.random.split(key, 20)
    f32 = jnp.float32
    n = lambda k, shape, s: jax.random.normal(k, shape, f32) * s
    d = D_MODEL
    w_in = n(ks[4], (DEPTH, d, IN_PROJ_WIDTH), d ** -0.5)
    v_cols = jnp.zeros((IN_PROJ_WIDTH,), f32).at[2 * SB_WIDTH:3 * SB_WIDTH].set(1.0)
    w_in = w_in * (1.0 + (DEEPNORM_BETA - 1.0) * v_cols)
    return {
        "x": n(ks[0], (BATCH, SEQ, d), 1.0),
        "c": n(ks[1], (BATCH, d), 1.0),
        "w_ada": n(ks[2], (DEPTH, d, N_MOD * d), 0.1 * d ** -0.5),
        "b_ada": n(ks[3], (DEPTH, N_MOD * d), 0.01),
        "w_in": w_in,
        "w_pool": n(ks[5], (DEPTH, N_POOL_GROUPS, POOL_GROUP_DIM, POOL_GROUP_DIM), POOL_GROUP_DIM ** -0.5),
        "pool_scale": 1.0 + n(ks[6], (DEPTH, POOL_WIDTH), 0.05),
        "w_out": n(ks[7], (DEPTH, MIX_WIDTH, d), DEEPNORM_BETA * MIX_WIDTH ** -0.5),
        "ln1_g": 1.0 + n(ks[8], (DEPTH, d), 0.02),
        "ln1_b": n(ks[9], (DEPTH, d), 0.02),
        "w_router_group": n(ks[10], (DEPTH, d, N_EXPERT_GROUPS), d ** -0.5),
        "b_router_group": n(ks[11], (DEPTH, N_EXPERT_GROUPS), 0.01),
        "w_router_expert": n(ks[12], (DEPTH, d, N_EXPERTS), d ** -0.5),
        "b_router_expert": n(ks[13], (DEPTH, N_EXPERTS), 0.01),
        "w_gate": n(ks[14], (DEPTH, N_EXPERTS, d, EXPERT_HIDDEN), d ** -0.5),
        "w_up": n(ks[15], (DEPTH, N_EXPERTS, d, EXPERT_HIDDEN), DEEPNORM_BETA * d ** -0.5),
        "w_down": n(ks[16], (DEPTH, N_EXPERTS, EXPERT_HIDDEN, d), DEEPNORM_BETA * EXPERT_HIDDEN ** -0.5),
        "ln2_g": 1.0 + n(ks[17], (DEPTH, d), 0.02),
        "ln2_b": n(ks[18], (DEPTH, d), 0.02),
    }


def reference(x, c, w_ada, b_ada, w_in, w_pool, pool_scale, w_out, ln1_g, ln1_b,
              w_router_group, b_router_group, w_router_expert, b_router_expert,
              w_gate, w_up, w_down, ln2_g, ln2_b):
    b, s, d = x.shape
    heads = lambda t: t.reshape(b, s, N_SB_HEADS, SB_HEAD_DIM).transpose(0, 2, 1, 3)
    for layer in range(DEPTH):
        mod = jax.nn.silu(c) @ w_ada[layer] + b_ada[layer]
        shift1, scale1, gate1, shift2, scale2, gate2 = [m[:, None, :] for m in jnp.split(mod, N_MOD, axis=-1)]

        u = layer_norm(x) * (1.0 + scale1) + shift1
        proj = u @ w_in[layer]
        q, k, v, p = jnp.split(proj, [SB_WIDTH, 2 * SB_WIDTH, 3 * SB_WIDTH], axis=-1)
        o_sb = stick_breaking_attention(heads(q), heads(k), heads(v))
        o_sb = o_sb.transpose(0, 2, 1, 3).reshape(b, s, SB_WIDTH)
        o_pool = multiscale_pool(p, w_pool[layer], pool_scale[layer])
        mixed = jnp.concatenate([o_sb, o_pool], axis=-1) @ w_out[layer]
        x = layer_norm(DEEPNORM_ALPHA * x + (1.0 + gate1) * mixed, ln1_g[layer], ln1_b[layer])

        u = layer_norm(x) * (1.0 + scale2) + shift2
        y = hierarchical_moe(u.reshape(b * s, d), w_router_group[layer], b_router_group[layer],
                             w_router_expert[layer], b_router_expert[layer],
                             w_gate[layer], w_up[layer], w_down[layer]).reshape(b, s, d)
        x = layer_norm(DEEPNORM_ALPHA * x + (1.0 + gate2) * y, ln2_g[layer], ln2_b[layer])
    return x
```

```python
import functools

import jax
import jax.numpy as jnp
from jax import lax
from jax.experimental import pallas as pl
from jax.experimental.pallas import tpu as pltpu

D_MODEL = 1024
N_SB_HEADS = 8
SB_HEAD_DIM = 64
SB_WIDTH = N_SB_HEADS * SB_HEAD_DIM
POOL_WINDOWS = (2, 4, 8, 16)
POOL_GROUP_DIM = 128
POOL_WIDTH = len(POOL_WINDOWS) * POOL_GROUP_DIM
N_EXPERT_GROUPS = 4
EXPERTS_PER_GROUP = 4
N_EXPERTS = N_EXPERT_GROUPS * EXPERTS_PER_GROUP
EXPERT_HIDDEN = 512
DEPTH = 1
DEEPNORM_ALPHA = (2.0 * DEPTH) ** 0.25
LN_EPS = 1e-5
N_MOD = 6

LANES = 128
HALO = max(POOL_WINDOWS)
VMEM_LIMIT = 56 * 1024 * 1024

ROW_TILE = 512
ATTN_TILE = 256
MOE_ROW_TILE = 512


def _ln(x):
    mu = jnp.mean(x, axis=-1, keepdims=True)
    xc = x - mu
    var = jnp.mean(xc * xc, axis=-1, keepdims=True)
    return xc * lax.rsqrt(var + LN_EPS)


def _adaln_kernel(c_ref, w_ref, b_ref, o_ref):
    c = c_ref[...]
    a = c * jax.nn.sigmoid(c)
    o_ref[...] = jnp.dot(a, w_ref[...], preferred_element_type=jnp.float32,
                         precision=lax.Precision.HIGHEST) + b_ref[...]


def _adaln(c_pad, w_ada, b_ada):
    rows, d = c_pad.shape
    n = w_ada.shape[1]
    tn = 1024
    return pl.pallas_call(
        _adaln_kernel,
        grid=(n // tn,),
        in_specs=[pl.BlockSpec((rows, d), lambda j: (0, 0)),
                  pl.BlockSpec((d, tn), lambda j: (0, j)),
                  pl.BlockSpec((1, tn), lambda j: (0, j))],
        out_specs=pl.BlockSpec((rows, tn), lambda j: (0, j)),
        out_shape=jax.ShapeDtypeStruct((rows, n), jnp.float32),
        compiler_params=pltpu.CompilerParams(vmem_limit_bytes=VMEM_LIMIT),
        name="adaln",
    )(c_pad, w_ada, b_ada)


def _ln_inproj_kernel(x_ref, mod_ref, w_ref, qkv_ref, p_ref):
    shift = mod_ref[0, 0:1, :]
    scale = mod_ref[0, 1:2, :]
    u = (_ln(x_ref[...]) * (1.0 + scale) + shift).astype(jnp.bfloat16)
    qk_scale = SB_HEAD_DIM ** -0.5
    q = jnp.dot(u, w_ref[:, 0:SB_WIDTH], preferred_element_type=jnp.float32) * qk_scale
    qkv_ref[:, 0:SB_WIDTH] = q.astype(jnp.bfloat16)
    for j in (1, 2):
        kv = jnp.dot(u, w_ref[:, j * SB_WIDTH:(j + 1) * SB_WIDTH], preferred_element_type=jnp.float32)
        qkv_ref[:, j * SB_WIDTH:(j + 1) * SB_WIDTH] = kv.astype(jnp.bfloat16)
    p_ref[...] = jnp.dot(u, w_ref[:, 3 * SB_WIDTH:], preferred_element_type=jnp.float32)


def _ln_inproj(x2d, mod, w_in_bf16, seq):
    t, d = x2d.shape
    tm = ROW_TILE
    tiles_per_seq = seq // tm
    return pl.pallas_call(
        _ln_inproj_kernel,
        grid=(t // tm,),
        in_specs=[pl.BlockSpec((tm, d), lambda i: (i, 0)),
                  pl.BlockSpec((1, N_MOD, d), lambda i: (i // tiles_per_seq, 0, 0)),
                  pl.BlockSpec(w_in_bf16.shape, lambda i: (0, 0))],
        out_specs=[pl.BlockSpec((tm, 3 * SB_WIDTH), lambda i: (i, 0)),
                   pl.BlockSpec((tm, POOL_WIDTH), lambda i: (i, 0))],
        out_shape=[jax.ShapeDtypeStruct((t, 3 * SB_WIDTH), jnp.bfloat16),
                   jax.ShapeDtypeStruct((t, POOL_WIDTH), jnp.float32)],
        compiler_params=pltpu.CompilerParams(vmem_limit_bytes=VMEM_LIMIT),
        name="ln_inproj",
    )(x2d, mod, w_in_bf16)


def _sb_attn_kernel(q_ref, k_ref, v_ref, o_ref):
    tq = tk = ATTN_TILE
    qi = pl.program_id(2)
    q2 = q_ref[0]
    lane = lax.broadcasted_iota(jnp.int32, (tq, LANES), 1)
    first_head = lane < SB_HEAD_DIM
    zero = jnp.zeros_like(q2)
    q_heads = (jnp.where(first_head, q2, zero), jnp.where(first_head, zero, q2))

    row = lax.broadcasted_iota(jnp.int32, (tk, tk), 0)
    col = lax.broadcasted_iota(jnp.int32, (tk, tk), 1)
    suffix = (row > col).astype(jnp.bfloat16)
    suffix2 = jnp.concatenate([suffix, suffix], axis=0)
    causal = col < row

    def tile(kb, state, diagonal):
        start = pl.multiple_of(kb * tk, tk)
        k_blk = k_ref[0, pl.ds(start, tk), :]
        v_blk = v_ref[0, pl.ds(start, tk), :]
        new_state = []
        for q_h, (carry, acc) in zip(q_heads, state):
            z = lax.dot_general(q_h, k_blk, (((1,), (1,)), ((), ())),
                                preferred_element_type=jnp.float32)
            softplus_neg_abs = jnp.log(1.0 + jnp.exp(-jnp.abs(z)))
            log_beta = jnp.minimum(z, 0.0) - softplus_neg_abs
            log_not_beta = log_beta - z
            if diagonal:
                log_not_beta = jnp.where(causal, log_not_beta, 0.0)
            hi = log_not_beta.astype(jnp.bfloat16)
            lo = (log_not_beta - hi.astype(jnp.float32)).astype(jnp.bfloat16)
            after = jnp.dot(jnp.concatenate([hi, lo], axis=1), suffix2,
                            preferred_element_type=jnp.float32)
            att = jnp.exp(log_beta + after + carry)
            if diagonal:
                att = jnp.where(causal, att, 0.0)
            acc = acc + jnp.dot(att.astype(jnp.bfloat16), v_blk, preferred_element_type=jnp.float32)
            carry = carry + after[:, 0:1] + log_not_beta[:, 0:1]
            new_state.append((carry, acc))
        return tuple(new_state)

    init = tuple((jnp.zeros((tq, 1), jnp.float32), jnp.zeros((tq, LANES), jnp.float32)) for _ in q_heads)
    state = tile(qi, init, diagonal=True)
    state = lax.fori_loop(0, qi, lambda j, s: tile(qi - 1 - j, s, diagonal=False), state)
    o_ref[0] = jnp.where(first_head, state[0][1], state[1][1]).astype(o_ref.dtype)


def _sb_attn(qkv, batch, seq):
    qkv3 = qkv.reshape(batch, seq, 3 * SB_WIDTH)
    tq = ATTN_TILE
    pairs = SB_WIDTH // LANES
    return pl.pallas_call(
        _sb_attn_kernel,
        grid=(batch, pairs, seq // tq),
        in_specs=[pl.BlockSpec((1, tq, LANES), lambda b, h, i: (b, i, h)),
                  pl.BlockSpec((1, seq, LANES), lambda b, h, i: (b, 0, pairs + h)),
                  pl.BlockSpec((1, seq, LANES), lambda b, h, i: (b, 0, 2 * pairs + h))],
        out_specs=pl.BlockSpec((1, tq, LANES), lambda b, h, i: (b, i, h)),
        out_shape=jax.ShapeDtypeStruct((batch, seq, SB_WIDTH), jnp.bfloat16),
        compiler_params=pltpu.CompilerParams(vmem_limit_bytes=VMEM_LIMIT),
        name="sb_attn",
    )(qkv3, qkv3, qkv3)


def _route(logits):
    lane = lax.broadcasted_iota(jnp.int32, logits.shape, 1)
    neg = jnp.float32(-jnp.inf)
    big = jnp.int32(LANES)
    is_group = (lane >= N_EXPERTS) & (lane < N_EXPERTS + N_EXPERT_GROUPS)
    gl = jnp.where(is_group, logits, neg)
    g_max = jnp.max(gl, axis=-1, keepdims=True)
    g_sum = jnp.sum(jnp.exp(gl - g_max), axis=-1, keepdims=True)
    g_p = 1.0 / g_sum
    g_idx = jnp.min(jnp.where(gl == g_max, lane, big), axis=-1, keepdims=True) - N_EXPERTS
    in_group = (lane >= g_idx * EXPERTS_PER_GROUP) & (lane < (g_idx + 1) * EXPERTS_PER_GROUP)
    el = jnp.where(in_group, logits, neg)
    e1 = jnp.max(el, axis=-1, keepdims=True)
    i1 = jnp.min(jnp.where(el == e1, lane, big), axis=-1, keepdims=True)
    el2 = jnp.where(lane == i1, neg, el)
    e2 = jnp.max(el2, axis=-1, keepdims=True)
    i2 = jnp.min(jnp.where(el2 == e2, lane, big), axis=-1, keepdims=True)
    r = jnp.exp(e2 - e1)
    w1 = g_p / (1.0 + r)
    w2 = w1 * r
    return jnp.where(lane == i1, w1, 0.0) + jnp.where(lane == i2, w2, 0.0)


def _mix_ln1_kernel(tiles_per_seq, osb_ref, p_ref, halo_ref, wpool_ref, pscale_ref, wout_ref, x_ref, mod_ref,
                    g1_ref, b1_ref, wr_ref, br_ref, x1_ref, u2_ref, comb_ref, pext_ref):
    tm = p_ref.shape[0]
    tile_in_seq = pl.program_id(0) % tiles_per_seq
    p = p_ref[...]
    pext_ref[0:HALO, :] = jnp.where(tile_in_seq == 0, 0.0, halo_ref[...])
    pext_ref[HALO:, :] = p
    pos = tile_in_seq * tm + lax.broadcasted_iota(jnp.int32, (tm, 1), 0)

    mixed = jnp.dot(osb_ref[...], wout_ref[0:SB_WIDTH, :], preferred_element_type=jnp.float32)
    for g, w in enumerate(POOL_WINDOWS):
        cols = slice(g * POOL_GROUP_DIM, (g + 1) * POOL_GROUP_DIM)
        win = p[:, cols]
        for i in range(1, w):
            win = win + pext_ref[HALO - i:HALO - i + tm, cols]
        count = jnp.minimum(pos + 1, w).astype(jnp.float32)
        pooled = win / count - p[:, cols]
        o_pool = jnp.dot(pooled.astype(jnp.bfloat16), wpool_ref[g], preferred_element_type=jnp.float32)
        o_pool = o_pool * pscale_ref[:, cols]
        mixed = mixed + jnp.dot(o_pool.astype(jnp.bfloat16),
                                wout_ref[SB_WIDTH + g * POOL_GROUP_DIM:SB_WIDTH + (g + 1) * POOL_GROUP_DIM, :],
                                preferred_element_type=jnp.float32)

    gate1 = mod_ref[0, 2:3, :]
    shift2 = mod_ref[0, 3:4, :]
    scale2 = mod_ref[0, 4:5, :]
    x1 = _ln(DEEPNORM_ALPHA * x_ref[...] + (1.0 + gate1) * mixed) * g1_ref[...] + b1_ref[...]
    x1_ref[...] = x1
    u2 = _ln(x1) * (1.0 + scale2) + shift2
    u2_ref[...] = u2.astype(jnp.bfloat16)
    logits = jnp.dot(u2, wr_ref[...], preferred_element_type=jnp.float32,
                     precision=lax.Precision.HIGHEST) + br_ref[...]
    comb_ref[...] = _route(logits)


def _mix_ln1(o_sb, p, w_pool_bf16, pool_scale, w_out_bf16, x2d, mod, ln1_g, ln1_b, w_router, b_router, seq):
    t, d = x2d.shape
    tm = ROW_TILE
    tiles_per_seq = seq // tm
    halo_blocks_per_tile = tm // HALO
    row = lambda i: (i, 0)
    const2 = lambda i: (0, 0)
    return pl.pallas_call(
        functools.partial(_mix_ln1_kernel, tiles_per_seq),
        grid=(t // tm,),
        in_specs=[pl.BlockSpec((tm, SB_WIDTH), row),
                  pl.BlockSpec((tm, POOL_WIDTH), row),
                  pl.BlockSpec((HALO, POOL_WIDTH), lambda i: (jnp.maximum(i * halo_blocks_per_tile - 1, 0), 0)),
                  pl.BlockSpec(w_pool_bf16.shape, lambda i: (0, 0, 0)),
                  pl.BlockSpec((1, POOL_WIDTH), const2),
                  pl.BlockSpec(w_out_bf16.shape, const2),
                  pl.BlockSpec((tm, d), row),
                  pl.BlockSpec((1, N_MOD, d), lambda i: (i // tiles_per_seq, 0, 0)),
                  pl.BlockSpec((1, d), const2),
                  pl.BlockSpec((1, d), const2),
                  pl.BlockSpec((d, LANES), const2),
                  pl.BlockSpec((1, LANES), const2)],
        out_specs=[pl.BlockSpec((tm, d), row),
                   pl.BlockSpec((tm, d), row),
                   pl.BlockSpec((tm, LANES), row)],
        out_shape=[jax.ShapeDtypeStruct((t, d), jnp.float32),
                   jax.ShapeDtypeStruct((t, d), jnp.bfloat16),
                   jax.ShapeDtypeStruct((t, LANES), jnp.float32)],
        scratch_shapes=[pltpu.VMEM((HALO + tm, POOL_WIDTH), jnp.float32)],
        compiler_params=pltpu.CompilerParams(vmem_limit_bytes=VMEM_LIMIT),
        name="mix_ln1",
    )(o_sb, p, p, w_pool_bf16, pool_scale, w_out_bf16, x2d, mod, ln1_g, ln1_b, w_router, b_router)


def _moe_ln2_kernel(u_ref, comb_ref, wg_ref, wu_ref, wd_ref, x1_ref, mod_ref, g2_ref, b2_ref, o_ref, acc_ref):
    e = pl.program_id(1)

    @pl.when(e == 0)
    def _():
        acc_ref[...] = jnp.zeros_like(acc_ref)

    u = u_ref[...]
    gate = jnp.dot(u, wg_ref[0], preferred_element_type=jnp.float32)
    up = jnp.dot(u, wu_ref[0], preferred_element_type=jnp.float32)
    h = gate * jax.nn.sigmoid(gate) * up
    y = jnp.dot(h.astype(jnp.bfloat16), wd_ref[0], preferred_element_type=jnp.float32)
    comb = comb_ref[...]
    lane = lax.broadcasted_iota(jnp.int32, comb.shape, 1)
    weight = jnp.sum(jnp.where(lane == e, comb, 0.0), axis=-1, keepdims=True)
    acc_ref[...] += weight * y

    @pl.when(e == N_EXPERTS - 1)
    def _():
        gate2 = mod_ref[0, 5:6, :]
        o_ref[...] = _ln(DEEPNORM_ALPHA * x1_ref[...] + (1.0 + gate2) * acc_ref[...]) * g2_ref[...] + b2_ref[...]


def _moe_ln2(u2, combine, wg, wu, wd, x1, mod, ln2_g, ln2_b, seq):
    t, d = x1.shape
    tm = MOE_ROW_TILE
    tiles_per_seq = seq // tm
    row = lambda i, e: (i, 0)
    const2 = lambda i, e: (0, 0)
    expert = lambda i, e: (e, 0, 0)
    return pl.pallas_call(
        _moe_ln2_kernel,
        grid=(t // tm, N_EXPERTS),
        in_specs=[pl.BlockSpec((tm, d), row),
                  pl.BlockSpec((tm, LANES), row),
                  pl.BlockSpec((1, d, EXPERT_HIDDEN), expert),
                  pl.BlockSpec((1, d, EXPERT_HIDDEN), expert),
                  pl.BlockSpec((1, EXPERT_HIDDEN, d), expert),
                  pl.BlockSpec((tm, d), row),
                  pl.BlockSpec((1, N_MOD, d), lambda i, e: (i // tiles_per_seq, 0, 0)),
                  pl.BlockSpec((1, d), const2),
                  pl.BlockSpec((1, d), const2)],
        out_specs=pl.BlockSpec((tm, d), row),
        out_shape=jax.ShapeDtypeStruct((t, d), jnp.float32),
        scratch_shapes=[pltpu.VMEM((tm, d), jnp.float32)],
        compiler_params=pltpu.CompilerParams(vmem_limit_bytes=VMEM_LIMIT,
                                             dimension_semantics=("arbitrary", "arbitrary")),
        name="moe_ln2",
    )(u2, combine, wg, wu, wd, x1, mod, ln2_g, ln2_b)


def kernel(x, c, w_ada, b_ada, w_in, w_pool, pool_scale, w_out, ln1_g, ln1_b, w_router_group, b_router_group,
           w_router_expert, b_router_expert, w_gate, w_up, w_down, ln2_g, ln2_b):
    b, s, d = x.shape
    bf16 = jnp.bfloat16
    for layer in range(DEPTH):
        c_pad = jnp.pad(c, ((0, 8 - b), (0, 0)))
        mod = _adaln(c_pad, w_ada[layer], b_ada[layer][None, :])[:b].reshape(b, N_MOD, d)
        x2d = x.reshape(b * s, d)
        qkv, p = _ln_inproj(x2d, mod, w_in[layer].astype(bf16), s)
        o_sb = _sb_attn(qkv, b, s).reshape(b * s, SB_WIDTH)
        pad = LANES - N_EXPERTS - N_EXPERT_GROUPS
        w_router = jnp.pad(jnp.concatenate([w_router_expert[layer], w_router_group[layer]], axis=1),
                           ((0, 0), (0, pad)))
        b_router = jnp.pad(jnp.concatenate([b_router_expert[layer], b_router_group[layer]]), (0, pad))[None, :]
        x1, u2, combine = _mix_ln1(o_sb, p, w_pool[layer].astype(bf16), pool_scale[layer][None, :],
                                   w_out[layer].astype(bf16), x2d, mod, ln1_g[layer][None, :],
                                   ln1_b[layer][None, :], w_router, b_router, s)
        x2 = _moe_ln2(u2, combine, w_gate[layer].astype(bf16), w_up[layer].astype(bf16),
                      w_down[layer].astype(bf16), x1, mod, ln2_g[layer][None, :], ln2_b[layer][None, :], s)
        x = x2.reshape(b, s, d)
    return x
```

```python
import functools

import jax
import jax.numpy as jnp
from jax import lax
from jax.experimental import pallas as pl
from jax.experimental.pallas import tpu as pltpu

D_MODEL = 1024
N_SB_HEADS = 8
SB_HEAD_DIM = 64
SB_WIDTH = N_SB_HEADS * SB_HEAD_DIM
POOL_WINDOWS = (2, 4, 8, 16)
POOL_GROUP_DIM = 128
POOL_WIDTH = len(POOL_WINDOWS) * POOL_GROUP_DIM
N_EXPERT_GROUPS = 4
EXPERTS_PER_GROUP = 4
N_EXPERTS = N_EXPERT_GROUPS * EXPERTS_PER_GROUP
EXPERT_HIDDEN = 512
DEPTH = 1
DEEPNORM_ALPHA = (2.0 * DEPTH) ** 0.25
LN_EPS = 1e-5
N_MOD = 6

LANES = 128
HALO = max(POOL_WINDOWS)
VMEM_LIMIT = 56 * 1024 * 1024

ROW_TILE = 512
ATTN_TILE = 256
MOE_ROW_TILE = 512


def _ln(x):
    mu = jnp.mean(x, axis=-1, keepdims=True)
    xc = x - mu
    var = jnp.mean(xc * xc, axis=-1, keepdims=True)
    return xc * lax.rsqrt(var + LN_EPS)


def _adaln_kernel(c_ref, w_ref, b_ref, o_ref):
    c = c_ref[...]
    a = c * jax.nn.sigmoid(c)
    o_ref[...] = jnp.dot(a, w_ref[...], preferred_element_type=jnp.float32,
                         precision=lax.Precision.HIGHEST) + b_ref[...]


def _adaln(c_pad, w_ada, b_ada):
    rows, d = c_pad.shape
    n = w_ada.shape[1]
    tn = 1024
    return pl.pallas_call(
        _adaln_kernel,
        grid=(n // tn,),
        in_specs=[pl.BlockSpec((rows, d), lambda j: (0, 0)),
                  pl.BlockSpec((d, tn), lambda j: (0, j)),
                  pl.BlockSpec((1, tn), lambda j: (0, j))],
        out_specs=pl.BlockSpec((rows, tn), lambda j: (0, j)),
        out_shape=jax.ShapeDtypeStruct((rows, n), jnp.float32),
        compiler_params=pltpu.CompilerParams(vmem_limit_bytes=VMEM_LIMIT),
        name="adaln",
    )(c_pad, w_ada, b_ada)


def _ln_inproj_kernel(x_ref, mod_ref, w_ref, qkv_ref, p_ref):
    shift = mod_ref[0, 0:1, :]
    scale = mod_ref[0, 1:2, :]
    u = (_ln(x_ref[...]) * (1.0 + scale) + shift).astype(jnp.bfloat16)
    qk_scale = SB_HEAD_DIM ** -0.5 * 1.4426950408889634
    q =jnp.dot(u, w_ref[:, 0:SB_WIDTH], preferred_element_type=jnp.float32) * qk_scale
    qkv_ref[:, 0:SB_WIDTH] = q.astype(jnp.bfloat16)
    for j in (1, 2):
        kv = jnp.dot(u, w_ref[:, j * SB_WIDTH:(j + 1) * SB_WIDTH], preferred_element_type=jnp.float32)
        qkv_ref[:, j * SB_WIDTH:(j + 1) * SB_WIDTH] = kv.astype(jnp.bfloat16)
    p_ref[...] = jnp.dot(u, w_ref[:, 3 * SB_WIDTH:], preferred_element_type=jnp.float32)


def _ln_inproj(x2d, mod, w_in_bf16, seq):
    t, d = x2d.shape
    tm = ROW_TILE
    tiles_per_seq = seq // tm
    return pl.pallas_call(
        _ln_inproj_kernel,
        grid=(t // tm,),
        in_specs=[pl.BlockSpec((tm, d), lambda i: (i, 0)),
                  pl.BlockSpec((1, N_MOD, d), lambda i: (i // tiles_per_seq, 0, 0)),
                  pl.BlockSpec(w_in_bf16.shape, lambda i: (0, 0))],
        out_specs=[pl.BlockSpec((tm, 3 * SB_WIDTH), lambda i: (i, 0)),
                   pl.BlockSpec((tm, POOL_WIDTH), lambda i: (i, 0))],
        out_shape=[jax.ShapeDtypeStruct((t, 3 * SB_WIDTH), jnp.bfloat16),
                   jax.ShapeDtypeStruct((t, POOL_WIDTH), jnp.float32)],
        compiler_params=pltpu.CompilerParams(vmem_limit_bytes=VMEM_LIMIT),
        name="ln_inproj",
    )(x2d, mod, w_in_bf16)


def _sb_attn_kernel(q_ref, k_ref, v_ref, o_ref, z_ref, zkeep_ref, split_ref, att_ref, acc_ref, carry_ref):
    tq = tk = ATTN_TILE
    heads = (0, 1)
    qi = pl.program_id(2)
    q2 = q_ref[0]
    lane = lax.broadcasted_iota(jnp.int32, (tq, LANES), 1)
    first_head = lane < SB_HEAD_DIM
    zero = jnp.zeros_like(q2)
    q_heads = (jnp.where(first_head, q2, zero), jnp.where(first_head, zero, q2))

    row = lax.broadcasted_iota(jnp.int32, (tk, tk), 0)
    col = lax.broadcasted_iota(jnp.int32, (tk, tk), 1)
    neg_suffix = jnp.where(row >= col, -1.0, 0.0).astype(jnp.bfloat16)
    neg_suffix2 = jnp.concatenate([neg_suffix, neg_suffix], axis=0)
    causal = col < row

    def key_start(j):
        return pl.multiple_of(jnp.maximum(qi - j, 0) * tk, tk)

    def stage_a(j, par):
        k_blk = k_ref[0, pl.ds(key_start(j), tk), :]
        for h in heads:
            z_ref[par, h] = lax.dot_general(q_heads[h], k_blk, (((1,), (1,)), ((), ())),
                                            preferred_element_type=jnp.float32)

    def stage_b(par, diagonal):
        for h in heads:
            z = z_ref[par, h]
            n = jnp.maximum(z, 0.0) + jnp.log2(1.0 + jnp.exp2(-jnp.abs(z)))
            if diagonal:
                n = jnp.where(causal, n, 0.0)
            hi = n.astype(jnp.bfloat16)
            lo = (n - hi.astype(jnp.float32)).astype(jnp.bfloat16)
            split_ref[par, h, :, 0:tk] = hi
            split_ref[par, h, :, tk:] = lo
            zkeep_ref[par, h] = z

    def stage_c(par, diagonal):
        incls = [jnp.dot(split_ref[par, h], neg_suffix2, preferred_element_type=jnp.float32)
                 for h in heads]
        for h in heads:
            carry = carry_ref[h]
            att = jnp.exp2(zkeep_ref[par, h] + incls[h] + jnp.concatenate([carry] * (tk // LANES), axis=1))
            if diagonal:
                att = jnp.where(causal, att, 0.0)
            att_ref[par, h] = att.astype(jnp.bfloat16)
            carry_ref[h] = carry + jnp.broadcast_to(incls[h][:, 0:1], (tq, LANES))

    def stage_d(j, par):
        v_blk = v_ref[0, pl.ds(key_start(j), tk), :]
        for h in heads:
            acc_ref[h] += jnp.dot(att_ref[par, h], v_blk, preferred_element_type=jnp.float32)

    def step(i, par):
        stage_b(1 - par, diagonal=False)
        stage_c(par, diagonal=False)
        stage_a(i + 2, par)
        stage_d(i - 1, 1 - par)

    carry_ref[...] = jnp.zeros_like(carry_ref)
    acc_ref[...] = jnp.zeros_like(acc_ref)
    stage_a(0, 0)
    stage_b(0, diagonal=True)
    stage_a(1, 1)
    stage_c(0, diagonal=True)
    stage_a(2, 0)
    stage_b(1, diagonal=False)

    def two_steps(m, _):
        step(2 * m + 1, 1)
        step(2 * m + 2, 0)
        return 0

    lax.fori_loop(0, qi // 2, two_steps, 0)

    @pl.when(qi % 2 == 1)
    def _():
        step(qi, 1)
        stage_d(qi, 1)

    @pl.when(qi % 2 == 0)
    def _():
        stage_d(qi, 0)

    o_ref[0] = jnp.where(first_head, acc_ref[0], acc_ref[1]).astype(o_ref.dtype)


def _sb_attn(qkv, batch, seq):
    qkv3 = qkv.reshape(batch, seq, 3 * SB_WIDTH)
    tq = ATTN_TILE
    pairs = SB_WIDTH // LANES
    return pl.pallas_call(
        _sb_attn_kernel,
        grid=(batch, pairs, seq // tq),
        in_specs=[pl.BlockSpec((1, tq, LANES), lambda b, h, i: (b, i, h)),
                  pl.BlockSpec((1, seq, LANES), lambda b, h, i: (b, 0, pairs + h)),
                  pl.BlockSpec((1, seq, LANES), lambda b, h, i: (b, 0, 2 * pairs + h))],
        out_specs=pl.BlockSpec((1, tq, LANES), lambda b, h, i: (b, i, h)),
        out_shape=jax.ShapeDtypeStruct((batch, seq, SB_WIDTH), jnp.bfloat16),
        scratch_shapes=[pltpu.VMEM((2, 2, tq, tq), jnp.float32),
                        pltpu.VMEM((2, 2, tq, tq), jnp.float32),
                        pltpu.VMEM((2, 2, tq, 2 * tq), jnp.bfloat16),
                        pltpu.VMEM((2, 2, tq, tq), jnp.bfloat16),
                        pltpu.VMEM((2, tq, LANES), jnp.float32),
                        pltpu.VMEM((2, tq, LANES), jnp.float32)],
        compiler_params=pltpu.CompilerParams(vmem_limit_bytes=VMEM_LIMIT),
        name="sb_attn",
    )(qkv3, qkv3, qkv3)


def _route(logits):
    lane = lax.broadcasted_iota(jnp.int32, logits.shape, 1)
    neg = jnp.float32(-jnp.inf)
    big = jnp.int32(LANES)
    is_group = (lane >= N_EXPERTS) & (lane < N_EXPERTS + N_EXPERT_GROUPS)
    gl = jnp.where(is_group, logits, neg)
    g_max = jnp.max(gl, axis=-1, keepdims=True)
    g_sum = jnp.sum(jnp.exp(gl - g_max), axis=-1, keepdims=True)
    g_p = 1.0 / g_sum
    g_idx = jnp.min(jnp.where(gl == g_max, lane, big), axis=-1, keepdims=True) - N_EXPERTS
    in_group = (lane >= g_idx * EXPERTS_PER_GROUP) & (lane < (g_idx + 1) * EXPERTS_PER_GROUP)
    el = jnp.where(in_group, logits, neg)
    e1 = jnp.max(el, axis=-1, keepdims=True)
    i1 = jnp.min(jnp.where(el == e1, lane, big), axis=-1, keepdims=True)
    el2 = jnp.where(lane == i1, neg, el)
    e2 = jnp.max(el2, axis=-1, keepdims=True)
    i2 = jnp.min(jnp.where(el2 == e2, lane, big), axis=-1, keepdims=True)
    r = jnp.exp(e2 - e1)
    w1 = g_p / (1.0 + r)
    w2 = w1 * r
    return jnp.where(lane == i1, w1, 0.0) + jnp.where(lane == i2, w2, 0.0)


def _mix_ln1_kernel(tiles_per_seq, osb_ref, p_ref, halo_ref, wpool_ref, pscale_ref, wout_ref, x_ref, mod_ref,
                    g1_ref, b1_ref, wr_ref, br_ref, x1_ref, u2_ref, comb_ref, pext_ref):
    tm = p_ref.shape[0]
    tile_in_seq = pl.program_id(0) % tiles_per_seq
    p = p_ref[...]
    pext_ref[0:HALO, :] = jnp.where(tile_in_seq == 0, 0.0, halo_ref[...])
    pext_ref[HALO:, :] = p
    pos = tile_in_seq * tm + lax.broadcasted_iota(jnp.int32, (tm, 1), 0)

    mixed = jnp.dot(osb_ref[...], wout_ref[0:SB_WIDTH, :], preferred_element_type=jnp.float32)
    for g, w in enumerate(POOL_WINDOWS):
        cols = slice(g * POOL_GROUP_DIM, (g + 1) * POOL_GROUP_DIM)
        win = p[:, cols]
        for i in range(1, w):
            win = win + pext_ref[HALO - i:HALO - i + tm, cols]
        count = jnp.minimum(pos + 1, w).astype(jnp.float32)
        pooled = win / count - p[:, cols]
        o_pool = jnp.dot(pooled.astype(jnp.bfloat16), wpool_ref[g], preferred_element_type=jnp.float32)
        o_pool = o_pool * pscale_ref[:, cols]
        mixed = mixed + jnp.dot(o_pool.astype(jnp.bfloat16),
                                wout_ref[SB_WIDTH + g * POOL_GROUP_DIM:SB_WIDTH + (g + 1) * POOL_GROUP_DIM, :],
                                preferred_element_type=jnp.float32)

    gate1 = mod_ref[0, 2:3, :]
    shift2 = mod_ref[0, 3:4, :]
    scale2 = mod_ref[0, 4:5, :]
    x1 = _ln(DEEPNORM_ALPHA * x_ref[...] + (1.0 + gate1) * mixed) * g1_ref[...] + b1_ref[...]
    x1_ref[...] = x1
    u2 = _ln(x1) * (1.0 + scale2) + shift2
    u2_ref[...] = u2.astype(jnp.bfloat16)
    logits = jnp.dot(u2, wr_ref[...], preferred_element_type=jnp.float32,
                     precision=lax.Precision.HIGHEST) + br_ref[...]
    comb_ref[...] = _route(logits)


def _mix_ln1(o_sb, p, w_pool_bf16, pool_scale, w_out_bf16, x2d, mod, ln1_g, ln1_b, w_router, b_router, seq):
    t, d = x2d.shape
    tm = ROW_TILE
    tiles_per_seq = seq // tm
    halo_blocks_per_tile = tm // HALO
    row = lambda i: (i, 0)
    const2 = lambda i: (0, 0)
    return pl.pallas_call(
        functools.partial(_mix_ln1_kernel, tiles_per_seq),
        grid=(t // tm,),
        in_specs=[pl.BlockSpec((tm, SB_WIDTH), row),
                  pl.BlockSpec((tm, POOL_WIDTH), row),
                  pl.BlockSpec((HALO, POOL_WIDTH), lambda i: (jnp.maximum(i * halo_blocks_per_tile - 1, 0), 0)),
                  pl.BlockSpec(w_pool_bf16.shape, lambda i: (0, 0, 0)),
                  pl.BlockSpec((1, POOL_WIDTH), const2),
                  pl.BlockSpec(w_out_bf16.shape, const2),
                  pl.BlockSpec((tm, d), row),
                  pl.BlockSpec((1, N_MOD, d), lambda i: (i // tiles_per_seq, 0, 0)),
                  pl.BlockSpec((1, d), const2),
                  pl.BlockSpec((1, d), const2),
                  pl.BlockSpec((d, LANES), const2),
                  pl.BlockSpec((1, LANES), const2)],
        out_specs=[pl.BlockSpec((tm, d), row),
                   pl.BlockSpec((tm, d), row),
                   pl.BlockSpec((tm, LANES), row)],
        out_shape=[jax.ShapeDtypeStruct((t, d), jnp.float32),
                   jax.ShapeDtypeStruct((t, d), jnp.bfloat16),
                   jax.ShapeDtypeStruct((t, LANES), jnp.float32)],
        scratch_shapes=[pltpu.VMEM((HALO + tm, POOL_WIDTH), jnp.float32)],
        compiler_params=pltpu.CompilerParams(vmem_limit_bytes=VMEM_LIMIT),
        name="mix_ln1",
    )(o_sb, p, p, w_pool_bf16, pool_scale, w_out_bf16, x2d, mod, ln1_g, ln1_b, w_router, b_router)


def _moe_ln2_kernel(u_ref, comb_ref, wg_ref, wu_ref, wd_ref, x1_ref, mod_ref, g2_ref, b2_ref, o_ref, acc_ref):
    e = pl.program_id(1)

    @pl.when(e == 0)
    def _():
        acc_ref[...] = jnp.zeros_like(acc_ref)

    u = u_ref[...]
    gate = jnp.dot(u, wg_ref[0], preferred_element_type=jnp.float32)
    up = jnp.dot(u, wu_ref[0], preferred_element_type=jnp.float32)
    h = gate * jax.nn.sigmoid(gate) * up
    y = jnp.dot(h.astype(jnp.bfloat16), wd_ref[0], preferred_element_type=jnp.float32)
    comb = comb_ref[...]
    lane = lax.broadcasted_iota(jnp.int32, comb.shape, 1)
    weight = jnp.sum(jnp.where(lane == e, comb, 0.0), axis=-1, keepdims=True)
    acc_ref[...] += weight * y

    @pl.when(e == N_EXPERTS - 1)
    def _():
        gate2 = mod_ref[0, 5:6, :]
        o_ref[...] = _ln(DEEPNORM_ALPHA * x1_ref[...] + (1.0 + gate2) * acc_ref[...]) * g2_ref[...] + b2_ref[...]


def _moe_ln2(u2, combine, wg, wu, wd, x1, mod, ln2_g, ln2_b, seq):
    t, d = x1.shape
    tm = MOE_ROW_TILE
    tiles_per_seq = seq // tm
    row = lambda i, e: (i, 0)
    const2 = lambda i, e: (0, 0)
    expert = lambda i, e: (e, 0, 0)
    return pl.pallas_call(
        _moe_ln2_kernel,
        grid=(t // tm, N_EXPERTS),
        in_specs=[pl.BlockSpec((tm, d), row),
                  pl.BlockSpec((tm, LANES), row),
                  pl.BlockSpec((1, d, EXPERT_HIDDEN), expert),
                  pl.BlockSpec((1, d, EXPERT_HIDDEN), expert),
                  pl.BlockSpec((1, EXPERT_HIDDEN, d), expert),
                  pl.BlockSpec((tm, d), row),
                  pl.BlockSpec((1, N_MOD, d), lambda i, e: (i // tiles_per_seq, 0, 0)),
                  pl.BlockSpec((1, d), const2),
                  pl.BlockSpec((1, d), const2)],
        out_specs=pl.BlockSpec((tm, d), row),
        out_shape=jax.ShapeDtypeStruct((t, d), jnp.float32),
        scratch_shapes=[pltpu.VMEM((tm, d), jnp.float32)],
        compiler_params=pltpu.CompilerParams(vmem_limit_bytes=VMEM_LIMIT,
                                             dimension_semantics=("arbitrary", "arbitrary")),
        name="moe_ln2",
    )(u2, combine, wg, wu, wd, x1, mod, ln2_g, ln2_b)


def kernel(x, c, w_ada, b_ada, w_in, w_pool, pool_scale, w_out, ln1_g, ln1_b, w_router_group, b_router_group,
           w_router_expert, b_router_expert, w_gate, w_up, w_down, ln2_g, ln2_b):
    b, s, d = x.shape
    bf16 = jnp.bfloat16
    for layer in range(DEPTH):
        c_pad = jnp.pad(c, ((0, 8 - b), (0, 0)))
        mod = _adaln(c_pad, w_ada[layer], b_ada[layer][None, :])[:b].reshape(b, N_MOD, d)
        x2d = x.reshape(b * s, d)
        qkv, p = _ln_inproj(x2d, mod, w_in[layer].astype(bf16), s)
        o_sb = _sb_attn(qkv, b, s).reshape(b * s, SB_WIDTH)
        pad = LANES - N_EXPERTS - N_EXPERT_GROUPS
        w_router = jnp.pad(jnp.concatenate([w_router_expert[layer], w_router_group[layer]], axis=1),
                           ((0, 0), (0, pad)))
        b_router = jnp.pad(jnp.concatenate([b_router_expert[layer], b_router_group[layer]]), (0, pad))[None, :]
        x1, u2, combine = _mix_ln1(o_sb, p, w_pool[layer].astype(bf16), pool_scale[layer][None, :],
                                   w_out[layer].astype(bf16), x2d, mod, ln1_g[layer][None, :],
                                   ln1_b[layer][None, :], w_router, b_router, s)
        x2 = _moe_ln2(u2, combine, w_gate[layer].astype(bf16), w_up[layer].astype(bf16),
                      w_down[layer].astype(bf16), x1, mod, ln2_g[layer][None, :], ln2_b[layer][None, :], s)
        x = x2.reshape(b, s, d)
    return x
```

```python
import functools

import jax
import jax.numpy as jnp
from jax import lax
from jax.experimental import pallas as pl
from jax.experimental.pallas import tpu as pltpu

D_MODEL = 1024
N_SB_HEADS = 8
SB_HEAD_DIM = 64
SB_WIDTH = N_SB_HEADS * SB_HEAD_DIM
POOL_WINDOWS = (2, 4, 8, 16)
POOL_GROUP_DIM = 128
POOL_WIDTH = len(POOL_WINDOWS) * POOL_GROUP_DIM
N_EXPERT_GROUPS = 4
EXPERTS_PER_GROUP = 4
N_EXPERTS = N_EXPERT_GROUPS * EXPERTS_PER_GROUP
EXPERT_HIDDEN = 512
DEPTH = 1
DEEPNORM_ALPHA = (2.0 * DEPTH) ** 0.25
LN_EPS = 1e-5
N_MOD = 6

LANES = 128
SUBLANES = 8
HALO = max(POOL_WINDOWS)
VMEM_LIMIT = 56 * 1024 * 1024

ROW_TILE = 512
ATTN_TILE = 256
MOE_ROW_TILE = 512
COMBINE_ROW_TILE = 256
GATHER_UNROLL = 8


def _ln(x):
    mu = jnp.mean(x, axis=-1, keepdims=True)
    xc = x - mu
    var = jnp.mean(xc * xc, axis=-1, keepdims=True)
    return xc * lax.rsqrt(var + LN_EPS)


def _adaln_kernel(c_ref, w_ref, b_ref, o_ref):
    c = c_ref[...]
    a = c * jax.nn.sigmoid(c)
    o_ref[...] = jnp.dot(a, w_ref[...], preferred_element_type=jnp.float32,
                         precision=lax.Precision.HIGHEST) + b_ref[...]


def _adaln(c_pad, w_ada, b_ada):
    rows, d = c_pad.shape
    n = w_ada.shape[1]
    tn = 1024
    return pl.pallas_call(
        _adaln_kernel,
        grid=(n // tn,),
        in_specs=[pl.BlockSpec((rows, d), lambda j: (0, 0)),
                  pl.BlockSpec((d, tn), lambda j: (0, j)),
                  pl.BlockSpec((1, tn), lambda j: (0, j))],
        out_specs=pl.BlockSpec((rows, tn), lambda j: (0, j)),
        out_shape=jax.ShapeDtypeStruct((rows, n), jnp.float32),
        compiler_params=pltpu.CompilerParams(vmem_limit_bytes=VMEM_LIMIT),
        name="adaln",
    )(c_pad, w_ada, b_ada)


def _ln_inproj_kernel(x_ref, mod_ref, w_ref, qkv_ref, p_ref):
    shift = mod_ref[0, 0:1, :]
    scale = mod_ref[0, 1:2, :]
    u = (_ln(x_ref[...]) * (1.0 + scale) + shift).astype(jnp.bfloat16)
    qk_scale = SB_HEAD_DIM ** -0.5 * 1.4426950408889634
    q = jnp.dot(u, w_ref[:, 0:SB_WIDTH], preferred_element_type=jnp.float32) * qk_scale
    qkv_ref[:, 0:SB_WIDTH] = q.astype(jnp.bfloat16)
    for j in (1, 2):
        kv = jnp.dot(u, w_ref[:, j * SB_WIDTH:(j + 1) * SB_WIDTH], preferred_element_type=jnp.float32)
        qkv_ref[:, j * SB_WIDTH:(j + 1) * SB_WIDTH] = kv.astype(jnp.bfloat16)
    p_ref[...] = jnp.dot(u, w_ref[:, 3 * SB_WIDTH:], preferred_element_type=jnp.float32)


def _ln_inproj(x2d, mod, w_in_bf16, seq):
    t, d = x2d.shape
    tm = ROW_TILE
    tiles_per_seq = seq // tm
    return pl.pallas_call(
        _ln_inproj_kernel,
        grid=(t // tm,),
        in_specs=[pl.BlockSpec((tm, d), lambda i: (i, 0)),
                  pl.BlockSpec((1, N_MOD, d), lambda i: (i // tiles_per_seq, 0, 0)),
                  pl.BlockSpec(w_in_bf16.shape, lambda i: (0, 0))],
        out_specs=[pl.BlockSpec((tm, 3 * SB_WIDTH), lambda i: (i, 0)),
                   pl.BlockSpec((tm, POOL_WIDTH), lambda i: (i, 0))],
        out_shape=[jax.ShapeDtypeStruct((t, 3 * SB_WIDTH), jnp.bfloat16),
                   jax.ShapeDtypeStruct((t, POOL_WIDTH), jnp.float32)],
        compiler_params=pltpu.CompilerParams(vmem_limit_bytes=VMEM_LIMIT),
        name="ln_inproj",
    )(x2d, mod, w_in_bf16)


def _sb_attn_kernel(q_ref, k_ref, v_ref, o_ref, z_ref, zkeep_ref, split_ref, att_ref, acc_ref, carry_ref):
    tq = tk = ATTN_TILE
    heads = (0, 1)
    qi = pl.program_id(2)
    q2 = q_ref[0]
    lane = lax.broadcasted_iota(jnp.int32, (tq, LANES), 1)
    first_head = lane < SB_HEAD_DIM
    zero = jnp.zeros_like(q2)
    q_heads = (jnp.where(first_head, q2, zero), jnp.where(first_head, zero, q2))

    row = lax.broadcasted_iota(jnp.int32, (tk, tk), 0)
    col = lax.broadcasted_iota(jnp.int32, (tk, tk), 1)
    neg_suffix = jnp.where(row >= col, -1.0, 0.0).astype(jnp.bfloat16)
    neg_suffix2 = jnp.concatenate([neg_suffix, neg_suffix], axis=0)
    causal = col < row

    def key_start(j):
        return pl.multiple_of(jnp.maximum(qi - j, 0) * tk, tk)

    def stage_a(j, par):
        k_blk = k_ref[0, pl.ds(key_start(j), tk), :]
        for h in heads:
            z_ref[par, h] = lax.dot_general(q_heads[h], k_blk, (((1,), (1,)), ((), ())),
                                            preferred_element_type=jnp.float32)

    def stage_b(par, diagonal):
        for h in heads:
            z = z_ref[par, h]
            n = jnp.maximum(z, 0.0) + jnp.log2(1.0 + jnp.exp2(-jnp.abs(z)))
            if diagonal:
                n = jnp.where(causal, n, 0.0)
            hi = n.astype(jnp.bfloat16)
            lo = (n - hi.astype(jnp.float32)).astype(jnp.bfloat16)
            split_ref[par, h, :, 0:tk] = hi
            split_ref[par, h, :, tk:] = lo
            zkeep_ref[par, h] = z

    def stage_c(par, diagonal):
        incls = [jnp.dot(split_ref[par, h], neg_suffix2, preferred_element_type=jnp.float32)
                 for h in heads]
        for h in heads:
            carry = carry_ref[h]
            att = jnp.exp2(zkeep_ref[par, h] + incls[h] + jnp.concatenate([carry] * (tk // LANES), axis=1))
            if diagonal:
                att = jnp.where(causal, att, 0.0)
            att_ref[par, h] = att.astype(jnp.bfloat16)
            carry_ref[h] = carry + jnp.broadcast_to(incls[h][:, 0:1], (tq, LANES))

    def stage_d(j, par):
        v_blk = v_ref[0, pl.ds(key_start(j), tk), :]
        for h in heads:
            acc_ref[h] += jnp.dot(att_ref[par, h], v_blk, preferred_element_type=jnp.float32)

    def step(i, par):
        stage_b(1 - par, diagonal=False)
        stage_c(par, diagonal=False)
        stage_a(i + 2, par)
        stage_d(i - 1, 1 - par)

    carry_ref[...] = jnp.zeros_like(carry_ref)
    acc_ref[...] = jnp.zeros_like(acc_ref)
    stage_a(0, 0)
    stage_b(0, diagonal=True)
    stage_a(1, 1)
    stage_c(0, diagonal=True)
    stage_a(2, 0)
    stage_b(1, diagonal=False)

    def two_steps(m, _):
        step(2 * m + 1, 1)
        step(2 * m + 2, 0)
        return 0

    lax.fori_loop(0, qi // 2, two_steps, 0)

    @pl.when(qi % 2 == 1)
    def _():
        step(qi, 1)
        stage_d(qi, 1)

    @pl.when(qi % 2 == 0)
    def _():
        stage_d(qi, 0)

    o_ref[0] = jnp.where(first_head, acc_ref[0], acc_ref[1]).astype(o_ref.dtype)


def _sb_attn(qkv, batch, seq):
    qkv3 = qkv.reshape(batch, seq, 3 * SB_WIDTH)
    tq = ATTN_TILE
    pairs = SB_WIDTH // LANES
    return pl.pallas_call(
        _sb_attn_kernel,
        grid=(batch, pairs, seq // tq),
        in_specs=[pl.BlockSpec((1, tq, LANES), lambda b, h, i: (b, i, h)),
                  pl.BlockSpec((1, seq, LANES), lambda b, h, i: (b, 0, pairs + h)),
                  pl.BlockSpec((1, seq, LANES), lambda b, h, i: (b, 0, 2 * pairs + h))],
        out_specs=pl.BlockSpec((1, tq, LANES), lambda b, h, i: (b, i, h)),
        out_shape=jax.ShapeDtypeStruct((batch, seq, SB_WIDTH), jnp.bfloat16),
        scratch_shapes=[pltpu.VMEM((2, 2, tq, tq), jnp.float32),
                        pltpu.VMEM((2, 2, tq, tq), jnp.float32),
                        pltpu.VMEM((2, 2, tq, 2 * tq), jnp.bfloat16),
                        pltpu.VMEM((2, 2, tq, tq), jnp.bfloat16),
                        pltpu.VMEM((2, tq, LANES), jnp.float32),
                        pltpu.VMEM((2, tq, LANES), jnp.float32)],
        compiler_params=pltpu.CompilerParams(vmem_limit_bytes=VMEM_LIMIT),
        name="sb_attn",
    )(qkv3, qkv3, qkv3)


def _route(logits):
    lane = lax.broadcasted_iota(jnp.int32, logits.shape, 1)
    neg = jnp.float32(-jnp.inf)
    big = jnp.int32(LANES)
    is_group = (lane >= N_EXPERTS) & (lane < N_EXPERTS + N_EXPERT_GROUPS)
    gl = jnp.where(is_group, logits, neg)
    g_max = jnp.max(gl, axis=-1, keepdims=True)
    g_sum = jnp.sum(jnp.exp(gl - g_max), axis=-1, keepdims=True)
    g_p = 1.0 / g_sum
    g_idx = jnp.min(jnp.where(gl == g_max, lane, big), axis=-1, keepdims=True) - N_EXPERTS
    in_group = (lane >= g_idx * EXPERTS_PER_GROUP) & (lane < (g_idx + 1) * EXPERTS_PER_GROUP)
    el = jnp.where(in_group, logits, neg)
    e1 = jnp.max(el, axis=-1, keepdims=True)
    i1 = jnp.min(jnp.where(el == e1, lane, big), axis=-1, keepdims=True)
    el2 = jnp.where(lane == i1, neg, el)
    e2 = jnp.max(el2, axis=-1, keepdims=True)
    i2 = jnp.min(jnp.where(el2 == e2, lane, big), axis=-1, keepdims=True)
    r = jnp.exp(e2 - e1)
    w1 = g_p / (1.0 + r)
    w2 = w1 * r
    return i1, i2, w1, w2


ROUTE_W1, ROUTE_W2, ROUTE_E1, ROUTE_E2, ROUTE_RANK1, ROUTE_RANK2 = range(6)


def _mix_ln1_kernel(tiles_per_seq, osb_ref, p_ref, halo_ref, wpool_ref, pscale_ref, wout_ref, x_ref, mod_ref,
                    g1_ref, b1_ref, wr_ref, br_ref, x1_ref, u2_ref, route_ref, count_ref, pext_ref, running_ref):
    tm = p_ref.shape[0]
    tile_in_seq = pl.program_id(0) % tiles_per_seq
    p = p_ref[...]
    pext_ref[0:HALO, :] = jnp.where(tile_in_seq == 0, 0.0, halo_ref[...])
    pext_ref[HALO:, :] = p
    pos = tile_in_seq * tm + lax.broadcasted_iota(jnp.int32, (tm, 1), 0)

    mixed = jnp.dot(osb_ref[...], wout_ref[0:SB_WIDTH, :], preferred_element_type=jnp.float32)
    for g, w in enumerate(POOL_WINDOWS):
        cols = slice(g * POOL_GROUP_DIM, (g + 1) * POOL_GROUP_DIM)
        win = p[:, cols]
        for i in range(1, w):
            win = win + pext_ref[HALO - i:HALO - i + tm, cols]
        count = jnp.minimum(pos + 1, w).astype(jnp.float32)
        pooled = win / count - p[:, cols]
        o_pool = jnp.dot(pooled.astype(jnp.bfloat16), wpool_ref[g], preferred_element_type=jnp.float32)
        o_pool = o_pool * pscale_ref[:, cols]
        mixed = mixed + jnp.dot(o_pool.astype(jnp.bfloat16),
                                wout_ref[SB_WIDTH + g * POOL_GROUP_DIM:SB_WIDTH + (g + 1) * POOL_GROUP_DIM, :],
                                preferred_element_type=jnp.float32)

    gate1 = mod_ref[0, 2:3, :]
    shift2 = mod_ref[0, 3:4, :]
    scale2 = mod_ref[0, 4:5, :]
    x1 = _ln(DEEPNORM_ALPHA * x_ref[...] + (1.0 + gate1) * mixed) * g1_ref[...] + b1_ref[...]
    x1_ref[...] = x1
    u2 = _ln(x1) * (1.0 + scale2) + shift2
    u2_ref[...] = u2
    logits = jnp.dot(u2, wr_ref[...], preferred_element_type=jnp.float32,
                     precision=lax.Precision.HIGHEST) + br_ref[...]
    i1, i2, w1, w2 = _route(logits)

    @pl.when(pl.program_id(0) == 0)
    def _():
        running_ref[...] = jnp.zeros_like(running_ref)

    lane = lax.broadcasted_iota(jnp.int32, (tm, LANES), 1)
    first, second = lane == i1, lane == i2
    uses = jnp.where(first | second, 1.0, 0.0)
    earlier = (lax.broadcasted_iota(jnp.int32, (tm, tm), 1)
               < lax.broadcasted_iota(jnp.int32, (tm, tm), 0)).astype(jnp.bfloat16)
    before = jnp.dot(earlier, uses.astype(jnp.bfloat16), preferred_element_type=jnp.float32)
    before = before + running_ref[0:1, :]
    rank1 = jnp.sum(jnp.where(first, before, 0.0), axis=-1, keepdims=True)
    rank2 = jnp.sum(jnp.where(second, before, 0.0), axis=-1, keepdims=True)
    running = running_ref[0:1, :] + jnp.sum(uses, axis=0, keepdims=True)
    running_ref[...] = jnp.broadcast_to(running, running_ref.shape)
    count_ref[...] = jnp.broadcast_to(running, count_ref.shape)

    record = jnp.zeros((tm, LANES), jnp.float32)
    for slot, value in ((ROUTE_W1, w1), (ROUTE_W2, w2), (ROUTE_E1, i1.astype(jnp.float32)),
                        (ROUTE_E2, i2.astype(jnp.float32)), (ROUTE_RANK1, rank1), (ROUTE_RANK2, rank2)):
        record = jnp.where(lane == slot, value, record)
    route_ref[...] = record


def _mix_ln1(o_sb, p, w_pool_bf16, pool_scale, w_out_bf16, x2d, mod, ln1_g, ln1_b, w_router, b_router, seq):
    t, d = x2d.shape
    tm = ROW_TILE
    tiles_per_seq = seq // tm
    halo_blocks_per_tile = tm // HALO
    row = lambda i: (i, 0)
    const2 = lambda i: (0, 0)
    return pl.pallas_call(
        functools.partial(_mix_ln1_kernel, tiles_per_seq),
        grid=(t // tm,),
        in_specs=[pl.BlockSpec((tm, SB_WIDTH), row),
                  pl.BlockSpec((tm, POOL_WIDTH), row),
                  pl.BlockSpec((HALO, POOL_WIDTH), lambda i: (jnp.maximum(i * halo_blocks_per_tile - 1, 0), 0)),
                  pl.BlockSpec(w_pool_bf16.shape, lambda i: (0, 0, 0)),
                  pl.BlockSpec((1, POOL_WIDTH), const2),
                  pl.BlockSpec(w_out_bf16.shape, const2),
                  pl.BlockSpec((tm, d), row),
                  pl.BlockSpec((1, N_MOD, d), lambda i: (i // tiles_per_seq, 0, 0)),
                  pl.BlockSpec((1, d), const2),
                  pl.BlockSpec((1, d), const2),
                  pl.BlockSpec((d, LANES), const2),
                  pl.BlockSpec((1, LANES), const2)],
        out_specs=[pl.BlockSpec((tm, d), row),
                   pl.BlockSpec((tm, d), row),
                   pl.BlockSpec((tm, LANES), row),
                   pl.BlockSpec((SUBLANES, LANES), const2)],
        out_shape=[jax.ShapeDtypeStruct((t, d), jnp.float32),
                   jax.ShapeDtypeStruct((t, d), jnp.float32),
                   jax.ShapeDtypeStruct((t, LANES), jnp.float32),
                   jax.ShapeDtypeStruct((SUBLANES, LANES), jnp.float32)],
        scratch_shapes=[pltpu.VMEM((HALO + tm, POOL_WIDTH), jnp.float32),
                        pltpu.VMEM((SUBLANES, LANES), jnp.float32)],
        compiler_params=pltpu.CompilerParams(vmem_limit_bytes=VMEM_LIMIT,
                                             dimension_semantics=("arbitrary",)),
        name="mix_ln1",
    )(o_sb, p, p, w_pool_bf16, pool_scale, w_out_bf16, x2d, mod, ln1_g, ln1_b, w_router, b_router)


def _moe_grouped_kernel(tile_expert_ref, n_tiles_ref, src_ref, src_next_ref, u_hbm, wg_ref, wu_ref, wd_ref,
                        y_ref, buf_ref, sem_ref):
    del tile_expert_ref
    i = pl.program_id(0)
    n_tiles = n_tiles_ref[0]
    tm = y_ref.shape[0]

    def row_copy(token, r, slot):
        return pltpu.make_async_copy(u_hbm.at[pl.ds(token, 1), :], buf_ref.at[slot, pl.ds(r, 1), :],
                                     sem_ref.at[slot])

    def start_gather(rows_ref, slot):
        def body(r, _):
            row_copy(rows_ref[0, 0, r], r, slot).start()
            return 0
        lax.fori_loop(0, tm, body, 0, unroll=GATHER_UNROLL)

    @pl.when(i == 0)
    def _():
        start_gather(src_ref, 0)

    @pl.when(i + 1 < n_tiles)
    def _():
        start_gather(src_next_ref, (i + 1) % 2)

    @pl.when(i < n_tiles)
    def _():
        slot = i % 2
        pltpu.make_async_copy(u_hbm.at[pl.ds(0, tm), :], buf_ref.at[slot], sem_ref.at[slot]).wait()
        u = buf_ref[slot].astype(jnp.bfloat16)
        gate = jnp.dot(u, wg_ref[0], preferred_element_type=jnp.float32)
        up = jnp.dot(u, wu_ref[0], preferred_element_type=jnp.float32)
        h = gate * jax.nn.sigmoid(gate) * up
        y_ref[...] = jnp.dot(h.astype(jnp.bfloat16), wd_ref[0], preferred_element_type=jnp.float32)

    @pl.when(i >= n_tiles)
    def _():
        y_ref[...] = jnp.zeros_like(y_ref)


def _moe_grouped(tile_expert, n_tiles, src, u2, wg, wu, wd):
    t, d = u2.shape
    tm = MOE_ROW_TILE
    max_tiles = src.shape[0]
    expert = lambda i, te, nt: (te[i], 0, 0)
    grid_spec = pltpu.PrefetchScalarGridSpec(
        num_scalar_prefetch=2,
        grid=(max_tiles,),
        in_specs=[pl.BlockSpec((1, 1, tm), lambda i, te, nt: (i, 0, 0), memory_space=pltpu.SMEM),
                  pl.BlockSpec((1, 1, tm), lambda i, te, nt: (jnp.minimum(i + 1, max_tiles - 1), 0, 0),
                               memory_space=pltpu.SMEM),
                  pl.BlockSpec(memory_space=pl.ANY),
                  pl.BlockSpec((1, d, EXPERT_HIDDEN), expert),
                  pl.BlockSpec((1, d, EXPERT_HIDDEN), expert),
                  pl.BlockSpec((1, EXPERT_HIDDEN, d), expert)],
        out_specs=pl.BlockSpec((tm, d), lambda i, te, nt: (i, 0)),
        scratch_shapes=[pltpu.VMEM((2, tm, d), jnp.float32),
                        pltpu.SemaphoreType.DMA((2,))],
    )
    return pl.pallas_call(
        _moe_grouped_kernel,
        grid_spec=grid_spec,
        out_shape=jax.ShapeDtypeStruct((max_tiles * tm, d), jnp.float32),
        compiler_params=pltpu.CompilerParams(vmem_limit_bytes=VMEM_LIMIT,
                                             dimension_semantics=("arbitrary",)),
        name="moe_grouped",
    )(tile_expert, n_tiles, src, src, u2, wg, wu, wd)


def _combine_ln2_kernel(pos_ref, pos_next_ref, y_hbm, route_ref, x1_ref, mod_ref, g2_ref, b2_ref, o_ref,
                        buf_ref, sem_ref):
    i = pl.program_id(0)
    n = pl.num_programs(0)
    tm = o_ref.shape[0]
    rows = 2 * tm

    def start_gather(rows_ref, slot):
        def body(r, _):
            pltpu.make_async_copy(y_hbm.at[pl.ds(rows_ref[0, 0, r], 1), :], buf_ref.at[slot, pl.ds(r, 1), :],
                                  sem_ref.at[slot]).start()
            return 0
        lax.fori_loop(0, rows, body, 0, unroll=GATHER_UNROLL)

    @pl.when(i == 0)
    def _():
        start_gather(pos_ref, 0)

    @pl.when(i + 1 < n)
    def _():
        start_gather(pos_next_ref, (i + 1) % 2)

    slot = i % 2
    pltpu.make_async_copy(y_hbm.at[pl.ds(0, rows), :], buf_ref.at[slot], sem_ref.at[slot]).wait()
    route = route_ref[...]
    w1 = route[:, ROUTE_W1:ROUTE_W1 + 1]
    w2 = route[:, ROUTE_W2:ROUTE_W2 + 1]
    y = w1 * buf_ref[slot, 0:tm, :] + w2 * buf_ref[slot, tm:, :]
    gate2 = mod_ref[0, 5:6, :]
    o_ref[...] = _ln(DEEPNORM_ALPHA * x1_ref[...] + (1.0 + gate2) * y) * g2_ref[...] + b2_ref[...]


def _combine_ln2(pos, y_sorted, route, x1, mod, ln2_g, ln2_b, seq):
    t, d = x1.shape
    tm = COMBINE_ROW_TILE
    n = t // tm
    tiles_per_seq = seq // tm
    row = lambda i: (i, 0)
    const2 = lambda i: (0, 0)
    return pl.pallas_call(
        _combine_ln2_kernel,
        grid=(n,),
        in_specs=[pl.BlockSpec((1, 1, 2 * tm), lambda i: (i, 0, 0), memory_space=pltpu.SMEM),
                  pl.BlockSpec((1, 1, 2 * tm), lambda i: (jnp.minimum(i + 1, n - 1), 0, 0),
                               memory_space=pltpu.SMEM),
                  pl.BlockSpec(memory_space=pl.ANY),
                  pl.BlockSpec((tm, LANES), row),
                  pl.BlockSpec((tm, d), row),
                  pl.BlockSpec((1, N_MOD, d), lambda i: (i // tiles_per_seq, 0, 0)),
                  pl.BlockSpec((1, d), const2),
                  pl.BlockSpec((1, d), const2)],
        out_specs=pl.BlockSpec((tm, d), row),
        out_shape=jax.ShapeDtypeStruct((t, d), jnp.float32),
        scratch_shapes=[pltpu.VMEM((2, 2 * tm, d), jnp.float32),
                        pltpu.SemaphoreType.DMA((2,))],
        compiler_params=pltpu.CompilerParams(vmem_limit_bytes=VMEM_LIMIT,
                                             dimension_semantics=("arbitrary",)),
        name="combine_ln2",
    )(pos, pos, y_sorted, route, x1, mod, ln2_g, ln2_b)


def _dispatch_plan(route, counts):
    t = route.shape[0]
    tm = MOE_ROW_TILE
    max_tiles = 2 * t // tm + N_EXPERTS
    counts = counts[0, :N_EXPERTS].astype(jnp.int32)
    tiles = (counts + tm - 1) // tm
    tile_end = jnp.cumsum(tiles)
    row_start = (tile_end - tiles) * tm
    experts = route[:, ROUTE_E1:ROUTE_E2 + 1].astype(jnp.int32)
    ranks = route[:, ROUTE_RANK1:ROUTE_RANK2 + 1].astype(jnp.int32)
    pos = row_start[experts] + ranks
    token = jnp.broadcast_to(jnp.arange(t, dtype=jnp.int32)[:, None], (t, 2))
    src = jnp.zeros((max_tiles * tm,), jnp.int32).at[pos.reshape(-1)].set(token.reshape(-1), unique_indices=True)
    tile_ids = jnp.arange(max_tiles, dtype=jnp.int32)
    tile_expert = jnp.minimum(jnp.sum(tile_end[None, :] <= tile_ids[:, None], axis=1), N_EXPERTS - 1)
    tc = COMBINE_ROW_TILE
    pos_tiles = pos.reshape(t // tc, tc, 2).transpose(0, 2, 1).reshape(t // tc, 1, 2 * tc)
    return (tile_expert.astype(jnp.int32), tile_end[-1:].astype(jnp.int32), src.reshape(max_tiles, 1, tm),
            pos_tiles)


def kernel(x, c, w_ada, b_ada, w_in, w_pool, pool_scale, w_out, ln1_g, ln1_b, w_router_group, b_router_group,
           w_router_expert, b_router_expert, w_gate, w_up, w_down, ln2_g, ln2_b):
    b, s, d = x.shape
    bf16 = jnp.bfloat16
    for layer in range(DEPTH):
        c_pad = jnp.pad(c, ((0, SUBLANES - b), (0, 0)))
        mod = _adaln(c_pad, w_ada[layer], b_ada[layer][None, :])[:b].reshape(b, N_MOD, d)
        x2d = x.reshape(b * s, d)
        qkv, p = _ln_inproj(x2d, mod, w_in[layer].astype(bf16), s)
        o_sb = _sb_attn(qkv, b, s).reshape(b * s, SB_WIDTH)
        pad = LANES - N_EXPERTS - N_EXPERT_GROUPS
        w_router = jnp.pad(jnp.concatenate([w_router_expert[layer], w_router_group[layer]], axis=1),
                           ((0, 0), (0, pad)))
        b_router = jnp.pad(jnp.concatenate([b_router_expert[layer], b_router_group[layer]]), (0, pad))[None, :]
        x1, u2, route, counts = _mix_ln1(o_sb, p, w_pool[layer].astype(bf16), pool_scale[layer][None, :],
                                         w_out[layer].astype(bf16), x2d, mod, ln1_g[layer][None, :],
                                         ln1_b[layer][None, :], w_router, b_router, s)
        tile_expert, n_tiles, src, pos = _dispatch_plan(route, counts)
        y_sorted = _moe_grouped(tile_expert, n_tiles, src, u2, w_gate[layer].astype(bf16),
                                w_up[layer].astype(bf16), w_down[layer].astype(bf16))
        x2 = _combine_ln2(pos, y_sorted, route, x1, mod, ln2_g[layer][None, :], ln2_b[layer][None, :], s)
        x = x2.reshape(b, s, d)
    return x
```

```python
import functools

import jax
import jax.numpy as jnp
from jax import lax
from jax.experimental import pallas as pl
from jax.experimental.pallas import tpu as pltpu

D_MODEL = 1024
N_SB_HEADS = 8
SB_HEAD_DIM = 64
SB_WIDTH = N_SB_HEADS * SB_HEAD_DIM
POOL_WINDOWS = (2, 4, 8, 16)
POOL_GROUP_DIM = 128
POOL_WIDTH = len(POOL_WINDOWS) * POOL_GROUP_DIM
N_EXPERT_GROUPS = 4
EXPERTS_PER_GROUP = 4
N_EXPERTS = N_EXPERT_GROUPS * EXPERTS_PER_GROUP
EXPERT_HIDDEN = 512
DEPTH = 1
DEEPNORM_ALPHA = (2.0 * DEPTH) ** 0.25
LN_EPS = 1e-5
N_MOD = 6

LANES = 128
SUBLANES = 8
HALO = max(POOL_WINDOWS)
VMEM_LIMIT = 56 * 1024 * 1024

ROW_TILE = 512
ATTN_TILE = 256
MOE_ROW_TILE = 512
COMBINE_ROW_TILE = 256
GATHER_UNROLL = 8


def _ln(x):
    mu = jnp.mean(x, axis=-1, keepdims=True)
    xc = x - mu
    var = jnp.mean(xc * xc, axis=-1, keepdims=True)
    return xc * lax.rsqrt(var + LN_EPS)


def _adaln_kernel(c_ref, w_ref, b_ref, o_ref):
    c = c_ref[...]
    a = c * jax.nn.sigmoid(c)
    o_ref[...] = jnp.dot(a, w_ref[...], preferred_element_type=jnp.float32,
                         precision=lax.Precision.HIGHEST) + b_ref[...]


def _adaln(c_pad, w_ada, b_ada):
    rows, d = c_pad.shape
    n = w_ada.shape[1]
    tn = 1024
    return pl.pallas_call(
        _adaln_kernel,
        grid=(n // tn,),
        in_specs=[pl.BlockSpec((rows, d), lambda j: (0, 0)),
                  pl.BlockSpec((d, tn), lambda j: (0, j)),
                  pl.BlockSpec((1, tn), lambda j: (0, j))],
        out_specs=pl.BlockSpec((rows, tn), lambda j: (0, j)),
        out_shape=jax.ShapeDtypeStruct((rows, n), jnp.float32),
        compiler_params=pltpu.CompilerParams(vmem_limit_bytes=VMEM_LIMIT),
        name="adaln",
    )(c_pad, w_ada, b_ada)


def _ln_inproj_kernel(x_ref, mod_ref, w_ref, qkv_ref, p_ref):
    shift = mod_ref[0, 0:1, :]
    scale = mod_ref[0, 1:2, :]
    u = (_ln(x_ref[...]) * (1.0 + scale) + shift).astype(jnp.bfloat16)
    qk_scale = SB_HEAD_DIM ** -0.5 * 1.4426950408889634
    q = jnp.dot(u, w_ref[:, 0:SB_WIDTH], preferred_element_type=jnp.float32) * qk_scale
    qkv_ref[:, 0:SB_WIDTH] = q.astype(jnp.bfloat16)
    for j in (1, 2):
        kv = jnp.dot(u, w_ref[:, j * SB_WIDTH:(j + 1) * SB_WIDTH], preferred_element_type=jnp.float32)
        qkv_ref[:, j * SB_WIDTH:(j + 1) * SB_WIDTH] = kv.astype(jnp.bfloat16)
    p_ref[...] = jnp.dot(u, w_ref[:, 3 * SB_WIDTH:], preferred_element_type=jnp.float32)


def _ln_inproj(x2d, mod, w_in_bf16, seq):
    t, d = x2d.shape
    tm = ROW_TILE
    tiles_per_seq = seq // tm
    return pl.pallas_call(
        _ln_inproj_kernel,
        grid=(t // tm,),
        in_specs=[pl.BlockSpec((tm, d), lambda i: (i, 0)),
                  pl.BlockSpec((1, N_MOD, d), lambda i: (i // tiles_per_seq, 0, 0)),
                  pl.BlockSpec(w_in_bf16.shape, lambda i: (0, 0))],
        out_specs=[pl.BlockSpec((tm, 3 * SB_WIDTH), lambda i: (i, 0)),
                   pl.BlockSpec((tm, POOL_WIDTH), lambda i: (i, 0))],
        out_shape=[jax.ShapeDtypeStruct((t, 3 * SB_WIDTH), jnp.bfloat16),
                   jax.ShapeDtypeStruct((t, POOL_WIDTH), jnp.float32)],
        compiler_params=pltpu.CompilerParams(vmem_limit_bytes=VMEM_LIMIT),
        name="ln_inproj",
    )(x2d, mod, w_in_bf16)


def _sb_attn_kernel(q_ref, k_ref, v_ref, o_ref, z_ref, zkeep_ref, split_ref, att_ref, acc_ref, carry_ref):
    tq = tk = ATTN_TILE
    heads = (0, 1)
    qi = pl.program_id(2)
    q2 = q_ref[0]
    lane = lax.broadcasted_iota(jnp.int32, (tq, LANES), 1)
    first_head = lane < SB_HEAD_DIM
    zero = jnp.zeros_like(q2)
    q_heads = (jnp.where(first_head, q2, zero), jnp.where(first_head, zero, q2))

    row = lax.broadcasted_iota(jnp.int32, (tk, tk), 0)
    col = lax.broadcasted_iota(jnp.int32, (tk, tk), 1)
    neg_suffix = jnp.where(row >= col, -1.0, 0.0).astype(jnp.bfloat16)
    neg_suffix2 = jnp.concatenate([neg_suffix, neg_suffix], axis=0)
    causal = col < row

    def key_start(j):
        return pl.multiple_of(jnp.maximum(qi - j, 0) * tk, tk)

    def stage_a(j, par):
        k_blk = k_ref[0, pl.ds(key_start(j), tk), :]
        for h in heads:
            z_ref[par, h] = lax.dot_general(q_heads[h], k_blk, (((1,), (1,)), ((), ())),
                                            preferred_element_type=jnp.float32)

    def stage_b(par, diagonal):
        for h in heads:
            z = z_ref[par, h]
            n = jnp.maximum(z, 0.0) + jnp.log2(1.0 + jnp.exp2(-jnp.abs(z)))
            if diagonal:
                n = jnp.where(causal, n, 0.0)
            hi = n.astype(jnp.bfloat16)
            lo = (n - hi.astype(jnp.float32)).astype(jnp.bfloat16)
            split_ref[par, h, :, 0:tk] = hi
            split_ref[par, h, :, tk:] = lo
            zkeep_ref[par, h] = z

    def stage_c(par, diagonal):
        incls = [jnp.dot(split_ref[par, h], neg_suffix2, preferred_element_type=jnp.float32)
                 for h in heads]
        for h in heads:
            carry = carry_ref[h]
            att = jnp.exp2(zkeep_ref[par, h] + incls[h] + jnp.concatenate([carry] * (tk // LANES), axis=1))
            if diagonal:
                att = jnp.where(causal, att, 0.0)
            att_ref[par, h] = att.astype(jnp.bfloat16)
            carry_ref[h] = carry + jnp.broadcast_to(incls[h][:, 0:1], (tq, LANES))

    def stage_d(j, par):
        v_blk = v_ref[0, pl.ds(key_start(j), tk), :]
        for h in heads:
            acc_ref[h] += jnp.dot(att_ref[par, h], v_blk, preferred_element_type=jnp.float32)

    def step(i, par):
        stage_b(1 - par, diagonal=False)
        stage_c(par, diagonal=False)
        stage_a(i + 2, par)
        stage_d(i - 1, 1 - par)

    carry_ref[...] = jnp.zeros_like(carry_ref)
    acc_ref[...] = jnp.zeros_like(acc_ref)
    stage_a(0, 0)
    stage_b(0, diagonal=True)
    stage_a(1, 1)
    stage_c(0, diagonal=True)
    stage_a(2, 0)
    stage_b(1, diagonal=False)

    def two_steps(m, _):
        step(2 * m + 1, 1)
        step(2 * m + 2, 0)
        return 0

    lax.fori_loop(0, qi // 2, two_steps, 0)

    @pl.when(qi % 2 == 1)
    def _():
        step(qi, 1)
        stage_d(qi, 1)

    @pl.when(qi % 2 == 0)
    def _():
        stage_d(qi, 0)

    o_ref[0] = jnp.where(first_head, acc_ref[0], acc_ref[1]).astype(o_ref.dtype)


def _sb_attn(qkv, batch, seq):
    qkv3 = qkv.reshape(batch, seq, 3 * SB_WIDTH)
    tq = ATTN_TILE
    pairs = SB_WIDTH // LANES
    return pl.pallas_call(
        _sb_attn_kernel,
        grid=(batch, pairs, seq // tq),
        in_specs=[pl.BlockSpec((1, tq, LANES), lambda b, h, i: (b, i, h)),
                  pl.BlockSpec((1, seq, LANES), lambda b, h, i: (b, 0, pairs + h)),
                  pl.BlockSpec((1, seq, LANES), lambda b, h, i: (b, 0, 2 * pairs + h))],
        out_specs=pl.BlockSpec((1, tq, LANES), lambda b, h, i: (b, i, h)),
        out_shape=jax.ShapeDtypeStruct((batch, seq, SB_WIDTH), jnp.bfloat16),
        scratch_shapes=[pltpu.VMEM((2, 2, tq, tq), jnp.float32),
                        pltpu.VMEM((2, 2, tq, tq), jnp.float32),
                        pltpu.VMEM((2, 2, tq, 2 * tq), jnp.bfloat16),
                        pltpu.VMEM((2, 2, tq, tq), jnp.bfloat16),
                        pltpu.VMEM((2, tq, LANES), jnp.float32),
                        pltpu.VMEM((2, tq, LANES), jnp.float32)],
        compiler_params=pltpu.CompilerParams(vmem_limit_bytes=VMEM_LIMIT),
        name="sb_attn",
    )(qkv3, qkv3, qkv3)


def _route(logits):
    lane = lax.broadcasted_iota(jnp.int32, logits.shape, 1)
    neg = jnp.float32(-jnp.inf)
    big = jnp.int32(LANES)
    is_group = (lane >= N_EXPERTS) & (lane < N_EXPERTS + N_EXPERT_GROUPS)
    gl = jnp.where(is_group, logits, neg)
    g_max = jnp.max(gl, axis=-1, keepdims=True)
    g_sum = jnp.sum(jnp.exp(gl - g_max), axis=-1, keepdims=True)
    g_p = 1.0 / g_sum
    g_idx = jnp.min(jnp.where(gl == g_max, lane, big), axis=-1, keepdims=True) - N_EXPERTS
    in_group = (lane >= g_idx * EXPERTS_PER_GROUP) & (lane < (g_idx + 1) * EXPERTS_PER_GROUP)
    el = jnp.where(in_group, logits, neg)
    e1 = jnp.max(el, axis=-1, keepdims=True)
    i1 = jnp.min(jnp.where(el == e1, lane, big), axis=-1, keepdims=True)
    el2 = jnp.where(lane == i1, neg, el)
    e2 = jnp.max(el2, axis=-1, keepdims=True)
    i2 = jnp.min(jnp.where(el2 == e2, lane, big), axis=-1, keepdims=True)
    r = jnp.exp(e2 - e1)
    w1 = g_p / (1.0 + r)
    w2 = w1 * r
    return i1, i2, w1, w2


ROUTE_W1, ROUTE_W2, ROUTE_E1, ROUTE_E2, ROUTE_RANK1, ROUTE_RANK2 = range(6)


def _mix_ln1_kernel(tiles_per_seq, osb_ref, p_ref, halo_ref, wpool_ref, pscale_ref, wout_ref, x_ref, mod_ref,
                    g1_ref, b1_ref, wr_ref, br_ref, x1_ref, u2_ref, route_ref, count_ref, pext_ref, running_ref):
    tm = p_ref.shape[0]
    tile_in_seq = pl.program_id(0) % tiles_per_seq
    p = p_ref[...]
    pext_ref[0:HALO, :] = jnp.where(tile_in_seq == 0, 0.0, halo_ref[...])
    pext_ref[HALO:, :] = p
    pos = tile_in_seq * tm + lax.broadcasted_iota(jnp.int32, (tm, 1), 0)

    mixed = jnp.dot(osb_ref[...], wout_ref[0:SB_WIDTH, :], preferred_element_type=jnp.float32)
    for g, w in enumerate(POOL_WINDOWS):
        cols = slice(g * POOL_GROUP_DIM, (g + 1) * POOL_GROUP_DIM)
        win = p[:, cols]
        for i in range(1, w):
            win = win + pext_ref[HALO - i:HALO - i + tm, cols]
        count = jnp.minimum(pos + 1, w).astype(jnp.float32)
        pooled = win / count - p[:, cols]
        o_pool = jnp.dot(pooled.astype(jnp.bfloat16), wpool_ref[g], preferred_element_type=jnp.float32)
        o_pool = o_pool * pscale_ref[:, cols]
        mixed = mixed + jnp.dot(o_pool.astype(jnp.bfloat16),
                                wout_ref[SB_WIDTH + g * POOL_GROUP_DIM:SB_WIDTH + (g + 1) * POOL_GROUP_DIM, :],
                                preferred_element_type=jnp.float32)

    gate1 = mod_ref[0, 2:3, :]
    shift2 = mod_ref[0, 3:4, :]
    scale2 = mod_ref[0, 4:5, :]
    x1 = _ln(DEEPNORM_ALPHA * x_ref[...] + (1.0 + gate1) * mixed) * g1_ref[...] + b1_ref[...]
    x1_ref[...] = x1
    u2 = _ln(x1) * (1.0 + scale2) + shift2
    u2_ref[...] = u2
    logits = jnp.dot(u2, wr_ref[...], preferred_element_type=jnp.float32,
                     precision=lax.Precision.HIGHEST) + br_ref[...]
    i1, i2, w1, w2 = _route(logits)

    @pl.when(pl.program_id(0) == 0)
    def _():
        running_ref[...] = jnp.zeros_like(running_ref)

    lane = lax.broadcasted_iota(jnp.int32, (tm, LANES), 1)
    first, second = lane == i1, lane == i2
    uses = jnp.where(first | second, 1.0, 0.0)
    earlier = (lax.broadcasted_iota(jnp.int32, (tm, tm), 1)
               < lax.broadcasted_iota(jnp.int32, (tm, tm), 0)).astype(jnp.bfloat16)
    before = jnp.dot(earlier, uses.astype(jnp.bfloat16), preferred_element_type=jnp.float32)
    before = before + running_ref[0:1, :]
    rank1 = jnp.sum(jnp.where(first, before, 0.0), axis=-1, keepdims=True)
    rank2 = jnp.sum(jnp.where(second, before, 0.0), axis=-1, keepdims=True)
    running = running_ref[0:1, :] + jnp.sum(uses, axis=0, keepdims=True)
    running_ref[...] = jnp.broadcast_to(running, running_ref.shape)
    count_ref[...] = jnp.broadcast_to(running, count_ref.shape)

    record = jnp.zeros((tm, LANES), jnp.float32)
    for slot, value in ((ROUTE_W1, w1), (ROUTE_W2, w2), (ROUTE_E1, i1.astype(jnp.float32)),
                        (ROUTE_E2, i2.astype(jnp.float32)), (ROUTE_RANK1, rank1), (ROUTE_RANK2, rank2)):
        record = jnp.where(lane == slot, value, record)
    route_ref[...] = record


def _mix_ln1(o_sb, p, w_pool_bf16, pool_scale, w_out_bf16, x2d, mod, ln1_g, ln1_b, w_router, b_router, seq):
    t, d = x2d.shape
    tm = ROW_TILE
    tiles_per_seq = seq // tm
    halo_blocks_per_tile = tm // HALO
    row = lambda i: (i, 0)
    const2 = lambda i: (0, 0)
    return pl.pallas_call(
        functools.partial(_mix_ln1_kernel, tiles_per_seq),
        grid=(t // tm,),
        in_specs=[pl.BlockSpec((tm, SB_WIDTH), row),
                  pl.BlockSpec((tm, POOL_WIDTH), row),
                  pl.BlockSpec((HALO, POOL_WIDTH), lambda i: (jnp.maximum(i * halo_blocks_per_tile - 1, 0), 0)),
                  pl.BlockSpec(w_pool_bf16.shape, lambda i: (0, 0, 0)),
                  pl.BlockSpec((1, POOL_WIDTH), const2),
                  pl.BlockSpec(w_out_bf16.shape, const2),
                  pl.BlockSpec((tm, d), row),
                  pl.BlockSpec((1, N_MOD, d), lambda i: (i // tiles_per_seq, 0, 0)),
                  pl.BlockSpec((1, d), const2),
                  pl.BlockSpec((1, d), const2),
                  pl.BlockSpec((d, LANES), const2),
                  pl.BlockSpec((1, LANES), const2)],
        out_specs=[pl.BlockSpec((tm, d), row),
                   pl.BlockSpec((tm, d), row),
                   pl.BlockSpec((tm, LANES), row),
                   pl.BlockSpec((SUBLANES, LANES), const2)],
        out_shape=[jax.ShapeDtypeStruct((t, d), jnp.float32),
                   jax.ShapeDtypeStruct((t, d), jnp.float32),
                   jax.ShapeDtypeStruct((t, LANES), jnp.float32),
                   jax.ShapeDtypeStruct((SUBLANES, LANES), jnp.float32)],
        scratch_shapes=[pltpu.VMEM((HALO + tm, POOL_WIDTH), jnp.float32),
                        pltpu.VMEM((SUBLANES, LANES), jnp.float32)],
        compiler_params=pltpu.CompilerParams(vmem_limit_bytes=VMEM_LIMIT,
                                             dimension_semantics=("arbitrary",)),
        name="mix_ln1",
    )(o_sb, p, p, w_pool_bf16, pool_scale, w_out_bf16, x2d, mod, ln1_g, ln1_b, w_router, b_router)


def _dispatch_kernel(pos_ref, u_hbm, init_hbm, sorted_hbm, sem):
    del init_hbm
    i = pl.program_id(0)
    tm = pos_ref.shape[2] // 2

    def wait_step():
        pltpu.make_async_copy(u_hbm.at[pl.ds(0, 2 * tm), :], sorted_hbm.at[pl.ds(0, 2 * tm), :], sem).wait()

    @pl.when(i > 0)
    def _():
        wait_step()

    def body(r, _):
        row = u_hbm.at[pl.ds(i * tm + r, 1), :]
        pltpu.make_async_copy(row, sorted_hbm.at[pl.ds(pos_ref[0, 0, r], 1), :], sem).start()
        pltpu.make_async_copy(row, sorted_hbm.at[pl.ds(pos_ref[0, 0, tm + r], 1), :], sem).start()
        return 0

    lax.fori_loop(0, tm, body, 0, unroll=GATHER_UNROLL // 2)

    @pl.when(i == pl.num_programs(0) - 1)
    def _():
        wait_step()


def _dispatch(pos, u2, n_rows):
    t, d = u2.shape
    tm = pos.shape[2] // 2
    return pl.pallas_call(
        _dispatch_kernel,
        grid=(t // tm,),
        in_specs=[pl.BlockSpec((1, 1, 2 * tm), lambda i: (i, 0, 0), memory_space=pltpu.SMEM),
                  pl.BlockSpec(memory_space=pl.ANY),
                  pl.BlockSpec(memory_space=pl.ANY)],
        out_specs=pl.BlockSpec(memory_space=pl.ANY),
        out_shape=jax.ShapeDtypeStruct((n_rows, d), jnp.float32),
        scratch_shapes=[pltpu.SemaphoreType.DMA(())],
        input_output_aliases={2: 0},
        compiler_params=pltpu.CompilerParams(dimension_semantics=("arbitrary",)),
        name="dispatch",
    )(pos, u2, jnp.zeros((n_rows, d), jnp.float32))


def _moe_grouped_kernel(tile_expert_ref, n_tiles_ref, u_ref, wg_ref, wu_ref, wd_ref, y_ref):
    del tile_expert_ref
    i = pl.program_id(0)

    @pl.when(i < n_tiles_ref[0])
    def _():
        u = u_ref[...].astype(jnp.bfloat16)
        gate = jnp.dot(u, wg_ref[0], preferred_element_type=jnp.float32)
        up = jnp.dot(u, wu_ref[0], preferred_element_type=jnp.float32)
        h = gate * jax.nn.sigmoid(gate) * up
        y_ref[...] = jnp.dot(h.astype(jnp.bfloat16), wd_ref[0], preferred_element_type=jnp.float32)

    @pl.when(i >= n_tiles_ref[0])
    def _():
        y_ref[...] = jnp.zeros_like(y_ref)


def _moe_grouped(tile_expert, n_tiles, u_sorted, wg, wu, wd):
    n_rows, d = u_sorted.shape
    tm = MOE_ROW_TILE
    expert = lambda i, te, nt: (te[i], 0, 0)
    row = lambda i, te, nt: (i, 0)
    grid_spec = pltpu.PrefetchScalarGridSpec(
        num_scalar_prefetch=2,
        grid=(n_rows // tm,),
        in_specs=[pl.BlockSpec((tm, d), row),
                  pl.BlockSpec((1, d, EXPERT_HIDDEN), expert),
                  pl.BlockSpec((1, d, EXPERT_HIDDEN), expert),
                  pl.BlockSpec((1, EXPERT_HIDDEN, d), expert)],
        out_specs=pl.BlockSpec((tm, d), row),
    )
    return pl.pallas_call(
        _moe_grouped_kernel,
        grid_spec=grid_spec,
        out_shape=jax.ShapeDtypeStruct((n_rows, d), jnp.float32),
        compiler_params=pltpu.CompilerParams(vmem_limit_bytes=VMEM_LIMIT,
                                             dimension_semantics=("arbitrary",)),
        name="moe_grouped",
    )(tile_expert, n_tiles, u_sorted, wg, wu, wd)


def _combine_ln2_kernel(pos_ref, pos_next_ref, y_hbm, route_ref, x1_ref, mod_ref, g2_ref, b2_ref, o_ref,
                        buf_ref, sem_ref):
    i = pl.program_id(0)
    n = pl.num_programs(0)
    tm = o_ref.shape[0]
    rows = 2 * tm

    def start_gather(rows_ref, slot):
        def body(r, _):
            pltpu.make_async_copy(y_hbm.at[pl.ds(rows_ref[0, 0, r], 1), :], buf_ref.at[slot, pl.ds(r, 1), :],
                                  sem_ref.at[slot]).start()
            return 0
        lax.fori_loop(0, rows, body, 0, unroll=GATHER_UNROLL)

    @pl.when(i == 0)
    def _():
        start_gather(pos_ref, 0)

    @pl.when(i + 1 < n)
    def _():
        start_gather(pos_next_ref, (i + 1) % 2)

    slot = i % 2
    pltpu.make_async_copy(y_hbm.at[pl.ds(0, rows), :], buf_ref.at[slot], sem_ref.at[slot]).wait()
    route = route_ref[...]
    w1 = route[:, ROUTE_W1:ROUTE_W1 + 1]
    w2 = route[:, ROUTE_W2:ROUTE_W2 + 1]
    y = w1 * buf_ref[slot, 0:tm, :] + w2 * buf_ref[slot, tm:, :]
    gate2 = mod_ref[0, 5:6, :]
    o_ref[...] = _ln(DEEPNORM_ALPHA * x1_ref[...] + (1.0 + gate2) * y) * g2_ref[...] + b2_ref[...]


def _combine_ln2(pos, y_sorted, route, x1, mod, ln2_g, ln2_b, seq):
    t, d = x1.shape
    tm = COMBINE_ROW_TILE
    n = t // tm
    tiles_per_seq = seq // tm
    row = lambda i: (i, 0)
    const2 = lambda i: (0, 0)
    return pl.pallas_call(
        _combine_ln2_kernel,
        grid=(n,),
        in_specs=[pl.BlockSpec((1, 1, 2 * tm), lambda i: (i, 0, 0), memory_space=pltpu.SMEM),
                  pl.BlockSpec((1, 1, 2 * tm), lambda i: (jnp.minimum(i + 1, n - 1), 0, 0),
                               memory_space=pltpu.SMEM),
                  pl.BlockSpec(memory_space=pl.ANY),
                  pl.BlockSpec((tm, LANES), row),
                  pl.BlockSpec((tm, d), row),
                  pl.BlockSpec((1, N_MOD, d), lambda i: (i // tiles_per_seq, 0, 0)),
                  pl.BlockSpec((1, d), const2),
                  pl.BlockSpec((1, d), const2)],
        out_specs=pl.BlockSpec((tm, d), row),
        out_shape=jax.ShapeDtypeStruct((t, d), jnp.float32),
        scratch_shapes=[pltpu.VMEM((2, 2 * tm, d), jnp.float32),
                        pltpu.SemaphoreType.DMA((2,))],
        compiler_params=pltpu.CompilerParams(vmem_limit_bytes=VMEM_LIMIT,
                                             dimension_semantics=("arbitrary",)),
        name="combine_ln2",
    )(pos, pos, y_sorted, route, x1, mod, ln2_g, ln2_b)


def _dispatch_plan(route, counts):
    t = route.shape[0]
    tm = MOE_ROW_TILE
    max_tiles = 2 * t // tm + N_EXPERTS
    counts = counts[0, :N_EXPERTS].astype(jnp.int32)
    tiles = (counts + tm - 1) // tm
    tile_end = jnp.cumsum(tiles)
    row_start = (tile_end - tiles) * tm
    experts = route[:, ROUTE_E1:ROUTE_E2 + 1].astype(jnp.int32)
    ranks = route[:, ROUTE_RANK1:ROUTE_RANK2 + 1].astype(jnp.int32)
    pos = row_start[experts] + ranks
    tile_ids = jnp.arange(max_tiles, dtype=jnp.int32)
    tile_expert = jnp.minimum(jnp.sum(tile_end[None, :] <= tile_ids[:, None], axis=1), N_EXPERTS - 1)
    tc = COMBINE_ROW_TILE
    pos_tiles = pos.reshape(t // tc, tc, 2).transpose(0, 2, 1).reshape(t // tc, 1, 2 * tc)
    return tile_expert.astype(jnp.int32), tile_end[-1:].astype(jnp.int32), pos_tiles, max_tiles * tm


def kernel(x, c, w_ada, b_ada, w_in, w_pool, pool_scale, w_out, ln1_g, ln1_b, w_router_group, b_router_group,
           w_router_expert, b_router_expert, w_gate, w_up, w_down, ln2_g, ln2_b):
    b, s, d = x.shape
    bf16 = jnp.bfloat16
    for layer in range(DEPTH):
        c_pad = jnp.pad(c, ((0, SUBLANES - b), (0, 0)))
        mod = _adaln(c_pad, w_ada[layer], b_ada[layer][None, :])[:b].reshape(b, N_MOD, d)
        x2d = x.reshape(b * s, d)
        qkv, p = _ln_inproj(x2d, mod, w_in[layer].astype(bf16), s)
        o_sb = _sb_attn(qkv, b, s).reshape(b * s, SB_WIDTH)
        pad = LANES - N_EXPERTS - N_EXPERT_GROUPS
        w_router = jnp.pad(jnp.concatenate([w_router_expert[layer], w_router_group[layer]], axis=1),
                           ((0, 0), (0, pad)))
        b_router = jnp.pad(jnp.concatenate([b_router_expert[layer], b_router_group[layer]]), (0, pad))[None, :]
        x1, u2, route, counts = _mix_ln1(o_sb, p, w_pool[layer].astype(bf16), pool_scale[layer][None, :],
                                         w_out[layer].astype(bf16), x2d, mod, ln1_g[layer][None, :],
                                         ln1_b[layer][None, :], w_router, b_router, s)
        tile_expert, n_tiles, pos, n_rows = _dispatch_plan(route, counts)
        u_sorted = _dispatch(pos, u2, n_rows)
        y_sorted = _moe_grouped(tile_expert, n_tiles, u_sorted, w_gate[layer].astype(bf16),
                                w_up[layer].astype(bf16), w_down[layer].astype(bf16))
        x2 = _combine_ln2(pos, y_sorted, route, x1, mod, ln2_g[layer][None, :], ln2_b[layer][None, :], s)
        x = x2.reshape(b, s, d)
    return x
```

```python
import functools

import jax
import jax.numpy as jnp
from jax import lax
from jax.experimental import pallas as pl
from jax.experimental.pallas import tpu as pltpu

D_MODEL = 1024
N_SB_HEADS = 8
SB_HEAD_DIM = 64
SB_WIDTH = N_SB_HEADS * SB_HEAD_DIM
POOL_WINDOWS = (2, 4, 8, 16)
POOL_GROUP_DIM = 128
POOL_WIDTH = len(POOL_WINDOWS) * POOL_GROUP_DIM
N_EXPERT_GROUPS = 4
EXPERTS_PER_GROUP = 4
N_EXPERTS = N_EXPERT_GROUPS * EXPERTS_PER_GROUP
EXPERT_HIDDEN = 512
DEPTH = 1
DEEPNORM_ALPHA = (2.0 * DEPTH) ** 0.25
LN_EPS = 1e-5
N_MOD = 6

LANES = 128
SUBLANES = 8
HALO = max(POOL_WINDOWS)
VMEM_LIMIT = 56 * 1024 * 1024

ROW_TILE = 512
ATTN_TILE = 256
MOE_ROW_TILE = 512
DISPATCH_ROW_TILE = 1024
COMBINE_ROW_TILE = 256
GATHER_UNROLL = 8


def _ln(x):
    mu = jnp.mean(x, axis=-1, keepdims=True)
    xc = x - mu
    var = jnp.mean(xc * xc, axis=-1, keepdims=True)
    return xc * lax.rsqrt(var + LN_EPS)


def _adaln_kernel(c_ref, w_ref, b_ref, o_ref):
    c = c_ref[...]
    a = c * jax.nn.sigmoid(c)
    o_ref[...] = jnp.dot(a, w_ref[...], preferred_element_type=jnp.float32,
                         precision=lax.Precision.HIGHEST) + b_ref[...]


def _adaln(c_pad, w_ada, b_ada):
    rows, d = c_pad.shape
    n = w_ada.shape[1]
    tn = 1024
    return pl.pallas_call(
        _adaln_kernel,
        grid=(n // tn,),
        in_specs=[pl.BlockSpec((rows, d), lambda j: (0, 0)),
                  pl.BlockSpec((d, tn), lambda j: (0, j)),
                  pl.BlockSpec((1, tn), lambda j: (0, j))],
        out_specs=pl.BlockSpec((rows, tn), lambda j: (0, j)),
        out_shape=jax.ShapeDtypeStruct((rows, n), jnp.float32),
        compiler_params=pltpu.CompilerParams(vmem_limit_bytes=VMEM_LIMIT),
        name="adaln",
    )(c_pad, w_ada, b_ada)


def _ln_inproj_kernel(x_ref, mod_ref, w_ref, qkv_ref, p_ref):
    shift = mod_ref[0, 0:1, :]
    scale = mod_ref[0, 1:2, :]
    u = (_ln(x_ref[...]) * (1.0 + scale) + shift).astype(jnp.bfloat16)
    qk_scale = SB_HEAD_DIM ** -0.5 * 1.4426950408889634
    q = jnp.dot(u, w_ref[:, 0:SB_WIDTH], preferred_element_type=jnp.float32) * qk_scale
    qkv_ref[:, 0:SB_WIDTH] = q.astype(jnp.bfloat16)
    for j in (1, 2):
        kv = jnp.dot(u, w_ref[:, j * SB_WIDTH:(j + 1) * SB_WIDTH], preferred_element_type=jnp.float32)
        qkv_ref[:, j * SB_WIDTH:(j + 1) * SB_WIDTH] = kv.astype(jnp.bfloat16)
    p_ref[...] = jnp.dot(u, w_ref[:, 3 * SB_WIDTH:], preferred_element_type=jnp.float32)


def _ln_inproj(x2d, mod, w_in_bf16, seq):
    t, d = x2d.shape
    tm = ROW_TILE
    tiles_per_seq = seq // tm
    return pl.pallas_call(
        _ln_inproj_kernel,
        grid=(t // tm,),
        in_specs=[pl.BlockSpec((tm, d), lambda i: (i, 0)),
                  pl.BlockSpec((1, N_MOD, d), lambda i: (i // tiles_per_seq, 0, 0)),
                  pl.BlockSpec(w_in_bf16.shape, lambda i: (0, 0))],
        out_specs=[pl.BlockSpec((tm, 3 * SB_WIDTH), lambda i: (i, 0)),
                   pl.BlockSpec((tm, POOL_WIDTH), lambda i: (i, 0))],
        out_shape=[jax.ShapeDtypeStruct((t, 3 * SB_WIDTH), jnp.bfloat16),
                   jax.ShapeDtypeStruct((t, POOL_WIDTH), jnp.float32)],
        compiler_params=pltpu.CompilerParams(vmem_limit_bytes=VMEM_LIMIT),
        name="ln_inproj",
    )(x2d, mod, w_in_bf16)


def _sb_attn_kernel(q_ref, k_ref, v_ref, o_ref, z_ref, zkeep_ref, split_ref, att_ref, acc_ref, carry_ref):
    tq = tk = ATTN_TILE
    heads = (0, 1)
    qi = pl.program_id(2)
    q2 = q_ref[0]
    lane = lax.broadcasted_iota(jnp.int32, (tq, LANES), 1)
    first_head = lane < SB_HEAD_DIM
    zero = jnp.zeros_like(q2)
    q_heads = (jnp.where(first_head, q2, zero), jnp.where(first_head, zero, q2))

    row = lax.broadcasted_iota(jnp.int32, (tk, tk), 0)
    col = lax.broadcasted_iota(jnp.int32, (tk, tk), 1)
    neg_suffix = jnp.where(row >= col, -1.0, 0.0).astype(jnp.bfloat16)
    neg_suffix2 = jnp.concatenate([neg_suffix, neg_suffix], axis=0)
    causal = col < row

    def key_start(j):
        return pl.multiple_of(jnp.maximum(qi - j, 0) * tk, tk)

    def stage_a(j, par):
        k_blk = k_ref[0, pl.ds(key_start(j), tk), :]
        for h in heads:
            z_ref[par, h] = lax.dot_general(q_heads[h], k_blk, (((1,), (1,)), ((), ())),
                                            preferred_element_type=jnp.float32)

    def stage_b(par, diagonal):
        for h in heads:
            z = z_ref[par, h]
            n = jnp.maximum(z, 0.0) + jnp.log2(1.0 + jnp.exp2(-jnp.abs(z)))
            if diagonal:
                n = jnp.where(causal, n, 0.0)
            hi = n.astype(jnp.bfloat16)
            lo = (n - hi.astype(jnp.float32)).astype(jnp.bfloat16)
            split_ref[par, h, :, 0:tk] = hi
            split_ref[par, h, :, tk:] = lo
            zkeep_ref[par, h] = z

    def stage_c(par, diagonal):
        incls = [jnp.dot(split_ref[par, h], neg_suffix2, preferred_element_type=jnp.float32)
                 for h in heads]
        for h in heads:
            carry = carry_ref[h]
            att = jnp.exp2(zkeep_ref[par, h] + incls[h] + jnp.concatenate([carry] * (tk // LANES), axis=1))
            if diagonal:
                att = jnp.where(causal, att, 0.0)
            att_ref[par, h] = att.astype(jnp.bfloat16)
            carry_ref[h] = carry + jnp.broadcast_to(incls[h][:, 0:1], (tq, LANES))

    def stage_d(j, par):
        v_blk = v_ref[0, pl.ds(key_start(j), tk), :]
        for h in heads:
            acc_ref[h] += jnp.dot(att_ref[par, h], v_blk, preferred_element_type=jnp.float32)

    def step(i, par):
        stage_b(1 - par, diagonal=False)
        stage_c(par, diagonal=False)
        stage_a(i + 2, par)
        stage_d(i - 1, 1 - par)

    carry_ref[...] = jnp.zeros_like(carry_ref)
    acc_ref[...] = jnp.zeros_like(acc_ref)
    stage_a(0, 0)
    stage_b(0, diagonal=True)
    stage_a(1, 1)
    stage_c(0, diagonal=True)
    stage_a(2, 0)
    stage_b(1, diagonal=False)

    def two_steps(m, _):
        step(2 * m + 1, 1)
        step(2 * m + 2, 0)
        return 0

    lax.fori_loop(0, qi // 2, two_steps, 0)

    @pl.when(qi % 2 == 1)
    def _():
        step(qi, 1)
        stage_d(qi, 1)

    @pl.when(qi % 2 == 0)
    def _():
        stage_d(qi, 0)

    o_ref[0] = jnp.where(first_head, acc_ref[0], acc_ref[1]).astype(o_ref.dtype)


def _sb_attn(qkv, batch, seq):
    qkv3 = qkv.reshape(batch, seq, 3 * SB_WIDTH)
    tq = ATTN_TILE
    pairs = SB_WIDTH // LANES
    return pl.pallas_call(
        _sb_attn_kernel,
        grid=(batch, pairs, seq // tq),
        in_specs=[pl.BlockSpec((1, tq, LANES), lambda b, h, i: (b, i, h)),
                  pl.BlockSpec((1, seq, LANES), lambda b, h, i: (b, 0, pairs + h)),
                  pl.BlockSpec((1, seq, LANES), lambda b, h, i: (b, 0, 2 * pairs + h))],
        out_specs=pl.BlockSpec((1, tq, LANES), lambda b, h, i: (b, i, h)),
        out_shape=jax.ShapeDtypeStruct((batch, seq, SB_WIDTH), jnp.bfloat16),
        scratch_shapes=[pltpu.VMEM((2, 2, tq, tq), jnp.float32),
                        pltpu.VMEM((2, 2, tq, tq), jnp.float32),
                        pltpu.VMEM((2, 2, tq, 2 * tq), jnp.bfloat16),
                        pltpu.VMEM((2, 2, tq, tq), jnp.bfloat16),
                        pltpu.VMEM((2, tq, LANES), jnp.float32),
                        pltpu.VMEM((2, tq, LANES), jnp.float32)],
        compiler_params=pltpu.CompilerParams(vmem_limit_bytes=VMEM_LIMIT),
        name="sb_attn",
    )(qkv3, qkv3, qkv3)


def _route(logits):
    lane = lax.broadcasted_iota(jnp.int32, logits.shape, 1)
    neg = jnp.float32(-jnp.inf)
    big = jnp.int32(LANES)
    is_group = (lane >= N_EXPERTS) & (lane < N_EXPERTS + N_EXPERT_GROUPS)
    gl = jnp.where(is_group, logits, neg)
    g_max = jnp.max(gl, axis=-1, keepdims=True)
    g_sum = jnp.sum(jnp.exp(gl - g_max), axis=-1, keepdims=True)
    g_p = 1.0 / g_sum
    g_idx = jnp.min(jnp.where(gl == g_max, lane, big), axis=-1, keepdims=True) - N_EXPERTS
    in_group = (lane >= g_idx * EXPERTS_PER_GROUP) & (lane < (g_idx + 1) * EXPERTS_PER_GROUP)
    el = jnp.where(in_group, logits, neg)
    e1 = jnp.max(el, axis=-1, keepdims=True)
    i1 = jnp.min(jnp.where(el == e1, lane, big), axis=-1, keepdims=True)
    el2 = jnp.where(lane == i1, neg, el)
    e2 = jnp.max(el2, axis=-1, keepdims=True)
    i2 = jnp.min(jnp.where(el2 == e2, lane, big), axis=-1, keepdims=True)
    r = jnp.exp(e2 - e1)
    w1 = g_p / (1.0 + r)
    w2 = w1 * r
    return i1, i2, w1, w2


ROUTE_W1, ROUTE_W2, ROUTE_E1, ROUTE_E2, ROUTE_RANK1, ROUTE_RANK2 = range(6)


def _mix_ln1_kernel(tiles_per_seq, osb_ref, p_ref, halo_ref, wpool_ref, pscale_ref, wout_ref, x_ref, mod_ref,
                    g1_ref, b1_ref, wr_ref, br_ref, x1_ref, u2_ref, route_ref, count_ref, pext_ref, running_ref):
    tm = p_ref.shape[0]
    tile_in_seq = pl.program_id(0) % tiles_per_seq
    p = p_ref[...]
    pext_ref[0:HALO, :] = jnp.where(tile_in_seq == 0, 0.0, halo_ref[...])
    pext_ref[HALO:, :] = p
    pos = tile_in_seq * tm + lax.broadcasted_iota(jnp.int32, (tm, 1), 0)

    mixed = jnp.dot(osb_ref[...], wout_ref[0:SB_WIDTH, :], preferred_element_type=jnp.float32)
    for g, w in enumerate(POOL_WINDOWS):
        cols = slice(g * POOL_GROUP_DIM, (g + 1) * POOL_GROUP_DIM)
        win = p[:, cols]
        for i in range(1, w):
            win = win + pext_ref[HALO - i:HALO - i + tm, cols]
        count = jnp.minimum(pos + 1, w).astype(jnp.float32)
        pooled = win / count - p[:, cols]
        o_pool = jnp.dot(pooled.astype(jnp.bfloat16), wpool_ref[g], preferred_element_type=jnp.float32)
        o_pool = o_pool * pscale_ref[:, cols]
        mixed = mixed + jnp.dot(o_pool.astype(jnp.bfloat16),
                                wout_ref[SB_WIDTH + g * POOL_GROUP_DIM:SB_WIDTH + (g + 1) * POOL_GROUP_DIM, :],
                                preferred_element_type=jnp.float32)

    gate1 = mod_ref[0, 2:3, :]
    shift2 = mod_ref[0, 3:4, :]
    scale2 = mod_ref[0, 4:5, :]
    x1 = _ln(DEEPNORM_ALPHA * x_ref[...] + (1.0 + gate1) * mixed) * g1_ref[...] + b1_ref[...]
    x1_ref[...] = x1
    u2 = _ln(x1) * (1.0 + scale2) + shift2
    u2_ref[...] = u2
    logits = jnp.dot(u2, wr_ref[...], preferred_element_type=jnp.float32,
                     precision=lax.Precision.HIGHEST) + br_ref[...]
    i1, i2, w1, w2 = _route(logits)

    @pl.when(pl.program_id(0) == 0)
    def _():
        running_ref[...] = jnp.zeros_like(running_ref)

    lane = lax.broadcasted_iota(jnp.int32, (tm, LANES), 1)
    first, second = lane == i1, lane == i2
    uses = jnp.where(first | second, 1.0, 0.0)
    earlier = (lax.broadcasted_iota(jnp.int32, (tm, tm), 1)
               < lax.broadcasted_iota(jnp.int32, (tm, tm), 0)).astype(jnp.bfloat16)
    before = jnp.dot(earlier, uses.astype(jnp.bfloat16), preferred_element_type=jnp.float32)
    before = before + running_ref[0:1, :]
    rank1 = jnp.sum(jnp.where(first, before, 0.0), axis=-1, keepdims=True)
    rank2 = jnp.sum(jnp.where(second, before, 0.0), axis=-1, keepdims=True)
    running = running_ref[0:1, :] + jnp.sum(uses, axis=0, keepdims=True)
    running_ref[...] = jnp.broadcast_to(running, running_ref.shape)
    count_ref[...] = jnp.broadcast_to(running, count_ref.shape)

    record = jnp.zeros((tm, LANES), jnp.float32)
    for slot, value in ((ROUTE_W1, w1), (ROUTE_W2, w2), (ROUTE_E1, i1.astype(jnp.float32)),
                        (ROUTE_E2, i2.astype(jnp.float32)), (ROUTE_RANK1, rank1), (ROUTE_RANK2, rank2)):
        record = jnp.where(lane == slot, value, record)
    route_ref[...] = record


def _mix_ln1(o_sb, p, w_pool_bf16, pool_scale, w_out_bf16, x2d, mod, ln1_g, ln1_b, w_router, b_router, seq):
    t, d = x2d.shape
    tm = ROW_TILE
    tiles_per_seq = seq // tm
    halo_blocks_per_tile = tm // HALO
    row = lambda i: (i, 0)
    const2 = lambda i: (0, 0)
    return pl.pallas_call(
        functools.partial(_mix_ln1_kernel, tiles_per_seq),
        grid=(t // tm,),
        in_specs=[pl.BlockSpec((tm, SB_WIDTH), row),
                  pl.BlockSpec((tm, POOL_WIDTH), row),
                  pl.BlockSpec((HALO, POOL_WIDTH), lambda i: (jnp.maximum(i * halo_blocks_per_tile - 1, 0), 0)),
                  pl.BlockSpec(w_pool_bf16.shape, lambda i: (0, 0, 0)),
                  pl.BlockSpec((1, POOL_WIDTH), const2),
                  pl.BlockSpec(w_out_bf16.shape, const2),
                  pl.BlockSpec((tm, d), row),
                  pl.BlockSpec((1, N_MOD, d), lambda i: (i // tiles_per_seq, 0, 0)),
                  pl.BlockSpec((1, d), const2),
                  pl.BlockSpec((1, d), const2),
                  pl.BlockSpec((d, LANES), const2),
                  pl.BlockSpec((1, LANES), const2)],
        out_specs=[pl.BlockSpec((tm, d), row),
                   pl.BlockSpec((tm, d), row),
                   pl.BlockSpec((tm, LANES), row),
                   pl.BlockSpec((SUBLANES, LANES), const2)],
        out_shape=[jax.ShapeDtypeStruct((t, d), jnp.float32),
                   jax.ShapeDtypeStruct((t, d), jnp.float32),
                   jax.ShapeDtypeStruct((t, LANES), jnp.float32),
                   jax.ShapeDtypeStruct((SUBLANES, LANES), jnp.float32)],
        scratch_shapes=[pltpu.VMEM((HALO + tm, POOL_WIDTH), jnp.float32),
                        pltpu.VMEM((SUBLANES, LANES), jnp.float32)],
        compiler_params=pltpu.CompilerParams(vmem_limit_bytes=VMEM_LIMIT,
                                             dimension_semantics=("arbitrary",)),
        name="mix_ln1",
    )(o_sb, p, p, w_pool_bf16, pool_scale, w_out_bf16, x2d, mod, ln1_g, ln1_b, w_router, b_router)


def _dispatch_kernel(pos_ref, u_ref, init_hbm, sorted_hbm, sem):
    del init_hbm
    tm = u_ref.shape[0]

    def body(r, _):
        row = u_ref.at[pl.ds(r, 1), :]
        pltpu.make_async_copy(row, sorted_hbm.at[pl.ds(pos_ref[0, 0, r], 1), :], sem).start()
        pltpu.make_async_copy(row, sorted_hbm.at[pl.ds(pos_ref[0, 0, tm + r], 1), :], sem).start()
        return 0

    lax.fori_loop(0, tm, body, 0, unroll=GATHER_UNROLL // 2)
    for _ in range(2):
        pltpu.make_async_copy(u_ref, sorted_hbm.at[pl.ds(0, tm), :], sem).wait()


def _dispatch(pos, u2, n_rows):
    t, d = u2.shape
    tm = pos.shape[2] // 2
    return pl.pallas_call(
        _dispatch_kernel,
        grid=(t // tm,),
        in_specs=[pl.BlockSpec((1, 1, 2 * tm), lambda i: (i, 0, 0), memory_space=pltpu.SMEM),
                  pl.BlockSpec((tm, d), lambda i: (i, 0)),
                  pl.BlockSpec(memory_space=pl.ANY)],
        out_specs=pl.BlockSpec(memory_space=pl.ANY),
        out_shape=jax.ShapeDtypeStruct((n_rows, d), jnp.float32),
        scratch_shapes=[pltpu.SemaphoreType.DMA(())],
        input_output_aliases={2: 0},
        compiler_params=pltpu.CompilerParams(vmem_limit_bytes=VMEM_LIMIT,
                                             dimension_semantics=("arbitrary",)),
        name="dispatch",
    )(pos, u2, jnp.zeros((n_rows, d), jnp.float32))


def _moe_grouped_kernel(tile_expert_ref, n_tiles_ref, u_ref, wg_ref, wu_ref, wd_ref, y_ref):
    del tile_expert_ref
    i = pl.program_id(0)

    @pl.when(i < n_tiles_ref[0])
    def _():
        u = u_ref[...].astype(jnp.bfloat16)
        gate = jnp.dot(u, wg_ref[0], preferred_element_type=jnp.float32)
        up = jnp.dot(u, wu_ref[0], preferred_element_type=jnp.float32)
        h = gate * jax.nn.sigmoid(gate) * up
        y_ref[...] = jnp.dot(h.astype(jnp.bfloat16), wd_ref[0], preferred_element_type=jnp.float32)

    @pl.when(i >= n_tiles_ref[0])
    def _():
        y_ref[...] = jnp.zeros_like(y_ref)


def _moe_grouped(tile_expert, n_tiles, u_sorted, wg, wu, wd):
    n_rows, d = u_sorted.shape
    tm = MOE_ROW_TILE
    expert = lambda i, te, nt: (te[i], 0, 0)
    row = lambda i, te, nt: (i, 0)
    grid_spec = pltpu.PrefetchScalarGridSpec(
        num_scalar_prefetch=2,
        grid=(n_rows // tm,),
        in_specs=[pl.BlockSpec((tm, d), row),
                  pl.BlockSpec((1, d, EXPERT_HIDDEN), expert),
                  pl.BlockSpec((1, d, EXPERT_HIDDEN), expert),
                  pl.BlockSpec((1, EXPERT_HIDDEN, d), expert)],
        out_specs=pl.BlockSpec((tm, d), row),
    )
    return pl.pallas_call(
        _moe_grouped_kernel,
        grid_spec=grid_spec,
        out_shape=jax.ShapeDtypeStruct((n_rows, d), jnp.float32),
        compiler_params=pltpu.CompilerParams(vmem_limit_bytes=VMEM_LIMIT,
                                             dimension_semantics=("arbitrary",)),
        name="moe_grouped",
    )(tile_expert, n_tiles, u_sorted, wg, wu, wd)


def _combine_ln2_kernel(pos_ref, pos_next_ref, y_hbm, route_ref, x1_ref, mod_ref, g2_ref, b2_ref, o_ref,
                        buf_ref, sem_ref):
    i = pl.program_id(0)
    n = pl.num_programs(0)
    tm = o_ref.shape[0]
    rows = 2 * tm

    def start_gather(rows_ref, slot):
        def body(r, _):
            pltpu.make_async_copy(y_hbm.at[pl.ds(rows_ref[0, 0, r], 1), :], buf_ref.at[slot, pl.ds(r, 1), :],
                                  sem_ref.at[slot]).start()
            return 0
        lax.fori_loop(0, rows, body, 0, unroll=GATHER_UNROLL)

    @pl.when(i == 0)
    def _():
        start_gather(pos_ref, 0)

    @pl.when(i + 1 < n)
    def _():
        start_gather(pos_next_ref, (i + 1) % 2)

    slot = i % 2
    pltpu.make_async_copy(y_hbm.at[pl.ds(0, rows), :], buf_ref.at[slot], sem_ref.at[slot]).wait()
    route = route_ref[...]
    w1 = route[:, ROUTE_W1:ROUTE_W1 + 1]
    w2 = route[:, ROUTE_W2:ROUTE_W2 + 1]
    y = w1 * buf_ref[slot, 0:tm, :] + w2 * buf_ref[slot, tm:, :]
    gate2 = mod_ref[0, 5:6, :]
    o_ref[...] = _ln(DEEPNORM_ALPHA * x1_ref[...] + (1.0 + gate2) * y) * g2_ref[...] + b2_ref[...]


def _combine_ln2(pos, y_sorted, route, x1, mod, ln2_g, ln2_b, seq):
    t, d = x1.shape
    tm = COMBINE_ROW_TILE
    n = t // tm
    tiles_per_seq = seq // tm
    row = lambda i: (i, 0)
    const2 = lambda i: (0, 0)
    return pl.pallas_call(
        _combine_ln2_kernel,
        grid=(n,),
        in_specs=[pl.BlockSpec((1, 1, 2 * tm), lambda i: (i, 0, 0), memory_space=pltpu.SMEM),
                  pl.BlockSpec((1, 1, 2 * tm), lambda i: (jnp.minimum(i + 1, n - 1), 0, 0),
                               memory_space=pltpu.SMEM),
                  pl.BlockSpec(memory_space=pl.ANY),
                  pl.BlockSpec((tm, LANES), row),
                  pl.BlockSpec((tm, d), row),
                  pl.BlockSpec((1, N_MOD, d), lambda i: (i // tiles_per_seq, 0, 0)),
                  pl.BlockSpec((1, d), const2),
                  pl.BlockSpec((1, d), const2)],
        out_specs=pl.BlockSpec((tm, d), row),
        out_shape=jax.ShapeDtypeStruct((t, d), jnp.float32),
        scratch_shapes=[pltpu.VMEM((2, 2 * tm, d), jnp.float32),
                        pltpu.SemaphoreType.DMA((2,))],
        compiler_params=pltpu.CompilerParams(vmem_limit_bytes=VMEM_LIMIT,
                                             dimension_semantics=("arbitrary",)),
        name="combine_ln2",
    )(pos, pos, y_sorted, route, x1, mod, ln2_g, ln2_b)


def _dispatch_plan(route, counts):
    t = route.shape[0]
    tm = MOE_ROW_TILE
    max_tiles = 2 * t // tm + N_EXPERTS
    counts = counts[0, :N_EXPERTS].astype(jnp.int32)
    tiles = (counts + tm - 1) // tm
    tile_end = jnp.cumsum(tiles)
    row_start = (tile_end - tiles) * tm
    experts = route[:, ROUTE_E1:ROUTE_E2 + 1].astype(jnp.int32)
    ranks = route[:, ROUTE_RANK1:ROUTE_RANK2 + 1].astype(jnp.int32)
    pos = row_start[experts] + ranks
    tile_ids = jnp.arange(max_tiles, dtype=jnp.int32)
    tile_expert = jnp.minimum(jnp.sum(tile_end[None, :] <= tile_ids[:, None], axis=1), N_EXPERTS - 1)
    by_tile = lambda tc: pos.reshape(t // tc, tc, 2).transpose(0, 2, 1).reshape(t // tc, 1, 2 * tc)
    return (tile_expert.astype(jnp.int32), tile_end[-1:].astype(jnp.int32), by_tile(DISPATCH_ROW_TILE),
            by_tile(COMBINE_ROW_TILE), max_tiles * tm)


def kernel(x, c, w_ada, b_ada, w_in, w_pool, pool_scale, w_out, ln1_g, ln1_b, w_router_group, b_router_group,
           w_router_expert, b_router_expert, w_gate, w_up, w_down, ln2_g, ln2_b):
    b, s, d = x.shape
    bf16 = jnp.bfloat16
    for layer in range(DEPTH):
        c_pad = jnp.pad(c, ((0, SUBLANES - b), (0, 0)))
        mod = _adaln(c_pad, w_ada[layer], b_ada[layer][None, :])[:b].reshape(b, N_MOD, d)
        x2d = x.reshape(b * s, d)
        qkv, p = _ln_inproj(x2d, mod, w_in[layer].astype(bf16), s)
        o_sb = _sb_attn(qkv, b, s).reshape(b * s, SB_WIDTH)
        pad = LANES - N_EXPERTS - N_EXPERT_GROUPS
        w_router = jnp.pad(jnp.concatenate([w_router_expert[layer], w_router_group[layer]], axis=1),
                           ((0, 0), (0, pad)))
        b_router = jnp.pad(jnp.concatenate([b_router_expert[layer], b_router_group[layer]]), (0, pad))[None, :]
        x1, u2, route, counts = _mix_ln1(o_sb, p, w_pool[layer].astype(bf16), pool_scale[layer][None, :],
                                         w_out[layer].astype(bf16), x2d, mod, ln1_g[layer][None, :],
                                         ln1_b[layer][None, :], w_router, b_router, s)
        tile_expert, n_tiles, pos_dispatch, pos, n_rows = _dispatch_plan(route, counts)
        u_sorted = _dispatch(pos_dispatch, u2, n_rows)
        y_sorted = _moe_grouped(tile_expert, n_tiles, u_sorted, w_gate[layer].astype(bf16),
                                w_up[layer].astype(bf16), w_down[layer].astype(bf16))
        x2 = _combine_ln2(pos, y_sorted, route, x1, mod, ln2_g[layer][None, :], ln2_b[layer][None, :], s)
        x = x2.reshape(b, s, d)
    return x
```

```python
import functools

import jax
import jax.numpy as jnp
from jax import lax
from jax.experimental import pallas as pl
from jax.experimental.pallas import tpu as pltpu

D_MODEL = 1024
N_SB_HEADS = 8
SB_HEAD_DIM = 64
SB_WIDTH = N_SB_HEADS * SB_HEAD_DIM
POOL_WINDOWS = (2, 4, 8, 16)
POOL_GROUP_DIM = 128
POOL_WIDTH = len(POOL_WINDOWS) * POOL_GROUP_DIM
N_EXPERT_GROUPS = 4
EXPERTS_PER_GROUP = 4
N_EXPERTS = N_EXPERT_GROUPS * EXPERTS_PER_GROUP
EXPERT_HIDDEN = 512
DEPTH = 1
DEEPNORM_ALPHA = (2.0 * DEPTH) ** 0.25
LN_EPS = 1e-5
N_MOD = 6

LANES = 128
SUBLANES = 8
HALO = max(POOL_WINDOWS)
VMEM_LIMIT = 56 * 1024 * 1024

ROW_TILE = 512
ATTN_TILE = 256
MOE_ROW_TILE = 512
DISPATCH_ROW_TILE = 1024
COMBINE_ROW_TILE = 256
GATHER_UNROLL = 8


def _ln(x):
    mu = jnp.mean(x, axis=-1, keepdims=True)
    xc = x - mu
    var = jnp.mean(xc * xc, axis=-1, keepdims=True)
    return xc * lax.rsqrt(var + LN_EPS)


def _adaln_kernel(c_ref, w_ref, b_ref, o_ref):
    c = c_ref[...]
    a = c * jax.nn.sigmoid(c)
    o_ref[...] = jnp.dot(a, w_ref[...], preferred_element_type=jnp.float32,
                         precision=lax.Precision.HIGHEST) + b_ref[...]


def _adaln(c_pad, w_ada, b_ada):
    rows, d = c_pad.shape
    n = w_ada.shape[1]
    tn = 1024
    return pl.pallas_call(
        _adaln_kernel,
        grid=(n // tn,),
        in_specs=[pl.BlockSpec((rows, d), lambda j: (0, 0)),
                  pl.BlockSpec((d, tn), lambda j: (0, j)),
                  pl.BlockSpec((1, tn), lambda j: (0, j))],
        out_specs=pl.BlockSpec((rows, tn), lambda j: (0, j)),
        out_shape=jax.ShapeDtypeStruct((rows, n), jnp.float32),
        compiler_params=pltpu.CompilerParams(vmem_limit_bytes=VMEM_LIMIT),
        name="adaln",
    )(c_pad, w_ada, b_ada)


def _ln_inproj_kernel(x_ref, mod_ref, w_ref, qkv_ref, p_ref):
    shift = mod_ref[0, 0:1, :]
    scale = mod_ref[0, 1:2, :]
    u = (_ln(x_ref[...]) * (1.0 + scale) + shift).astype(jnp.bfloat16)
    qk_scale = SB_HEAD_DIM ** -0.5 * 1.4426950408889634
    q = jnp.dot(u, w_ref[:, 0:SB_WIDTH], preferred_element_type=jnp.float32) * qk_scale
    qkv_ref[:, 0:SB_WIDTH] = q.astype(jnp.bfloat16)
    for j in (1, 2):
        kv = jnp.dot(u, w_ref[:, j * SB_WIDTH:(j + 1) * SB_WIDTH], preferred_element_type=jnp.float32)
        qkv_ref[:, j * SB_WIDTH:(j + 1) * SB_WIDTH] = kv.astype(jnp.bfloat16)
    p_ref[...] = jnp.dot(u, w_ref[:, 3 * SB_WIDTH:], preferred_element_type=jnp.float32)


def _ln_inproj(x2d, mod, w_in_bf16, seq):
    t, d = x2d.shape
    tm = ROW_TILE
    tiles_per_seq = seq // tm
    return pl.pallas_call(
        _ln_inproj_kernel,
        grid=(t // tm,),
        in_specs=[pl.BlockSpec((tm, d), lambda i: (i, 0)),
                  pl.BlockSpec((1, N_MOD, d), lambda i: (i // tiles_per_seq, 0, 0)),
                  pl.BlockSpec(w_in_bf16.shape, lambda i: (0, 0))],
        out_specs=[pl.BlockSpec((tm, 3 * SB_WIDTH), lambda i: (i, 0)),
                   pl.BlockSpec((tm, POOL_WIDTH), lambda i: (i, 0))],
        out_shape=[jax.ShapeDtypeStruct((t, 3 * SB_WIDTH), jnp.bfloat16),
                   jax.ShapeDtypeStruct((t, POOL_WIDTH), jnp.float32)],
        compiler_params=pltpu.CompilerParams(vmem_limit_bytes=VMEM_LIMIT),
        name="ln_inproj",
    )(x2d, mod, w_in_bf16)


def _sb_attn_kernel(q_ref, k_ref, v_ref, o_ref, z_ref, zfix_ref, nb_ref, att_ref, acc_ref, carry_ref):
    tq = tk = ATTN_TILE
    heads = (0, 1)
    qi = pl.program_id(2)
    q2 = q_ref[0]
    lane = lax.broadcasted_iota(jnp.int32, (tq, LANES), 1)
    first_head = lane < SB_HEAD_DIM
    zero = jnp.zeros_like(q2)
    q_heads = (jnp.where(first_head, q2, zero), jnp.where(first_head, zero, q2))

    row = lax.broadcasted_iota(jnp.int32, (tk, tk), 0)
    col = lax.broadcasted_iota(jnp.int32, (tk, tk), 1)
    neg_suffix = jnp.where(row >= col, -1.0, 0.0).astype(jnp.bfloat16)
    causal = col < row

    def key_start(j):
        return pl.multiple_of(jnp.maximum(qi - j, 0) * tk, tk)

    def stage_a(j, par):
        k_blk = k_ref[0, pl.ds(key_start(j), tk), :]
        for h in heads:
            z_ref[par, h] = lax.dot_general(q_heads[h], k_blk, (((1,), (1,)), ((), ())),
                                            preferred_element_type=jnp.float32)

    def stage_b(par, diagonal):
        for h in heads:
            z = z_ref[par, h]
            n = jnp.maximum(z, 0.0) + jnp.log2(1.0 + jnp.exp2(-jnp.abs(z)))
            if diagonal:
                n = jnp.where(causal, n, 0.0)
            n_bf16 = n.astype(jnp.bfloat16)
            nb_ref[par, h] = n_bf16
            zfix_ref[par, h] = z - (n - n_bf16.astype(jnp.float32))

    def stage_c(par, diagonal):
        incls = [jnp.dot(nb_ref[par, h], neg_suffix, preferred_element_type=jnp.float32)
                 for h in heads]
        for h in heads:
            carry = carry_ref[h]
            att = jnp.exp2(zfix_ref[par, h] + incls[h] + jnp.concatenate([carry] * (tk // LANES), axis=1))
            if diagonal:
                att = jnp.where(causal, att, 0.0)
            att_ref[par, h] = att.astype(jnp.bfloat16)
            carry_ref[h] = carry + jnp.broadcast_to(incls[h][:, 0:1], (tq, LANES))

    def stage_d(j, par):
        v_blk = v_ref[0, pl.ds(key_start(j), tk), :]
        for h in heads:
            acc_ref[h] += jnp.dot(att_ref[par, h], v_blk, preferred_element_type=jnp.float32)

    def step(i, par):
        stage_b(1 - par, diagonal=False)
        stage_c(par, diagonal=False)
        stage_a(i + 2, par)
        stage_d(i - 1, 1 - par)

    carry_ref[...] = jnp.zeros_like(carry_ref)
    acc_ref[...] = jnp.zeros_like(acc_ref)
    stage_a(0, 0)
    stage_b(0, diagonal=True)
    stage_a(1, 1)
    stage_c(0, diagonal=True)
    stage_a(2, 0)
    stage_b(1, diagonal=False)

    def two_steps(m, _):
        step(2 * m + 1, 1)
        step(2 * m + 2, 0)
        return 0

    lax.fori_loop(0, qi // 2, two_steps, 0)

    @pl.when(qi % 2 == 1)
    def _():
        step(qi, 1)
        stage_d(qi, 1)

    @pl.when(qi % 2 == 0)
    def _():
        stage_d(qi, 0)

    o_ref[0] = jnp.where(first_head, acc_ref[0], acc_ref[1]).astype(o_ref.dtype)


def _sb_attn(qkv, batch, seq):
    qkv3 = qkv.reshape(batch, seq, 3 * SB_WIDTH)
    tq = ATTN_TILE
    pairs = SB_WIDTH // LANES
    return pl.pallas_call(
        _sb_attn_kernel,
        grid=(batch, pairs, seq // tq),
        in_specs=[pl.BlockSpec((1, tq, LANES), lambda b, h, i: (b, i, h)),
                  pl.BlockSpec((1, seq, LANES), lambda b, h, i: (b, 0, pairs + h)),
                  pl.BlockSpec((1, seq, LANES), lambda b, h, i: (b, 0, 2 * pairs + h))],
        out_specs=pl.BlockSpec((1, tq, LANES), lambda b, h, i: (b, i, h)),
        out_shape=jax.ShapeDtypeStruct((batch, seq, SB_WIDTH), jnp.bfloat16),
        scratch_shapes=[pltpu.VMEM((2, 2, tq, tq), jnp.float32),
                        pltpu.VMEM((2, 2, tq, tq), jnp.float32),
                        pltpu.VMEM((2, 2, tq, tq), jnp.bfloat16),
                        pltpu.VMEM((2, 2, tq, tq), jnp.bfloat16),
                        pltpu.VMEM((2, tq, LANES), jnp.float32),
                        pltpu.VMEM((2, tq, LANES), jnp.float32)],
        compiler_params=pltpu.CompilerParams(vmem_limit_bytes=VMEM_LIMIT),
        name="sb_attn",
    )(qkv3, qkv3, qkv3)


def _route(logits):
    lane = lax.broadcasted_iota(jnp.int32, logits.shape, 1)
    neg = jnp.float32(-jnp.inf)
    big = jnp.int32(LANES)
    is_group = (lane >= N_EXPERTS) & (lane < N_EXPERTS + N_EXPERT_GROUPS)
    gl = jnp.where(is_group, logits, neg)
    g_max = jnp.max(gl, axis=-1, keepdims=True)
    g_sum = jnp.sum(jnp.exp(gl - g_max), axis=-1, keepdims=True)
    g_p = 1.0 / g_sum
    g_idx = jnp.min(jnp.where(gl == g_max, lane, big), axis=-1, keepdims=True) - N_EXPERTS
    in_group = (lane >= g_idx * EXPERTS_PER_GROUP) & (lane < (g_idx + 1) * EXPERTS_PER_GROUP)
    el = jnp.where(in_group, logits, neg)
    e1 = jnp.max(el, axis=-1, keepdims=True)
    i1 = jnp.min(jnp.where(el == e1, lane, big), axis=-1, keepdims=True)
    el2 = jnp.where(lane == i1, neg, el)
    e2 = jnp.max(el2, axis=-1, keepdims=True)
    i2 = jnp.min(jnp.where(el2 == e2, lane, big), axis=-1, keepdims=True)
    r = jnp.exp(e2 - e1)
    w1 = g_p / (1.0 + r)
    w2 = w1 * r
    return i1, i2, w1, w2


ROUTE_W1, ROUTE_W2, ROUTE_E1, ROUTE_E2, ROUTE_RANK1, ROUTE_RANK2 = range(6)


def _mix_ln1_kernel(tiles_per_seq, osb_ref, p_ref, halo_ref, wpool_ref, pscale_ref, wout_ref, x_ref, mod_ref,
                    g1_ref, b1_ref, wr_ref, br_ref, x1_ref, u2_ref, route_ref, count_ref, pext_ref, running_ref):
    tm = p_ref.shape[0]
    tile_in_seq = pl.program_id(0) % tiles_per_seq
    p = p_ref[...]
    pext_ref[0:HALO, :] = jnp.where(tile_in_seq == 0, 0.0, halo_ref[...])
    pext_ref[HALO:, :] = p
    pos = tile_in_seq * tm + lax.broadcasted_iota(jnp.int32, (tm, 1), 0)

    mixed = jnp.dot(osb_ref[...], wout_ref[0:SB_WIDTH, :], preferred_element_type=jnp.float32)
    for g, w in enumerate(POOL_WINDOWS):
        cols = slice(g * POOL_GROUP_DIM, (g + 1) * POOL_GROUP_DIM)
        win = p[:, cols]
        for i in range(1, w):
            win = win + pext_ref[HALO - i:HALO - i + tm, cols]
        count = jnp.minimum(pos + 1, w).astype(jnp.float32)
        pooled = win / count - p[:, cols]
        o_pool = jnp.dot(pooled.astype(jnp.bfloat16), wpool_ref[g], preferred_element_type=jnp.float32)
        o_pool = o_pool * pscale_ref[:, cols]
        mixed = mixed + jnp.dot(o_pool.astype(jnp.bfloat16),
                                wout_ref[SB_WIDTH + g * POOL_GROUP_DIM:SB_WIDTH + (g + 1) * POOL_GROUP_DIM, :],
                                preferred_element_type=jnp.float32)

    gate1 = mod_ref[0, 2:3, :]
    shift2 = mod_ref[0, 3:4, :]
    scale2 = mod_ref[0, 4:5, :]
    x1 = _ln(DEEPNORM_ALPHA * x_ref[...] + (1.0 + gate1) * mixed) * g1_ref[...] + b1_ref[...]
    x1_ref[...] = x1
    u2 = _ln(x1) * (1.0 + scale2) + shift2
    u2_ref[...] = u2
    logits = jnp.dot(u2, wr_ref[...], preferred_element_type=jnp.float32,
                     precision=lax.Precision.HIGHEST) + br_ref[...]
    i1, i2, w1, w2 = _route(logits)

    @pl.when(pl.program_id(0) == 0)
    def _():
        running_ref[...] = jnp.zeros_like(running_ref)

    lane = lax.broadcasted_iota(jnp.int32, (tm, LANES), 1)
    first, second = lane == i1, lane == i2
    uses = jnp.where(first | second, 1.0, 0.0)
    earlier = (lax.broadcasted_iota(jnp.int32, (tm, tm), 1)
               < lax.broadcasted_iota(jnp.int32, (tm, tm), 0)).astype(jnp.bfloat16)
    before = jnp.dot(earlier, uses.astype(jnp.bfloat16), preferred_element_type=jnp.float32)
    before = before + running_ref[0:1, :]
    rank1 = jnp.sum(jnp.where(first, before, 0.0), axis=-1, keepdims=True)
    rank2 = jnp.sum(jnp.where(second, before, 0.0), axis=-1, keepdims=True)
    running = running_ref[0:1, :] + jnp.sum(uses, axis=0, keepdims=True)
    running_ref[...] = jnp.broadcast_to(running, running_ref.shape)
    count_ref[...] = jnp.broadcast_to(running, count_ref.shape)

    record = jnp.zeros((tm, LANES), jnp.float32)
    for slot, value in ((ROUTE_W1, w1), (ROUTE_W2, w2), (ROUTE_E1, i1.astype(jnp.float32)),
                        (ROUTE_E2, i2.astype(jnp.float32)), (ROUTE_RANK1, rank1), (ROUTE_RANK2, rank2)):
        record = jnp.where(lane == slot, value, record)
    route_ref[...] = record


def _mix_ln1(o_sb, p, w_pool_bf16, pool_scale, w_out_bf16, x2d, mod, ln1_g, ln1_b, w_router, b_router, seq):
    t, d = x2d.shape
    tm = ROW_TILE
    tiles_per_seq = seq // tm
    halo_blocks_per_tile = tm // HALO
    row = lambda i: (i, 0)
    const2 = lambda i: (0, 0)
    return pl.pallas_call(
        functools.partial(_mix_ln1_kernel, tiles_per_seq),
        grid=(t // tm,),
        in_specs=[pl.BlockSpec((tm, SB_WIDTH), row),
                  pl.BlockSpec((tm, POOL_WIDTH), row),
                  pl.BlockSpec((HALO, POOL_WIDTH), lambda i: (jnp.maximum(i * halo_blocks_per_tile - 1, 0), 0)),
                  pl.BlockSpec(w_pool_bf16.shape, lambda i: (0, 0, 0)),
                  pl.BlockSpec((1, POOL_WIDTH), const2),
                  pl.BlockSpec(w_out_bf16.shape, const2),
                  pl.BlockSpec((tm, d), row),
                  pl.BlockSpec((1, N_MOD, d), lambda i: (i // tiles_per_seq, 0, 0)),
                  pl.BlockSpec((1, d), const2),
                  pl.BlockSpec((1, d), const2),
                  pl.BlockSpec((d, LANES), const2),
                  pl.BlockSpec((1, LANES), const2)],
        out_specs=[pl.BlockSpec((tm, d), row),
                   pl.BlockSpec((tm, d), row),
                   pl.BlockSpec((tm, LANES), row),
                   pl.BlockSpec((SUBLANES, LANES), const2)],
        out_shape=[jax.ShapeDtypeStruct((t, d), jnp.float32),
                   jax.ShapeDtypeStruct((t, d), jnp.float32),
                   jax.ShapeDtypeStruct((t, LANES), jnp.float32),
                   jax.ShapeDtypeStruct((SUBLANES, LANES), jnp.float32)],
        scratch_shapes=[pltpu.VMEM((HALO + tm, POOL_WIDTH), jnp.float32),
                        pltpu.VMEM((SUBLANES, LANES), jnp.float32)],
        compiler_params=pltpu.CompilerParams(vmem_limit_bytes=VMEM_LIMIT,
                                             dimension_semantics=("arbitrary",)),
        name="mix_ln1",
    )(o_sb, p, p, w_pool_bf16, pool_scale, w_out_bf16, x2d, mod, ln1_g, ln1_b, w_router, b_router)


def _dispatch_kernel(pos_ref, u_ref, init_hbm, sorted_hbm, sem):
    del init_hbm
    tm = u_ref.shape[0]

    def body(r, _):
        row = u_ref.at[pl.ds(r, 1), :]
        pltpu.make_async_copy(row, sorted_hbm.at[pl.ds(pos_ref[0, 0, r], 1), :], sem).start()
        pltpu.make_async_copy(row, sorted_hbm.at[pl.ds(pos_ref[0, 0, tm + r], 1), :], sem).start()
        return 0

    lax.fori_loop(0, tm, body, 0, unroll=GATHER_UNROLL // 2)
    for _ in range(2):
        pltpu.make_async_copy(u_ref, sorted_hbm.at[pl.ds(0, tm), :], sem).wait()


def _dispatch(pos, u2, n_rows):
    t, d = u2.shape
    tm = pos.shape[2] // 2
    return pl.pallas_call(
        _dispatch_kernel,
        grid=(t // tm,),
        in_specs=[pl.BlockSpec((1, 1, 2 * tm), lambda i: (i, 0, 0), memory_space=pltpu.SMEM),
                  pl.BlockSpec((tm, d), lambda i: (i, 0)),
                  pl.BlockSpec(memory_space=pl.ANY)],
        out_specs=pl.BlockSpec(memory_space=pl.ANY),
        out_shape=jax.ShapeDtypeStruct((n_rows, d), jnp.float32),
        scratch_shapes=[pltpu.SemaphoreType.DMA(())],
        input_output_aliases={2: 0},
        compiler_params=pltpu.CompilerParams(vmem_limit_bytes=VMEM_LIMIT,
                                             dimension_semantics=("arbitrary",)),
        name="dispatch",
    )(pos, u2, jnp.zeros((n_rows, d), jnp.float32))


def _moe_grouped_kernel(tile_expert_ref, n_tiles_ref, u_ref, wg_ref, wu_ref, wd_ref, y_ref):
    del tile_expert_ref
    i = pl.program_id(0)

    @pl.when(i < n_tiles_ref[0])
    def _():
        u = u_ref[...].astype(jnp.bfloat16)
        gate = jnp.dot(u, wg_ref[0], preferred_element_type=jnp.float32)
        up = jnp.dot(u, wu_ref[0], preferred_element_type=jnp.float32)
        h = gate * jax.nn.sigmoid(gate) * up
        y_ref[...] = jnp.dot(h.astype(jnp.bfloat16), wd_ref[0], preferred_element_type=jnp.float32)

    @pl.when(i >= n_tiles_ref[0])
    def _():
        y_ref[...] = jnp.zeros_like(y_ref)


def _moe_grouped(tile_expert, n_tiles, u_sorted, wg, wu, wd):
    n_rows, d = u_sorted.shape
    tm = MOE_ROW_TILE
    expert = lambda i, te, nt: (te[i], 0, 0)
    row = lambda i, te, nt: (i, 0)
    grid_spec = pltpu.PrefetchScalarGridSpec(
        num_scalar_prefetch=2,
        grid=(n_rows // tm,),
        in_specs=[pl.BlockSpec((tm, d), row),
                  pl.BlockSpec((1, d, EXPERT_HIDDEN), expert),
                  pl.BlockSpec((1, d, EXPERT_HIDDEN), expert),
                  pl.BlockSpec((1, EXPERT_HIDDEN, d), expert)],
        out_specs=pl.BlockSpec((tm, d), row),
    )
    return pl.pallas_call(
        _moe_grouped_kernel,
        grid_spec=grid_spec,
        out_shape=jax.ShapeDtypeStruct((n_rows, d), jnp.float32),
        compiler_params=pltpu.CompilerParams(vmem_limit_bytes=VMEM_LIMIT,
                                             dimension_semantics=("arbitrary",)),
        name="moe_grouped",
    )(tile_expert, n_tiles, u_sorted, wg, wu, wd)


def _combine_ln2_kernel(pos_ref, pos_next_ref, y_hbm, route_ref, x1_ref, mod_ref, g2_ref, b2_ref, o_ref,
                        buf_ref, sem_ref):
    i = pl.program_id(0)
    n = pl.num_programs(0)
    tm = o_ref.shape[0]
    rows = 2 * tm

    def start_gather(rows_ref, slot):
        def body(r, _):
            pltpu.make_async_copy(y_hbm.at[pl.ds(rows_ref[0, 0, r], 1), :], buf_ref.at[slot, pl.ds(r, 1), :],
                                  sem_ref.at[slot]).start()
            return 0
        lax.fori_loop(0, rows, body, 0, unroll=GATHER_UNROLL)

    @pl.when(i == 0)
    def _():
        start_gather(pos_ref, 0)

    @pl.when(i + 1 < n)
    def _():
        start_gather(pos_next_ref, (i + 1) % 2)

    slot = i % 2
    pltpu.make_async_copy(y_hbm.at[pl.ds(0, rows), :], buf_ref.at[slot], sem_ref.at[slot]).wait()
    route = route_ref[...]
    w1 = route[:, ROUTE_W1:ROUTE_W1 + 1]
    w2 = route[:, ROUTE_W2:ROUTE_W2 + 1]
    y = w1 * buf_ref[slot, 0:tm, :] + w2 * buf_ref[slot, tm:, :]
    gate2 = mod_ref[0, 5:6, :]
    o_ref[...] = _ln(DEEPNORM_ALPHA * x1_ref[...] + (1.0 + gate2) * y) * g2_ref[...] + b2_ref[...]


def _combine_ln2(pos, y_sorted, route, x1, mod, ln2_g, ln2_b, seq):
    t, d = x1.shape
    tm = COMBINE_ROW_TILE
    n = t // tm
    tiles_per_seq = seq // tm
    row = lambda i: (i, 0)
    const2 = lambda i: (0, 0)
    return pl.pallas_call(
        _combine_ln2_kernel,
        grid=(n,),
        in_specs=[pl.BlockSpec((1, 1, 2 * tm), lambda i: (i, 0, 0), memory_space=pltpu.SMEM),
                  pl.BlockSpec((1, 1, 2 * tm), lambda i: (jnp.minimum(i + 1, n - 1), 0, 0),
                               memory_space=pltpu.SMEM),
                  pl.BlockSpec(memory_space=pl.ANY),
                  pl.BlockSpec((tm, LANES), row),
                  pl.BlockSpec((tm, d), row),
                  pl.BlockSpec((1, N_MOD, d), lambda i: (i // tiles_per_seq, 0, 0)),
                  pl.BlockSpec((1, d), const2),
                  pl.BlockSpec((1, d), const2)],
        out_specs=pl.BlockSpec((tm, d), row),
        out_shape=jax.ShapeDtypeStruct((t, d), jnp.float32),
        scratch_shapes=[pltpu.VMEM((2, 2 * tm, d), jnp.float32),
                        pltpu.SemaphoreType.DMA((2,))],
        compiler_params=pltpu.CompilerParams(vmem_limit_bytes=VMEM_LIMIT,
                                             dimension_semantics=("arbitrary",)),
        name="combine_ln2",
    )(pos, pos, y_sorted, route, x1, mod, ln2_g, ln2_b)


def _dispatch_plan(route, counts):
    t = route.shape[0]
    tm = MOE_ROW_TILE
    max_tiles = 2 * t // tm + N_EXPERTS
    counts = counts[0, :N_EXPERTS].astype(jnp.int32)
    tiles = (counts + tm - 1) // tm
    tile_end = jnp.cumsum(tiles)
    row_start = (tile_end - tiles) * tm
    experts = route[:, ROUTE_E1:ROUTE_E2 + 1].astype(jnp.int32)
    ranks = route[:, ROUTE_RANK1:ROUTE_RANK2 + 1].astype(jnp.int32)
    pos = row_start[experts] + ranks
    tile_ids = jnp.arange(max_tiles, dtype=jnp.int32)
    tile_expert = jnp.minimum(jnp.sum(tile_end[None, :] <= tile_ids[:, None], axis=1), N_EXPERTS - 1)
    by_tile = lambda tc: pos.reshape(t // tc, tc, 2).transpose(0, 2, 1).reshape(t // tc, 1, 2 * tc)
    return (tile_expert.astype(jnp.int32), tile_end[-1:].astype(jnp.int32), by_tile(DISPATCH_ROW_TILE),
            by_tile(COMBINE_ROW_TILE), max_tiles * tm)


def kernel(x, c, w_ada, b_ada, w_in, w_pool, pool_scale, w_out, ln1_g, ln1_b, w_router_group, b_router_group,
           w_router_expert, b_router_expert, w_gate, w_up, w_down, ln2_g, ln2_b):
    b, s, d = x.shape
    bf16 = jnp.bfloat16
    for layer in range(DEPTH):
        c_pad = jnp.pad(c, ((0, SUBLANES - b), (0, 0)))
        mod = _adaln(c_pad, w_ada[layer], b_ada[layer][None, :])[:b].reshape(b, N_MOD, d)
        x2d = x.reshape(b * s, d)
        qkv, p = _ln_inproj(x2d, mod, w_in[layer].astype(bf16), s)
        o_sb = _sb_attn(qkv, b, s).reshape(b * s, SB_WIDTH)
        pad = LANES - N_EXPERTS - N_EXPERT_GROUPS
        w_router = jnp.pad(jnp.concatenate([w_router_expert[layer], w_router_group[layer]], axis=1),
                           ((0, 0), (0, pad)))
        b_router = jnp.pad(jnp.concatenate([b_router_expert[layer], b_router_group[layer]]), (0, pad))[None, :]
        x1, u2, route, counts = _mix_ln1(o_sb, p, w_pool[layer].astype(bf16), pool_scale[layer][None, :],
                                         w_out[layer].astype(bf16), x2d, mod, ln1_g[layer][None, :],
                                         ln1_b[layer][None, :], w_router, b_router, s)
        tile_expert, n_tiles, pos_dispatch, pos, n_rows = _dispatch_plan(route, counts)
        u_sorted = _dispatch(pos_dispatch, u2, n_rows)
        y_sorted = _moe_grouped(tile_expert, n_tiles, u_sorted, w_gate[layer].astype(bf16),
                                w_up[layer].astype(bf16), w_down[layer].astype(bf16))
        x2 = _combine_ln2(pos, y_sorted, route, x1, mod, ln2_g[layer][None, :], ln2_b[layer][None, :], s)
        x = x2.reshape(b, s, d)
    return x
```

```python
import functools

import jax
import jax.numpy as jnp
from jax import lax
from jax.experimental import pallas as pl
from jax.experimental.pallas import tpu as pltpu

D_MODEL = 1024
N_SB_HEADS = 8
SB_HEAD_DIM = 64
SB_WIDTH = N_SB_HEADS * SB_HEAD_DIM
POOL_WINDOWS = (2, 4, 8, 16)
POOL_GROUP_DIM = 128
POOL_WIDTH = len(POOL_WINDOWS) * POOL_GROUP_DIM
N_EXPERT_GROUPS = 4
EXPERTS_PER_GROUP = 4
N_EXPERTS = N_EXPERT_GROUPS * EXPERTS_PER_GROUP
EXPERT_HIDDEN = 512
DEPTH = 1
DEEPNORM_ALPHA = (2.0 * DEPTH) ** 0.25
LN_EPS = 1e-5
N_MOD = 6

LANES = 128
SUBLANES = 8
HALO = max(POOL_WINDOWS)
VMEM_LIMIT = 56 * 1024 * 1024

ROW_TILE = 512
ATTN_TILE = 256
STICK_GONE_LOG2 = -180.0
MOE_ROW_TILE = 512
DISPATCH_ROW_TILE = 1024
COMBINE_ROW_TILE = 256
GATHER_UNROLL = 8


def _ln(x):
    mu = jnp.mean(x, axis=-1, keepdims=True)
    xc = x - mu
    var = jnp.mean(xc * xc, axis=-1, keepdims=True)
    return xc * lax.rsqrt(var + LN_EPS)


def _adaln_kernel(c_ref, w_ref, b_ref, o_ref):
    c = c_ref[...]
    a = c * jax.nn.sigmoid(c)
    o_ref[...] = jnp.dot(a, w_ref[...], preferred_element_type=jnp.float32,
                         precision=lax.Precision.HIGHEST) + b_ref[...]


def _adaln(c_pad, w_ada, b_ada):
    rows, d = c_pad.shape
    n = w_ada.shape[1]
    tn = 1024
    return pl.pallas_call(
        _adaln_kernel,
        grid=(n // tn,),
        in_specs=[pl.BlockSpec((rows, d), lambda j: (0, 0)),
                  pl.BlockSpec((d, tn), lambda j: (0, j)),
                  pl.BlockSpec((1, tn), lambda j: (0, j))],
        out_specs=pl.BlockSpec((rows, tn), lambda j: (0, j)),
        out_shape=jax.ShapeDtypeStruct((rows, n), jnp.float32),
        compiler_params=pltpu.CompilerParams(vmem_limit_bytes=VMEM_LIMIT),
        name="adaln",
    )(c_pad, w_ada, b_ada)


def _ln_inproj_kernel(x_ref, mod_ref, w_ref, qkv_ref, p_ref):
    shift = mod_ref[0, 0:1, :]
    scale = mod_ref[0, 1:2, :]
    u = (_ln(x_ref[...]) * (1.0 + scale) + shift).astype(jnp.bfloat16)
    qk_scale = SB_HEAD_DIM ** -0.5 * 1.4426950408889634
    q = jnp.dot(u, w_ref[:, 0:SB_WIDTH], preferred_element_type=jnp.float32) * qk_scale
    qkv_ref[:, 0:SB_WIDTH] = q.astype(jnp.bfloat16)
    for j in (1, 2):
        kv = jnp.dot(u, w_ref[:, j * SB_WIDTH:(j + 1) * SB_WIDTH], preferred_element_type=jnp.float32)
        qkv_ref[:, j * SB_WIDTH:(j + 1) * SB_WIDTH] = kv.astype(jnp.bfloat16)
    p_ref[...] = jnp.dot(u, w_ref[:, 3 * SB_WIDTH:], preferred_element_type=jnp.float32)


def _ln_inproj(x2d, mod, w_in_bf16, seq):
    t, d = x2d.shape
    tm = ROW_TILE
    tiles_per_seq = seq // tm
    return pl.pallas_call(
        _ln_inproj_kernel,
        grid=(t // tm,),
        in_specs=[pl.BlockSpec((tm, d), lambda i: (i, 0)),
                  pl.BlockSpec((1, N_MOD, d), lambda i: (i // tiles_per_seq, 0, 0)),
                  pl.BlockSpec(w_in_bf16.shape, lambda i: (0, 0))],
        out_specs=[pl.BlockSpec((tm, 3 * SB_WIDTH), lambda i: (i, 0)),
                   pl.BlockSpec((tm, POOL_WIDTH), lambda i: (i, 0))],
        out_shape=[jax.ShapeDtypeStruct((t, 3 * SB_WIDTH), jnp.bfloat16),
                   jax.ShapeDtypeStruct((t, POOL_WIDTH), jnp.float32)],
        compiler_params=pltpu.CompilerParams(vmem_limit_bytes=VMEM_LIMIT),
        name="ln_inproj",
    )(x2d, mod, w_in_bf16)


def _sb_attn_kernel(q_ref, k_ref, v_ref, o_ref, z_ref, zfix_ref, nb_ref, att_ref, acc_ref, carry_ref):
    tq = tk = ATTN_TILE
    heads = (0, 1)
    qi = pl.program_id(2)
    q2 = q_ref[0]
    lane = lax.broadcasted_iota(jnp.int32, (tq, LANES), 1)
    first_head = lane < SB_HEAD_DIM
    zero = jnp.zeros_like(q2)
    q_heads = (jnp.where(first_head, q2, zero), jnp.where(first_head, zero, q2))

    row = lax.broadcasted_iota(jnp.int32, (tk, tk), 0)
    col = lax.broadcasted_iota(jnp.int32, (tk, tk), 1)
    neg_suffix = jnp.where(row >= col, -1.0, 0.0).astype(jnp.bfloat16)
    causal = col < row

    def key_start(j):
        return pl.multiple_of(jnp.maximum(qi - j, 0) * tk, tk)

    def stage_a(j, par):
        k_blk = k_ref[0, pl.ds(key_start(j), tk), :]
        for h in heads:
            z_ref[par, h] = lax.dot_general(q_heads[h], k_blk, (((1,), (1,)), ((), ())),
                                            preferred_element_type=jnp.float32)

    def stage_b(par, diagonal):
        for h in heads:
            z = z_ref[par, h]
            n = jnp.maximum(z, 0.0) + jnp.log2(1.0 + jnp.exp2(-jnp.abs(z)))
            if diagonal:
                n = jnp.where(causal, n, 0.0)
            n_bf16 = n.astype(jnp.bfloat16)
            nb_ref[par, h] = n_bf16
            zfix_ref[par, h] = z - (n - n_bf16.astype(jnp.float32))

    def stage_c(par, diagonal):
        incls = [jnp.dot(nb_ref[par, h], neg_suffix, preferred_element_type=jnp.float32)
                 for h in heads]
        for h in heads:
            carry = carry_ref[h]
            att = jnp.exp2(zfix_ref[par, h] + incls[h] + jnp.concatenate([carry] * (tk // LANES), axis=1))
            if diagonal:
                att = jnp.where(causal, att, 0.0)
            att_ref[par, h] = att.astype(jnp.bfloat16)
            carry_ref[h] = carry + jnp.broadcast_to(incls[h][:, 0:1], (tq, LANES))

    def stage_d(j, par):
        v_blk = v_ref[0, pl.ds(key_start(j), tk), :]
        for h in heads:
            acc_ref[h] += jnp.dot(att_ref[par, h], v_blk, preferred_element_type=jnp.float32)

    def step(i, par):
        stage_b(1 - par, diagonal=False)
        stage_c(par, diagonal=False)
        stage_a(i + 2, par)
        stage_d(i - 1, 1 - par)

    carry_ref[...] = jnp.zeros_like(carry_ref)
    acc_ref[...] = jnp.zeros_like(acc_ref)
    stage_a(0, 0)
    stage_b(0, diagonal=True)
    stage_a(1, 1)
    stage_c(0, diagonal=True)
    stage_a(2, 0)
    stage_b(1, diagonal=False)

    def two_steps(state):
        m, _ = state
        step(2 * m + 1, 1)
        step(2 * m + 2, 0)
        return m + 1, jnp.max(carry_ref[...]) < STICK_GONE_LOG2

    pairs, stick_gone = lax.while_loop(lambda state: (state[0] < qi // 2) & jnp.logical_not(state[1]),
                                       two_steps, (jnp.int32(0), jnp.bool_(False)))
    last = 2 * pairs
    one_more = jnp.logical_not(stick_gone) & (last < qi)

    @pl.when(one_more)
    def _():
        step(qi, 1)
        stage_d(qi, 1)

    @pl.when(jnp.logical_not(one_more))
    def _():
        stage_d(last, 0)

    o_ref[0] = jnp.where(first_head, acc_ref[0], acc_ref[1]).astype(o_ref.dtype)


def _sb_attn(qkv, batch, seq):
    qkv3 = qkv.reshape(batch, seq, 3 * SB_WIDTH)
    tq = ATTN_TILE
    pairs = SB_WIDTH // LANES
    return pl.pallas_call(
        _sb_attn_kernel,
        grid=(batch, pairs, seq // tq),
        in_specs=[pl.BlockSpec((1, tq, LANES), lambda b, h, i: (b, i, h)),
                  pl.BlockSpec((1, seq, LANES), lambda b, h, i: (b, 0, pairs + h)),
                  pl.BlockSpec((1, seq, LANES), lambda b, h, i: (b, 0, 2 * pairs + h))],
        out_specs=pl.BlockSpec((1, tq, LANES), lambda b, h, i: (b, i, h)),
        out_shape=jax.ShapeDtypeStruct((batch, seq, SB_WIDTH), jnp.bfloat16),
        scratch_shapes=[pltpu.VMEM((2, 2, tq, tq), jnp.float32),
                        pltpu.VMEM((2, 2, tq, tq), jnp.float32),
                        pltpu.VMEM((2, 2, tq, tq), jnp.bfloat16),
                        pltpu.VMEM((2, 2, tq, tq), jnp.bfloat16),
                        pltpu.VMEM((2, tq, LANES), jnp.float32),
                        pltpu.VMEM((2, tq, LANES), jnp.float32)],
        compiler_params=pltpu.CompilerParams(vmem_limit_bytes=VMEM_LIMIT),
        name="sb_attn",
    )(qkv3, qkv3, qkv3)


def _route(logits):
    lane = lax.broadcasted_iota(jnp.int32, logits.shape, 1)
    neg = jnp.float32(-jnp.inf)
    big = jnp.int32(LANES)
    is_group = (lane >= N_EXPERTS) & (lane < N_EXPERTS + N_EXPERT_GROUPS)
    gl = jnp.where(is_group, logits, neg)
    g_max = jnp.max(gl, axis=-1, keepdims=True)
    g_sum = jnp.sum(jnp.exp(gl - g_max), axis=-1, keepdims=True)
    g_p = 1.0 / g_sum
    g_idx = jnp.min(jnp.where(gl == g_max, lane, big), axis=-1, keepdims=True) - N_EXPERTS
    in_group = (lane >= g_idx * EXPERTS_PER_GROUP) & (lane < (g_idx + 1) * EXPERTS_PER_GROUP)
    el = jnp.where(in_group, logits, neg)
    e1 = jnp.max(el, axis=-1, keepdims=True)
    i1 = jnp.min(jnp.where(el == e1, lane, big), axis=-1, keepdims=True)
    el2 = jnp.where(lane == i1, neg, el)
    e2 = jnp.max(el2, axis=-1, keepdims=True)
    i2 = jnp.min(jnp.where(el2 == e2, lane, big), axis=-1, keepdims=True)
    r = jnp.exp(e2 - e1)
    w1 = g_p / (1.0 + r)
    w2 = w1 * r
    return i1, i2, w1, w2


ROUTE_W1, ROUTE_W2, ROUTE_E1, ROUTE_E2, ROUTE_RANK1, ROUTE_RANK2 = range(6)


def _mix_ln1_kernel(tiles_per_seq, osb_ref, p_ref, halo_ref, wpool_ref, pscale_ref, wout_ref, x_ref, mod_ref,
                    g1_ref, b1_ref, wr_ref, br_ref, x1_ref, u2_ref, route_ref, count_ref, pext_ref, running_ref):
    tm = p_ref.shape[0]
    tile_in_seq = pl.program_id(0) % tiles_per_seq
    p = p_ref[...]
    pext_ref[0:HALO, :] = jnp.where(tile_in_seq == 0, 0.0, halo_ref[...])
    pext_ref[HALO:, :] = p
    pos = tile_in_seq * tm + lax.broadcasted_iota(jnp.int32, (tm, 1), 0)

    mixed = jnp.dot(osb_ref[...], wout_ref[0:SB_WIDTH, :], preferred_element_type=jnp.float32)
    for g, w in enumerate(POOL_WINDOWS):
        cols = slice(g * POOL_GROUP_DIM, (g + 1) * POOL_GROUP_DIM)
        win = p[:, cols]
        for i in range(1, w):
            win = win + pext_ref[HALO - i:HALO - i + tm, cols]
        count = jnp.minimum(pos + 1, w).astype(jnp.float32)
        pooled = win / count - p[:, cols]
        o_pool = jnp.dot(pooled.astype(jnp.bfloat16), wpool_ref[g], preferred_element_type=jnp.float32)
        o_pool = o_pool * pscale_ref[:, cols]
        mixed = mixed + jnp.dot(o_pool.astype(jnp.bfloat16),
                                wout_ref[SB_WIDTH + g * POOL_GROUP_DIM:SB_WIDTH + (g + 1) * POOL_GROUP_DIM, :],
                                preferred_element_type=jnp.float32)

    gate1 = mod_ref[0, 2:3, :]
    shift2 = mod_ref[0, 3:4, :]
    scale2 = mod_ref[0, 4:5, :]
    x1 = _ln(DEEPNORM_ALPHA * x_ref[...] + (1.0 + gate1) * mixed) * g1_ref[...] + b1_ref[...]
    x1_ref[...] = x1
    u2 = _ln(x1) * (1.0 + scale2) + shift2
    u2_ref[...] = u2
    logits = jnp.dot(u2, wr_ref[...], preferred_element_type=jnp.float32,
                     precision=lax.Precision.HIGHEST) + br_ref[...]
    i1, i2, w1, w2 = _route(logits)

    @pl.when(pl.program_id(0) == 0)
    def _():
        running_ref[...] = jnp.zeros_like(running_ref)

    lane = lax.broadcasted_iota(jnp.int32, (tm, LANES), 1)
    first, second = lane == i1, lane == i2
    uses = jnp.where(first | second, 1.0, 0.0)
    earlier = (lax.broadcasted_iota(jnp.int32, (tm, tm), 1)
               < lax.broadcasted_iota(jnp.int32, (tm, tm), 0)).astype(jnp.bfloat16)
    before = jnp.dot(earlier, uses.astype(jnp.bfloat16), preferred_element_type=jnp.float32)
    before = before + running_ref[0:1, :]
    rank1 = jnp.sum(jnp.where(first, before, 0.0), axis=-1, keepdims=True)
    rank2 = jnp.sum(jnp.where(second, before, 0.0), axis=-1, keepdims=True)
    running = running_ref[0:1, :] + jnp.sum(uses, axis=0, keepdims=True)
    running_ref[...] = jnp.broadcast_to(running, running_ref.shape)
    count_ref[...] = jnp.broadcast_to(running, count_ref.shape)

    record = jnp.zeros((tm, LANES), jnp.float32)
    for slot, value in ((ROUTE_W1, w1), (ROUTE_W2, w2), (ROUTE_E1, i1.astype(jnp.float32)),
                        (ROUTE_E2, i2.astype(jnp.float32)), (ROUTE_RANK1, rank1), (ROUTE_RANK2, rank2)):
        record = jnp.where(lane == slot, value, record)
    route_ref[...] = record


def _mix_ln1(o_sb, p, w_pool_bf16, pool_scale, w_out_bf16, x2d, mod, ln1_g, ln1_b, w_router, b_router, seq):
    t, d = x2d.shape
    tm = ROW_TILE
    tiles_per_seq = seq // tm
    halo_blocks_per_tile = tm // HALO
    row = lambda i: (i, 0)
    const2 = lambda i: (0, 0)
    return pl.pallas_call(
        functools.partial(_mix_ln1_kernel, tiles_per_seq),
        grid=(t // tm,),
        in_specs=[pl.BlockSpec((tm, SB_WIDTH), row),
                  pl.BlockSpec((tm, POOL_WIDTH), row),
                  pl.BlockSpec((HALO, POOL_WIDTH), lambda i: (jnp.maximum(i * halo_blocks_per_tile - 1, 0), 0)),
                  pl.BlockSpec(w_pool_bf16.shape, lambda i: (0, 0, 0)),
                  pl.BlockSpec((1, POOL_WIDTH), const2),
                  pl.BlockSpec(w_out_bf16.shape, const2),
                  pl.BlockSpec((tm, d), row),
                  pl.BlockSpec((1, N_MOD, d), lambda i: (i // tiles_per_seq, 0, 0)),
                  pl.BlockSpec((1, d), const2),
                  pl.BlockSpec((1, d), const2),
                  pl.BlockSpec((d, LANES), const2),
                  pl.BlockSpec((1, LANES), const2)],
        out_specs=[pl.BlockSpec((tm, d), row),
                   pl.BlockSpec((tm, d), row),
                   pl.BlockSpec((tm, LANES), row),
                   pl.BlockSpec((SUBLANES, LANES), const2)],
        out_shape=[jax.ShapeDtypeStruct((t, d), jnp.float32),
                   jax.ShapeDtypeStruct((t, d), jnp.float32),
                   jax.ShapeDtypeStruct((t, LANES), jnp.float32),
                   jax.ShapeDtypeStruct((SUBLANES, LANES), jnp.float32)],
        scratch_shapes=[pltpu.VMEM((HALO + tm, POOL_WIDTH), jnp.float32),
                        pltpu.VMEM((SUBLANES, LANES), jnp.float32)],
        compiler_params=pltpu.CompilerParams(vmem_limit_bytes=VMEM_LIMIT,
                                             dimension_semantics=("arbitrary",)),
        name="mix_ln1",
    )(o_sb, p, p, w_pool_bf16, pool_scale, w_out_bf16, x2d, mod, ln1_g, ln1_b, w_router, b_router)


def _dispatch_kernel(pos_ref, u_ref, init_hbm, sorted_hbm, sem):
    del init_hbm
    tm = u_ref.shape[0]

    def body(r, _):
        row = u_ref.at[pl.ds(r, 1), :]
        pltpu.make_async_copy(row, sorted_hbm.at[pl.ds(pos_ref[0, 0, r], 1), :], sem).start()
        pltpu.make_async_copy(row, sorted_hbm.at[pl.ds(pos_ref[0, 0, tm + r], 1), :], sem).start()
        return 0

    lax.fori_loop(0, tm, body, 0, unroll=GATHER_UNROLL // 2)
    for _ in range(2):
        pltpu.make_async_copy(u_ref, sorted_hbm.at[pl.ds(0, tm), :], sem).wait()


def _dispatch(pos, u2, n_rows):
    t, d = u2.shape
    tm = pos.shape[2] // 2
    return pl.pallas_call(
        _dispatch_kernel,
        grid=(t // tm,),
        in_specs=[pl.BlockSpec((1, 1, 2 * tm), lambda i: (i, 0, 0), memory_space=pltpu.SMEM),
                  pl.BlockSpec((tm, d), lambda i: (i, 0)),
                  pl.BlockSpec(memory_space=pl.ANY)],
        out_specs=pl.BlockSpec(memory_space=pl.ANY),
        out_shape=jax.ShapeDtypeStruct((n_rows, d), jnp.float32),
        scratch_shapes=[pltpu.SemaphoreType.DMA(())],
        input_output_aliases={2: 0},
        compiler_params=pltpu.CompilerParams(vmem_limit_bytes=VMEM_LIMIT,
                                             dimension_semantics=("arbitrary",)),
        name="dispatch",
    )(pos, u2, jnp.zeros((n_rows, d), jnp.float32))


def _moe_grouped_kernel(tile_expert_ref, n_tiles_ref, u_ref, wg_ref, wu_ref, wd_ref, y_ref):
    del tile_expert_ref
    i = pl.program_id(0)

    @pl.when(i < n_tiles_ref[0])
    def _():
        u = u_ref[...].astype(jnp.bfloat16)
        gate = jnp.dot(u, wg_ref[0], preferred_element_type=jnp.float32)
        up = jnp.dot(u, wu_ref[0], preferred_element_type=jnp.float32)
        h = gate * jax.nn.sigmoid(gate) * up
        y_ref[...] = jnp.dot(h.astype(jnp.bfloat16), wd_ref[0], preferred_element_type=jnp.float32)

    @pl.when(i >= n_tiles_ref[0])
    def _():
        y_ref[...] = jnp.zeros_like(y_ref)


def _moe_grouped(tile_expert, n_tiles, u_sorted, wg, wu, wd):
    n_rows, d = u_sorted.shape
    tm = MOE_ROW_TILE
    expert = lambda i, te, nt: (te[i], 0, 0)
    row = lambda i, te, nt: (i, 0)
    grid_spec = pltpu.PrefetchScalarGridSpec(
        num_scalar_prefetch=2,
        grid=(n_rows // tm,),
        in_specs=[pl.BlockSpec((tm, d), row),
                  pl.BlockSpec((1, d, EXPERT_HIDDEN), expert),
                  pl.BlockSpec((1, d, EXPERT_HIDDEN), expert),
                  pl.BlockSpec((1, EXPERT_HIDDEN, d), expert)],
        out_specs=pl.BlockSpec((tm, d), row),
    )
    return pl.pallas_call(
        _moe_grouped_kernel,
        grid_spec=grid_spec,
        out_shape=jax.ShapeDtypeStruct((n_rows, d), jnp.float32),
        compiler_params=pltpu.CompilerParams(vmem_limit_bytes=VMEM_LIMIT,
                                             dimension_semantics=("arbitrary",)),
        name="moe_grouped",
    )(tile_expert, n_tiles, u_sorted, wg, wu, wd)


def _combine_ln2_kernel(pos_ref, pos_next_ref, y_hbm, route_ref, x1_ref, mod_ref, g2_ref, b2_ref, o_ref,
                        buf_ref, sem_ref):
    i = pl.program_id(0)
    n = pl.num_programs(0)
    tm = o_ref.shape[0]
    rows = 2 * tm

    def start_gather(rows_ref, slot):
        def body(r, _):
            pltpu.make_async_copy(y_hbm.at[pl.ds(rows_ref[0, 0, r], 1), :], buf_ref.at[slot, pl.ds(r, 1), :],
                                  sem_ref.at[slot]).start()
            return 0
        lax.fori_loop(0, rows, body, 0, unroll=GATHER_UNROLL)

    @pl.when(i == 0)
    def _():
        start_gather(pos_ref, 0)

    @pl.when(i + 1 < n)
    def _():
        start_gather(pos_next_ref, (i + 1) % 2)

    slot = i % 2
    pltpu.make_async_copy(y_hbm.at[pl.ds(0, rows), :], buf_ref.at[slot], sem_ref.at[slot]).wait()
    route = route_ref[...]
    w1 = route[:, ROUTE_W1:ROUTE_W1 + 1]
    w2 = route[:, ROUTE_W2:ROUTE_W2 + 1]
    y = w1 * buf_ref[slot, 0:tm, :] + w2 * buf_ref[slot, tm:, :]
    gate2 = mod_ref[0, 5:6, :]
    o_ref[...] = _ln(DEEPNORM_ALPHA * x1_ref[...] + (1.0 + gate2) * y) * g2_ref[...] + b2_ref[...]


def _combine_ln2(pos, y_sorted, route, x1, mod, ln2_g, ln2_b, seq):
    t, d = x1.shape
    tm = COMBINE_ROW_TILE
    n = t // tm
    tiles_per_seq = seq // tm
    row = lambda i: (i, 0)
    const2 = lambda i: (0, 0)
    return pl.pallas_call(
        _combine_ln2_kernel,
        grid=(n,),
        in_specs=[pl.BlockSpec((1, 1, 2 * tm), lambda i: (i, 0, 0), memory_space=pltpu.SMEM),
                  pl.BlockSpec((1, 1, 2 * tm), lambda i: (jnp.minimum(i + 1, n - 1), 0, 0),
                               memory_space=pltpu.SMEM),
                  pl.BlockSpec(memory_space=pl.ANY),
                  pl.BlockSpec((tm, LANES), row),
                  pl.BlockSpec((tm, d), row),
                  pl.BlockSpec((1, N_MOD, d), lambda i: (i // tiles_per_seq, 0, 0)),
                  pl.BlockSpec((1, d), const2),
                  pl.BlockSpec((1, d), const2)],
        out_specs=pl.BlockSpec((tm, d), row),
        out_shape=jax.ShapeDtypeStruct((t, d), jnp.float32),
        scratch_shapes=[pltpu.VMEM((2, 2 * tm, d), jnp.float32),
                        pltpu.SemaphoreType.DMA((2,))],
        compiler_params=pltpu.CompilerParams(vmem_limit_bytes=VMEM_LIMIT,
                                             dimension_semantics=("arbitrary",)),
        name="combine_ln2",
    )(pos, pos, y_sorted, route, x1, mod, ln2_g, ln2_b)


def _dispatch_plan(route, counts):
    t = route.shape[0]
    tm = MOE_ROW_TILE
    max_tiles = 2 * t // tm + N_EXPERTS
    counts = counts[0, :N_EXPERTS].astype(jnp.int32)
    tiles = (counts + tm - 1) // tm
    tile_end = jnp.cumsum(tiles)
    row_start = (tile_end - tiles) * tm
    experts = route[:, ROUTE_E1:ROUTE_E2 + 1].astype(jnp.int32)
    ranks = route[:, ROUTE_RANK1:ROUTE_RANK2 + 1].astype(jnp.int32)
    pos = row_start[experts] + ranks
    tile_ids = jnp.arange(max_tiles, dtype=jnp.int32)
    tile_expert = jnp.minimum(jnp.sum(tile_end[None, :] <= tile_ids[:, None], axis=1), N_EXPERTS - 1)
    by_tile = lambda tc: pos.reshape(t // tc, tc, 2).transpose(0, 2, 1).reshape(t // tc, 1, 2 * tc)
    return (tile_expert.astype(jnp.int32), tile_end[-1:].astype(jnp.int32), by_tile(DISPATCH_ROW_TILE),
            by_tile(COMBINE_ROW_TILE), max_tiles * tm)


def kernel(x, c, w_ada, b_ada, w_in, w_pool, pool_scale, w_out, ln1_g, ln1_b, w_router_group, b_router_group,
           w_router_expert, b_router_expert, w_gate, w_up, w_down, ln2_g, ln2_b):
    b, s, d = x.shape
    bf16 = jnp.bfloat16
    for layer in range(DEPTH):
        c_pad = jnp.pad(c, ((0, SUBLANES - b), (0, 0)))
        mod = _adaln(c_pad, w_ada[layer], b_ada[layer][None, :])[:b].reshape(b, N_MOD, d)
        x2d = x.reshape(b * s, d)
        qkv, p = _ln_inproj(x2d, mod, w_in[layer].astype(bf16), s)
        o_sb = _sb_attn(qkv, b, s).reshape(b * s, SB_WIDTH)
        pad = LANES - N_EXPERTS - N_EXPERT_GROUPS
        w_router = jnp.pad(jnp.concatenate([w_router_expert[layer], w_router_group[layer]], axis=1),
                           ((0, 0), (0, pad)))
        b_router = jnp.pad(jnp.concatenate([b_router_expert[layer], b_router_group[layer]]), (0, pad))[None, :]
        x1, u2, route, counts = _mix_ln1(o_sb, p, w_pool[layer].astype(bf16), pool_scale[layer][None, :],
                                         w_out[layer].astype(bf16), x2d, mod, ln1_g[layer][None, :],
                                         ln1_b[layer][None, :], w_router, b_router, s)
        tile_expert, n_tiles, pos_dispatch, pos, n_rows = _dispatch_plan(route, counts)
        u_sorted = _dispatch(pos_dispatch, u2, n_rows)
        y_sorted = _moe_grouped(tile_expert, n_tiles, u_sorted, w_gate[layer].astype(bf16),
                                w_up[layer].astype(bf16), w_down[layer].astype(bf16))
        x2 = _combine_ln2(pos, y_sorted, route, x1, mod, ln2_g[layer][None, :], ln2_b[layer][None, :], s)
        x = x2.reshape(b, s, d)
    return x
```

```python
import functools

import jax
import jax.numpy as jnp
from jax import lax
from jax.experimental import pallas as pl
from jax.experimental.pallas import tpu as pltpu

D_MODEL = 1024
N_SB_HEADS = 8
SB_HEAD_DIM = 64
SB_WIDTH = N_SB_HEADS * SB_HEAD_DIM
POOL_WINDOWS = (2, 4, 8, 16)
POOL_GROUP_DIM = 128
POOL_WIDTH = len(POOL_WINDOWS) * POOL_GROUP_DIM
N_EXPERT_GROUPS = 4
EXPERTS_PER_GROUP = 4
N_EXPERTS = N_EXPERT_GROUPS * EXPERTS_PER_GROUP
EXPERT_HIDDEN = 512
DEPTH = 1
DEEPNORM_ALPHA = (2.0 * DEPTH) ** 0.25
LN_EPS = 1e-5
N_MOD = 6

LANES = 128
SUBLANES = 8
HALO = max(POOL_WINDOWS)
VMEM_LIMIT = 56 * 1024 * 1024

ROW_TILE = 512
ATTN_TILE = 256
STICK_GONE_LOG2 = -180.0
MOE_ROW_TILE = 512
DISPATCH_ROW_TILE = 1024
COMBINE_ROW_TILE = 256
GATHER_UNROLL = 8


def _ln(x):
    mu = jnp.mean(x, axis=-1, keepdims=True)
    xc = x - mu
    var = jnp.mean(xc * xc, axis=-1, keepdims=True)
    return xc * lax.rsqrt(var + LN_EPS)


def _adaln_kernel(c_ref, w_ref, b_ref, o_ref):
    c = c_ref[...]
    a = c * jax.nn.sigmoid(c)
    o_ref[...] = jnp.dot(a, w_ref[...], preferred_element_type=jnp.float32,
                         precision=lax.Precision.HIGHEST) + b_ref[...]


def _adaln(c_pad, w_ada, b_ada):
    rows, d = c_pad.shape
    n = w_ada.shape[1]
    tn = 1024
    return pl.pallas_call(
        _adaln_kernel,
        grid=(n // tn,),
        in_specs=[pl.BlockSpec((rows, d), lambda j: (0, 0)),
                  pl.BlockSpec((d, tn), lambda j: (0, j)),
                  pl.BlockSpec((1, tn), lambda j: (0, j))],
        out_specs=pl.BlockSpec((rows, tn), lambda j: (0, j)),
        out_shape=jax.ShapeDtypeStruct((rows, n), jnp.float32),
        compiler_params=pltpu.CompilerParams(vmem_limit_bytes=VMEM_LIMIT),
        name="adaln",
    )(c_pad, w_ada, b_ada)


def _ln_inproj_kernel(x_ref, mod_ref, w_ref, qkv_ref, p_ref):
    shift = mod_ref[0, 0:1, :]
    scale = mod_ref[0, 1:2, :]
    u = (_ln(x_ref[...]) * (1.0 + scale) + shift).astype(jnp.bfloat16)
    qk_scale = SB_HEAD_DIM ** -0.5 * 1.4426950408889634
    q = jnp.dot(u, w_ref[:, 0:SB_WIDTH], preferred_element_type=jnp.float32) * qk_scale
    qkv_ref[:, 0:SB_WIDTH] = q.astype(jnp.bfloat16)
    for j in (1, 2):
        kv = jnp.dot(u, w_ref[:, j * SB_WIDTH:(j + 1) * SB_WIDTH], preferred_element_type=jnp.float32)
        qkv_ref[:, j * SB_WIDTH:(j + 1) * SB_WIDTH] = kv.astype(jnp.bfloat16)
    p_ref[...] = jnp.dot(u, w_ref[:, 3 * SB_WIDTH:], preferred_element_type=jnp.float32)


def _ln_inproj(x2d, mod, w_in_bf16, seq):
    t, d = x2d.shape
    tm = ROW_TILE
    tiles_per_seq = seq // tm
    return pl.pallas_call(
        _ln_inproj_kernel,
        grid=(t // tm,),
        in_specs=[pl.BlockSpec((tm, d), lambda i: (i, 0)),
                  pl.BlockSpec((1, N_MOD, d), lambda i: (i // tiles_per_seq, 0, 0)),
                  pl.BlockSpec(w_in_bf16.shape, lambda i: (0, 0))],
        out_specs=[pl.BlockSpec((tm, 3 * SB_WIDTH), lambda i: (i, 0)),
                   pl.BlockSpec((tm, POOL_WIDTH), lambda i: (i, 0))],
        out_shape=[jax.ShapeDtypeStruct((t, 3 * SB_WIDTH), jnp.bfloat16),
                   jax.ShapeDtypeStruct((t, POOL_WIDTH), jnp.float32)],
        compiler_params=pltpu.CompilerParams(vmem_limit_bytes=VMEM_LIMIT),
        name="ln_inproj",
    )(x2d, mod, w_in_bf16)


def _sb_attn_kernel(q_ref, k_ref, v_ref, o_ref, z_ref, zfix_ref, nb_ref, att_ref, acc_ref, carry_ref):
    tq = tk = ATTN_TILE
    heads = (0, 1)
    qi = pl.program_id(2)
    q2 = q_ref[0]
    lane = lax.broadcasted_iota(jnp.int32, (tq, LANES), 1)
    first_head = lane < SB_HEAD_DIM
    zero = jnp.zeros_like(q2)
    q_heads = (jnp.where(first_head, q2, zero), jnp.where(first_head, zero, q2))

    row = lax.broadcasted_iota(jnp.int32, (tk, tk), 0)
    col = lax.broadcasted_iota(jnp.int32, (tk, tk), 1)
    neg_suffix = jnp.where(row >= col, -1.0, 0.0).astype(jnp.bfloat16)
    causal = col < row

    def key_start(j):
        return pl.multiple_of(jnp.maximum(qi - j, 0) * tk, tk)

    def stage_a(j, par):
        k_blk = k_ref[0, pl.ds(key_start(j), tk), :]
        for h in heads:
            z_ref[par, h] = lax.dot_general(q_heads[h], k_blk, (((1,), (1,)), ((), ())),
                                            preferred_element_type=jnp.float32)

    def stage_b(par, diagonal):
        for h in heads:
            z = z_ref[par, h]
            n = jnp.maximum(z, 0.0) + jnp.log2(1.0 + jnp.exp2(-jnp.abs(z)))
            if diagonal:
                n = jnp.where(causal, n, 0.0)
            n_bf16 = n.astype(jnp.bfloat16)
            nb_ref[par, h] = n_bf16
            zfix_ref[par, h] = z - (n - n_bf16.astype(jnp.float32))

    def stage_c(par, diagonal):
        incls = [jnp.dot(nb_ref[par, h], neg_suffix, preferred_element_type=jnp.float32)
                 for h in heads]
        for h in heads:
            carry = carry_ref[h]
            att = jnp.exp2(zfix_ref[par, h] + incls[h] + jnp.concatenate([carry] * (tk // LANES), axis=1))
            if diagonal:
                att = jnp.where(causal, att, 0.0)
            att_ref[par, h] = att.astype(jnp.bfloat16)
            carry_ref[h] = carry + jnp.broadcast_to(incls[h][:, 0:1], (tq, LANES))

    def stage_d(j, par):
        v_blk = v_ref[0, pl.ds(key_start(j), tk), :]
        for h in heads:
            acc_ref[h] += jnp.dot(att_ref[par, h], v_blk, preferred_element_type=jnp.float32)

    def step(i, par):
        stage_b(1 - par, diagonal=False)
        stage_c(par, diagonal=False)
        stage_a(i + 2, par)
        stage_d(i - 1, 1 - par)

    carry_ref[...] = jnp.zeros_like(carry_ref)
    acc_ref[...] = jnp.zeros_like(acc_ref)
    stage_a(0, 0)
    stage_b(0, diagonal=True)
    stage_a(1, 1)
    stage_c(0, diagonal=True)
    stage_a(2, 0)
    stage_b(1, diagonal=False)

    def two_steps(state):
        m, _ = state
        step(2 * m + 1, 1)
        step(2 * m + 2, 0)
        return m + 1, jnp.max(carry_ref[...]) < STICK_GONE_LOG2

    pairs, stick_gone = lax.while_loop(lambda state: (state[0] < qi // 2) & jnp.logical_not(state[1]),
                                       two_steps, (jnp.int32(0), jnp.bool_(False)))
    last = 2 * pairs
    one_more = jnp.logical_not(stick_gone) & (last < qi)

    @pl.when(one_more)
    def _():
        step(qi, 1)
        stage_d(qi, 1)

    @pl.when(jnp.logical_not(one_more))
    def _():
        stage_d(last, 0)

    o_ref[0] = jnp.where(first_head, acc_ref[0], acc_ref[1]).astype(o_ref.dtype)


def _sb_attn(qkv, batch, seq):
    qkv3 = qkv.reshape(batch, seq, 3 * SB_WIDTH)
    tq = ATTN_TILE
    pairs = SB_WIDTH // LANES
    return pl.pallas_call(
        _sb_attn_kernel,
        grid=(batch, pairs, seq // tq),
        in_specs=[pl.BlockSpec((1, tq, LANES), lambda b, h, i: (b, i, h)),
                  pl.BlockSpec((1, seq, LANES), lambda b, h, i: (b, 0, pairs + h)),
                  pl.BlockSpec((1, seq, LANES), lambda b, h, i: (b, 0, 2 * pairs + h))],
        out_specs=pl.BlockSpec((1, tq, LANES), lambda b, h, i: (b, i, h)),
        out_shape=jax.ShapeDtypeStruct((batch, seq, SB_WIDTH), jnp.bfloat16),
        scratch_shapes=[pltpu.VMEM((2, 2, tq, tq), jnp.float32),
                        pltpu.VMEM((2, 2, tq, tq), jnp.float32),
                        pltpu.VMEM((2, 2, tq, tq), jnp.bfloat16),
                        pltpu.VMEM((2, 2, tq, tq), jnp.bfloat16),
                        pltpu.VMEM((2, tq, LANES), jnp.float32),
                        pltpu.VMEM((2, tq, LANES), jnp.float32)],
        compiler_params=pltpu.CompilerParams(vmem_limit_bytes=VMEM_LIMIT),
        name="sb_attn",
    )(qkv3, qkv3, qkv3)


def _route(logits):
    lane = lax.broadcasted_iota(jnp.int32, logits.shape, 1)
    neg = jnp.float32(-jnp.inf)
    big = jnp.int32(LANES)
    is_group = (lane >= N_EXPERTS) & (lane < N_EXPERTS + N_EXPERT_GROUPS)
    gl = jnp.where(is_group, logits, neg)
    g_max = jnp.max(gl, axis=-1, keepdims=True)
    g_sum = jnp.sum(jnp.exp(gl - g_max), axis=-1, keepdims=True)
    g_p = 1.0 / g_sum
    g_idx = jnp.min(jnp.where(gl == g_max, lane, big), axis=-1, keepdims=True) - N_EXPERTS
    in_group = (lane >= g_idx * EXPERTS_PER_GROUP) & (lane < (g_idx + 1) * EXPERTS_PER_GROUP)
    el = jnp.where(in_group, logits, neg)
    e1 = jnp.max(el, axis=-1, keepdims=True)
    i1 = jnp.min(jnp.where(el == e1, lane, big), axis=-1, keepdims=True)
    el2 = jnp.where(lane == i1, neg, el)
    e2 = jnp.max(el2, axis=-1, keepdims=True)
    i2 = jnp.min(jnp.where(el2 == e2, lane, big), axis=-1, keepdims=True)
    r = jnp.exp(e2 - e1)
    w1 = g_p / (1.0 + r)
    w2 = w1 * r
    return i1, i2, w1, w2


ROUTE_W1, ROUTE_W2, ROUTE_E1, ROUTE_E2, ROUTE_RANK1, ROUTE_RANK2 = range(6)


def _mix_ln1_kernel(tiles_per_seq, osb_ref, p_ref, halo_ref, wpool_ref, pscale_ref, wout_ref, x_ref, mod_ref,
                    g1_ref, b1_ref, wr_ref, br_ref, earlier_ref, x1_ref, u2_ref, route_ref, count_ref, pext_ref,
                    running_ref):
    tm = p_ref.shape[0]
    tile_in_seq = pl.program_id(0) % tiles_per_seq
    p = p_ref[...]
    pext_ref[0:HALO, :] = jnp.where(tile_in_seq == 0, 0.0, halo_ref[...])
    pext_ref[HALO:, :] = p
    pos = tile_in_seq * tm + lax.broadcasted_iota(jnp.int32, (tm, 1), 0)

    mixer_out = [osb_ref[...]]
    for g, w in enumerate(POOL_WINDOWS):
        cols = slice(g * POOL_GROUP_DIM, (g + 1) * POOL_GROUP_DIM)
        win = p[:, cols]
        for i in range(1, w):
            win = win + pext_ref[HALO - i:HALO - i + tm, cols]
        count = jnp.minimum(pos + 1, w).astype(jnp.float32)
        pooled = win / count - p[:, cols]
        o_pool = jnp.dot(pooled.astype(jnp.bfloat16), wpool_ref[g], preferred_element_type=jnp.float32)
        mixer_out.append((o_pool * pscale_ref[:, cols]).astype(jnp.bfloat16))
    mixed = jnp.dot(jnp.concatenate(mixer_out, axis=1), wout_ref[...], preferred_element_type=jnp.float32)

    gate1 = mod_ref[0, 2:3, :]
    shift2 = mod_ref[0, 3:4, :]
    scale2 = mod_ref[0, 4:5, :]
    x1 = _ln(DEEPNORM_ALPHA * x_ref[...] + (1.0 + gate1) * mixed) * g1_ref[...] + b1_ref[...]
    x1_ref[...] = x1
    u2 = _ln(x1) * (1.0 + scale2) + shift2
    u2_ref[...] = u2
    u2_hi = u2.astype(jnp.bfloat16)
    u2_lo = (u2 - u2_hi.astype(jnp.float32)).astype(jnp.bfloat16)
    logits = jnp.dot(jnp.concatenate([u2_hi, u2_lo, u2_hi], axis=1), wr_ref[...],
                     preferred_element_type=jnp.float32) + br_ref[...]
    i1, i2, w1, w2 = _route(logits)

    @pl.when(pl.program_id(0) == 0)
    def _():
        running_ref[...] = jnp.zeros_like(running_ref)

    lane = lax.broadcasted_iota(jnp.int32, (tm, LANES), 1)
    first, second = lane == i1, lane == i2
    uses = jnp.where(first | second, 1.0, 0.0)
    before = jnp.dot(earlier_ref[...], uses.astype(jnp.bfloat16), preferred_element_type=jnp.float32)
    before = before + running_ref[0:1, :]
    rank1 = jnp.sum(jnp.where(first, before, 0.0), axis=-1, keepdims=True)
    rank2 = jnp.sum(jnp.where(second, before, 0.0), axis=-1, keepdims=True)
    running = running_ref[0:1, :] + jnp.sum(uses, axis=0, keepdims=True)
    running_ref[...] = jnp.broadcast_to(running, running_ref.shape)
    count_ref[...] = jnp.broadcast_to(running, count_ref.shape)

    record = jnp.zeros((tm, LANES), jnp.float32)
    for slot, value in ((ROUTE_W1, w1), (ROUTE_W2, w2), (ROUTE_E1, i1.astype(jnp.float32)),
                        (ROUTE_E2, i2.astype(jnp.float32)), (ROUTE_RANK1, rank1), (ROUTE_RANK2, rank2)):
        record = jnp.where(lane == slot, value, record)
    route_ref[...] = record


def _mix_ln1(o_sb, p, w_pool_bf16, pool_scale, w_out_bf16, x2d, mod, ln1_g, ln1_b, w_router, b_router, seq):
    t, d = x2d.shape
    tm = ROW_TILE
    tiles_per_seq = seq // tm
    halo_blocks_per_tile = tm // HALO
    row = lambda i: (i, 0)
    const2 = lambda i: (0, 0)
    return pl.pallas_call(
        functools.partial(_mix_ln1_kernel, tiles_per_seq),
        grid=(t // tm,),
        in_specs=[pl.BlockSpec((tm, SB_WIDTH), row),
                  pl.BlockSpec((tm, POOL_WIDTH), row),
                  pl.BlockSpec((HALO, POOL_WIDTH), lambda i: (jnp.maximum(i * halo_blocks_per_tile - 1, 0), 0)),
                  pl.BlockSpec(w_pool_bf16.shape, lambda i: (0, 0, 0)),
                  pl.BlockSpec((1, POOL_WIDTH), const2),
                  pl.BlockSpec(w_out_bf16.shape, const2),
                  pl.BlockSpec((tm, d), row),
                  pl.BlockSpec((1, N_MOD, d), lambda i: (i // tiles_per_seq, 0, 0)),
                  pl.BlockSpec((1, d), const2),
                  pl.BlockSpec((1, d), const2),
                  pl.BlockSpec(w_router.shape, const2),
                  pl.BlockSpec((1, LANES), const2),
                  pl.BlockSpec((tm, tm), const2)],
        out_specs=[pl.BlockSpec((tm, d), row),
                   pl.BlockSpec((tm, d), row),
                   pl.BlockSpec((tm, LANES), row),
                   pl.BlockSpec((SUBLANES, LANES), const2)],
        out_shape=[jax.ShapeDtypeStruct((t, d), jnp.float32),
                   jax.ShapeDtypeStruct((t, d), jnp.float32),
                   jax.ShapeDtypeStruct((t, LANES), jnp.float32),
                   jax.ShapeDtypeStruct((SUBLANES, LANES), jnp.float32)],
        scratch_shapes=[pltpu.VMEM((HALO + tm, POOL_WIDTH), jnp.float32),
                        pltpu.VMEM((SUBLANES, LANES), jnp.float32)],
        compiler_params=pltpu.CompilerParams(vmem_limit_bytes=VMEM_LIMIT,
                                             dimension_semantics=("arbitrary",)),
        name="mix_ln1",
    )(o_sb, p, p, w_pool_bf16, pool_scale, w_out_bf16, x2d, mod, ln1_g, ln1_b, w_router, b_router,
      jnp.tri(tm, tm, -1, dtype=jnp.bfloat16))


def _dispatch_kernel(pos_ref, u_ref, init_hbm, sorted_hbm, sem):
    del init_hbm
    tm = u_ref.shape[0]

    def body(r, _):
        row = u_ref.at[pl.ds(r, 1), :]
        pltpu.make_async_copy(row, sorted_hbm.at[pl.ds(pos_ref[0, 0, r], 1), :], sem).start()
        pltpu.make_async_copy(row, sorted_hbm.at[pl.ds(pos_ref[0, 0, tm + r], 1), :], sem).start()
        return 0

    lax.fori_loop(0, tm, body, 0, unroll=GATHER_UNROLL // 2)
    for _ in range(2):
        pltpu.make_async_copy(u_ref, sorted_hbm.at[pl.ds(0, tm), :], sem).wait()


def _dispatch(pos, u2, n_rows):
    t, d = u2.shape
    tm = pos.shape[2] // 2
    return pl.pallas_call(
        _dispatch_kernel,
        grid=(t // tm,),
        in_specs=[pl.BlockSpec((1, 1, 2 * tm), lambda i: (i, 0, 0), memory_space=pltpu.SMEM),
                  pl.BlockSpec((tm, d), lambda i: (i, 0)),
                  pl.BlockSpec(memory_space=pl.ANY)],
        out_specs=pl.BlockSpec(memory_space=pl.ANY),
        out_shape=jax.ShapeDtypeStruct((n_rows, d), jnp.float32),
        scratch_shapes=[pltpu.SemaphoreType.DMA(())],
        input_output_aliases={2: 0},
        compiler_params=pltpu.CompilerParams(vmem_limit_bytes=VMEM_LIMIT,
                                             dimension_semantics=("arbitrary",)),
        name="dispatch",
    )(pos, u2, jnp.zeros((n_rows, d), jnp.float32))


def _moe_grouped_kernel(tile_expert_ref, n_tiles_ref, u_ref, wg_ref, wu_ref, wd_ref, y_ref):
    del tile_expert_ref
    i = pl.program_id(0)

    @pl.when(i < n_tiles_ref[0])
    def _():
        bf16 = jnp.bfloat16
        u = u_ref[...].astype(bf16)
        gate = jnp.dot(u, wg_ref[0].astype(bf16), preferred_element_type=jnp.float32)
        up = jnp.dot(u, wu_ref[0].astype(bf16), preferred_element_type=jnp.float32)
        h = gate * jax.nn.sigmoid(gate) * up
        y_ref[...] = jnp.dot(h.astype(bf16), wd_ref[0].astype(bf16), preferred_element_type=jnp.float32)

    @pl.when(i >= n_tiles_ref[0])
    def _():
        y_ref[...] = jnp.zeros_like(y_ref)


def _moe_grouped(tile_expert, n_tiles, u_sorted, wg, wu, wd):
    n_rows, d = u_sorted.shape
    tm = MOE_ROW_TILE
    expert = lambda i, te, nt: (te[i], 0, 0)
    row = lambda i, te, nt: (i, 0)
    grid_spec = pltpu.PrefetchScalarGridSpec(
        num_scalar_prefetch=2,
        grid=(n_rows // tm,),
        in_specs=[pl.BlockSpec((tm, d), row),
                  pl.BlockSpec((1, d, EXPERT_HIDDEN), expert),
                  pl.BlockSpec((1, d, EXPERT_HIDDEN), expert),
                  pl.BlockSpec((1, EXPERT_HIDDEN, d), expert)],
        out_specs=pl.BlockSpec((tm, d), row),
    )
    return pl.pallas_call(
        _moe_grouped_kernel,
        grid_spec=grid_spec,
        out_shape=jax.ShapeDtypeStruct((n_rows, d), jnp.float32),
        compiler_params=pltpu.CompilerParams(vmem_limit_bytes=VMEM_LIMIT,
                                             dimension_semantics=("arbitrary",)),
        name="moe_grouped",
    )(tile_expert, n_tiles, u_sorted, wg, wu, wd)


def _combine_ln2_kernel(pos_ref, pos_next_ref, y_hbm, route_ref, x1_ref, mod_ref, g2_ref, b2_ref, o_ref,
                        buf_ref, sem_ref):
    i = pl.program_id(0)
    n = pl.num_programs(0)
    tm = o_ref.shape[0]
    rows = 2 * tm

    def start_gather(rows_ref, slot):
        def body(r, _):
            pltpu.make_async_copy(y_hbm.at[pl.ds(rows_ref[0, 0, r], 1), :], buf_ref.at[slot, pl.ds(r, 1), :],
                                  sem_ref.at[slot]).start()
            return 0
        lax.fori_loop(0, rows, body, 0, unroll=GATHER_UNROLL)

    @pl.when(i == 0)
    def _():
        start_gather(pos_ref, 0)

    @pl.when(i + 1 < n)
    def _():
        start_gather(pos_next_ref, (i + 1) % 2)

    slot = i % 2
    pltpu.make_async_copy(y_hbm.at[pl.ds(0, rows), :], buf_ref.at[slot], sem_ref.at[slot]).wait()
    route = route_ref[...]
    w1 = route[:, ROUTE_W1:ROUTE_W1 + 1]
    w2 = route[:, ROUTE_W2:ROUTE_W2 + 1]
    y = w1 * buf_ref[slot, 0:tm, :] + w2 * buf_ref[slot, tm:, :]
    gate2 = mod_ref[0, 5:6, :]
    o_ref[...] = _ln(DEEPNORM_ALPHA * x1_ref[...] + (1.0 + gate2) * y) * g2_ref[...] + b2_ref[...]


def _combine_ln2(pos, y_sorted, route, x1, mod, ln2_g, ln2_b, seq):
    t, d = x1.shape
    tm = COMBINE_ROW_TILE
    n = t // tm
    tiles_per_seq = seq // tm
    row = lambda i: (i, 0)
    const2 = lambda i: (0, 0)
    return pl.pallas_call(
        _combine_ln2_kernel,
        grid=(n,),
        in_specs=[pl.BlockSpec((1, 1, 2 * tm), lambda i: (i, 0, 0), memory_space=pltpu.SMEM),
                  pl.BlockSpec((1, 1, 2 * tm), lambda i: (jnp.minimum(i + 1, n - 1), 0, 0),
                               memory_space=pltpu.SMEM),
                  pl.BlockSpec(memory_space=pl.ANY),
                  pl.BlockSpec((tm, LANES), row),
                  pl.BlockSpec((tm, d), row),
                  pl.BlockSpec((1, N_MOD, d), lambda i: (i // tiles_per_seq, 0, 0)),
                  pl.BlockSpec((1, d), const2),
                  pl.BlockSpec((1, d), const2)],
        out_specs=pl.BlockSpec((tm, d), row),
        out_shape=jax.ShapeDtypeStruct((t, d), jnp.float32),
        scratch_shapes=[pltpu.VMEM((2, 2 * tm, d), jnp.float32),
                        pltpu.SemaphoreType.DMA((2,))],
        compiler_params=pltpu.CompilerParams(vmem_limit_bytes=VMEM_LIMIT,
                                             dimension_semantics=("arbitrary",)),
        name="combine_ln2",
    )(pos, pos, y_sorted, route, x1, mod, ln2_g, ln2_b)


def _dispatch_plan(route, counts):
    t = route.shape[0]
    tm = MOE_ROW_TILE
    max_tiles = 2 * t // tm + N_EXPERTS
    counts = counts[0, :N_EXPERTS].astype(jnp.int32)
    tiles = (counts + tm - 1) // tm
    tile_end = jnp.cumsum(tiles)
    row_start = (tile_end - tiles) * tm
    experts = route[:, ROUTE_E1:ROUTE_E2 + 1].astype(jnp.int32)
    ranks = route[:, ROUTE_RANK1:ROUTE_RANK2 + 1].astype(jnp.int32)
    pos = row_start[experts] + ranks
    tile_ids = jnp.arange(max_tiles, dtype=jnp.int32)
    tile_expert = jnp.minimum(jnp.sum(tile_end[None, :] <= tile_ids[:, None], axis=1), N_EXPERTS - 1)
    by_tile = lambda tc: pos.reshape(t // tc, tc, 2).transpose(0, 2, 1).reshape(t // tc, 1, 2 * tc)
    return (tile_expert.astype(jnp.int32), tile_end[-1:].astype(jnp.int32), by_tile(DISPATCH_ROW_TILE),
            by_tile(COMBINE_ROW_TILE), max_tiles * tm)


def kernel(x, c, w_ada, b_ada, w_in, w_pool, pool_scale, w_out, ln1_g, ln1_b, w_router_group, b_router_group,
           w_router_expert, b_router_expert, w_gate, w_up, w_down, ln2_g, ln2_b):
    b, s, d = x.shape
    bf16 = jnp.bfloat16
    for layer in range(DEPTH):
        c_pad = jnp.pad(c, ((0, SUBLANES - b), (0, 0)))
        mod = _adaln(c_pad, w_ada[layer], b_ada[layer][None, :])[:b].reshape(b, N_MOD, d)
        x2d = x.reshape(b * s, d)
        qkv, p = _ln_inproj(x2d, mod, w_in[layer].astype(bf16), s)
        o_sb = _sb_attn(qkv, b, s).reshape(b * s, SB_WIDTH)
        pad = LANES - N_EXPERTS - N_EXPERT_GROUPS
        w_router = jnp.pad(jnp.concatenate([w_router_expert[layer], w_router_group[layer]], axis=1),
                           ((0, 0), (0, pad)))
        b_router = jnp.pad(jnp.concatenate([b_router_expert[layer], b_router_group[layer]]), (0, pad))[None, :]
        w_router_hi = w_router.astype(bf16)
        w_router_lo = (w_router - w_router_hi.astype(jnp.float32)).astype(bf16)
        w_router = jnp.concatenate([w_router_hi, w_router_hi, w_router_lo], axis=0)
        x1, u2, route, counts = _mix_ln1(o_sb, p, w_pool[layer].astype(bf16), pool_scale[layer][None, :],
                                         w_out[layer].astype(bf16), x2d, mod, ln1_g[layer][None, :],
                                         ln1_b[layer][None, :], w_router, b_router, s)
        tile_expert, n_tiles, pos_dispatch, pos, n_rows = _dispatch_plan(route, counts)
        u_sorted = _dispatch(pos_dispatch, u2, n_rows)
        y_sorted = _moe_grouped(tile_expert, n_tiles, u_sorted, w_gate[layer], w_up[layer], w_down[layer])
        x2 = _combine_ln2(pos, y_sorted, route, x1, mod, ln2_g[layer][None, :], ln2_b[layer][None, :], s)
        x = x2.reshape(b, s, d)
    return x
```

```python
import functools

import jax
import jax.numpy as jnp
from jax import lax
from jax.experimental import pallas as pl
from jax.experimental.pallas import tpu as pltpu

D_MODEL = 1024
N_SB_HEADS = 8
SB_HEAD_DIM = 64
SB_WIDTH = N_SB_HEADS * SB_HEAD_DIM
POOL_WINDOWS = (2, 4, 8, 16)
POOL_GROUP_DIM = 128
POOL_WIDTH = len(POOL_WINDOWS) * POOL_GROUP_DIM
N_EXPERT_GROUPS = 4
EXPERTS_PER_GROUP = 4
N_EXPERTS = N_EXPERT_GROUPS * EXPERTS_PER_GROUP
EXPERT_HIDDEN = 512
DEPTH = 1
DEEPNORM_ALPHA = (2.0 * DEPTH) ** 0.25
LN_EPS = 1e-5
N_MOD = 6

LANES = 128
SUBLANES = 8
HALO = max(POOL_WINDOWS)
VMEM_LIMIT = 56 * 1024 * 1024

ROW_TILE = 512
ATTN_TILE = 256
STICK_GONE_LOG2 = -180.0
MOE_ROW_TILE = 512
DISPATCH_ROW_TILE = 1024
COMBINE_ROW_TILE = 256
GATHER_UNROLL = 8


def _ln(x):
    mu = jnp.mean(x, axis=-1, keepdims=True)
    xc = x - mu
    var = jnp.mean(xc * xc, axis=-1, keepdims=True)
    return xc * lax.rsqrt(var + LN_EPS)


def _adaln_kernel(c_ref, w_ref, b_ref, o_ref):
    c = c_ref[...]
    a = c * jax.nn.sigmoid(c)
    o_ref[...] = jnp.dot(a, w_ref[...], preferred_element_type=jnp.float32,
                         precision=lax.Precision.HIGHEST) + b_ref[...]


def _adaln(c_pad, w_ada, b_ada):
    rows, d = c_pad.shape
    n = w_ada.shape[1]
    tn = 1024
    return pl.pallas_call(
        _adaln_kernel,
        grid=(n // tn,),
        in_specs=[pl.BlockSpec((rows, d), lambda j: (0, 0)),
                  pl.BlockSpec((d, tn), lambda j: (0, j)),
                  pl.BlockSpec((1, tn), lambda j: (0, j))],
        out_specs=pl.BlockSpec((rows, tn), lambda j: (0, j)),
        out_shape=jax.ShapeDtypeStruct((rows, n), jnp.float32),
        compiler_params=pltpu.CompilerParams(vmem_limit_bytes=VMEM_LIMIT),
        name="adaln",
    )(c_pad, w_ada, b_ada)


def _ln_inproj_kernel(x_ref, mod_ref, w_ref, qkv_ref, p_ref):
    shift = mod_ref[0, 0:1, :]
    scale = mod_ref[0, 1:2, :]
    u = (_ln(x_ref[...]) * (1.0 + scale) + shift).astype(jnp.bfloat16)
    qk_scale = SB_HEAD_DIM ** -0.5 * 1.4426950408889634
    q = jnp.dot(u, w_ref[:, 0:SB_WIDTH], preferred_element_type=jnp.float32) * qk_scale
    qkv_ref[:, 0:SB_WIDTH] = q.astype(jnp.bfloat16)
    for j in (1, 2):
        kv = jnp.dot(u, w_ref[:, j * SB_WIDTH:(j + 1) * SB_WIDTH], preferred_element_type=jnp.float32)
        qkv_ref[:, j * SB_WIDTH:(j + 1) * SB_WIDTH] = kv.astype(jnp.bfloat16)
    p_ref[...] = jnp.dot(u, w_ref[:, 3 * SB_WIDTH:], preferred_element_type=jnp.float32)


def _ln_inproj(x2d, mod, w_in_bf16, seq):
    t, d = x2d.shape
    tm = ROW_TILE
    tiles_per_seq = seq // tm
    return pl.pallas_call(
        _ln_inproj_kernel,
        grid=(t // tm,),
        in_specs=[pl.BlockSpec((tm, d), lambda i: (i, 0)),
                  pl.BlockSpec((1, N_MOD, d), lambda i: (i // tiles_per_seq, 0, 0)),
                  pl.BlockSpec(w_in_bf16.shape, lambda i: (0, 0))],
        out_specs=[pl.BlockSpec((tm, 3 * SB_WIDTH), lambda i: (i, 0)),
                   pl.BlockSpec((tm, POOL_WIDTH), lambda i: (i, 0))],
        out_shape=[jax.ShapeDtypeStruct((t, 3 * SB_WIDTH), jnp.bfloat16),
                   jax.ShapeDtypeStruct((t, POOL_WIDTH), jnp.float32)],
        compiler_params=pltpu.CompilerParams(vmem_limit_bytes=VMEM_LIMIT),
        name="ln_inproj",
    )(x2d, mod, w_in_bf16)


def _sb_attn_kernel(q_ref, k_ref, v_ref, o_ref, acc_ref, carry_ref):
    tq = tk = ATTN_TILE
    heads = (0, 1)
    qi = pl.program_id(2)
    q2 = q_ref[0]
    lane = lax.broadcasted_iota(jnp.int32, (tq, LANES), 1)
    first_head = lane < SB_HEAD_DIM
    zero = jnp.zeros_like(q2)
    q_heads = (jnp.where(first_head, q2, zero), jnp.where(first_head, zero, q2))

    row = lax.broadcasted_iota(jnp.int32, (tk, tk), 0)
    col = lax.broadcasted_iota(jnp.int32, (tk, tk), 1)
    neg_suffix = jnp.where(row >= col, -1.0, 0.0).astype(jnp.bfloat16)
    causal = col < row

    def visit(j, carries, diagonal):
        start = pl.multiple_of(jnp.maximum(qi - j, 0) * tk, tk)
        k_blk = k_ref[0, pl.ds(start, tk), :]
        v_blk = v_ref[0, pl.ds(start, tk), :]
        fixed, rounded = [], []
        for h in heads:
            z = lax.dot_general(q_heads[h], k_blk, (((1,), (1,)), ((), ())), preferred_element_type=jnp.float32)
            n = jnp.maximum(z, 0.0) + jnp.log2(1.0 + jnp.exp2(-jnp.abs(z)))
            if diagonal:
                n = jnp.where(causal, n, 0.0)
            n_bf16 = n.astype(jnp.bfloat16)
            rounded.append(n_bf16)
            fixed.append(z - (n - n_bf16.astype(jnp.float32)))
        incls = [jnp.dot(r, neg_suffix, preferred_element_type=jnp.float32) for r in rounded]
        outs, new_carries = [], []
        for h in heads:
            carry = carries[h]
            att = jnp.exp2(fixed[h] + incls[h] + jnp.concatenate([carry] * (tk // LANES), axis=1))
            if diagonal:
                att = jnp.where(causal, att, 0.0)
            outs.append(jnp.dot(att.astype(jnp.bfloat16), v_blk, preferred_element_type=jnp.float32))
            new_carries.append(carry + jnp.broadcast_to(incls[h][:, 0:1], (tq, LANES)))
        return outs, new_carries

    zero_carry = jnp.zeros((tq, LANES), jnp.float32)
    out0, carries = visit(0, [zero_carry, zero_carry], diagonal=True)
    out1, carries = visit(1, carries, diagonal=False)
    has_second = qi >= 1
    for h in heads:
        acc_ref[h] = out0[h] + jnp.where(has_second, out1[h], 0.0)
        carry_ref[h] = carries[h]

    def more(state):
        j, stick_gone = state
        return (j <= qi) & jnp.logical_not(stick_gone)

    def visit_next(state):
        j, _ = state
        outs, new_carries = visit(j, [carry_ref[h] for h in heads], diagonal=False)
        for h in heads:
            acc_ref[h] += outs[h]
            carry_ref[h] = new_carries[h]
        return j + 1, jnp.max(carry_ref[...]) < STICK_GONE_LOG2

    lax.while_loop(more, visit_next, (jnp.int32(2), jnp.max(carry_ref[...]) < STICK_GONE_LOG2))
    o_ref[0] = jnp.where(first_head, acc_ref[0], acc_ref[1]).astype(o_ref.dtype)


def _sb_attn(qkv, batch, seq):
    qkv3 = qkv.reshape(batch, seq, 3 * SB_WIDTH)
    tq = ATTN_TILE
    pairs = SB_WIDTH // LANES
    return pl.pallas_call(
        _sb_attn_kernel,
        grid=(batch, pairs, seq // tq),
        in_specs=[pl.BlockSpec((1, tq, LANES), lambda b, h, i: (b, i, h)),
                  pl.BlockSpec((1, seq, LANES), lambda b, h, i: (b, 0, pairs + h)),
                  pl.BlockSpec((1, seq, LANES), lambda b, h, i: (b, 0, 2 * pairs + h))],
        out_specs=pl.BlockSpec((1, tq, LANES), lambda b, h, i: (b, i, h)),
        out_shape=jax.ShapeDtypeStruct((batch, seq, SB_WIDTH), jnp.bfloat16),
        scratch_shapes=[pltpu.VMEM((2, tq, LANES), jnp.float32),
                        pltpu.VMEM((2, tq, LANES), jnp.float32)],
        compiler_params=pltpu.CompilerParams(vmem_limit_bytes=VMEM_LIMIT),
        name="sb_attn",
    )(qkv3, qkv3, qkv3)


def _route(logits):
    lane = lax.broadcasted_iota(jnp.int32, logits.shape, 1)
    neg = jnp.float32(-jnp.inf)
    big = jnp.int32(LANES)
    is_group = (lane >= N_EXPERTS) & (lane < N_EXPERTS + N_EXPERT_GROUPS)
    gl = jnp.where(is_group, logits, neg)
    g_max = jnp.max(gl, axis=-1, keepdims=True)
    g_sum = jnp.sum(jnp.exp(gl - g_max), axis=-1, keepdims=True)
    g_p = 1.0 / g_sum
    g_idx = jnp.min(jnp.where(gl == g_max, lane, big), axis=-1, keepdims=True) - N_EXPERTS
    in_group = (lane >= g_idx * EXPERTS_PER_GROUP) & (lane < (g_idx + 1) * EXPERTS_PER_GROUP)
    el = jnp.where(in_group, logits, neg)
    e1 = jnp.max(el, axis=-1, keepdims=True)
    i1 = jnp.min(jnp.where(el == e1, lane, big), axis=-1, keepdims=True)
    el2 = jnp.where(lane == i1, neg, el)
    e2 = jnp.max(el2, axis=-1, keepdims=True)
    i2 = jnp.min(jnp.where(el2 == e2, lane, big), axis=-1, keepdims=True)
    r = jnp.exp(e2 - e1)
    w1 = g_p / (1.0 + r)
    w2 = w1 * r
    return i1, i2, w1, w2


ROUTE_W1, ROUTE_W2, ROUTE_E1, ROUTE_E2, ROUTE_RANK1, ROUTE_RANK2 = range(6)


def _mix_ln1_kernel(tiles_per_seq, osb_ref, p_ref, halo_ref, wpool_ref, pscale_ref, wout_ref, x_ref, mod_ref,
                    g1_ref, b1_ref, wr_ref, br_ref, earlier_ref, x1_ref, u2_ref, route_ref, count_ref, pext_ref,
                    running_ref):
    tm = p_ref.shape[0]
    tile_in_seq = pl.program_id(0) % tiles_per_seq
    p = p_ref[...]
    pext_ref[0:HALO, :] = jnp.where(tile_in_seq == 0, 0.0, halo_ref[...])
    pext_ref[HALO:, :] = p
    pos = tile_in_seq * tm + lax.broadcasted_iota(jnp.int32, (tm, 1), 0)

    mixer_out = [osb_ref[...]]
    for g, w in enumerate(POOL_WINDOWS):
        cols = slice(g * POOL_GROUP_DIM, (g + 1) * POOL_GROUP_DIM)
        win = p[:, cols]
        for i in range(1, w):
            win = win + pext_ref[HALO - i:HALO - i + tm, cols]
        count = jnp.minimum(pos + 1, w).astype(jnp.float32)
        pooled = win / count - p[:, cols]
        o_pool = jnp.dot(pooled.astype(jnp.bfloat16), wpool_ref[g], preferred_element_type=jnp.float32)
        mixer_out.append((o_pool * pscale_ref[:, cols]).astype(jnp.bfloat16))
    mixed = jnp.dot(jnp.concatenate(mixer_out, axis=1), wout_ref[...], preferred_element_type=jnp.float32)

    gate1 = mod_ref[0, 2:3, :]
    shift2 = mod_ref[0, 3:4, :]
    scale2 = mod_ref[0, 4:5, :]
    x1 = _ln(DEEPNORM_ALPHA * x_ref[...] + (1.0 + gate1) * mixed) * g1_ref[...] + b1_ref[...]
    x1_ref[...] = x1
    u2 = _ln(x1) * (1.0 + scale2) + shift2
    u2_ref[...] = u2
    u2_hi = u2.astype(jnp.bfloat16)
    u2_lo = (u2 - u2_hi.astype(jnp.float32)).astype(jnp.bfloat16)
    logits = jnp.dot(jnp.concatenate([u2_hi, u2_lo, u2_hi], axis=1), wr_ref[...],
                     preferred_element_type=jnp.float32) + br_ref[...]
    i1, i2, w1, w2 = _route(logits)

    @pl.when(pl.program_id(0) == 0)
    def _():
        running_ref[...] = jnp.zeros_like(running_ref)

    lane = lax.broadcasted_iota(jnp.int32, (tm, LANES), 1)
    first, second = lane == i1, lane == i2
    uses = jnp.where(first | second, 1.0, 0.0)
    before = jnp.dot(earlier_ref[...], uses.astype(jnp.bfloat16), preferred_element_type=jnp.float32)
    before = before + running_ref[0:1, :]
    rank1 = jnp.sum(jnp.where(first, before, 0.0), axis=-1, keepdims=True)
    rank2 = jnp.sum(jnp.where(second, before, 0.0), axis=-1, keepdims=True)
    running = running_ref[0:1, :] + jnp.sum(uses, axis=0, keepdims=True)
    running_ref[...] = jnp.broadcast_to(running, running_ref.shape)
    count_ref[...] = jnp.broadcast_to(running, count_ref.shape)

    record = jnp.zeros((tm, LANES), jnp.float32)
    for slot, value in ((ROUTE_W1, w1), (ROUTE_W2, w2), (ROUTE_E1, i1.astype(jnp.float32)),
                        (ROUTE_E2, i2.astype(jnp.float32)), (ROUTE_RANK1, rank1), (ROUTE_RANK2, rank2)):
        record = jnp.where(lane == slot, value, record)
    route_ref[...] = record


def _mix_ln1(o_sb, p, w_pool_bf16, pool_scale, w_out_bf16, x2d, mod, ln1_g, ln1_b, w_router, b_router, seq):
    t, d = x2d.shape
    tm = ROW_TILE
    tiles_per_seq = seq // tm
    halo_blocks_per_tile = tm // HALO
    row = lambda i: (i, 0)
    const2 = lambda i: (0, 0)
    return pl.pallas_call(
        functools.partial(_mix_ln1_kernel, tiles_per_seq),
        grid=(t // tm,),
        in_specs=[pl.BlockSpec((tm, SB_WIDTH), row),
                  pl.BlockSpec((tm, POOL_WIDTH), row),
                  pl.BlockSpec((HALO, POOL_WIDTH), lambda i: (jnp.maximum(i * halo_blocks_per_tile - 1, 0), 0)),
                  pl.BlockSpec(w_pool_bf16.shape, lambda i: (0, 0, 0)),
                  pl.BlockSpec((1, POOL_WIDTH), const2),
                  pl.BlockSpec(w_out_bf16.shape, const2),
                  pl.BlockSpec((tm, d), row),
                  pl.BlockSpec((1, N_MOD, d), lambda i: (i // tiles_per_seq, 0, 0)),
                  pl.BlockSpec((1, d), const2),
                  pl.BlockSpec((1, d), const2),
                  pl.BlockSpec(w_router.shape, const2),
                  pl.BlockSpec((1, LANES), const2),
                  pl.BlockSpec((tm, tm), const2)],
        out_specs=[pl.BlockSpec((tm, d), row),
                   pl.BlockSpec((tm, d), row),
                   pl.BlockSpec((tm, LANES), row),
                   pl.BlockSpec((SUBLANES, LANES), const2)],
        out_shape=[jax.ShapeDtypeStruct((t, d), jnp.float32),
                   jax.ShapeDtypeStruct((t, d), jnp.float32),
                   jax.ShapeDtypeStruct((t, LANES), jnp.float32),
                   jax.ShapeDtypeStruct((SUBLANES, LANES), jnp.float32)],
        scratch_shapes=[pltpu.VMEM((HALO + tm, POOL_WIDTH), jnp.float32),
                        pltpu.VMEM((SUBLANES, LANES), jnp.float32)],
        compiler_params=pltpu.CompilerParams(vmem_limit_bytes=VMEM_LIMIT,
                                             dimension_semantics=("arbitrary",)),
        name="mix_ln1",
    )(o_sb, p, p, w_pool_bf16, pool_scale, w_out_bf16, x2d, mod, ln1_g, ln1_b, w_router, b_router,
      jnp.tri(tm, tm, -1, dtype=jnp.bfloat16))


def _dispatch_kernel(pos_ref, u_ref, init_hbm, sorted_hbm, sem):
    del init_hbm
    tm = u_ref.shape[0]

    def body(r, _):
        row = u_ref.at[pl.ds(r, 1), :]
        pltpu.make_async_copy(row, sorted_hbm.at[pl.ds(pos_ref[0, 0, r], 1), :], sem).start()
        pltpu.make_async_copy(row, sorted_hbm.at[pl.ds(pos_ref[0, 0, tm + r], 1), :], sem).start()
        return 0

    lax.fori_loop(0, tm, body, 0, unroll=GATHER_UNROLL // 2)
    for _ in range(2):
        pltpu.make_async_copy(u_ref, sorted_hbm.at[pl.ds(0, tm), :], sem).wait()


def _dispatch(pos, u2, n_rows):
    t, d = u2.shape
    tm = pos.shape[2] // 2
    return pl.pallas_call(
        _dispatch_kernel,
        grid=(t // tm,),
        in_specs=[pl.BlockSpec((1, 1, 2 * tm), lambda i: (i, 0, 0), memory_space=pltpu.SMEM),
                  pl.BlockSpec((tm, d), lambda i: (i, 0)),
                  pl.BlockSpec(memory_space=pl.ANY)],
        out_specs=pl.BlockSpec(memory_space=pl.ANY),
        out_shape=jax.ShapeDtypeStruct((n_rows, d), jnp.float32),
        scratch_shapes=[pltpu.SemaphoreType.DMA(())],
        input_output_aliases={2: 0},
        compiler_params=pltpu.CompilerParams(vmem_limit_bytes=VMEM_LIMIT,
                                             dimension_semantics=("arbitrary",)),
        name="dispatch",
    )(pos, u2, jnp.zeros((n_rows, d), jnp.float32))


def _moe_grouped_kernel(tile_expert_ref, n_tiles_ref, u_ref, wg_ref, wu_ref, wd_ref, y_ref):
    del tile_expert_ref
    i = pl.program_id(0)

    @pl.when(i < n_tiles_ref[0])
    def _():
        bf16 = jnp.bfloat16
        u = u_ref[...].astype(bf16)
        gate = jnp.dot(u, wg_ref[0].astype(bf16), preferred_element_type=jnp.float32)
        up = jnp.dot(u, wu_ref[0].astype(bf16), preferred_element_type=jnp.float32)
        h = gate * jax.nn.sigmoid(gate) * up
        y_ref[...] = jnp.dot(h.astype(bf16), wd_ref[0].astype(bf16), preferred_element_type=jnp.float32)

    @pl.when(i >= n_tiles_ref[0])
    def _():
        y_ref[...] = jnp.zeros_like(y_ref)


def _moe_grouped(tile_expert, n_tiles, u_sorted, wg, wu, wd):
    n_rows, d = u_sorted.shape
    tm = MOE_ROW_TILE
    expert = lambda i, te, nt: (te[i], 0, 0)
    row = lambda i, te, nt: (i, 0)
    grid_spec = pltpu.PrefetchScalarGridSpec(
        num_scalar_prefetch=2,
        grid=(n_rows // tm,),
        in_specs=[pl.BlockSpec((tm, d), row),
                  pl.BlockSpec((1, d, EXPERT_HIDDEN), expert),
                  pl.BlockSpec((1, d, EXPERT_HIDDEN), expert),
                  pl.BlockSpec((1, EXPERT_HIDDEN, d), expert)],
        out_specs=pl.BlockSpec((tm, d), row),
    )
    return pl.pallas_call(
        _moe_grouped_kernel,
        grid_spec=grid_spec,
        out_shape=jax.ShapeDtypeStruct((n_rows, d), jnp.float32),
        compiler_params=pltpu.CompilerParams(vmem_limit_bytes=VMEM_LIMIT,
                                             dimension_semantics=("arbitrary",)),
        name="moe_grouped",
    )(tile_expert, n_tiles, u_sorted, wg, wu, wd)


def _combine_ln2_kernel(pos_ref, pos_next_ref, y_hbm, route_ref, x1_ref, mod_ref, g2_ref, b2_ref, o_ref,
                        buf_ref, sem_ref):
    i = pl.program_id(0)
    n = pl.num_programs(0)
    tm = o_ref.shape[0]
    rows = 2 * tm

    def start_gather(rows_ref, slot):
        def body(r, _):
            pltpu.make_async_copy(y_hbm.at[pl.ds(rows_ref[0, 0, r], 1), :], buf_ref.at[slot, pl.ds(r, 1), :],
                                  sem_ref.at[slot]).start()
            return 0
        lax.fori_loop(0, rows, body, 0, unroll=GATHER_UNROLL)

    @pl.when(i == 0)
    def _():
        start_gather(pos_ref, 0)

    @pl.when(i + 1 < n)
    def _():
        start_gather(pos_next_ref, (i + 1) % 2)

    slot = i % 2
    pltpu.make_async_copy(y_hbm.at[pl.ds(0, rows), :], buf_ref.at[slot], sem_ref.at[slot]).wait()
    route = route_ref[...]
    w1 = route[:, ROUTE_W1:ROUTE_W1 + 1]
    w2 = route[:, ROUTE_W2:ROUTE_W2 + 1]
    y = w1 * buf_ref[slot, 0:tm, :] + w2 * buf_ref[slot, tm:, :]
    gate2 = mod_ref[0, 5:6, :]
    o_ref[...] = _ln(DEEPNORM_ALPHA * x1_ref[...] + (1.0 + gate2) * y) * g2_ref[...] + b2_ref[...]


def _combine_ln2(pos, y_sorted, route, x1, mod, ln2_g, ln2_b, seq):
    t, d = x1.shape
    tm = COMBINE_ROW_TILE
    n = t // tm
    tiles_per_seq = seq // tm
    row = lambda i: (i, 0)
    const2 = lambda i: (0, 0)
    return pl.pallas_call(
        _combine_ln2_kernel,
        grid=(n,),
        in_specs=[pl.BlockSpec((1, 1, 2 * tm), lambda i: (i, 0, 0), memory_space=pltpu.SMEM),
                  pl.BlockSpec((1, 1, 2 * tm), lambda i: (jnp.minimum(i + 1, n - 1), 0, 0),
                               memory_space=pltpu.SMEM),
                  pl.BlockSpec(memory_space=pl.ANY),
                  pl.BlockSpec((tm, LANES), row),
                  pl.BlockSpec((tm, d), row),
                  pl.BlockSpec((1, N_MOD, d), lambda i: (i // tiles_per_seq, 0, 0)),
                  pl.BlockSpec((1, d), const2),
                  pl.BlockSpec((1, d), const2)],
        out_specs=pl.BlockSpec((tm, d), row),
        out_shape=jax.ShapeDtypeStruct((t, d), jnp.float32),
        scratch_shapes=[pltpu.VMEM((2, 2 * tm, d), jnp.float32),
                        pltpu.SemaphoreType.DMA((2,))],
        compiler_params=pltpu.CompilerParams(vmem_limit_bytes=VMEM_LIMIT,
                                             dimension_semantics=("arbitrary",)),
        name="combine_ln2",
    )(pos, pos, y_sorted, route, x1, mod, ln2_g, ln2_b)


def _dispatch_plan(route, counts):
    t = route.shape[0]
    tm = MOE_ROW_TILE
    max_tiles = 2 * t // tm + N_EXPERTS
    counts = counts[0, :N_EXPERTS].astype(jnp.int32)
    tiles = (counts + tm - 1) // tm
    tile_end = jnp.cumsum(tiles)
    row_start = (tile_end - tiles) * tm
    experts = route[:, ROUTE_E1:ROUTE_E2 + 1].astype(jnp.int32)
    ranks = route[:, ROUTE_RANK1:ROUTE_RANK2 + 1].astype(jnp.int32)
    pos = row_start[experts] + ranks
    tile_ids = jnp.arange(max_tiles, dtype=jnp.int32)
    tile_expert = jnp.minimum(jnp.sum(tile_end[None, :] <= tile_ids[:, None], axis=1), N_EXPERTS - 1)
    by_tile = lambda tc: pos.reshape(t // tc, tc, 2).transpose(0, 2, 1).reshape(t // tc, 1, 2 * tc)
    return (tile_expert.astype(jnp.int32), tile_end[-1:].astype(jnp.int32), by_tile(DISPATCH_ROW_TILE),
            by_tile(COMBINE_ROW_TILE), max_tiles * tm)


def kernel(x, c, w_ada, b_ada, w_in, w_pool, pool_scale, w_out, ln1_g, ln1_b, w_router_group, b_router_group,
           w_router_expert, b_router_expert, w_gate, w_up, w_down, ln2_g, ln2_b):
    b, s, d = x.shape
    bf16 = jnp.bfloat16
    for layer in range(DEPTH):
        c_pad = jnp.pad(c, ((0, SUBLANES - b), (0, 0)))
        mod = _adaln(c_pad, w_ada[layer], b_ada[layer][None, :])[:b].reshape(b, N_MOD, d)
        x2d = x.reshape(b * s, d)
        qkv, p = _ln_inproj(x2d, mod, w_in[layer].astype(bf16), s)
        o_sb = _sb_attn(qkv, b, s).reshape(b * s, SB_WIDTH)
        pad = LANES - N_EXPERTS - N_EXPERT_GROUPS
        w_router = jnp.pad(jnp.concatenate([w_router_expert[layer], w_router_group[layer]], axis=1),
                           ((0, 0), (0, pad)))
        b_router = jnp.pad(jnp.concatenate([b_router_expert[layer], b_router_group[layer]]), (0, pad))[None, :]
        w_router_hi = w_router.astype(bf16)
        w_router_lo = (w_router - w_router_hi.astype(jnp.float32)).astype(bf16)
        w_router = jnp.concatenate([w_router_hi, w_router_hi, w_router_lo], axis=0)
        x1, u2, route, counts = _mix_ln1(o_sb, p, w_pool[layer].astype(bf16), pool_scale[layer][None, :],
                                         w_out[layer].astype(bf16), x2d, mod, ln1_g[layer][None, :],
                                         ln1_b[layer][None, :], w_router, b_router, s)
        tile_expert, n_tiles, pos_dispatch, pos, n_rows = _dispatch_plan(route, counts)
        u_sorted = _dispatch(pos_dispatch, u2, n_rows)
        y_sorted = _moe_grouped(tile_expert, n_tiles, u_sorted, w_gate[layer], w_up[layer], w_down[layer])
        x2 = _combine_ln2(pos, y_sorted, route, x1, mod, ln2_g[layer][None, :], ln2_b[layer][None, :], s)
        x = x2.reshape(b, s, d)
    return x
```

```python
import functools

import jax
import jax.numpy as jnp
from jax import lax
from jax.experimental import pallas as pl
from jax.experimental.pallas import tpu as pltpu

D_MODEL = 1024
N_SB_HEADS = 8
SB_HEAD_DIM = 64
SB_WIDTH = N_SB_HEADS * SB_HEAD_DIM
POOL_WINDOWS = (2, 4, 8, 16)
POOL_GROUP_DIM = 128
POOL_WIDTH = len(POOL_WINDOWS) * POOL_GROUP_DIM
N_EXPERT_GROUPS = 4
EXPERTS_PER_GROUP = 4
N_EXPERTS = N_EXPERT_GROUPS * EXPERTS_PER_GROUP
EXPERT_HIDDEN = 512
DEPTH = 1
DEEPNORM_ALPHA = (2.0 * DEPTH) ** 0.25
LN_EPS = 1e-5
N_MOD = 6

LANES = 128
SUBLANES = 8
HALO = max(POOL_WINDOWS)
VMEM_LIMIT = 56 * 1024 * 1024

ROW_TILE = 512
ATTN_TILE = 256
STICK_GONE_LOG2 = -180.0
MOE_ROW_TILE = 512
DISPATCH_ROW_TILE = 1024
COMBINE_ROW_TILE = 256
GATHER_UNROLL = 8


def _ln(x):
    mu = jnp.mean(x, axis=-1, keepdims=True)
    xc = x - mu
    var = jnp.mean(xc * xc, axis=-1, keepdims=True)
    return xc * lax.rsqrt(var + LN_EPS)


def _adaln_kernel(c_ref, w_ref, b_ref, o_ref):
    c = c_ref[...]
    a = c * jax.nn.sigmoid(c)
    o_ref[...] = jnp.dot(a, w_ref[...], preferred_element_type=jnp.float32,
                         precision=lax.Precision.HIGHEST) + b_ref[...]


def _adaln(c_pad, w_ada, b_ada):
    rows, d = c_pad.shape
    n = w_ada.shape[1]
    tn = 1024
    return pl.pallas_call(
        _adaln_kernel,
        grid=(n // tn,),
        in_specs=[pl.BlockSpec((rows, d), lambda j: (0, 0)),
                  pl.BlockSpec((d, tn), lambda j: (0, j)),
                  pl.BlockSpec((1, tn), lambda j: (0, j))],
        out_specs=pl.BlockSpec((rows, tn), lambda j: (0, j)),
        out_shape=jax.ShapeDtypeStruct((rows, n), jnp.float32),
        compiler_params=pltpu.CompilerParams(vmem_limit_bytes=VMEM_LIMIT),
        name="adaln",
    )(c_pad, w_ada, b_ada)


def _ln_inproj_kernel(x_ref, mod_ref, w_ref, qkv_ref, p_ref):
    shift = mod_ref[0, 0:1, :]
    scale = mod_ref[0, 1:2, :]
    u = (_ln(x_ref[...]) * (1.0 + scale) + shift).astype(jnp.bfloat16)
    qk_scale = SB_HEAD_DIM ** -0.5 * 1.4426950408889634
    q = jnp.dot(u, w_ref[:, 0:SB_WIDTH], preferred_element_type=jnp.float32) * qk_scale
    qkv_ref[:, 0:SB_WIDTH] = q.astype(jnp.bfloat16)
    for j in (1, 2):
        kv = jnp.dot(u, w_ref[:, j * SB_WIDTH:(j + 1) * SB_WIDTH], preferred_element_type=jnp.float32)
        qkv_ref[:, j * SB_WIDTH:(j + 1) * SB_WIDTH] = kv.astype(jnp.bfloat16)
    p_ref[...] = jnp.dot(u, w_ref[:, 3 * SB_WIDTH:], preferred_element_type=jnp.float32)


def _ln_inproj(x2d, mod, w_in_bf16, seq):
    t, d = x2d.shape
    tm = ROW_TILE
    tiles_per_seq = seq // tm
    return pl.pallas_call(
        _ln_inproj_kernel,
        grid=(t // tm,),
        in_specs=[pl.BlockSpec((tm, d), lambda i: (i, 0)),
                  pl.BlockSpec((1, N_MOD, d), lambda i: (i // tiles_per_seq, 0, 0)),
                  pl.BlockSpec(w_in_bf16.shape, lambda i: (0, 0))],
        out_specs=[pl.BlockSpec((tm, 3 * SB_WIDTH), lambda i: (i, 0)),
                   pl.BlockSpec((tm, POOL_WIDTH), lambda i: (i, 0))],
        out_shape=[jax.ShapeDtypeStruct((t, 3 * SB_WIDTH), jnp.bfloat16),
                   jax.ShapeDtypeStruct((t, POOL_WIDTH), jnp.float32)],
        compiler_params=pltpu.CompilerParams(vmem_limit_bytes=VMEM_LIMIT),
        name="ln_inproj",
    )(x2d, mod, w_in_bf16)


def _sb_attn_kernel(q_ref, k_ref, v_ref, o_ref, acc_ref, carry_ref):
    tq = tk = ATTN_TILE
    heads = (0, 1)
    qi = pl.program_id(2)
    q2 = q_ref[0]
    lane = lax.broadcasted_iota(jnp.int32, (tq, LANES), 1)
    first_head = lane < SB_HEAD_DIM
    zero = jnp.zeros_like(q2)
    q_heads = (jnp.where(first_head, q2, zero), jnp.where(first_head, zero, q2))

    row = lax.broadcasted_iota(jnp.int32, (tk, tk), 0)
    col = lax.broadcasted_iota(jnp.int32, (tk, tk), 1)
    neg_suffix = jnp.where(row >= col, -1.0, 0.0).astype(jnp.bfloat16)
    causal = col < row

    def visit(j, carries, diagonal):
        start = pl.multiple_of(jnp.maximum(qi - j, 0) * tk, tk)
        k_blk = k_ref[0, pl.ds(start, tk), :]
        v_blk = v_ref[0, pl.ds(start, tk), :]
        fixed, rounded = [], []
        for h in heads:
            z = lax.dot_general(q_heads[h], k_blk, (((1,), (1,)), ((), ())), preferred_element_type=jnp.float32)
            n = jnp.maximum(z, 0.0) + jnp.log2(1.0 + jnp.exp2(-jnp.abs(z)))
            if diagonal:
                n = jnp.where(causal, n, 0.0)
            n_bf16 = n.astype(jnp.bfloat16)
            rounded.append(n_bf16)
            fixed.append(z - (n - n_bf16.astype(jnp.float32)))
        incls = [jnp.dot(r, neg_suffix, preferred_element_type=jnp.float32) for r in rounded]
        outs, new_carries = [], []
        for h in heads:
            carry = carries[h]
            att = jnp.exp2(fixed[h] + incls[h] + jnp.concatenate([carry] * (tk // LANES), axis=1))
            if diagonal:
                att = jnp.where(causal, att, 0.0)
            outs.append(jnp.dot(att.astype(jnp.bfloat16), v_blk, preferred_element_type=jnp.float32))
            new_carries.append(carry + jnp.broadcast_to(incls[h][:, 0:1], (tq, LANES)))
        return outs, new_carries

    zero_carry = jnp.zeros((tq, LANES), jnp.float32)
    out0, carries = visit(0, [zero_carry, zero_carry], diagonal=True)
    out1, carries = visit(1, carries, diagonal=False)
    has_second = qi >= 1
    for h in heads:
        acc_ref[h] = out0[h] + jnp.where(has_second, out1[h], 0.0)
        carry_ref[h] = carries[h]

    def more(state):
        j, stick_gone = state
        return (j <= qi) & jnp.logical_not(stick_gone)

    def visit_next(state):
        j, _ = state
        outs, new_carries = visit(j, [carry_ref[h] for h in heads], diagonal=False)
        for h in heads:
            acc_ref[h] += outs[h]
            carry_ref[h] = new_carries[h]
        return j + 1, jnp.max(carry_ref[...]) < STICK_GONE_LOG2

    lax.while_loop(more, visit_next, (jnp.int32(2), jnp.max(carry_ref[...]) < STICK_GONE_LOG2))
    o_ref[0] = jnp.where(first_head, acc_ref[0], acc_ref[1]).astype(o_ref.dtype)


def _sb_attn(qkv, batch, seq):
    qkv3 = qkv.reshape(batch, seq, 3 * SB_WIDTH)
    tq = ATTN_TILE
    pairs = SB_WIDTH // LANES
    return pl.pallas_call(
        _sb_attn_kernel,
        grid=(batch, pairs, seq // tq),
        in_specs=[pl.BlockSpec((1, tq, LANES), lambda b, h, i: (b, i, h)),
                  pl.BlockSpec((1, seq, LANES), lambda b, h, i: (b, 0, pairs + h)),
                  pl.BlockSpec((1, seq, LANES), lambda b, h, i: (b, 0, 2 * pairs + h))],
        out_specs=pl.BlockSpec((1, tq, LANES), lambda b, h, i: (b, i, h)),
        out_shape=jax.ShapeDtypeStruct((batch, seq, SB_WIDTH), jnp.bfloat16),
        scratch_shapes=[pltpu.VMEM((2, tq, LANES), jnp.float32),
                        pltpu.VMEM((2, tq, LANES), jnp.float32)],
        compiler_params=pltpu.CompilerParams(vmem_limit_bytes=VMEM_LIMIT),
        name="sb_attn",
    )(qkv3, qkv3, qkv3)


def _route(logits):
    lane = lax.broadcasted_iota(jnp.int32, logits.shape, 1)
    neg = jnp.float32(-jnp.inf)
    big = jnp.int32(LANES)
    is_group = (lane >= N_EXPERTS) & (lane < N_EXPERTS + N_EXPERT_GROUPS)
    gl = jnp.where(is_group, logits, neg)
    g_max = jnp.max(gl, axis=-1, keepdims=True)
    g_sum = jnp.sum(jnp.exp(gl - g_max), axis=-1, keepdims=True)
    g_p = 1.0 / g_sum
    g_idx = jnp.min(jnp.where(gl == g_max, lane, big), axis=-1, keepdims=True) - N_EXPERTS
    in_group = (lane >= g_idx * EXPERTS_PER_GROUP) & (lane < (g_idx + 1) * EXPERTS_PER_GROUP)
    el = jnp.where(in_group, logits, neg)
    e1 = jnp.max(el, axis=-1, keepdims=True)
    i1 = jnp.min(jnp.where(el == e1, lane, big), axis=-1, keepdims=True)
    el2 = jnp.where(lane == i1, neg, el)
    e2 = jnp.max(el2, axis=-1, keepdims=True)
    i2 = jnp.min(jnp.where(el2 == e2, lane, big), axis=-1, keepdims=True)
    r = jnp.exp(e2 - e1)
    w1 = g_p / (1.0 + r)
    w2 = w1 * r
    return i1, i2, w1, w2


ROUTE_W1, ROUTE_W2, ROUTE_E1, ROUTE_E2, ROUTE_RANK1, ROUTE_RANK2 = range(6)


def _mix_ln1_kernel(tiles_per_seq, osb_ref, p_ref, halo_ref, wpool_ref, pscale_ref, wout_ref, x_ref, mod_ref,
                    g1_ref, b1_ref, wr_ref, br_ref, earlier_ref, x1_ref, u2_ref, route_ref, count_ref, pext_ref,
                    running_ref):
    tm = p_ref.shape[0]
    tile_in_seq = pl.program_id(0) % tiles_per_seq
    p = p_ref[...]
    pext_ref[0:HALO, :] = jnp.where(tile_in_seq == 0, 0.0, halo_ref[...])
    pext_ref[HALO:, :] = p
    pos = tile_in_seq * tm + lax.broadcasted_iota(jnp.int32, (tm, 1), 0)

    mixer_out = [osb_ref[...]]
    for g, w in enumerate(POOL_WINDOWS):
        cols = slice(g * POOL_GROUP_DIM, (g + 1) * POOL_GROUP_DIM)
        win = p[:, cols]
        for i in range(1, w):
            win = win + pext_ref[HALO - i:HALO - i + tm, cols]
        count = jnp.minimum(pos + 1, w).astype(jnp.float32)
        pooled = win / count - p[:, cols]
        o_pool = jnp.dot(pooled.astype(jnp.bfloat16), wpool_ref[g], preferred_element_type=jnp.float32)
        mixer_out.append((o_pool * pscale_ref[:, cols]).astype(jnp.bfloat16))
    mixed = jnp.dot(jnp.concatenate(mixer_out, axis=1), wout_ref[...], preferred_element_type=jnp.float32)

    gate1 = mod_ref[0, 2:3, :]
    shift2 = mod_ref[0, 3:4, :]
    scale2 = mod_ref[0, 4:5, :]
    x1 = _ln(DEEPNORM_ALPHA * x_ref[...] + (1.0 + gate1) * mixed) * g1_ref[...] + b1_ref[...]
    x1_ref[...] = x1
    u2 = _ln(x1) * (1.0 + scale2) + shift2
    u2_ref[...] = u2
    u2_hi = u2.astype(jnp.bfloat16)
    u2_lo = (u2 - u2_hi.astype(jnp.float32)).astype(jnp.bfloat16)
    logits = jnp.dot(jnp.concatenate([u2_hi, u2_lo, u2_hi], axis=1), wr_ref[...],
                     preferred_element_type=jnp.float32) + br_ref[...]
    i1, i2, w1, w2 = _route(logits)

    @pl.when(pl.program_id(0) == 0)
    def _():
        running_ref[...] = jnp.zeros_like(running_ref)

    lane = lax.broadcasted_iota(jnp.int32, (tm, LANES), 1)
    first, second = lane == i1, lane == i2
    uses = jnp.where(first | second, 1.0, 0.0)
    before = jnp.dot(earlier_ref[...], uses.astype(jnp.bfloat16), preferred_element_type=jnp.float32)
    before = before + running_ref[0:1, :]
    rank1 = jnp.sum(jnp.where(first, before, 0.0), axis=-1, keepdims=True)
    rank2 = jnp.sum(jnp.where(second, before, 0.0), axis=-1, keepdims=True)
    running = running_ref[0:1, :] + jnp.sum(uses, axis=0, keepdims=True)
    running_ref[...] = jnp.broadcast_to(running, running_ref.shape)
    count_ref[...] = jnp.broadcast_to(running, count_ref.shape)

    record = jnp.zeros((tm, LANES), jnp.float32)
    for slot, value in ((ROUTE_W1, w1), (ROUTE_W2, w2), (ROUTE_E1, i1.astype(jnp.float32)),
                        (ROUTE_E2, i2.astype(jnp.float32)), (ROUTE_RANK1, rank1), (ROUTE_RANK2, rank2)):
        record = jnp.where(lane == slot, value, record)
    route_ref[...] = record


def _mix_ln1(o_sb, p, w_pool_bf16, pool_scale, w_out_bf16, x2d, mod, ln1_g, ln1_b, w_router, b_router, seq):
    t, d = x2d.shape
    tm = ROW_TILE
    tiles_per_seq = seq // tm
    halo_blocks_per_tile = tm // HALO
    row = lambda i: (i, 0)
    const2 = lambda i: (0, 0)
    return pl.pallas_call(
        functools.partial(_mix_ln1_kernel, tiles_per_seq),
        grid=(t // tm,),
        in_specs=[pl.BlockSpec((tm, SB_WIDTH), row),
                  pl.BlockSpec((tm, POOL_WIDTH), row),
                  pl.BlockSpec((HALO, POOL_WIDTH), lambda i: (jnp.maximum(i * halo_blocks_per_tile - 1, 0), 0)),
                  pl.BlockSpec(w_pool_bf16.shape, lambda i: (0, 0, 0)),
                  pl.BlockSpec((1, POOL_WIDTH), const2),
                  pl.BlockSpec(w_out_bf16.shape, const2),
                  pl.BlockSpec((tm, d), row),
                  pl.BlockSpec((1, N_MOD, d), lambda i: (i // tiles_per_seq, 0, 0)),
                  pl.BlockSpec((1, d), const2),
                  pl.BlockSpec((1, d), const2),
                  pl.BlockSpec(w_router.shape, const2),
                  pl.BlockSpec((1, LANES), const2),
                  pl.BlockSpec((tm, tm), const2)],
        out_specs=[pl.BlockSpec((tm, d), row),
                   pl.BlockSpec((tm, d), row),
                   pl.BlockSpec((tm, LANES), row),
                   pl.BlockSpec((SUBLANES, LANES), const2)],
        out_shape=[jax.ShapeDtypeStruct((t, d), jnp.float32),
                   jax.ShapeDtypeStruct((t, d), jnp.float32),
                   jax.ShapeDtypeStruct((t, LANES), jnp.float32),
                   jax.ShapeDtypeStruct((SUBLANES, LANES), jnp.float32)],
        scratch_shapes=[pltpu.VMEM((HALO + tm, POOL_WIDTH), jnp.float32),
                        pltpu.VMEM((SUBLANES, LANES), jnp.float32)],
        compiler_params=pltpu.CompilerParams(vmem_limit_bytes=VMEM_LIMIT,
                                             dimension_semantics=("arbitrary",)),
        name="mix_ln1",
    )(o_sb, p, p, w_pool_bf16, pool_scale, w_out_bf16, x2d, mod, ln1_g, ln1_b, w_router, b_router,
      jnp.tri(tm, tm, -1, dtype=jnp.bfloat16))


def _dispatch_kernel(pad_start_ref, n_tiles_ref, pos_ref, u_ref, sorted_hbm, zeros_ref, sem):
    tm = u_ref.shape[0]
    tile = MOE_ROW_TILE
    pad_rows = zeros_ref.shape[0]

    @pl.when(pl.program_id(0) == 0)
    def _():
        zeros_ref[...] = jnp.zeros_like(zeros_ref)
        fills = [pltpu.make_async_copy(
            zeros_ref, sorted_hbm.at[pl.ds(pl.multiple_of(pad_start_ref[e], SUBLANES), pad_rows), :], sem)
            for e in range(N_EXPERTS)]
        for fill in fills:
            fill.start()
        for fill in fills:
            fill.wait()

        def zero_tile(i, _):
            fill = pltpu.make_async_copy(zeros_ref.at[pl.ds(0, tile), :],
                                         sorted_hbm.at[pl.ds(pl.multiple_of(i * tile, tile), tile), :], sem)
            fill.start()
            fill.wait()
            return 0

        lax.fori_loop(n_tiles_ref[0], sorted_hbm.shape[0] // tile, zero_tile, 0)

    def body(r, _):
        row = u_ref.at[pl.ds(r, 1), :]
        pltpu.make_async_copy(row, sorted_hbm.at[pl.ds(pos_ref[0, 0, r], 1), :], sem).start()
        pltpu.make_async_copy(row, sorted_hbm.at[pl.ds(pos_ref[0, 0, tm + r], 1), :], sem).start()
        return 0

    lax.fori_loop(0, tm, body, 0, unroll=GATHER_UNROLL // 2)
    for _ in range(2):
        pltpu.make_async_copy(u_ref, sorted_hbm.at[pl.ds(0, tm), :], sem).wait()


def _dispatch(pad_start, n_tiles, pos, u2, n_rows):
    t, d = u2.shape
    tm = pos.shape[2] // 2
    grid_spec = pltpu.PrefetchScalarGridSpec(
        num_scalar_prefetch=2,
        grid=(t // tm,),
        in_specs=[pl.BlockSpec((1, 1, 2 * tm), lambda i, ps, nt: (i, 0, 0), memory_space=pltpu.SMEM),
                  pl.BlockSpec((tm, d), lambda i, ps, nt: (i, 0))],
        out_specs=pl.BlockSpec(memory_space=pl.ANY),
        scratch_shapes=[pltpu.VMEM((MOE_ROW_TILE + SUBLANES, d), jnp.float32),
                        pltpu.SemaphoreType.DMA(())],
    )
    return pl.pallas_call(
        _dispatch_kernel,
        grid_spec=grid_spec,
        out_shape=jax.ShapeDtypeStruct((n_rows + 2 * MOE_ROW_TILE, d), jnp.float32),
        compiler_params=pltpu.CompilerParams(vmem_limit_bytes=VMEM_LIMIT,
                                             dimension_semantics=("arbitrary",)),
        name="dispatch",
    )(pad_start, n_tiles, pos, u2)


def _moe_grouped_kernel(tile_expert_ref, n_tiles_ref, u_ref, wg_ref, wu_ref, wd_ref, y_ref,
                        wg_bf16_ref, wu_bf16_ref, wd_bf16_ref):
    i = pl.program_id(0)
    bf16 = jnp.bfloat16
    valid = i < n_tiles_ref[0]

    @pl.when(valid & ((i == 0) | (tile_expert_ref[i] != tile_expert_ref[jnp.maximum(i - 1, 0)])))
    def _():
        wg_bf16_ref[...] = wg_ref[0].astype(bf16)
        wu_bf16_ref[...] = wu_ref[0].astype(bf16)
        wd_bf16_ref[...] = wd_ref[0].astype(bf16)

    @pl.when(valid)
    def _():
        u = u_ref[...].astype(bf16)
        gate = jnp.dot(u, wg_bf16_ref[...], preferred_element_type=jnp.float32)
        up = jnp.dot(u, wu_bf16_ref[...], preferred_element_type=jnp.float32)
        h = gate * jax.nn.sigmoid(gate) * up
        y_ref[...] = jnp.dot(h.astype(bf16), wd_bf16_ref[...], preferred_element_type=jnp.float32)

    @pl.when(jnp.logical_not(valid))
    def _():
        y_ref[...] = jnp.zeros_like(y_ref)


def _moe_grouped(tile_expert, n_tiles, u_sorted, wg, wu, wd):
    d = u_sorted.shape[1]
    tm = MOE_ROW_TILE
    max_tiles = tile_expert.shape[0]
    expert = lambda i, te, nt: (te[i], 0, 0)
    grid_spec = pltpu.PrefetchScalarGridSpec(
        num_scalar_prefetch=2,
        grid=(max_tiles,),
        in_specs=[pl.BlockSpec((tm, d), lambda i, te, nt: (jnp.minimum(i, nt[0] - 1), 0)),
                  pl.BlockSpec((1, d, EXPERT_HIDDEN), expert),
                  pl.BlockSpec((1, d, EXPERT_HIDDEN), expert),
                  pl.BlockSpec((1, EXPERT_HIDDEN, d), expert)],
        out_specs=pl.BlockSpec((tm, d), lambda i, te, nt: (i, 0)),
        scratch_shapes=[pltpu.VMEM((d, EXPERT_HIDDEN), jnp.bfloat16),
                        pltpu.VMEM((d, EXPERT_HIDDEN), jnp.bfloat16),
                        pltpu.VMEM((EXPERT_HIDDEN, d), jnp.bfloat16)],
    )
    return pl.pallas_call(
        _moe_grouped_kernel,
        grid_spec=grid_spec,
        out_shape=jax.ShapeDtypeStruct((max_tiles * tm, d), jnp.float32),
        compiler_params=pltpu.CompilerParams(vmem_limit_bytes=VMEM_LIMIT,
                                             dimension_semantics=("arbitrary",)),
        name="moe_grouped",
    )(tile_expert, n_tiles, u_sorted, wg, wu, wd)


def _combine_ln2_kernel(pos_ref, pos_next_ref, y_hbm, route_ref, x1_ref, mod_ref, g2_ref, b2_ref, o_ref,
                        buf_ref, sem_ref):
    i = pl.program_id(0)
    n = pl.num_programs(0)
    tm = o_ref.shape[0]
    rows = 2 * tm

    def start_gather(rows_ref, slot):
        def body(r, _):
            pltpu.make_async_copy(y_hbm.at[pl.ds(rows_ref[0, 0, r], 1), :], buf_ref.at[slot, pl.ds(r, 1), :],
                                  sem_ref.at[slot]).start()
            return 0
        lax.fori_loop(0, rows, body, 0, unroll=GATHER_UNROLL)

    @pl.when(i == 0)
    def _():
        start_gather(pos_ref, 0)

    @pl.when(i + 1 < n)
    def _():
        start_gather(pos_next_ref, (i + 1) % 2)

    slot = i % 2
    pltpu.make_async_copy(y_hbm.at[pl.ds(0, rows), :], buf_ref.at[slot], sem_ref.at[slot]).wait()
    route = route_ref[...]
    w1 = route[:, ROUTE_W1:ROUTE_W1 + 1]
    w2 = route[:, ROUTE_W2:ROUTE_W2 + 1]
    y = w1 * buf_ref[slot, 0:tm, :] + w2 * buf_ref[slot, tm:, :]
    gate2 = mod_ref[0, 5:6, :]
    o_ref[...] = _ln(DEEPNORM_ALPHA * x1_ref[...] + (1.0 + gate2) * y) * g2_ref[...] + b2_ref[...]


def _combine_ln2(pos, y_sorted, route, x1, mod, ln2_g, ln2_b, seq):
    t, d = x1.shape
    tm = COMBINE_ROW_TILE
    n = t // tm
    tiles_per_seq = seq // tm
    row = lambda i: (i, 0)
    const2 = lambda i: (0, 0)
    return pl.pallas_call(
        _combine_ln2_kernel,
        grid=(n,),
        in_specs=[pl.BlockSpec((1, 1, 2 * tm), lambda i: (i, 0, 0), memory_space=pltpu.SMEM),
                  pl.BlockSpec((1, 1, 2 * tm), lambda i: (jnp.minimum(i + 1, n - 1), 0, 0),
                               memory_space=pltpu.SMEM),
                  pl.BlockSpec(memory_space=pl.ANY),
                  pl.BlockSpec((tm, LANES), row),
                  pl.BlockSpec((tm, d), row),
                  pl.BlockSpec((1, N_MOD, d), lambda i: (i // tiles_per_seq, 0, 0)),
                  pl.BlockSpec((1, d), const2),
                  pl.BlockSpec((1, d), const2)],
        out_specs=pl.BlockSpec((tm, d), row),
        out_shape=jax.ShapeDtypeStruct((t, d), jnp.float32),
        scratch_shapes=[pltpu.VMEM((2, 2 * tm, d), jnp.float32),
                        pltpu.SemaphoreType.DMA((2,))],
        compiler_params=pltpu.CompilerParams(vmem_limit_bytes=VMEM_LIMIT,
                                             dimension_semantics=("arbitrary",)),
        name="combine_ln2",
    )(pos, pos, y_sorted, route, x1, mod, ln2_g, ln2_b)


def _dispatch_plan(route, counts):
    t = route.shape[0]
    tm = MOE_ROW_TILE
    max_tiles = 2 * t // tm + N_EXPERTS
    counts = counts[0, :N_EXPERTS].astype(jnp.int32)
    tiles = (counts + tm - 1) // tm
    tile_end = jnp.cumsum(tiles)
    row_start = (tile_end - tiles) * tm
    experts = route[:, ROUTE_E1:ROUTE_E2 + 1].astype(jnp.int32)
    ranks = route[:, ROUTE_RANK1:ROUTE_RANK2 + 1].astype(jnp.int32)
    is_expert = experts[:, :, None] == jnp.arange(N_EXPERTS, dtype=jnp.int32)
    pos = jnp.sum(jnp.where(is_expert, row_start, 0), axis=-1) + ranks
    tile_ids = jnp.arange(max_tiles, dtype=jnp.int32)
    tile_expert = jnp.minimum(jnp.sum(tile_end[None, :] <= tile_ids[:, None], axis=1), N_EXPERTS - 1)
    by_tile = lambda tc: pos.reshape(t // tc, tc, 2).transpose(0, 2, 1).reshape(t // tc, 1, 2 * tc)
    pad_start = (row_start + counts) // SUBLANES * SUBLANES
    return (tile_expert.astype(jnp.int32), tile_end[-1:].astype(jnp.int32), pad_start.astype(jnp.int32),
            by_tile(DISPATCH_ROW_TILE), by_tile(COMBINE_ROW_TILE), max_tiles * tm)


def kernel(x, c, w_ada, b_ada, w_in, w_pool, pool_scale, w_out, ln1_g, ln1_b, w_router_group, b_router_group,
           w_router_expert, b_router_expert, w_gate, w_up, w_down, ln2_g, ln2_b):
    b, s, d = x.shape
    bf16 = jnp.bfloat16
    for layer in range(DEPTH):
        c_pad = jnp.pad(c, ((0, SUBLANES - b), (0, 0)))
        mod = _adaln(c_pad, w_ada[layer], b_ada[layer][None, :])[:b].reshape(b, N_MOD, d)
        x2d = x.reshape(b * s, d)
        qkv, p = _ln_inproj(x2d, mod, w_in[layer].astype(bf16), s)
        o_sb = _sb_attn(qkv, b, s).reshape(b * s, SB_WIDTH)
        pad = LANES - N_EXPERTS - N_EXPERT_GROUPS
        w_router = jnp.pad(jnp.concatenate([w_router_expert[layer], w_router_group[layer]], axis=1),
                           ((0, 0), (0, pad)))
        b_router = jnp.pad(jnp.concatenate([b_router_expert[layer], b_router_group[layer]]), (0, pad))[None, :]
        w_router_hi = w_router.astype(bf16)
        w_router_lo = (w_router - w_router_hi.astype(jnp.float32)).astype(bf16)
        w_router = jnp.concatenate([w_router_hi, w_router_hi, w_router_lo], axis=0)
        x1, u2, route, counts = _mix_ln1(o_sb, p, w_pool[layer].astype(bf16), pool_scale[layer][None, :],
                                         w_out[layer].astype(bf16), x2d, mod, ln1_g[layer][None, :],
                                         ln1_b[layer][None, :], w_router, b_router, s)
        tile_expert, n_tiles, pad_start, pos_dispatch, pos, n_rows = _dispatch_plan(route, counts)
        u_sorted = _dispatch(pad_start, n_tiles, pos_dispatch, u2, n_rows)
        y_sorted = _moe_grouped(tile_expert, n_tiles, u_sorted, w_gate[layer], w_up[layer], w_down[layer])
        x2 = _combine_ln2(pos, y_sorted, route, x1, mod, ln2_g[layer][None, :], ln2_b[layer][None, :], s)
        x = x2.reshape(b, s, d)
    return x
```

```python
import functools

import jax
import jax.numpy as jnp
from jax import lax
from jax.experimental import pallas as pl
from jax.experimental.pallas import tpu as pltpu

D_MODEL = 1024
N_SB_HEADS = 8
SB_HEAD_DIM = 64
SB_WIDTH = N_SB_HEADS * SB_HEAD_DIM
POOL_WINDOWS = (2, 4, 8, 16)
POOL_GROUP_DIM = 128
POOL_WIDTH = len(POOL_WINDOWS) * POOL_GROUP_DIM
N_EXPERT_GROUPS = 4
EXPERTS_PER_GROUP = 4
N_EXPERTS = N_EXPERT_GROUPS * EXPERTS_PER_GROUP
EXPERT_HIDDEN = 512
DEPTH = 1
DEEPNORM_ALPHA = (2.0 * DEPTH) ** 0.25
LN_EPS = 1e-5
N_MOD = 6

LANES = 128
SUBLANES = 8
HALO = max(POOL_WINDOWS)
VMEM_LIMIT = 56 * 1024 * 1024

ROW_TILE = 512
ATTN_TILE = 256
STICK_GONE_LOG2 = -180.0
MOE_ROW_TILE = 512
DISPATCH_ROW_TILE = 1024
COMBINE_ROW_TILE = 256
GATHER_UNROLL = 8


ROW_CHUNKS = D_MODEL // LANES


def _store_tile_rows(ref, base, value):
    n = value.shape[0]
    for c in range(ROW_CHUNKS):
        ref[pl.ds(base + c, n, stride=ROW_CHUNKS), :] = value[:, c * LANES:(c + 1) * LANES]


def _load_tile_rows(ref, base, n):
    return jnp.concatenate([ref[pl.ds(base + c, n, stride=ROW_CHUNKS), :] for c in range(ROW_CHUNKS)], axis=1)


def _ln(x):
    mu = jnp.mean(x, axis=-1, keepdims=True)
    xc = x - mu
    var = jnp.mean(xc * xc, axis=-1, keepdims=True)
    return xc * lax.rsqrt(var + LN_EPS)


def _adaln_kernel(c_ref, w_ref, b_ref, o_ref):
    c = c_ref[...]
    a = c * jax.nn.sigmoid(c)
    o_ref[...] = jnp.dot(a, w_ref[...], preferred_element_type=jnp.float32,
                         precision=lax.Precision.HIGHEST) + b_ref[...]


def _adaln(c_pad, w_ada, b_ada):
    rows, d = c_pad.shape
    n = w_ada.shape[1]
    tn = 1024
    return pl.pallas_call(
        _adaln_kernel,
        grid=(n // tn,),
        in_specs=[pl.BlockSpec((rows, d), lambda j: (0, 0)),
                  pl.BlockSpec((d, tn), lambda j: (0, j)),
                  pl.BlockSpec((1, tn), lambda j: (0, j))],
        out_specs=pl.BlockSpec((rows, tn), lambda j: (0, j)),
        out_shape=jax.ShapeDtypeStruct((rows, n), jnp.float32),
        compiler_params=pltpu.CompilerParams(vmem_limit_bytes=VMEM_LIMIT),
        name="adaln",
    )(c_pad, w_ada, b_ada)


def _ln_inproj_kernel(x_ref, mod_ref, w_ref, qkv_ref, p_ref):
    shift = mod_ref[0, 0:1, :]
    scale = mod_ref[0, 1:2, :]
    u = (_ln(x_ref[...]) * (1.0 + scale) + shift).astype(jnp.bfloat16)
    qk_scale = SB_HEAD_DIM ** -0.5 * 1.4426950408889634
    q = jnp.dot(u, w_ref[:, 0:SB_WIDTH], preferred_element_type=jnp.float32) * qk_scale
    qkv_ref[:, 0:SB_WIDTH] = q.astype(jnp.bfloat16)
    for j in (1, 2):
        kv = jnp.dot(u, w_ref[:, j * SB_WIDTH:(j + 1) * SB_WIDTH], preferred_element_type=jnp.float32)
        qkv_ref[:, j * SB_WIDTH:(j + 1) * SB_WIDTH] = kv.astype(jnp.bfloat16)
    p_ref[...] = jnp.dot(u, w_ref[:, 3 * SB_WIDTH:], preferred_element_type=jnp.float32)


def _ln_inproj(x2d, mod, w_in_bf16, seq):
    t, d = x2d.shape
    tm = ROW_TILE
    tiles_per_seq = seq // tm
    return pl.pallas_call(
        _ln_inproj_kernel,
        grid=(t // tm,),
        in_specs=[pl.BlockSpec((tm, d), lambda i: (i, 0)),
                  pl.BlockSpec((1, N_MOD, d), lambda i: (i // tiles_per_seq, 0, 0)),
                  pl.BlockSpec(w_in_bf16.shape, lambda i: (0, 0))],
        out_specs=[pl.BlockSpec((tm, 3 * SB_WIDTH), lambda i: (i, 0)),
                   pl.BlockSpec((tm, POOL_WIDTH), lambda i: (i, 0))],
        out_shape=[jax.ShapeDtypeStruct((t, 3 * SB_WIDTH), jnp.bfloat16),
                   jax.ShapeDtypeStruct((t, POOL_WIDTH), jnp.float32)],
        compiler_params=pltpu.CompilerParams(vmem_limit_bytes=VMEM_LIMIT),
        name="ln_inproj",
    )(x2d, mod, w_in_bf16)


def _sb_attn_kernel(q_ref, k_ref, v_ref, o_ref, acc_ref, carry_ref):
    tq = tk = ATTN_TILE
    heads = (0, 1)
    qi = pl.program_id(2)
    q2 = q_ref[0]
    lane = lax.broadcasted_iota(jnp.int32, (tq, LANES), 1)
    first_head = lane < SB_HEAD_DIM
    zero = jnp.zeros_like(q2)
    q_heads = (jnp.where(first_head, q2, zero), jnp.where(first_head, zero, q2))

    row = lax.broadcasted_iota(jnp.int32, (tk, tk), 0)
    col = lax.broadcasted_iota(jnp.int32, (tk, tk), 1)
    neg_suffix = jnp.where(row >= col, -1.0, 0.0).astype(jnp.bfloat16)
    causal = col < row

    def visit(j, carries, diagonal):
        start = pl.multiple_of(jnp.maximum(qi - j, 0) * tk, tk)
        k_blk = k_ref[0, pl.ds(start, tk), :]
        v_blk = v_ref[0, pl.ds(start, tk), :]
        fixed, rounded = [], []
        for h in heads:
            z = lax.dot_general(q_heads[h], k_blk, (((1,), (1,)), ((), ())), preferred_element_type=jnp.float32)
            n = jnp.maximum(z, 0.0) + jnp.log2(1.0 + jnp.exp2(-jnp.abs(z)))
            if diagonal:
                n = jnp.where(causal, n, 0.0)
            n_bf16 = n.astype(jnp.bfloat16)
            rounded.append(n_bf16)
            fixed.append(z - (n - n_bf16.astype(jnp.float32)))
        incls = [jnp.dot(r, neg_suffix, preferred_element_type=jnp.float32) for r in rounded]
        outs, new_carries = [], []
        for h in heads:
            carry = carries[h]
            att = jnp.exp2(fixed[h] + incls[h] + jnp.concatenate([carry] * (tk // LANES), axis=1))
            if diagonal:
                att = jnp.where(causal, att, 0.0)
            outs.append(jnp.dot(att.astype(jnp.bfloat16), v_blk, preferred_element_type=jnp.float32))
            new_carries.append(carry + jnp.broadcast_to(incls[h][:, 0:1], (tq, LANES)))
        return outs, new_carries

    zero_carry = jnp.zeros((tq, LANES), jnp.float32)
    out0, carries = visit(0, [zero_carry, zero_carry], diagonal=True)
    out1, carries = visit(1, carries, diagonal=False)
    has_second = qi >= 1
    for h in heads:
        acc_ref[h] = out0[h] + jnp.where(has_second, out1[h], 0.0)
        carry_ref[h] = carries[h]

    def more(state):
        j, stick_gone = state
        return (j <= qi) & jnp.logical_not(stick_gone)

    def visit_next(state):
        j, _ = state
        outs, new_carries = visit(j, [carry_ref[h] for h in heads], diagonal=False)
        for h in heads:
            acc_ref[h] += outs[h]
            carry_ref[h] = new_carries[h]
        return j + 1, jnp.max(carry_ref[...]) < STICK_GONE_LOG2

    lax.while_loop(more, visit_next, (jnp.int32(2), jnp.max(carry_ref[...]) < STICK_GONE_LOG2))
    o_ref[0] = jnp.where(first_head, acc_ref[0], acc_ref[1]).astype(o_ref.dtype)


def _sb_attn(qkv, batch, seq):
    qkv3 = qkv.reshape(batch, seq, 3 * SB_WIDTH)
    tq = ATTN_TILE
    pairs = SB_WIDTH // LANES
    return pl.pallas_call(
        _sb_attn_kernel,
        grid=(batch, pairs, seq // tq),
        in_specs=[pl.BlockSpec((1, tq, LANES), lambda b, h, i: (b, i, h)),
                  pl.BlockSpec((1, seq, LANES), lambda b, h, i: (b, 0, pairs + h)),
                  pl.BlockSpec((1, seq, LANES), lambda b, h, i: (b, 0, 2 * pairs + h))],
        out_specs=pl.BlockSpec((1, tq, LANES), lambda b, h, i: (b, i, h)),
        out_shape=jax.ShapeDtypeStruct((batch, seq, SB_WIDTH), jnp.bfloat16),
        scratch_shapes=[pltpu.VMEM((2, tq, LANES), jnp.float32),
                        pltpu.VMEM((2, tq, LANES), jnp.float32)],
        compiler_params=pltpu.CompilerParams(vmem_limit_bytes=VMEM_LIMIT),
        name="sb_attn",
    )(qkv3, qkv3, qkv3)


def _route(logits):
    lane = lax.broadcasted_iota(jnp.int32, logits.shape, 1)
    neg = jnp.float32(-jnp.inf)
    big = jnp.int32(LANES)
    is_group = (lane >= N_EXPERTS) & (lane < N_EXPERTS + N_EXPERT_GROUPS)
    gl = jnp.where(is_group, logits, neg)
    g_max = jnp.max(gl, axis=-1, keepdims=True)
    g_sum = jnp.sum(jnp.exp(gl - g_max), axis=-1, keepdims=True)
    g_p = 1.0 / g_sum
    g_idx = jnp.min(jnp.where(gl == g_max, lane, big), axis=-1, keepdims=True) - N_EXPERTS
    in_group = (lane >= g_idx * EXPERTS_PER_GROUP) & (lane < (g_idx + 1) * EXPERTS_PER_GROUP)
    el = jnp.where(in_group, logits, neg)
    e1 = jnp.max(el, axis=-1, keepdims=True)
    i1 = jnp.min(jnp.where(el == e1, lane, big), axis=-1, keepdims=True)
    el2 = jnp.where(lane == i1, neg, el)
    e2 = jnp.max(el2, axis=-1, keepdims=True)
    i2 = jnp.min(jnp.where(el2 == e2, lane, big), axis=-1, keepdims=True)
    r = jnp.exp(e2 - e1)
    w1 = g_p / (1.0 + r)
    w2 = w1 * r
    return i1, i2, w1, w2


ROUTE_W1, ROUTE_W2, ROUTE_E1, ROUTE_E2, ROUTE_RANK1, ROUTE_RANK2 = range(6)


def _mix_ln1_kernel(tiles_per_seq, osb_ref, p_ref, halo_ref, wpool_ref, pscale_ref, wout_ref, x_ref, mod_ref,
                    g1_ref, b1_ref, wr_ref, br_ref, earlier_ref, x1_ref, u2_ref, route_ref, count_ref, pext_ref,
                    running_ref):
    tm = p_ref.shape[0]
    tile_in_seq = pl.program_id(0) % tiles_per_seq
    p = p_ref[...]
    pext_ref[0:HALO, :] = jnp.where(tile_in_seq == 0, 0.0, halo_ref[...])
    pext_ref[HALO:, :] = p
    pos = tile_in_seq * tm + lax.broadcasted_iota(jnp.int32, (tm, 1), 0)

    mixer_out = [osb_ref[...]]
    for g, w in enumerate(POOL_WINDOWS):
        cols = slice(g * POOL_GROUP_DIM, (g + 1) * POOL_GROUP_DIM)
        win = p[:, cols]
        for i in range(1, w):
            win = win + pext_ref[HALO - i:HALO - i + tm, cols]
        count = jnp.minimum(pos + 1, w).astype(jnp.float32)
        pooled = win / count - p[:, cols]
        o_pool = jnp.dot(pooled.astype(jnp.bfloat16), wpool_ref[g], preferred_element_type=jnp.float32)
        mixer_out.append((o_pool * pscale_ref[:, cols]).astype(jnp.bfloat16))
    mixed = jnp.dot(jnp.concatenate(mixer_out, axis=1), wout_ref[...], preferred_element_type=jnp.float32)

    gate1 = mod_ref[0, 2:3, :]
    shift2 = mod_ref[0, 3:4, :]
    scale2 = mod_ref[0, 4:5, :]
    x1 = _ln(DEEPNORM_ALPHA * x_ref[...] + (1.0 + gate1) * mixed) * g1_ref[...] + b1_ref[...]
    x1_ref[...] = x1
    u2 = _ln(x1) * (1.0 + scale2) + shift2
    _store_tile_rows(u2_ref, 0, u2)
    u2_hi = u2.astype(jnp.bfloat16)
    u2_lo = (u2 - u2_hi.astype(jnp.float32)).astype(jnp.bfloat16)
    logits = jnp.dot(jnp.concatenate([u2_hi, u2_lo, u2_hi], axis=1), wr_ref[...],
                     preferred_element_type=jnp.float32) + br_ref[...]
    i1, i2, w1, w2 = _route(logits)

    @pl.when(pl.program_id(0) == 0)
    def _():
        running_ref[...] = jnp.zeros_like(running_ref)

    lane = lax.broadcasted_iota(jnp.int32, (tm, LANES), 1)
    first, second = lane == i1, lane == i2
    uses = jnp.where(first | second, 1.0, 0.0)
    before = jnp.dot(earlier_ref[...], uses.astype(jnp.bfloat16), preferred_element_type=jnp.float32)
    before = before + running_ref[0:1, :]
    rank1 = jnp.sum(jnp.where(first, before, 0.0), axis=-1, keepdims=True)
    rank2 = jnp.sum(jnp.where(second, before, 0.0), axis=-1, keepdims=True)
    running = running_ref[0:1, :] + jnp.sum(uses, axis=0, keepdims=True)
    running_ref[...] = jnp.broadcast_to(running, running_ref.shape)
    count_ref[...] = jnp.broadcast_to(running, count_ref.shape)

    record = jnp.zeros((tm, LANES), jnp.float32)
    for slot, value in ((ROUTE_W1, w1), (ROUTE_W2, w2), (ROUTE_E1, i1.astype(jnp.float32)),
                        (ROUTE_E2, i2.astype(jnp.float32)), (ROUTE_RANK1, rank1), (ROUTE_RANK2, rank2)):
        record = jnp.where(lane == slot, value, record)
    route_ref[...] = record


def _mix_ln1(o_sb, p, w_pool_bf16, pool_scale, w_out_bf16, x2d, mod, ln1_g, ln1_b, w_router, b_router, seq):
    t, d = x2d.shape
    tm = ROW_TILE
    tiles_per_seq = seq // tm
    halo_blocks_per_tile = tm // HALO
    row = lambda i: (i, 0)
    const2 = lambda i: (0, 0)
    return pl.pallas_call(
        functools.partial(_mix_ln1_kernel, tiles_per_seq),
        grid=(t // tm,),
        in_specs=[pl.BlockSpec((tm, SB_WIDTH), row),
                  pl.BlockSpec((tm, POOL_WIDTH), row),
                  pl.BlockSpec((HALO, POOL_WIDTH), lambda i: (jnp.maximum(i * halo_blocks_per_tile - 1, 0), 0)),
                  pl.BlockSpec(w_pool_bf16.shape, lambda i: (0, 0, 0)),
                  pl.BlockSpec((1, POOL_WIDTH), const2),
                  pl.BlockSpec(w_out_bf16.shape, const2),
                  pl.BlockSpec((tm, d), row),
                  pl.BlockSpec((1, N_MOD, d), lambda i: (i // tiles_per_seq, 0, 0)),
                  pl.BlockSpec((1, d), const2),
                  pl.BlockSpec((1, d), const2),
                  pl.BlockSpec(w_router.shape, const2),
                  pl.BlockSpec((1, LANES), const2),
                  pl.BlockSpec((tm, tm), const2)],
        out_specs=[pl.BlockSpec((tm, d), row),
                   pl.BlockSpec((tm * ROW_CHUNKS, LANES), row),
                   pl.BlockSpec((tm, LANES), row),
                   pl.BlockSpec((SUBLANES, LANES), const2)],
        out_shape=[jax.ShapeDtypeStruct((t, d), jnp.float32),
                   jax.ShapeDtypeStruct((t * ROW_CHUNKS, LANES), jnp.float32),
                   jax.ShapeDtypeStruct((t, LANES), jnp.float32),
                   jax.ShapeDtypeStruct((SUBLANES, LANES), jnp.float32)],
        scratch_shapes=[pltpu.VMEM((HALO + tm, POOL_WIDTH), jnp.float32),
                        pltpu.VMEM((SUBLANES, LANES), jnp.float32)],
        compiler_params=pltpu.CompilerParams(vmem_limit_bytes=VMEM_LIMIT,
                                             dimension_semantics=("arbitrary",)),
        name="mix_ln1",
    )(o_sb, p, p, w_pool_bf16, pool_scale, w_out_bf16, x2d, mod, ln1_g, ln1_b, w_router, b_router,
      jnp.tri(tm, tm, -1, dtype=jnp.bfloat16))


def _dispatch_kernel(pad_start_ref, n_tiles_ref, pos_ref, u_ref, sorted_hbm, zeros_ref, sem):
    tm = u_ref.shape[0] // ROW_CHUNKS
    tile = MOE_ROW_TILE * ROW_CHUNKS
    stored = lambda r: pl.ds(pl.multiple_of(r * ROW_CHUNKS, ROW_CHUNKS), ROW_CHUNKS)

    @pl.when(pl.program_id(0) == 0)
    def _():
        zeros_ref[...] = jnp.zeros_like(zeros_ref)
        fills = [pltpu.make_async_copy(
            zeros_ref, sorted_hbm.at[pl.ds(pl.multiple_of(pad_start_ref[e] * ROW_CHUNKS, ROW_CHUNKS), tile), :], sem)
            for e in range(N_EXPERTS)]
        for fill in fills:
            fill.start()
        for fill in fills:
            fill.wait()

        def zero_tile(i, _):
            fill = pltpu.make_async_copy(zeros_ref, sorted_hbm.at[pl.ds(pl.multiple_of(i * tile, tile), tile), :],
                                         sem)
            fill.start()
            fill.wait()
            return 0

        lax.fori_loop(n_tiles_ref[0], sorted_hbm.shape[0] // tile, zero_tile, 0)

    def body(r, _):
        row = u_ref.at[stored(r), :]
        pltpu.make_async_copy(row, sorted_hbm.at[stored(pos_ref[0, 0, r]), :], sem).start()
        pltpu.make_async_copy(row, sorted_hbm.at[stored(pos_ref[0, 0, tm + r]), :], sem).start()
        return 0

    lax.fori_loop(0, tm, body, 0, unroll=GATHER_UNROLL // 2)
    for _ in range(2):
        pltpu.make_async_copy(u_ref, sorted_hbm.at[pl.ds(0, tm * ROW_CHUNKS), :], sem).wait()


def _dispatch(pad_start, n_tiles, pos, u2, n_rows):
    tm = pos.shape[2] // 2
    t = u2.shape[0] // ROW_CHUNKS
    grid_spec = pltpu.PrefetchScalarGridSpec(
        num_scalar_prefetch=2,
        grid=(t // tm,),
        in_specs=[pl.BlockSpec((1, 1, 2 * tm), lambda i, ps, nt: (i, 0, 0), memory_space=pltpu.SMEM),
                  pl.BlockSpec((tm * ROW_CHUNKS, LANES), lambda i, ps, nt: (i, 0))],
        out_specs=pl.BlockSpec(memory_space=pl.ANY),
        scratch_shapes=[pltpu.VMEM((MOE_ROW_TILE * ROW_CHUNKS, LANES), jnp.float32),
                        pltpu.SemaphoreType.DMA(())],
    )
    return pl.pallas_call(
        _dispatch_kernel,
        grid_spec=grid_spec,
        out_shape=jax.ShapeDtypeStruct(((n_rows + MOE_ROW_TILE) * ROW_CHUNKS, LANES), jnp.float32),
        compiler_params=pltpu.CompilerParams(vmem_limit_bytes=VMEM_LIMIT,
                                             dimension_semantics=("arbitrary",)),
        name="dispatch",
    )(pad_start, n_tiles, pos, u2)


def _moe_grouped_kernel(tile_expert_ref, n_tiles_ref, u_ref, wg_ref, wu_ref, wd_ref, y_ref,
                        wg_bf16_ref, wu_bf16_ref, wd_bf16_ref):
    i = pl.program_id(0)
    bf16 = jnp.bfloat16
    valid = i < n_tiles_ref[0]

    @pl.when(valid & ((i == 0) | (tile_expert_ref[i] != tile_expert_ref[jnp.maximum(i - 1, 0)])))
    def _():
        wg_bf16_ref[...] = wg_ref[0].astype(bf16)
        wu_bf16_ref[...] = wu_ref[0].astype(bf16)
        wd_bf16_ref[...] = wd_ref[0].astype(bf16)

    @pl.when(valid)
    def _():
        u = _load_tile_rows(u_ref, 0, MOE_ROW_TILE).astype(bf16)
        gate = jnp.dot(u, wg_bf16_ref[...], preferred_element_type=jnp.float32)
        up = jnp.dot(u, wu_bf16_ref[...], preferred_element_type=jnp.float32)
        h = gate * jax.nn.sigmoid(gate) * up
        _store_tile_rows(y_ref, 0, jnp.dot(h.astype(bf16), wd_bf16_ref[...], preferred_element_type=jnp.float32))

    @pl.when(jnp.logical_not(valid))
    def _():
        y_ref[...] = jnp.zeros_like(y_ref)


def _moe_grouped(tile_expert, n_tiles, u_sorted, wg, wu, wd):
    d = D_MODEL
    tm = MOE_ROW_TILE
    max_tiles = tile_expert.shape[0]
    expert = lambda i, te, nt: (te[i], 0, 0)
    grid_spec = pltpu.PrefetchScalarGridSpec(
        num_scalar_prefetch=2,
        grid=(max_tiles,),
        in_specs=[pl.BlockSpec((tm * ROW_CHUNKS, LANES), lambda i, te, nt: (jnp.minimum(i, nt[0] - 1), 0)),
                  pl.BlockSpec((1, d, EXPERT_HIDDEN), expert),
                  pl.BlockSpec((1, d, EXPERT_HIDDEN), expert),
                  pl.BlockSpec((1, EXPERT_HIDDEN, d), expert)],
        out_specs=pl.BlockSpec((tm * ROW_CHUNKS, LANES), lambda i, te, nt: (i, 0)),
        scratch_shapes=[pltpu.VMEM((d, EXPERT_HIDDEN), jnp.bfloat16),
                        pltpu.VMEM((d, EXPERT_HIDDEN), jnp.bfloat16),
                        pltpu.VMEM((EXPERT_HIDDEN, d), jnp.bfloat16)],
    )
    return pl.pallas_call(
        _moe_grouped_kernel,
        grid_spec=grid_spec,
        out_shape=jax.ShapeDtypeStruct((max_tiles * tm * ROW_CHUNKS, LANES), jnp.float32),
        compiler_params=pltpu.CompilerParams(vmem_limit_bytes=VMEM_LIMIT,
                                             dimension_semantics=("arbitrary",)),
        name="moe_grouped",
    )(tile_expert, n_tiles, u_sorted, wg, wu, wd)


def _combine_ln2_kernel(pos_ref, pos_next_ref, y_hbm, route_ref, x1_ref, mod_ref, g2_ref, b2_ref, o_ref,
                        buf_ref, sem_ref):
    i = pl.program_id(0)
    n = pl.num_programs(0)
    tm = o_ref.shape[0]
    rows = 2 * tm
    stored = lambda r: pl.ds(pl.multiple_of(r * ROW_CHUNKS, ROW_CHUNKS), ROW_CHUNKS)

    def start_gather(rows_ref, slot):
        def body(r, _):
            pltpu.make_async_copy(y_hbm.at[stored(rows_ref[0, 0, r]), :], buf_ref.at[slot, stored(r), :],
                                  sem_ref.at[slot]).start()
            return 0
        lax.fori_loop(0, rows, body, 0, unroll=GATHER_UNROLL)

    @pl.when(i == 0)
    def _():
        start_gather(pos_ref, 0)

    @pl.when(i + 1 < n)
    def _():
        start_gather(pos_next_ref, (i + 1) % 2)

    slot = i % 2
    pltpu.make_async_copy(y_hbm.at[pl.ds(0, rows * ROW_CHUNKS), :], buf_ref.at[slot], sem_ref.at[slot]).wait()
    route = route_ref[...]
    w1 = route[:, ROUTE_W1:ROUTE_W1 + 1]
    w2 = route[:, ROUTE_W2:ROUTE_W2 + 1]
    gathered = buf_ref.at[slot]
    y = w1 * _load_tile_rows(gathered, 0, tm) + w2 * _load_tile_rows(gathered, tm * ROW_CHUNKS, tm)
    gate2 = mod_ref[0, 5:6, :]
    o_ref[...] = _ln(DEEPNORM_ALPHA * x1_ref[...] + (1.0 + gate2) * y) * g2_ref[...] + b2_ref[...]


def _combine_ln2(pos, y_sorted, route, x1, mod, ln2_g, ln2_b, seq):
    t, d = x1.shape
    tm = COMBINE_ROW_TILE
    n = t // tm
    tiles_per_seq = seq // tm
    row = lambda i: (i, 0)
    const2 = lambda i: (0, 0)
    return pl.pallas_call(
        _combine_ln2_kernel,
        grid=(n,),
        in_specs=[pl.BlockSpec((1, 1, 2 * tm), lambda i: (i, 0, 0), memory_space=pltpu.SMEM),
                  pl.BlockSpec((1, 1, 2 * tm), lambda i: (jnp.minimum(i + 1, n - 1), 0, 0),
                               memory_space=pltpu.SMEM),
                  pl.BlockSpec(memory_space=pl.ANY),
                  pl.BlockSpec((tm, LANES), row),
                  pl.BlockSpec((tm, d), row),
                  pl.BlockSpec((1, N_MOD, d), lambda i: (i // tiles_per_seq, 0, 0)),
                  pl.BlockSpec((1, d), const2),
                  pl.BlockSpec((1, d), const2)],
        out_specs=pl.BlockSpec((tm, d), row),
        out_shape=jax.ShapeDtypeStruct((t, d), jnp.float32),
        scratch_shapes=[pltpu.VMEM((2, 2 * tm * ROW_CHUNKS, LANES), jnp.float32),
                        pltpu.SemaphoreType.DMA((2,))],
        compiler_params=pltpu.CompilerParams(vmem_limit_bytes=VMEM_LIMIT,
                                             dimension_semantics=("arbitrary",)),
        name="combine_ln2",
    )(pos, pos, y_sorted, route, x1, mod, ln2_g, ln2_b)


def _dispatch_plan(route, counts):
    t = route.shape[0]
    tm = MOE_ROW_TILE
    max_tiles = 2 * t // tm + N_EXPERTS
    counts = counts[0, :N_EXPERTS].astype(jnp.int32)
    tiles = (counts + tm - 1) // tm
    tile_end = jnp.cumsum(tiles)
    row_start = (tile_end - tiles) * tm
    experts = route[:, ROUTE_E1:ROUTE_E2 + 1].astype(jnp.int32)
    ranks = route[:, ROUTE_RANK1:ROUTE_RANK2 + 1].astype(jnp.int32)
    is_expert = experts[:, :, None] == jnp.arange(N_EXPERTS, dtype=jnp.int32)
    pos = jnp.sum(jnp.where(is_expert, row_start, 0), axis=-1) + ranks
    tile_ids = jnp.arange(max_tiles, dtype=jnp.int32)
    tile_expert = jnp.minimum(jnp.sum(tile_end[None, :] <= tile_ids[:, None], axis=1), N_EXPERTS - 1)
    by_tile = lambda tc: pos.reshape(t // tc, tc, 2).transpose(0, 2, 1).reshape(t // tc, 1, 2 * tc)
    pad_start = row_start + counts
    return (tile_expert.astype(jnp.int32), tile_end[-1:].astype(jnp.int32), pad_start.astype(jnp.int32),
            by_tile(DISPATCH_ROW_TILE), by_tile(COMBINE_ROW_TILE), max_tiles * tm)


def kernel(x, c, w_ada, b_ada, w_in, w_pool, pool_scale, w_out, ln1_g, ln1_b, w_router_group, b_router_group,
           w_router_expert, b_router_expert, w_gate, w_up, w_down, ln2_g, ln2_b):
    b, s, d = x.shape
    bf16 = jnp.bfloat16
    for layer in range(DEPTH):
        c_pad = jnp.pad(c, ((0, SUBLANES - b), (0, 0)))
        mod = _adaln(c_pad, w_ada[layer], b_ada[layer][None, :])[:b].reshape(b, N_MOD, d)
        x2d = x.reshape(b * s, d)
        qkv, p = _ln_inproj(x2d, mod, w_in[layer].astype(bf16), s)
        o_sb = _sb_attn(qkv, b, s).reshape(b * s, SB_WIDTH)
        pad = LANES - N_EXPERTS - N_EXPERT_GROUPS
        w_router = jnp.pad(jnp.concatenate([w_router_expert[layer], w_router_group[layer]], axis=1),
                           ((0, 0), (0, pad)))
        b_router = jnp.pad(jnp.concatenate([b_router_expert[layer], b_router_group[layer]]), (0, pad))[None, :]
        w_router_hi = w_router.astype(bf16)
        w_router_lo = (w_router - w_router_hi.astype(jnp.float32)).astype(bf16)
        w_router = jnp.concatenate([w_router_hi, w_router_hi, w_router_lo], axis=0)
        x1, u2, route, counts = _mix_ln1(o_sb, p, w_pool[layer].astype(bf16), pool_scale[layer][None, :],
                                         w_out[layer].astype(bf16), x2d, mod, ln1_g[layer][None, :],
                                         ln1_b[layer][None, :], w_router, b_router, s)
        tile_expert, n_tiles, pad_start, pos_dispatch, pos, n_rows = _dispatch_plan(route, counts)
        u_sorted = _dispatch(pad_start, n_tiles, pos_dispatch, u2, n_rows)
        y_sorted = _moe_grouped(tile_expert, n_tiles, u_sorted, w_gate[layer], w_up[layer], w_down[layer])
        x2 = _combine_ln2(pos, y_sorted, route, x1, mod, ln2_g[layer][None, :], ln2_b[layer][None, :], s)
        x = x2.reshape(b, s, d)
    return x
```

```python
import functools

import jax
import jax.numpy as jnp
from jax import lax
from jax.experimental import pallas as pl
from jax.experimental.pallas import tpu as pltpu

D_MODEL = 1024
N_SB_HEADS = 8
SB_HEAD_DIM = 64
SB_WIDTH = N_SB_HEADS * SB_HEAD_DIM
POOL_WINDOWS = (2, 4, 8, 16)
POOL_GROUP_DIM = 128
POOL_WIDTH = len(POOL_WINDOWS) * POOL_GROUP_DIM
N_EXPERT_GROUPS = 4
EXPERTS_PER_GROUP = 4
N_EXPERTS = N_EXPERT_GROUPS * EXPERTS_PER_GROUP
EXPERT_HIDDEN = 512
DEPTH = 1
DEEPNORM_ALPHA = (2.0 * DEPTH) ** 0.25
LN_EPS = 1e-5
N_MOD = 6

LANES = 128
SUBLANES = 8
HALO = max(POOL_WINDOWS)
VMEM_LIMIT = 56 * 1024 * 1024

ROW_TILE = 512
ATTN_TILE = 256
STICK_GONE_LOG2 = -180.0
MOE_ROW_TILE = 512
DISPATCH_ROW_TILE = 1024
COMBINE_ROW_TILE = 256
GATHER_UNROLL = 8


ROW_CHUNKS = D_MODEL // LANES


def _store_tile_rows(ref, base, value):
    n = value.shape[0]
    for c in range(ROW_CHUNKS):
        ref[pl.ds(base + c, n, stride=ROW_CHUNKS), :] = value[:, c * LANES:(c + 1) * LANES]


def _load_tile_rows(ref, base, n):
    return jnp.concatenate([ref[pl.ds(base + c, n, stride=ROW_CHUNKS), :] for c in range(ROW_CHUNKS)], axis=1)


def _ln(x):
    mu = jnp.mean(x, axis=-1, keepdims=True)
    xc = x - mu
    var = jnp.mean(xc * xc, axis=-1, keepdims=True)
    return xc * lax.rsqrt(var + LN_EPS)


def _adaln_kernel(c_ref, w_ref, b_ref, o_ref):
    c = c_ref[...]
    a = c * jax.nn.sigmoid(c)
    o_ref[...] = jnp.dot(a, w_ref[...], preferred_element_type=jnp.float32,
                         precision=lax.Precision.HIGHEST) + b_ref[...]


def _adaln(c_pad, w_ada, b_ada):
    rows, d = c_pad.shape
    n = w_ada.shape[1]
    tn = 1024
    return pl.pallas_call(
        _adaln_kernel,
        grid=(n // tn,),
        in_specs=[pl.BlockSpec((rows, d), lambda j: (0, 0)),
                  pl.BlockSpec((d, tn), lambda j: (0, j)),
                  pl.BlockSpec((1, tn), lambda j: (0, j))],
        out_specs=pl.BlockSpec((rows, tn), lambda j: (0, j)),
        out_shape=jax.ShapeDtypeStruct((rows, n), jnp.float32),
        compiler_params=pltpu.CompilerParams(vmem_limit_bytes=VMEM_LIMIT),
        name="adaln",
    )(c_pad, w_ada, b_ada)


def _ln_inproj_kernel(x_ref, mod_ref, w_ref, qkv_ref, p_ref):
    shift = mod_ref[0, 0:1, :]
    scale = mod_ref[0, 1:2, :]
    u = (_ln(x_ref[...]) * (1.0 + scale) + shift).astype(jnp.bfloat16)
    qk_scale = SB_HEAD_DIM ** -0.5 * 1.4426950408889634
    q = jnp.dot(u, w_ref[:, 0:SB_WIDTH], preferred_element_type=jnp.float32) * qk_scale
    qkv_ref[:, 0:SB_WIDTH] = q.astype(jnp.bfloat16)
    for j in (1, 2):
        kv = jnp.dot(u, w_ref[:, j * SB_WIDTH:(j + 1) * SB_WIDTH], preferred_element_type=jnp.float32)
        qkv_ref[:, j * SB_WIDTH:(j + 1) * SB_WIDTH] = kv.astype(jnp.bfloat16)
    p_ref[...] = jnp.dot(u, w_ref[:, 3 * SB_WIDTH:], preferred_element_type=jnp.float32)


def _ln_inproj(x2d, mod, w_in_bf16, seq):
    t, d = x2d.shape
    tm = ROW_TILE
    tiles_per_seq = seq // tm
    return pl.pallas_call(
        _ln_inproj_kernel,
        grid=(t // tm,),
        in_specs=[pl.BlockSpec((tm, d), lambda i: (i, 0)),
                  pl.BlockSpec((1, N_MOD, d), lambda i: (i // tiles_per_seq, 0, 0)),
                  pl.BlockSpec(w_in_bf16.shape, lambda i: (0, 0))],
        out_specs=[pl.BlockSpec((tm, 3 * SB_WIDTH), lambda i: (i, 0)),
                   pl.BlockSpec((tm, POOL_WIDTH), lambda i: (i, 0))],
        out_shape=[jax.ShapeDtypeStruct((t, 3 * SB_WIDTH), jnp.bfloat16),
                   jax.ShapeDtypeStruct((t, POOL_WIDTH), jnp.float32)],
        compiler_params=pltpu.CompilerParams(vmem_limit_bytes=VMEM_LIMIT),
        name="ln_inproj",
    )(x2d, mod, w_in_bf16)


def _sb_attn_kernel(q_ref, k_ref, v_ref, o_ref, acc_ref, carry_ref):
    tq = tk = ATTN_TILE
    heads = (0, 1)
    qi = pl.program_id(2)
    q2 = q_ref[0]
    lane = lax.broadcasted_iota(jnp.int32, (tq, LANES), 1)
    first_head = lane < SB_HEAD_DIM
    zero = jnp.zeros_like(q2)
    q_heads = (jnp.where(first_head, q2, zero), jnp.where(first_head, zero, q2))

    row = lax.broadcasted_iota(jnp.int32, (tk, tk), 0)
    col = lax.broadcasted_iota(jnp.int32, (tk, tk), 1)
    neg_suffix = jnp.where(row >= col, -1.0, 0.0).astype(jnp.bfloat16)
    causal = col < row

    def visit(j, carries, diagonal):
        start = pl.multiple_of(jnp.maximum(qi - j, 0) * tk, tk)
        k_blk = k_ref[0, pl.ds(start, tk), :]
        v_blk = v_ref[0, pl.ds(start, tk), :]
        fixed, rounded = [], []
        for h in heads:
            z = lax.dot_general(q_heads[h], k_blk, (((1,), (1,)), ((), ())), preferred_element_type=jnp.float32)
            n = jnp.maximum(z, 0.0) + jnp.log2(1.0 + jnp.exp2(-jnp.abs(z)))
            if diagonal:
                n = jnp.where(causal, n, 0.0)
            n_bf16 = n.astype(jnp.bfloat16)
            rounded.append(n_bf16)
            fixed.append(z - (n - n_bf16.astype(jnp.float32)))
        incls = [jnp.dot(r, neg_suffix, preferred_element_type=jnp.float32) for r in rounded]
        outs, new_carries = [], []
        for h in heads:
            carry = carries[h]
            att = jnp.exp2(fixed[h] + incls[h] + jnp.concatenate([carry] * (tk // LANES), axis=1))
            if diagonal:
                att = jnp.where(causal, att, 0.0)
            outs.append(jnp.dot(att.astype(jnp.bfloat16), v_blk, preferred_element_type=jnp.float32))
            new_carries.append(carry + jnp.broadcast_to(incls[h][:, 0:1], (tq, LANES)))
        return outs, new_carries

    zero_carry = jnp.zeros((tq, LANES), jnp.float32)
    out0, carries = visit(0, [zero_carry, zero_carry], diagonal=True)
    out1, carries = visit(1, carries, diagonal=False)
    has_second = qi >= 1
    for h in heads:
        acc_ref[h] = out0[h] + jnp.where(has_second, out1[h], 0.0)
        carry_ref[h] = carries[h]

    def more(state):
        j, stick_gone = state
        return (j <= qi) & jnp.logical_not(stick_gone)

    def visit_next(state):
        j, _ = state
        outs, new_carries = visit(j, [carry_ref[h] for h in heads], diagonal=False)
        for h in heads:
            acc_ref[h] += outs[h]
            carry_ref[h] = new_carries[h]
        return j + 1, jnp.max(carry_ref[...]) < STICK_GONE_LOG2

    lax.while_loop(more, visit_next, (jnp.int32(2), jnp.max(carry_ref[...]) < STICK_GONE_LOG2))
    o_ref[0] = jnp.where(first_head, acc_ref[0], acc_ref[1]).astype(o_ref.dtype)


def _sb_attn(qkv, batch, seq):
    qkv3 = qkv.reshape(batch, seq, 3 * SB_WIDTH)
    tq = ATTN_TILE
    pairs = SB_WIDTH // LANES
    return pl.pallas_call(
        _sb_attn_kernel,
        grid=(batch, pairs, seq // tq),
        in_specs=[pl.BlockSpec((1, tq, LANES), lambda b, h, i: (b, i, h)),
                  pl.BlockSpec((1, seq, LANES), lambda b, h, i: (b, 0, pairs + h)),
                  pl.BlockSpec((1, seq, LANES), lambda b, h, i: (b, 0, 2 * pairs + h))],
        out_specs=pl.BlockSpec((1, tq, LANES), lambda b, h, i: (b, i, h)),
        out_shape=jax.ShapeDtypeStruct((batch, seq, SB_WIDTH), jnp.bfloat16),
        scratch_shapes=[pltpu.VMEM((2, tq, LANES), jnp.float32),
                        pltpu.VMEM((2, tq, LANES), jnp.float32)],
        compiler_params=pltpu.CompilerParams(vmem_limit_bytes=VMEM_LIMIT),
        name="sb_attn",
    )(qkv3, qkv3, qkv3)


def _route(logits):
    lane = lax.broadcasted_iota(jnp.int32, logits.shape, 1).astype(jnp.float32)
    neg = jnp.float32(-jnp.inf)
    big = jnp.float32(LANES)
    is_group = (lane >= N_EXPERTS) & (lane < N_EXPERTS + N_EXPERT_GROUPS)
    gl = jnp.where(is_group, logits, neg)
    g_max = jnp.max(gl, axis=-1, keepdims=True)
    g_sum = jnp.sum(jnp.exp(gl - g_max), axis=-1, keepdims=True)
    g_p = 1.0 / g_sum
    g_idx = jnp.min(jnp.where(gl == g_max, lane, big), axis=-1, keepdims=True) - N_EXPERTS
    in_group = (lane >= g_idx * EXPERTS_PER_GROUP) & (lane < (g_idx + 1) * EXPERTS_PER_GROUP)
    el = jnp.where(in_group, logits, neg)
    e1 = jnp.max(el, axis=-1, keepdims=True)
    i1 = jnp.min(jnp.where(el == e1, lane, big), axis=-1, keepdims=True)
    el2 = jnp.where(lane == i1, neg, el)
    e2 = jnp.max(el2, axis=-1, keepdims=True)
    i2 = jnp.min(jnp.where(el2 == e2, lane, big), axis=-1, keepdims=True)
    r = jnp.exp(e2 - e1)
    w1 = g_p / (1.0 + r)
    w2 = w1 * r
    return i1, i2, w1, w2


ROUTE_W1, ROUTE_W2, ROUTE_E1, ROUTE_E2, ROUTE_RANK1, ROUTE_RANK2 = range(6)


def _mix_ln1_kernel(tiles_per_seq, osb_ref, p_ref, halo_ref, wpool_ref, pscale_ref, wout_ref, x_ref, mod_ref,
                    g1_ref, b1_ref, wr_ref, br_ref, earlier_ref, x1_ref, u2_ref, route_ref, count_ref, pext_ref,
                    running_ref):
    tm = p_ref.shape[0]
    tile_in_seq = pl.program_id(0) % tiles_per_seq
    p = p_ref[...]
    pext_ref[0:HALO, :] = jnp.where(tile_in_seq == 0, 0.0, halo_ref[...])
    pext_ref[HALO:, :] = p
    pos = tile_in_seq * tm + lax.broadcasted_iota(jnp.int32, (tm, 1), 0)

    mixer_out = [osb_ref[...]]
    for g, w in enumerate(POOL_WINDOWS):
        cols = slice(g * POOL_GROUP_DIM, (g + 1) * POOL_GROUP_DIM)
        win = p[:, cols]
        for i in range(1, w):
            win = win + pext_ref[HALO - i:HALO - i + tm, cols]
        count = jnp.minimum(pos + 1, w).astype(jnp.float32)
        pooled = win / count - p[:, cols]
        o_pool = jnp.dot(pooled.astype(jnp.bfloat16), wpool_ref[g], preferred_element_type=jnp.float32)
        mixer_out.append((o_pool * pscale_ref[:, cols]).astype(jnp.bfloat16))
    mixed = jnp.dot(jnp.concatenate(mixer_out, axis=1), wout_ref[...], preferred_element_type=jnp.float32)

    gate1 = mod_ref[0, 2:3, :]
    shift2 = mod_ref[0, 3:4, :]
    scale2 = mod_ref[0, 4:5, :]
    x1 = _ln(DEEPNORM_ALPHA * x_ref[...] + (1.0 + gate1) * mixed) * g1_ref[...] + b1_ref[...]
    x1_ref[...] = x1
    u2 = _ln(x1) * (1.0 + scale2) + shift2
    _store_tile_rows(u2_ref, 0, u2)
    u2_hi = u2.astype(jnp.bfloat16)
    u2_lo = (u2 - u2_hi.astype(jnp.float32)).astype(jnp.bfloat16)
    logits = jnp.dot(jnp.concatenate([u2_hi, u2_lo, u2_hi], axis=1), wr_ref[...],
                     preferred_element_type=jnp.float32) + br_ref[...]
    i1, i2, w1, w2 = _route(logits)

    @pl.when(pl.program_id(0) == 0)
    def _():
        running_ref[...] = jnp.zeros_like(running_ref)

    lane = lax.broadcasted_iota(jnp.int32, (tm, LANES), 1)
    lane_f = lane.astype(jnp.float32)
    first, second = lane_f == i1, lane_f == i2
    uses = jnp.where(first | second, 1.0, 0.0)
    before = jnp.dot(earlier_ref[...], uses.astype(jnp.bfloat16), preferred_element_type=jnp.float32)
    before = before + running_ref[0:1, :]
    rank1 = jnp.sum(jnp.where(first, before, 0.0), axis=-1, keepdims=True)
    rank2 = jnp.sum(jnp.where(second, before, 0.0), axis=-1, keepdims=True)
    running = running_ref[0:1, :] + jnp.sum(uses, axis=0, keepdims=True)
    running_ref[...] = jnp.broadcast_to(running, running_ref.shape)
    count_ref[...] = jnp.broadcast_to(running, count_ref.shape)

    record = jnp.zeros((tm, LANES), jnp.float32)
    for slot, value in ((ROUTE_W1, w1), (ROUTE_W2, w2), (ROUTE_E1, i1), (ROUTE_E2, i2),
                        (ROUTE_RANK1, rank1), (ROUTE_RANK2, rank2)):
        record = jnp.where(lane == slot, value, record)
    route_ref[...] = record


def _mix_ln1(o_sb, p, w_pool_bf16, pool_scale, w_out_bf16, x2d, mod, ln1_g, ln1_b, w_router, b_router, seq):
    t, d = x2d.shape
    tm = ROW_TILE
    tiles_per_seq = seq // tm
    halo_blocks_per_tile = tm // HALO
    row = lambda i: (i, 0)
    const2 = lambda i: (0, 0)
    return pl.pallas_call(
        functools.partial(_mix_ln1_kernel, tiles_per_seq),
        grid=(t // tm,),
        in_specs=[pl.BlockSpec((tm, SB_WIDTH), row),
                  pl.BlockSpec((tm, POOL_WIDTH), row),
                  pl.BlockSpec((HALO, POOL_WIDTH), lambda i: (jnp.maximum(i * halo_blocks_per_tile - 1, 0), 0)),
                  pl.BlockSpec(w_pool_bf16.shape, lambda i: (0, 0, 0)),
                  pl.BlockSpec((1, POOL_WIDTH), const2),
                  pl.BlockSpec(w_out_bf16.shape, const2),
                  pl.BlockSpec((tm, d), row),
                  pl.BlockSpec((1, N_MOD, d), lambda i: (i // tiles_per_seq, 0, 0)),
                  pl.BlockSpec((1, d), const2),
                  pl.BlockSpec((1, d), const2),
                  pl.BlockSpec(w_router.shape, const2),
                  pl.BlockSpec((1, LANES), const2),
                  pl.BlockSpec((tm, tm), const2)],
        out_specs=[pl.BlockSpec((tm, d), row),
                   pl.BlockSpec((tm * ROW_CHUNKS, LANES), row),
                   pl.BlockSpec((tm, LANES), row),
                   pl.BlockSpec((SUBLANES, LANES), const2)],
        out_shape=[jax.ShapeDtypeStruct((t, d), jnp.float32),
                   jax.ShapeDtypeStruct((t * ROW_CHUNKS, LANES), jnp.float32),
                   jax.ShapeDtypeStruct((t, LANES), jnp.float32),
                   jax.ShapeDtypeStruct((SUBLANES, LANES), jnp.float32)],
        scratch_shapes=[pltpu.VMEM((HALO + tm, POOL_WIDTH), jnp.float32),
                        pltpu.VMEM((SUBLANES, LANES), jnp.float32)],
        compiler_params=pltpu.CompilerParams(vmem_limit_bytes=VMEM_LIMIT,
                                             dimension_semantics=("arbitrary",)),
        name="mix_ln1",
    )(o_sb, p, p, w_pool_bf16, pool_scale, w_out_bf16, x2d, mod, ln1_g, ln1_b, w_router, b_router,
      jnp.tri(tm, tm, -1, dtype=jnp.bfloat16))


def _dispatch_kernel(tile_end_ref, pos_ref, u_ref, sorted_hbm, zeros_ref, sem):
    tm = u_ref.shape[0] // ROW_CHUNKS
    tile = MOE_ROW_TILE * ROW_CHUNKS
    stored = lambda r: pl.ds(pl.multiple_of(r * ROW_CHUNKS, ROW_CHUNKS), ROW_CHUNKS)

    @pl.when(pl.program_id(0) == 0)
    def _():
        zeros_ref[...] = jnp.zeros_like(zeros_ref)

        def tile_fill(i):
            return pltpu.make_async_copy(zeros_ref, sorted_hbm.at[pl.ds(pl.multiple_of(i * tile, tile), tile), :],
                                         sem)

        def last_tile_fills(act):
            for e in range(N_EXPERTS):
                first_tile = tile_end_ref[e - 1] if e else 0

                @pl.when(tile_end_ref[e] > first_tile)
                def _():
                    act(tile_fill(tile_end_ref[e] - 1))

        unused = (tile_end_ref[N_EXPERTS - 1], sorted_hbm.shape[0] // tile)
        last_tile_fills(lambda fill: fill.start())
        lax.fori_loop(*unused, lambda i, _: (tile_fill(i).start(), 0)[1], 0)
        last_tile_fills(lambda fill: fill.wait())
        lax.fori_loop(*unused, lambda i, _: (tile_fill(i).wait(), 0)[1], 0)

    def body(r, _):
        row = u_ref.at[stored(r), :]
        pltpu.make_async_copy(row, sorted_hbm.at[stored(pos_ref[0, 0, r]), :], sem).start()
        pltpu.make_async_copy(row, sorted_hbm.at[stored(pos_ref[0, 0, tm + r]), :], sem).start()
        return 0

    lax.fori_loop(0, tm, body, 0, unroll=GATHER_UNROLL // 2)
    for _ in range(2):
        pltpu.make_async_copy(u_ref, sorted_hbm.at[pl.ds(0, tm * ROW_CHUNKS), :], sem).wait()


def _dispatch(tile_end, pos, u2, n_rows):
    tm = pos.shape[2] // 2
    t = u2.shape[0] // ROW_CHUNKS
    grid_spec = pltpu.PrefetchScalarGridSpec(
        num_scalar_prefetch=1,
        grid=(t // tm,),
        in_specs=[pl.BlockSpec((1, 1, 2 * tm), lambda i, te: (i, 0, 0), memory_space=pltpu.SMEM),
                  pl.BlockSpec((tm * ROW_CHUNKS, LANES), lambda i, te: (i, 0))],
        out_specs=pl.BlockSpec(memory_space=pl.ANY),
        scratch_shapes=[pltpu.VMEM((MOE_ROW_TILE * ROW_CHUNKS, LANES), jnp.float32),
                        pltpu.SemaphoreType.DMA(())],
    )
    return pl.pallas_call(
        _dispatch_kernel,
        grid_spec=grid_spec,
        out_shape=jax.ShapeDtypeStruct((n_rows * ROW_CHUNKS, LANES), jnp.float32),
        compiler_params=pltpu.CompilerParams(vmem_limit_bytes=VMEM_LIMIT,
                                             dimension_semantics=("arbitrary",)),
        name="dispatch",
    )(tile_end, pos, u2)


def _moe_grouped_kernel(tile_expert_ref, n_tiles_ref, u_ref, wg_ref, wu_ref, wd_ref, y_ref,
                        wg_bf16_ref, wu_bf16_ref, wd_bf16_ref):
    i = pl.program_id(0)
    bf16 = jnp.bfloat16
    valid = i < n_tiles_ref[0]

    @pl.when(valid & ((i == 0) | (tile_expert_ref[i] != tile_expert_ref[jnp.maximum(i - 1, 0)])))
    def _():
        wg_bf16_ref[...] = wg_ref[0].astype(bf16)
        wu_bf16_ref[...] = wu_ref[0].astype(bf16)
        wd_bf16_ref[...] = wd_ref[0].astype(bf16)

    @pl.when(valid)
    def _():
        u = _load_tile_rows(u_ref, 0, MOE_ROW_TILE).astype(bf16)
        gate = jnp.dot(u, wg_bf16_ref[...], preferred_element_type=jnp.float32)
        up = jnp.dot(u, wu_bf16_ref[...], preferred_element_type=jnp.float32)
        h = gate * jax.nn.sigmoid(gate) * up
        _store_tile_rows(y_ref, 0, jnp.dot(h.astype(bf16), wd_bf16_ref[...], preferred_element_type=jnp.float32))

    @pl.when(jnp.logical_not(valid))
    def _():
        y_ref[...] = jnp.zeros_like(y_ref)


def _moe_grouped(tile_expert, n_tiles, u_sorted, wg, wu, wd):
    d = D_MODEL
    tm = MOE_ROW_TILE
    max_tiles = tile_expert.shape[0]
    expert = lambda i, te, nt: (te[i], 0, 0)
    grid_spec = pltpu.PrefetchScalarGridSpec(
        num_scalar_prefetch=2,
        grid=(max_tiles,),
        in_specs=[pl.BlockSpec((tm * ROW_CHUNKS, LANES), lambda i, te, nt: (jnp.minimum(i, nt[0] - 1), 0)),
                  pl.BlockSpec((1, d, EXPERT_HIDDEN), expert),
                  pl.BlockSpec((1, d, EXPERT_HIDDEN), expert),
                  pl.BlockSpec((1, EXPERT_HIDDEN, d), expert)],
        out_specs=pl.BlockSpec((tm * ROW_CHUNKS, LANES), lambda i, te, nt: (i, 0)),
        scratch_shapes=[pltpu.VMEM((d, EXPERT_HIDDEN), jnp.bfloat16),
                        pltpu.VMEM((d, EXPERT_HIDDEN), jnp.bfloat16),
                        pltpu.VMEM((EXPERT_HIDDEN, d), jnp.bfloat16)],
    )
    return pl.pallas_call(
        _moe_grouped_kernel,
        grid_spec=grid_spec,
        out_shape=jax.ShapeDtypeStruct((max_tiles * tm * ROW_CHUNKS, LANES), jnp.float32),
        compiler_params=pltpu.CompilerParams(vmem_limit_bytes=VMEM_LIMIT,
                                             dimension_semantics=("arbitrary",)),
        name="moe_grouped",
    )(tile_expert, n_tiles, u_sorted, wg, wu, wd)


def _combine_ln2_kernel(pos_ref, pos_next_ref, y_hbm, route_ref, x1_ref, mod_ref, g2_ref, b2_ref, o_ref,
                        buf_ref, sem_ref):
    i = pl.program_id(0)
    n = pl.num_programs(0)
    tm = o_ref.shape[0]
    rows = 2 * tm
    stored = lambda r: pl.ds(pl.multiple_of(r * ROW_CHUNKS, ROW_CHUNKS), ROW_CHUNKS)

    def start_gather(rows_ref, slot):
        def body(r, _):
            pltpu.make_async_copy(y_hbm.at[stored(rows_ref[0, 0, r]), :], buf_ref.at[slot, stored(r), :],
                                  sem_ref.at[slot]).start()
            return 0
        lax.fori_loop(0, rows, body, 0, unroll=GATHER_UNROLL)

    @pl.when(i == 0)
    def _():
        start_gather(pos_ref, 0)

    @pl.when(i + 1 < n)
    def _():
        start_gather(pos_next_ref, (i + 1) % 2)

    slot = i % 2
    pltpu.make_async_copy(y_hbm.at[pl.ds(0, rows * ROW_CHUNKS), :], buf_ref.at[slot], sem_ref.at[slot]).wait()
    route = route_ref[...]
    w1 = route[:, ROUTE_W1:ROUTE_W1 + 1]
    w2 = route[:, ROUTE_W2:ROUTE_W2 + 1]
    gathered = buf_ref.at[slot]
    y = w1 * _load_tile_rows(gathered, 0, tm) + w2 * _load_tile_rows(gathered, tm * ROW_CHUNKS, tm)
    gate2 = mod_ref[0, 5:6, :]
    o_ref[...] = _ln(DEEPNORM_ALPHA * x1_ref[...] + (1.0 + gate2) * y) * g2_ref[...] + b2_ref[...]


def _combine_ln2(pos, y_sorted, route, x1, mod, ln2_g, ln2_b, seq):
    t, d = x1.shape
    tm = COMBINE_ROW_TILE
    n = t // tm
    tiles_per_seq = seq // tm
    row = lambda i: (i, 0)
    const2 = lambda i: (0, 0)
    return pl.pallas_call(
        _combine_ln2_kernel,
        grid=(n,),
        in_specs=[pl.BlockSpec((1, 1, 2 * tm), lambda i: (i, 0, 0), memory_space=pltpu.SMEM),
                  pl.BlockSpec((1, 1, 2 * tm), lambda i: (jnp.minimum(i + 1, n - 1), 0, 0),
                               memory_space=pltpu.SMEM),
                  pl.BlockSpec(memory_space=pl.ANY),
                  pl.BlockSpec((tm, LANES), row),
                  pl.BlockSpec((tm, d), row),
                  pl.BlockSpec((1, N_MOD, d), lambda i: (i // tiles_per_seq, 0, 0)),
                  pl.BlockSpec((1, d), const2),
                  pl.BlockSpec((1, d), const2)],
        out_specs=pl.BlockSpec((tm, d), row),
        out_shape=jax.ShapeDtypeStruct((t, d), jnp.float32),
        scratch_shapes=[pltpu.VMEM((2, 2 * tm * ROW_CHUNKS, LANES), jnp.float32),
                        pltpu.SemaphoreType.DMA((2,))],
        compiler_params=pltpu.CompilerParams(vmem_limit_bytes=VMEM_LIMIT,
                                             dimension_semantics=("arbitrary",)),
        name="combine_ln2",
    )(pos, pos, y_sorted, route, x1, mod, ln2_g, ln2_b)


def _dispatch_plan(route, counts):
    t = route.shape[0]
    tm = MOE_ROW_TILE
    max_tiles = 2 * t // tm + N_EXPERTS
    counts = counts[0, :N_EXPERTS].astype(jnp.int32)
    tiles = (counts + tm - 1) // tm
    tile_end = jnp.cumsum(tiles)
    row_start = (tile_end - tiles) * tm
    experts = route[:, ROUTE_E1:ROUTE_E2 + 1].astype(jnp.int32)
    ranks = route[:, ROUTE_RANK1:ROUTE_RANK2 + 1].astype(jnp.int32)
    is_expert = experts[:, :, None] == jnp.arange(N_EXPERTS, dtype=jnp.int32)
    pos = jnp.sum(jnp.where(is_expert, row_start, 0), axis=-1) + ranks
    tile_ids = jnp.arange(max_tiles, dtype=jnp.int32)
    tile_expert = jnp.minimum(jnp.sum(tile_end[None, :] <= tile_ids[:, None], axis=1), N_EXPERTS - 1)
    by_tile = lambda tc: pos.reshape(t // tc, tc, 2).transpose(0, 2, 1).reshape(t // tc, 1, 2 * tc)
    return (tile_expert.astype(jnp.int32), tile_end[-1:].astype(jnp.int32), tile_end.astype(jnp.int32),
            by_tile(DISPATCH_ROW_TILE), by_tile(COMBINE_ROW_TILE), max_tiles * tm)


def kernel(x, c, w_ada, b_ada, w_in, w_pool, pool_scale, w_out, ln1_g, ln1_b, w_router_group, b_router_group,
           w_router_expert, b_router_expert, w_gate, w_up, w_down, ln2_g, ln2_b):
    b, s, d = x.shape
    bf16 = jnp.bfloat16
    for layer in range(DEPTH):
        c_pad = jnp.pad(c, ((0, SUBLANES - b), (0, 0)))
        mod = _adaln(c_pad, w_ada[layer], b_ada[layer][None, :])[:b].reshape(b, N_MOD, d)
        x2d = x.reshape(b * s, d)
        qkv, p = _ln_inproj(x2d, mod, w_in[layer].astype(bf16), s)
        o_sb = _sb_attn(qkv, b, s).reshape(b * s, SB_WIDTH)
        pad = LANES - N_EXPERTS - N_EXPERT_GROUPS
        w_router = jnp.pad(jnp.concatenate([w_router_expert[layer], w_router_group[layer]], axis=1),
                           ((0, 0), (0, pad)))
        b_router = jnp.pad(jnp.concatenate([b_router_expert[layer], b_router_group[layer]]), (0, pad))[None, :]
        w_router_hi = w_router.astype(bf16)
        w_router_lo = (w_router - w_router_hi.astype(jnp.float32)).astype(bf16)
        w_router = jnp.concatenate([w_router_hi, w_router_hi, w_router_lo], axis=0)
        x1, u2, route, counts = _mix_ln1(o_sb, p, w_pool[layer].astype(bf16), pool_scale[layer][None, :],
                                         w_out[layer].astype(bf16), x2d, mod, ln1_g[layer][None, :],
                                         ln1_b[layer][None, :], w_router, b_router, s)
        tile_expert, n_tiles, tile_end, pos_dispatch, pos, n_rows = _dispatch_plan(route, counts)
        u_sorted = _dispatch(tile_end, pos_dispatch, u2, n_rows)
        y_sorted = _moe_grouped(tile_expert, n_tiles, u_sorted, w_gate[layer], w_up[layer], w_down[layer])
        x2 = _combine_ln2(pos, y_sorted, route, x1, mod, ln2_g[layer][None, :], ln2_b[layer][None, :], s)
        x = x2.reshape(b, s, d)
    return x
```

```python
import functools

import jax
import jax.numpy as jnp
from jax import lax
from jax.experimental import pallas as pl
from jax.experimental.pallas import tpu as pltpu

D_MODEL = 1024
N_SB_HEADS = 8
SB_HEAD_DIM = 64
SB_WIDTH = N_SB_HEADS * SB_HEAD_DIM
POOL_WINDOWS = (2, 4, 8, 16)
POOL_GROUP_DIM = 128
POOL_WIDTH = len(POOL_WINDOWS) * POOL_GROUP_DIM
N_EXPERT_GROUPS = 4
EXPERTS_PER_GROUP = 4
N_EXPERTS = N_EXPERT_GROUPS * EXPERTS_PER_GROUP
EXPERT_HIDDEN = 512
DEPTH = 1
DEEPNORM_ALPHA = (2.0 * DEPTH) ** 0.25
LN_EPS = 1e-5
N_MOD = 6

LANES = 128
SUBLANES = 8
HALO = max(POOL_WINDOWS)
VMEM_LIMIT = 56 * 1024 * 1024

ROW_TILE = 512
ATTN_TILE = 256
STICK_GONE_LOG2 = -180.0
MOE_ROW_TILE = 512
DISPATCH_ROW_TILE = 1024
COMBINE_ROW_TILE = 256
GATHER_UNROLL = 8


ROW_CHUNKS = D_MODEL // LANES


def _store_tile_rows(ref, base, value):
    n = value.shape[0]
    for c in range(ROW_CHUNKS):
        ref[pl.ds(base + c, n, stride=ROW_CHUNKS), :] = value[:, c * LANES:(c + 1) * LANES]


def _load_tile_rows(ref, base, n):
    return jnp.concatenate([ref[pl.ds(base + c, n, stride=ROW_CHUNKS), :] for c in range(ROW_CHUNKS)], axis=1)


def _ln(x):
    mu = jnp.mean(x, axis=-1, keepdims=True)
    xc = x - mu
    var = jnp.mean(xc * xc, axis=-1, keepdims=True)
    return xc * lax.rsqrt(var + LN_EPS)


def _adaln_kernel(c_ref, w_ref, b_ref, o_ref):
    c = c_ref[...]
    a = c * jax.nn.sigmoid(c)
    o_ref[...] = jnp.dot(a, w_ref[...], preferred_element_type=jnp.float32,
                         precision=lax.Precision.HIGHEST) + b_ref[...]


def _adaln(c_pad, w_ada, b_ada):
    rows, d = c_pad.shape
    n = w_ada.shape[1]
    tn = 1024
    return pl.pallas_call(
        _adaln_kernel,
        grid=(n // tn,),
        in_specs=[pl.BlockSpec((rows, d), lambda j: (0, 0)),
                  pl.BlockSpec((d, tn), lambda j: (0, j)),
                  pl.BlockSpec((1, tn), lambda j: (0, j))],
        out_specs=pl.BlockSpec((rows, tn), lambda j: (0, j)),
        out_shape=jax.ShapeDtypeStruct((rows, n), jnp.float32),
        compiler_params=pltpu.CompilerParams(vmem_limit_bytes=VMEM_LIMIT),
        name="adaln",
    )(c_pad, w_ada, b_ada)


def _ln_inproj_kernel(x_ref, mod_ref, w_ref, qkv_ref, p_ref):
    shift = mod_ref[0, 0:1, :]
    scale = mod_ref[0, 1:2, :]
    u = (_ln(x_ref[...]) * (1.0 + scale) + shift).astype(jnp.bfloat16)
    qk_scale = SB_HEAD_DIM ** -0.5 * 1.4426950408889634
    q = jnp.dot(u, w_ref[:, 0:SB_WIDTH], preferred_element_type=jnp.float32) * qk_scale
    qkv_ref[:, 0:SB_WIDTH] = q.astype(jnp.bfloat16)
    for j in (1, 2):
        kv = jnp.dot(u, w_ref[:, j * SB_WIDTH:(j + 1) * SB_WIDTH], preferred_element_type=jnp.float32)
        qkv_ref[:, j * SB_WIDTH:(j + 1) * SB_WIDTH] = kv.astype(jnp.bfloat16)
    p_ref[...] = jnp.dot(u, w_ref[:, 3 * SB_WIDTH:], preferred_element_type=jnp.float32)


def _ln_inproj(x2d, mod, w_in_bf16, seq):
    t, d = x2d.shape
    tm = ROW_TILE
    tiles_per_seq = seq // tm
    return pl.pallas_call(
        _ln_inproj_kernel,
        grid=(t // tm,),
        in_specs=[pl.BlockSpec((tm, d), lambda i: (i, 0)),
                  pl.BlockSpec((1, N_MOD, d), lambda i: (i // tiles_per_seq, 0, 0)),
                  pl.BlockSpec(w_in_bf16.shape, lambda i: (0, 0))],
        out_specs=[pl.BlockSpec((tm, 3 * SB_WIDTH), lambda i: (i, 0)),
                   pl.BlockSpec((tm, POOL_WIDTH), lambda i: (i, 0))],
        out_shape=[jax.ShapeDtypeStruct((t, 3 * SB_WIDTH), jnp.bfloat16),
                   jax.ShapeDtypeStruct((t, POOL_WIDTH), jnp.float32)],
        compiler_params=pltpu.CompilerParams(vmem_limit_bytes=VMEM_LIMIT),
        name="ln_inproj",
    )(x2d, mod, w_in_bf16)


def _sb_attn_kernel(q_ref, k_ref, v_ref, o_ref, acc_ref, carry_ref):
    tq = tk = ATTN_TILE
    heads = (0, 1)
    qi = pl.program_id(2)
    q2 = q_ref[0]
    lane = lax.broadcasted_iota(jnp.int32, (tq, LANES), 1)
    first_head = lane < SB_HEAD_DIM
    zero = jnp.zeros_like(q2)
    q_heads = (jnp.where(first_head, q2, zero), jnp.where(first_head, zero, q2))

    row = lax.broadcasted_iota(jnp.int32, (tk, tk), 0)
    col = lax.broadcasted_iota(jnp.int32, (tk, tk), 1)
    neg_suffix = jnp.where(row >= col, -1.0, 0.0).astype(jnp.bfloat16)
    causal = col < row

    def visit(blocks, carries):
        k_blks, v_blks = [], []
        for j, _ in blocks:
            start = pl.multiple_of(jnp.maximum(qi - j, 0) * tk, tk)
            k_blks.append(k_ref[0, pl.ds(start, tk), :])
            v_blks.append(v_ref[0, pl.ds(start, tk), :])
        pairs = [(b, h) for b in range(len(blocks)) for h in heads]
        zs = {(b, h): lax.dot_general(q_heads[h], k_blks[b], (((1,), (1,)), ((), ())),
                                      preferred_element_type=jnp.float32) for b, h in pairs}
        fixed, rounded = {}, {}
        for b, h in pairs:
            z = zs[b, h]
            n = jnp.maximum(z, 0.0) + jnp.log2(1.0 + jnp.exp2(-jnp.abs(z)))
            if blocks[b][1]:
                n = jnp.where(causal, n, 0.0)
            n_bf16 = n.astype(jnp.bfloat16)
            rounded[b, h] = n_bf16
            fixed[b, h] = z - (n - n_bf16.astype(jnp.float32))
        incls = {bh: jnp.dot(rounded[bh], neg_suffix, preferred_element_type=jnp.float32) for bh in pairs}
        atts = {}
        carries = list(carries)
        for b, h in pairs:
            carry = carries[h]
            block_sum = jnp.broadcast_to(incls[b, h][:, 0:1], (tq, LANES))
            expo = fixed[b, h] + incls[b, h]
            if carry is None:
                carries[h] = block_sum
            else:
                expo = expo + jnp.concatenate([carry] * (tk // LANES), axis=1)
                carries[h] = carry + block_sum
            att = jnp.exp2(expo)
            if blocks[b][1]:
                att = jnp.where(causal, att, 0.0)
            atts[b, h] = att.astype(jnp.bfloat16)
        outs = {(b, h): jnp.dot(atts[b, h], v_blks[b], preferred_element_type=jnp.float32) for b, h in pairs}
        return outs, carries

    outs, carries = visit([(0, True), (1, False)], [None, None])
    has_second = qi >= 1
    for h in heads:
        acc_ref[h] = outs[0, h] + jnp.where(has_second, outs[1, h], 0.0)
        carry_ref[h] = carries[h]

    def more(state):
        j, stick_gone = state
        return (j <= qi) & jnp.logical_not(stick_gone)

    def visit_next(state):
        j, _ = state
        outs, new_carries = visit([(j, False)], [carry_ref[h] for h in heads])
        for h in heads:
            acc_ref[h] += outs[0, h]
            carry_ref[h] = new_carries[h]
        return j + 1, jnp.max(carry_ref[...]) < STICK_GONE_LOG2

    lax.while_loop(more, visit_next, (jnp.int32(2), jnp.max(carry_ref[...]) < STICK_GONE_LOG2))
    o_ref[0] = jnp.where(first_head, acc_ref[0], acc_ref[1]).astype(o_ref.dtype)


def _sb_attn(qkv, batch, seq):
    qkv3 = qkv.reshape(batch, seq, 3 * SB_WIDTH)
    tq = ATTN_TILE
    pairs = SB_WIDTH // LANES
    return pl.pallas_call(
        _sb_attn_kernel,
        grid=(batch, pairs, seq // tq),
        in_specs=[pl.BlockSpec((1, tq, LANES), lambda b, h, i: (b, i, h)),
                  pl.BlockSpec((1, seq, LANES), lambda b, h, i: (b, 0, pairs + h)),
                  pl.BlockSpec((1, seq, LANES), lambda b, h, i: (b, 0, 2 * pairs + h))],
        out_specs=pl.BlockSpec((1, tq, LANES), lambda b, h, i: (b, i, h)),
        out_shape=jax.ShapeDtypeStruct((batch, seq, SB_WIDTH), jnp.bfloat16),
        scratch_shapes=[pltpu.VMEM((2, tq, LANES), jnp.float32),
                        pltpu.VMEM((2, tq, LANES), jnp.float32)],
        compiler_params=pltpu.CompilerParams(vmem_limit_bytes=VMEM_LIMIT),
        name="sb_attn",
    )(qkv3, qkv3, qkv3)


def _route(logits):
    lane = lax.broadcasted_iota(jnp.int32, logits.shape, 1).astype(jnp.float32)
    neg = jnp.float32(-jnp.inf)
    big = jnp.float32(LANES)
    is_group = (lane >= N_EXPERTS) & (lane < N_EXPERTS + N_EXPERT_GROUPS)
    gl = jnp.where(is_group, logits, neg)
    g_max = jnp.max(gl, axis=-1, keepdims=True)
    g_sum = jnp.sum(jnp.exp(gl - g_max), axis=-1, keepdims=True)
    g_p = 1.0 / g_sum
    g_idx = jnp.min(jnp.where(gl == g_max, lane, big), axis=-1, keepdims=True) - N_EXPERTS
    in_group = (lane >= g_idx * EXPERTS_PER_GROUP) & (lane < (g_idx + 1) * EXPERTS_PER_GROUP)
    el = jnp.where(in_group, logits, neg)
    e1 = jnp.max(el, axis=-1, keepdims=True)
    i1 = jnp.min(jnp.where(el == e1, lane, big), axis=-1, keepdims=True)
    el2 = jnp.where(lane == i1, neg, el)
    e2 = jnp.max(el2, axis=-1, keepdims=True)
    i2 = jnp.min(jnp.where(el2 == e2, lane, big), axis=-1, keepdims=True)
    r = jnp.exp(e2 - e1)
    w1 = g_p / (1.0 + r)
    w2 = w1 * r
    return i1, i2, w1, w2


ROUTE_W1, ROUTE_W2, ROUTE_E1, ROUTE_E2, ROUTE_RANK1, ROUTE_RANK2 = range(6)


def _mix_ln1_kernel(tiles_per_seq, osb_ref, p_ref, halo_ref, wpool_ref, pscale_ref, wout_ref, x_ref, mod_ref,
                    g1_ref, b1_ref, wr_ref, br_ref, earlier_ref, x1_ref, u2_ref, route_ref, count_ref, pext_ref,
                    running_ref):
    tm = p_ref.shape[0]
    tile_in_seq = pl.program_id(0) % tiles_per_seq
    p = p_ref[...]
    pext_ref[0:HALO, :] = jnp.where(tile_in_seq == 0, 0.0, halo_ref[...])
    pext_ref[HALO:, :] = p
    pos = tile_in_seq * tm + lax.broadcasted_iota(jnp.int32, (tm, 1), 0)

    mixer_out = [osb_ref[...]]
    for g, w in enumerate(POOL_WINDOWS):
        cols = slice(g * POOL_GROUP_DIM, (g + 1) * POOL_GROUP_DIM)
        win = p[:, cols]
        for i in range(1, w):
            win = win + pext_ref[HALO - i:HALO - i + tm, cols]
        count = jnp.minimum(pos + 1, w).astype(jnp.float32)
        pooled = win / count - p[:, cols]
        o_pool = jnp.dot(pooled.astype(jnp.bfloat16), wpool_ref[g], preferred_element_type=jnp.float32)
        mixer_out.append((o_pool * pscale_ref[:, cols]).astype(jnp.bfloat16))
    mixed = jnp.dot(jnp.concatenate(mixer_out, axis=1), wout_ref[...], preferred_element_type=jnp.float32)

    gate1 = mod_ref[0, 2:3, :]
    shift2 = mod_ref[0, 3:4, :]
    scale2 = mod_ref[0, 4:5, :]
    x1 = _ln(DEEPNORM_ALPHA * x_ref[...] + (1.0 + gate1) * mixed) * g1_ref[...] + b1_ref[...]
    x1_ref[...] = x1
    u2 = _ln(x1) * (1.0 + scale2) + shift2
    _store_tile_rows(u2_ref, 0, u2)
    u2_hi = u2.astype(jnp.bfloat16)
    u2_lo = (u2 - u2_hi.astype(jnp.float32)).astype(jnp.bfloat16)
    logits = jnp.dot(jnp.concatenate([u2_hi, u2_lo, u2_hi], axis=1), wr_ref[...],
                     preferred_element_type=jnp.float32) + br_ref[...]
    i1, i2, w1, w2 = _route(logits)

    @pl.when(pl.program_id(0) == 0)
    def _():
        running_ref[...] = jnp.zeros_like(running_ref)

    lane = lax.broadcasted_iota(jnp.int32, (tm, LANES), 1)
    lane_f = lane.astype(jnp.float32)
    first, second = lane_f == i1, lane_f == i2
    uses = jnp.where(first | second, 1.0, 0.0)
    before = jnp.dot(earlier_ref[...], uses.astype(jnp.bfloat16), preferred_element_type=jnp.float32)
    before = before + running_ref[0:1, :]
    rank1 = jnp.sum(jnp.where(first, before, 0.0), axis=-1, keepdims=True)
    rank2 = jnp.sum(jnp.where(second, before, 0.0), axis=-1, keepdims=True)
    running = running_ref[0:1, :] + jnp.sum(uses, axis=0, keepdims=True)
    running_ref[...] = jnp.broadcast_to(running, running_ref.shape)
    count_ref[...] = jnp.broadcast_to(running, count_ref.shape)

    record = jnp.zeros((tm, LANES), jnp.float32)
    for slot, value in ((ROUTE_W1, w1), (ROUTE_W2, w2), (ROUTE_E1, i1), (ROUTE_E2, i2),
                        (ROUTE_RANK1, rank1), (ROUTE_RANK2, rank2)):
        record = jnp.where(lane == slot, value, record)
    route_ref[...] = record


def _mix_ln1(o_sb, p, w_pool_bf16, pool_scale, w_out_bf16, x2d, mod, ln1_g, ln1_b, w_router, b_router, seq):
    t, d = x2d.shape
    tm = ROW_TILE
    tiles_per_seq = seq // tm
    halo_blocks_per_tile = tm // HALO
    row = lambda i: (i, 0)
    const2 = lambda i: (0, 0)
    return pl.pallas_call(
        functools.partial(_mix_ln1_kernel, tiles_per_seq),
        grid=(t // tm,),
        in_specs=[pl.BlockSpec((tm, SB_WIDTH), row),
                  pl.BlockSpec((tm, POOL_WIDTH), row),
                  pl.BlockSpec((HALO, POOL_WIDTH), lambda i: (jnp.maximum(i * halo_blocks_per_tile - 1, 0), 0)),
                  pl.BlockSpec(w_pool_bf16.shape, lambda i: (0, 0, 0)),
                  pl.BlockSpec((1, POOL_WIDTH), const2),
                  pl.BlockSpec(w_out_bf16.shape, const2),
                  pl.BlockSpec((tm, d), row),
                  pl.BlockSpec((1, N_MOD, d), lambda i: (i // tiles_per_seq, 0, 0)),
                  pl.BlockSpec((1, d), const2),
                  pl.BlockSpec((1, d), const2),
                  pl.BlockSpec(w_router.shape, const2),
                  pl.BlockSpec((1, LANES), const2),
                  pl.BlockSpec((tm, tm), const2)],
        out_specs=[pl.BlockSpec((tm, d), row),
                   pl.BlockSpec((tm * ROW_CHUNKS, LANES), row),
                   pl.BlockSpec((tm, LANES), row),
                   pl.BlockSpec((SUBLANES, LANES), const2)],
        out_shape=[jax.ShapeDtypeStruct((t, d), jnp.float32),
                   jax.ShapeDtypeStruct((t * ROW_CHUNKS, LANES), jnp.float32),
                   jax.ShapeDtypeStruct((t, LANES), jnp.float32),
                   jax.ShapeDtypeStruct((SUBLANES, LANES), jnp.float32)],
        scratch_shapes=[pltpu.VMEM((HALO + tm, POOL_WIDTH), jnp.float32),
                        pltpu.VMEM((SUBLANES, LANES), jnp.float32)],
        compiler_params=pltpu.CompilerParams(vmem_limit_bytes=VMEM_LIMIT,
                                             dimension_semantics=("arbitrary",)),
        name="mix_ln1",
    )(o_sb, p, p, w_pool_bf16, pool_scale, w_out_bf16, x2d, mod, ln1_g, ln1_b, w_router, b_router,
      jnp.tri(tm, tm, -1, dtype=jnp.bfloat16))


def _dispatch_kernel(tile_end_ref, pos_ref, u_ref, sorted_hbm, zeros_ref, sem):
    tm = u_ref.shape[0] // ROW_CHUNKS
    tile = MOE_ROW_TILE * ROW_CHUNKS
    stored = lambda r: pl.ds(pl.multiple_of(r * ROW_CHUNKS, ROW_CHUNKS), ROW_CHUNKS)

    @pl.when(pl.program_id(0) == 0)
    def _():
        zeros_ref[...] = jnp.zeros_like(zeros_ref)

        def tile_fill(i):
            return pltpu.make_async_copy(zeros_ref, sorted_hbm.at[pl.ds(pl.multiple_of(i * tile, tile), tile), :],
                                         sem)

        def last_tile_fills(act):
            for e in range(N_EXPERTS):
                first_tile = tile_end_ref[e - 1] if e else 0

                @pl.when(tile_end_ref[e] > first_tile)
                def _():
                    act(tile_fill(tile_end_ref[e] - 1))

        unused = (tile_end_ref[N_EXPERTS - 1], sorted_hbm.shape[0] // tile)
        last_tile_fills(lambda fill: fill.start())
        lax.fori_loop(*unused, lambda i, _: (tile_fill(i).start(), 0)[1], 0)
        last_tile_fills(lambda fill: fill.wait())
        lax.fori_loop(*unused, lambda i, _: (tile_fill(i).wait(), 0)[1], 0)

    def body(r, _):
        row = u_ref.at[stored(r), :]
        pltpu.make_async_copy(row, sorted_hbm.at[stored(pos_ref[0, 0, r]), :], sem).start()
        pltpu.make_async_copy(row, sorted_hbm.at[stored(pos_ref[0, 0, tm + r]), :], sem).start()
        return 0

    lax.fori_loop(0, tm, body, 0, unroll=GATHER_UNROLL // 2)
    for _ in range(2):
        pltpu.make_async_copy(u_ref, sorted_hbm.at[pl.ds(0, tm * ROW_CHUNKS), :], sem).wait()


def _dispatch(tile_end, pos, u2, n_rows):
    tm = pos.shape[2] // 2
    t = u2.shape[0] // ROW_CHUNKS
    grid_spec = pltpu.PrefetchScalarGridSpec(
        num_scalar_prefetch=1,
        grid=(t // tm,),
        in_specs=[pl.BlockSpec((1, 1, 2 * tm), lambda i, te: (i, 0, 0), memory_space=pltpu.SMEM),
                  pl.BlockSpec((tm * ROW_CHUNKS, LANES), lambda i, te: (i, 0))],
        out_specs=pl.BlockSpec(memory_space=pl.ANY),
        scratch_shapes=[pltpu.VMEM((MOE_ROW_TILE * ROW_CHUNKS, LANES), jnp.float32),
                        pltpu.SemaphoreType.DMA(())],
    )
    return pl.pallas_call(
        _dispatch_kernel,
        grid_spec=grid_spec,
        out_shape=jax.ShapeDtypeStruct((n_rows * ROW_CHUNKS, LANES), jnp.float32),
        compiler_params=pltpu.CompilerParams(vmem_limit_bytes=VMEM_LIMIT,
                                             dimension_semantics=("arbitrary",)),
        name="dispatch",
    )(tile_end, pos, u2)


def _moe_grouped_kernel(tile_expert_ref, n_tiles_ref, u_ref, wg_ref, wu_ref, wd_ref, y_ref,
                        wg_bf16_ref, wu_bf16_ref, wd_bf16_ref):
    i = pl.program_id(0)
    bf16 = jnp.bfloat16
    valid = i < n_tiles_ref[0]

    @pl.when(valid & ((i == 0) | (tile_expert_ref[i] != tile_expert_ref[jnp.maximum(i - 1, 0)])))
    def _():
        wg_bf16_ref[...] = wg_ref[0].astype(bf16)
        wu_bf16_ref[...] = wu_ref[0].astype(bf16)
        wd_bf16_ref[...] = wd_ref[0].astype(bf16)

    @pl.when(valid)
    def _():
        u = _load_tile_rows(u_ref, 0, MOE_ROW_TILE).astype(bf16)
        gate = jnp.dot(u, wg_bf16_ref[...], preferred_element_type=jnp.float32)
        up = jnp.dot(u, wu_bf16_ref[...], preferred_element_type=jnp.float32)
        h = gate * jax.nn.sigmoid(gate) * up
        _store_tile_rows(y_ref, 0, jnp.dot(h.astype(bf16), wd_bf16_ref[...], preferred_element_type=jnp.float32))

    @pl.when(jnp.logical_not(valid))
    def _():
        y_ref[...] = jnp.zeros_like(y_ref)


def _moe_grouped(tile_expert, n_tiles, u_sorted, wg, wu, wd):
    d = D_MODEL
    tm = MOE_ROW_TILE
    max_tiles = tile_expert.shape[0]
    expert = lambda i, te, nt: (te[i], 0, 0)
    grid_spec = pltpu.PrefetchScalarGridSpec(
        num_scalar_prefetch=2,
        grid=(max_tiles,),
        in_specs=[pl.BlockSpec((tm * ROW_CHUNKS, LANES), lambda i, te, nt: (jnp.minimum(i, nt[0] - 1), 0)),
                  pl.BlockSpec((1, d, EXPERT_HIDDEN), expert),
                  pl.BlockSpec((1, d, EXPERT_HIDDEN), expert),
                  pl.BlockSpec((1, EXPERT_HIDDEN, d), expert)],
        out_specs=pl.BlockSpec((tm * ROW_CHUNKS, LANES), lambda i, te, nt: (i, 0)),
        scratch_shapes=[pltpu.VMEM((d, EXPERT_HIDDEN), jnp.bfloat16),
                        pltpu.VMEM((d, EXPERT_HIDDEN), jnp.bfloat16),
                        pltpu.VMEM((EXPERT_HIDDEN, d), jnp.bfloat16)],
    )
    return pl.pallas_call(
        _moe_grouped_kernel,
        grid_spec=grid_spec,
        out_shape=jax.ShapeDtypeStruct((max_tiles * tm * ROW_CHUNKS, LANES), jnp.float32),
        compiler_params=pltpu.CompilerParams(vmem_limit_bytes=VMEM_LIMIT,
                                             dimension_semantics=("arbitrary",)),
        name="moe_grouped",
    )(tile_expert, n_tiles, u_sorted, wg, wu, wd)


def _combine_ln2_kernel(pos_ref, pos_next_ref, y_hbm, route_ref, x1_ref, mod_ref, g2_ref, b2_ref, o_ref,
                        buf_ref, sem_ref):
    i = pl.program_id(0)
    n = pl.num_programs(0)
    tm = o_ref.shape[0]
    rows = 2 * tm
    stored = lambda r: pl.ds(pl.multiple_of(r * ROW_CHUNKS, ROW_CHUNKS), ROW_CHUNKS)

    def start_gather(rows_ref, slot):
        def body(r, _):
            pltpu.make_async_copy(y_hbm.at[stored(rows_ref[0, 0, r]), :], buf_ref.at[slot, stored(r), :],
                                  sem_ref.at[slot]).start()
            return 0
        lax.fori_loop(0, rows, body, 0, unroll=GATHER_UNROLL)

    @pl.when(i == 0)
    def _():
        start_gather(pos_ref, 0)

    @pl.when(i + 1 < n)
    def _():
        start_gather(pos_next_ref, (i + 1) % 2)

    slot = i % 2
    pltpu.make_async_copy(y_hbm.at[pl.ds(0, rows * ROW_CHUNKS), :], buf_ref.at[slot], sem_ref.at[slot]).wait()
    route = route_ref[...]
    w1 = route[:, ROUTE_W1:ROUTE_W1 + 1]
    w2 = route[:, ROUTE_W2:ROUTE_W2 + 1]
    gathered = buf_ref.at[slot]
    y = w1 * _load_tile_rows(gathered, 0, tm) + w2 * _load_tile_rows(gathered, tm * ROW_CHUNKS, tm)
    gate2 = mod_ref[0, 5:6, :]
    o_ref[...] = _ln(DEEPNORM_ALPHA * x1_ref[...] + (1.0 + gate2) * y) * g2_ref[...] + b2_ref[...]


def _combine_ln2(pos, y_sorted, route, x1, mod, ln2_g, ln2_b, seq):
    t, d = x1.shape
    tm = COMBINE_ROW_TILE
    n = t // tm
    tiles_per_seq = seq // tm
    row = lambda i: (i, 0)
    const2 = lambda i: (0, 0)
    return pl.pallas_call(
        _combine_ln2_kernel,
        grid=(n,),
        in_specs=[pl.BlockSpec((1, 1, 2 * tm), lambda i: (i, 0, 0), memory_space=pltpu.SMEM),
                  pl.BlockSpec((1, 1, 2 * tm), lambda i: (jnp.minimum(i + 1, n - 1), 0, 0),
                               memory_space=pltpu.SMEM),
                  pl.BlockSpec(memory_space=pl.ANY),
                  pl.BlockSpec((tm, LANES), row),
                  pl.BlockSpec((tm, d), row),
                  pl.BlockSpec((1, N_MOD, d), lambda i: (i // tiles_per_seq, 0, 0)),
                  pl.BlockSpec((1, d), const2),
                  pl.BlockSpec((1, d), const2)],
        out_specs=pl.BlockSpec((tm, d), row),
        out_shape=jax.ShapeDtypeStruct((t, d), jnp.float32),
        scratch_shapes=[pltpu.VMEM((2, 2 * tm * ROW_CHUNKS, LANES), jnp.float32),
                        pltpu.SemaphoreType.DMA((2,))],
        compiler_params=pltpu.CompilerParams(vmem_limit_bytes=VMEM_LIMIT,
                                             dimension_semantics=("arbitrary",)),
        name="combine_ln2",
    )(pos, pos, y_sorted, route, x1, mod, ln2_g, ln2_b)


def _dispatch_plan(route, counts):
    t = route.shape[0]
    tm = MOE_ROW_TILE
    max_tiles = 2 * t // tm + N_EXPERTS
    counts = counts[0, :N_EXPERTS].astype(jnp.int32)
    tiles = (counts + tm - 1) // tm
    tile_end = jnp.cumsum(tiles)
    row_start = (tile_end - tiles) * tm
    experts = route[:, ROUTE_E1:ROUTE_E2 + 1].astype(jnp.int32)
    ranks = route[:, ROUTE_RANK1:ROUTE_RANK2 + 1].astype(jnp.int32)
    is_expert = experts[:, :, None] == jnp.arange(N_EXPERTS, dtype=jnp.int32)
    pos = jnp.sum(jnp.where(is_expert, row_start, 0), axis=-1) + ranks
    tile_ids = jnp.arange(max_tiles, dtype=jnp.int32)
    tile_expert = jnp.minimum(jnp.sum(tile_end[None, :] <= tile_ids[:, None], axis=1), N_EXPERTS - 1)
    by_tile = lambda tc: pos.reshape(t // tc, tc, 2).transpose(0, 2, 1).reshape(t // tc, 1, 2 * tc)
    return (tile_expert.astype(jnp.int32), tile_end[-1:].astype(jnp.int32), tile_end.astype(jnp.int32),
            by_tile(DISPATCH_ROW_TILE), by_tile(COMBINE_ROW_TILE), max_tiles * tm)


def kernel(x, c, w_ada, b_ada, w_in, w_pool, pool_scale, w_out, ln1_g, ln1_b, w_router_group, b_router_group,
           w_router_expert, b_router_expert, w_gate, w_up, w_down, ln2_g, ln2_b):
    b, s, d = x.shape
    bf16 = jnp.bfloat16
    for layer in range(DEPTH):
        c_pad = jnp.pad(c, ((0, SUBLANES - b), (0, 0)))
        mod = _adaln(c_pad, w_ada[layer], b_ada[layer][None, :])[:b].reshape(b, N_MOD, d)
        x2d = x.reshape(b * s, d)
        qkv, p = _ln_inproj(x2d, mod, w_in[layer].astype(bf16), s)
        o_sb = _sb_attn(qkv, b, s).reshape(b * s, SB_WIDTH)
        pad = LANES - N_EXPERTS - N_EXPERT_GROUPS
        w_router = jnp.pad(jnp.concatenate([w_router_expert[layer], w_router_group[layer]], axis=1),
                           ((0, 0), (0, pad)))
        b_router = jnp.pad(jnp.concatenate([b_router_expert[layer], b_router_group[layer]]), (0, pad))[None, :]
        w_router_hi = w_router.astype(bf16)
        w_router_lo = (w_router - w_router_hi.astype(jnp.float32)).astype(bf16)
        w_router = jnp.concatenate([w_router_hi, w_router_hi, w_router_lo], axis=0)
        x1, u2, route, counts = _mix_ln1(o_sb, p, w_pool[layer].astype(bf16), pool_scale[layer][None, :],
                                         w_out[layer].astype(bf16), x2d, mod, ln1_g[layer][None, :],
                                         ln1_b[layer][None, :], w_router, b_router, s)
        tile_expert, n_tiles, tile_end, pos_dispatch, pos, n_rows = _dispatch_plan(route, counts)
        u_sorted = _dispatch(tile_end, pos_dispatch, u2, n_rows)
        y_sorted = _moe_grouped(tile_expert, n_tiles, u_sorted, w_gate[layer], w_up[layer], w_down[layer])
        x2 = _combine_ln2(pos, y_sorted, route, x1, mod, ln2_g[layer][None, :], ln2_b[layer][None, :], s)
        x = x2.reshape(b, s, d)
    return x
```

```python
import functools

import jax
import jax.numpy as jnp
from jax import lax
from jax.experimental import pallas as pl
from jax.experimental.pallas import tpu as pltpu

D_MODEL = 1024
N_SB_HEADS = 8
SB_HEAD_DIM = 64
SB_WIDTH = N_SB_HEADS * SB_HEAD_DIM
POOL_WINDOWS = (2, 4, 8, 16)
POOL_GROUP_DIM = 128
POOL_WIDTH = len(POOL_WINDOWS) * POOL_GROUP_DIM
N_EXPERT_GROUPS = 4
EXPERTS_PER_GROUP = 4
N_EXPERTS = N_EXPERT_GROUPS * EXPERTS_PER_GROUP
EXPERT_HIDDEN = 512
DEPTH = 1
DEEPNORM_ALPHA = (2.0 * DEPTH) ** 0.25
LN_EPS = 1e-5
N_MOD = 6

LANES = 128
SUBLANES = 8
HALO = max(POOL_WINDOWS)
VMEM_LIMIT = 56 * 1024 * 1024

ROW_TILE = 512
ATTN_TILE = 256
STICK_GONE_LOG2 = -180.0
MOE_ROW_TILE = 512
DISPATCH_ROW_TILE = 1024
COMBINE_ROW_TILE = 256
GATHER_UNROLL = 8


ROW_CHUNKS = D_MODEL // LANES


def _store_tile_rows(ref, base, value):
    n = value.shape[0]
    for c in range(ROW_CHUNKS):
        ref[pl.ds(base + c, n, stride=ROW_CHUNKS), :] = value[:, c * LANES:(c + 1) * LANES]


def _load_tile_rows(ref, base, n):
    return jnp.concatenate([ref[pl.ds(base + c, n, stride=ROW_CHUNKS), :] for c in range(ROW_CHUNKS)], axis=1)


def _ln(x):
    mu = jnp.mean(x, axis=-1, keepdims=True)
    xc = x - mu
    var = jnp.mean(xc * xc, axis=-1, keepdims=True)
    return xc * lax.rsqrt(var + LN_EPS)


def _adaln_kernel(c_ref, w_ref, b_ref, o_ref):
    c = c_ref[...]
    a = c * jax.nn.sigmoid(c)
    o_ref[...] = jnp.dot(a, w_ref[...], preferred_element_type=jnp.float32,
                         precision=lax.Precision.HIGHEST) + b_ref[...]


def _adaln(c_pad, w_ada, b_ada):
    rows, d = c_pad.shape
    n = w_ada.shape[1]
    tn = 1024
    return pl.pallas_call(
        _adaln_kernel,
        grid=(n // tn,),
        in_specs=[pl.BlockSpec((rows, d), lambda j: (0, 0)),
                  pl.BlockSpec((d, tn), lambda j: (0, j)),
                  pl.BlockSpec((1, tn), lambda j: (0, j))],
        out_specs=pl.BlockSpec((rows, tn), lambda j: (0, j)),
        out_shape=jax.ShapeDtypeStruct((rows, n), jnp.float32),
        compiler_params=pltpu.CompilerParams(vmem_limit_bytes=VMEM_LIMIT),
        name="adaln",
    )(c_pad, w_ada, b_ada)


def _ln_inproj_kernel(x_ref, mod_ref, w_ref, qkv_ref, p_ref):
    shift = mod_ref[0, 0:1, :]
    scale = mod_ref[0, 1:2, :]
    u = (_ln(x_ref[...]) * (1.0 + scale) + shift).astype(jnp.bfloat16)
    qk_scale = SB_HEAD_DIM ** -0.5 * 1.4426950408889634
    q = jnp.dot(u, w_ref[:, 0:SB_WIDTH], preferred_element_type=jnp.float32) * qk_scale
    qkv_ref[:, 0:SB_WIDTH] = q.astype(jnp.bfloat16)
    for j in (1, 2):
        kv = jnp.dot(u, w_ref[:, j * SB_WIDTH:(j + 1) * SB_WIDTH], preferred_element_type=jnp.float32)
        qkv_ref[:, j * SB_WIDTH:(j + 1) * SB_WIDTH] = kv.astype(jnp.bfloat16)
    p_ref[...] = jnp.dot(u, w_ref[:, 3 * SB_WIDTH:], preferred_element_type=jnp.float32)


def _ln_inproj(x2d, mod, w_in_bf16, seq):
    t, d = x2d.shape
    tm = ROW_TILE
    tiles_per_seq = seq // tm
    return pl.pallas_call(
        _ln_inproj_kernel,
        grid=(t // tm,),
        in_specs=[pl.BlockSpec((tm, d), lambda i: (i, 0)),
                  pl.BlockSpec((1, N_MOD, d), lambda i: (i // tiles_per_seq, 0, 0)),
                  pl.BlockSpec(w_in_bf16.shape, lambda i: (0, 0))],
        out_specs=[pl.BlockSpec((tm, 3 * SB_WIDTH), lambda i: (i, 0)),
                   pl.BlockSpec((tm, POOL_WIDTH), lambda i: (i, 0))],
        out_shape=[jax.ShapeDtypeStruct((t, 3 * SB_WIDTH), jnp.bfloat16),
                   jax.ShapeDtypeStruct((t, POOL_WIDTH), jnp.float32)],
        compiler_params=pltpu.CompilerParams(vmem_limit_bytes=VMEM_LIMIT),
        name="ln_inproj",
    )(x2d, mod, w_in_bf16)


def _sb_attn_kernel(q_ref, k_ref, v_ref, o_ref, acc_ref, carry_ref):
    tq = tk = ATTN_TILE
    heads = (0, 1)
    qi = pl.program_id(2)
    q2 = q_ref[0]
    lane = lax.broadcasted_iota(jnp.int32, (tq, LANES), 1)
    first_head = lane < SB_HEAD_DIM
    zero = jnp.zeros_like(q2)
    q_heads = (jnp.where(first_head, q2, zero), jnp.where(first_head, zero, q2))

    row = lax.broadcasted_iota(jnp.int32, (tk, tk), 0)
    col = lax.broadcasted_iota(jnp.int32, (tk, tk), 1)
    neg_suffix = jnp.where(row >= col, -1.0, 0.0).astype(jnp.bfloat16)
    causal = col < row

    def visit(blocks, carries):
        k_blks, v_blks = [], []
        for j, _ in blocks:
            start = pl.multiple_of(jnp.maximum(qi - j, 0) * tk, tk)
            k_blks.append(k_ref[0, pl.ds(start, tk), :])
            v_blks.append(v_ref[0, pl.ds(start, tk), :])
        pairs = [(b, h) for b in range(len(blocks)) for h in heads]
        zs = {(b, h): lax.dot_general(q_heads[h], k_blks[b], (((1,), (1,)), ((), ())),
                                      preferred_element_type=jnp.float32) for b, h in pairs}
        fixed, rounded = {}, {}
        for b, h in pairs:
            z = zs[b, h]
            n = jnp.maximum(z, 0.0) + jnp.log2(1.0 + jnp.exp2(-jnp.abs(z)))
            if blocks[b][1]:
                n = jnp.where(causal, n, 0.0)
            n_bf16 = n.astype(jnp.bfloat16)
            rounded[b, h] = n_bf16
            fixed[b, h] = z - (n - n_bf16.astype(jnp.float32))
        incls = {bh: jnp.dot(rounded[bh], neg_suffix, preferred_element_type=jnp.float32) for bh in pairs}
        atts = {}
        carries = list(carries)
        for b, h in pairs:
            carry = carries[h]
            block_sum = jnp.broadcast_to(incls[b, h][:, 0:1], (tq, LANES))
            expo = fixed[b, h] + incls[b, h]
            if carry is None:
                carries[h] = block_sum
            else:
                expo = expo + jnp.concatenate([carry] * (tk // LANES), axis=1)
                carries[h] = carry + block_sum
            att = jnp.exp2(expo)
            if blocks[b][1]:
                att = jnp.where(causal, att, 0.0)
            atts[b, h] = att.astype(jnp.bfloat16)
        outs = {(b, h): jnp.dot(atts[b, h], v_blks[b], preferred_element_type=jnp.float32) for b, h in pairs}
        return outs, carries

    outs, carries = visit([(0, True), (1, False)], [None, None])
    has_second = qi >= 1
    for h in heads:
        acc_ref[h] = outs[0, h] + jnp.where(has_second, outs[1, h], 0.0)
        carry_ref[h] = carries[h]

    def more(state):
        j, stick_gone = state
        return (j <= qi) & jnp.logical_not(stick_gone)

    def visit_next(state):
        j, _ = state
        outs, new_carries = visit([(j, False)], [carry_ref[h] for h in heads])
        for h in heads:
            acc_ref[h] += outs[0, h]
            carry_ref[h] = new_carries[h]
        return j + 1, jnp.max(carry_ref[...]) < STICK_GONE_LOG2

    lax.while_loop(more, visit_next, (jnp.int32(2), jnp.max(carry_ref[...]) < STICK_GONE_LOG2))
    o_ref[0] = jnp.where(first_head, acc_ref[0], acc_ref[1]).astype(o_ref.dtype)


def _sb_attn(qkv, batch, seq):
    qkv3 = qkv.reshape(batch, seq, 3 * SB_WIDTH)
    tq = ATTN_TILE
    pairs = SB_WIDTH // LANES
    return pl.pallas_call(
        _sb_attn_kernel,
        grid=(batch, pairs, seq // tq),
        in_specs=[pl.BlockSpec((1, tq, LANES), lambda b, h, i: (b, i, h)),
                  pl.BlockSpec((1, seq, LANES), lambda b, h, i: (b, 0, pairs + h)),
                  pl.BlockSpec((1, seq, LANES), lambda b, h, i: (b, 0, 2 * pairs + h))],
        out_specs=pl.BlockSpec((1, tq, LANES), lambda b, h, i: (b, i, h)),
        out_shape=jax.ShapeDtypeStruct((batch, seq, SB_WIDTH), jnp.bfloat16),
        scratch_shapes=[pltpu.VMEM((2, tq, LANES), jnp.float32),
                        pltpu.VMEM((2, tq, LANES), jnp.float32)],
        compiler_params=pltpu.CompilerParams(vmem_limit_bytes=VMEM_LIMIT),
        name="sb_attn",
    )(qkv3, qkv3, qkv3)


def _route(logits):
    lane = lax.broadcasted_iota(jnp.int32, logits.shape, 1).astype(jnp.float32)
    neg = jnp.float32(-jnp.inf)
    big = jnp.float32(LANES)
    is_group = (lane >= N_EXPERTS) & (lane < N_EXPERTS + N_EXPERT_GROUPS)
    gl = jnp.where(is_group, logits, neg)
    g_max = jnp.max(gl, axis=-1, keepdims=True)
    g_sum = jnp.sum(jnp.exp(gl - g_max), axis=-1, keepdims=True)
    g_p = 1.0 / g_sum
    g_idx = jnp.min(jnp.where(gl == g_max, lane, big), axis=-1, keepdims=True) - N_EXPERTS
    in_group = (lane >= g_idx * EXPERTS_PER_GROUP) & (lane < (g_idx + 1) * EXPERTS_PER_GROUP)
    el = jnp.where(in_group, logits, neg)
    e1 = jnp.max(el, axis=-1, keepdims=True)
    i1 = jnp.min(jnp.where(el == e1, lane, big), axis=-1, keepdims=True)
    el2 = jnp.where(lane == i1, neg, el)
    e2 = jnp.max(el2, axis=-1, keepdims=True)
    i2 = jnp.min(jnp.where(el2 == e2, lane, big), axis=-1, keepdims=True)
    r = jnp.exp(e2 - e1)
    w1 = g_p / (1.0 + r)
    w2 = w1 * r
    return i1, i2, w1, w2


ROUTE_W1, ROUTE_W2, ROUTE_E1, ROUTE_E2, ROUTE_RANK1, ROUTE_RANK2 = range(6)


def _mix_ln1_kernel(tiles_per_seq, osb_ref, p_ref, halo_ref, wpool_ref, pscale_ref, wout_ref, x_ref, mod_ref,
                    g1_ref, b1_ref, wr_ref, br_ref, earlier_ref, x1_ref, u2_ref, route_ref, count_ref, pext_ref,
                    running_ref):
    tm = p_ref.shape[0]
    tile_in_seq = pl.program_id(0) % tiles_per_seq
    p = p_ref[...]
    pext_ref[0:HALO, :] = jnp.where(tile_in_seq == 0, 0.0, halo_ref[...])
    pext_ref[HALO:, :] = p
    pos = tile_in_seq * tm + lax.broadcasted_iota(jnp.int32, (tm, 1), 0)

    mixer_out = [osb_ref[...]]
    for g, w in enumerate(POOL_WINDOWS):
        cols = slice(g * POOL_GROUP_DIM, (g + 1) * POOL_GROUP_DIM)
        win = p[:, cols]
        for i in range(1, w):
            win = win + pext_ref[HALO - i:HALO - i + tm, cols]
        count = jnp.minimum(pos + 1, w).astype(jnp.float32)
        pooled = win / count - p[:, cols]
        o_pool = jnp.dot(pooled.astype(jnp.bfloat16), wpool_ref[g], preferred_element_type=jnp.float32)
        mixer_out.append((o_pool * pscale_ref[:, cols]).astype(jnp.bfloat16))
    mixed = jnp.dot(jnp.concatenate(mixer_out, axis=1), wout_ref[...], preferred_element_type=jnp.float32)

    gate1 = mod_ref[0, 2:3, :]
    shift2 = mod_ref[0, 3:4, :]
    scale2 = mod_ref[0, 4:5, :]
    x1 = _ln(DEEPNORM_ALPHA * x_ref[...] + (1.0 + gate1) * mixed) * g1_ref[...] + b1_ref[...]
    x1_ref[...] = x1
    u2 = _ln(x1) * (1.0 + scale2) + shift2
    _store_tile_rows(u2_ref, 0, u2)
    u2_hi = u2.astype(jnp.bfloat16)
    u2_lo = (u2 - u2_hi.astype(jnp.float32)).astype(jnp.bfloat16)
    logits = jnp.dot(jnp.concatenate([u2_hi, u2_lo, u2_hi], axis=1), wr_ref[...],
                     preferred_element_type=jnp.float32) + br_ref[...]
    i1, i2, w1, w2 = _route(logits)

    @pl.when(pl.program_id(0) == 0)
    def _():
        running_ref[...] = jnp.zeros_like(running_ref)

    lane = lax.broadcasted_iota(jnp.int32, (tm, LANES), 1)
    lane_f = lane.astype(jnp.float32)
    first, second = lane_f == i1, lane_f == i2
    uses = jnp.where(first | second, 1.0, 0.0)
    before = jnp.dot(earlier_ref[...], uses.astype(jnp.bfloat16), preferred_element_type=jnp.float32)
    before = before + running_ref[0:1, :]
    rank1 = jnp.sum(jnp.where(first, before, 0.0), axis=-1, keepdims=True)
    rank2 = jnp.sum(jnp.where(second, before, 0.0), axis=-1, keepdims=True)
    running = running_ref[0:1, :] + jnp.sum(uses, axis=0, keepdims=True)
    running_ref[...] = jnp.broadcast_to(running, running_ref.shape)
    count_ref[...] = jnp.broadcast_to(running, count_ref.shape)

    record = jnp.zeros((tm, LANES), jnp.float32)
    for slot, value in ((ROUTE_W1, w1), (ROUTE_W2, w2), (ROUTE_E1, i1), (ROUTE_E2, i2),
                        (ROUTE_RANK1, rank1), (ROUTE_RANK2, rank2)):
        record = jnp.where(lane == slot, value, record)
    route_ref[...] = record


def _mix_ln1(o_sb, p, w_pool_bf16, pool_scale, w_out_bf16, x2d, mod, ln1_g, ln1_b, w_router, b_router, seq):
    t, d = x2d.shape
    tm = ROW_TILE
    tiles_per_seq = seq // tm
    halo_blocks_per_tile = tm // HALO
    row = lambda i: (i, 0)
    const2 = lambda i: (0, 0)
    return pl.pallas_call(
        functools.partial(_mix_ln1_kernel, tiles_per_seq),
        grid=(t // tm,),
        in_specs=[pl.BlockSpec((tm, SB_WIDTH), row),
                  pl.BlockSpec((tm, POOL_WIDTH), row),
                  pl.BlockSpec((HALO, POOL_WIDTH), lambda i: (jnp.maximum(i * halo_blocks_per_tile - 1, 0), 0)),
                  pl.BlockSpec(w_pool_bf16.shape, lambda i: (0, 0, 0)),
                  pl.BlockSpec((1, POOL_WIDTH), const2),
                  pl.BlockSpec(w_out_bf16.shape, const2),
                  pl.BlockSpec((tm, d), row),
                  pl.BlockSpec((1, N_MOD, d), lambda i: (i // tiles_per_seq, 0, 0)),
                  pl.BlockSpec((1, d), const2),
                  pl.BlockSpec((1, d), const2),
                  pl.BlockSpec(w_router.shape, const2),
                  pl.BlockSpec((1, LANES), const2),
                  pl.BlockSpec((tm, tm), const2)],
        out_specs=[pl.BlockSpec((tm, d), row),
                   pl.BlockSpec((tm * ROW_CHUNKS, LANES), row),
                   pl.BlockSpec((tm, LANES), row),
                   pl.BlockSpec((SUBLANES, LANES), const2)],
        out_shape=[jax.ShapeDtypeStruct((t, d), jnp.float32),
                   jax.ShapeDtypeStruct((t * ROW_CHUNKS, LANES), jnp.float32),
                   jax.ShapeDtypeStruct((t, LANES), jnp.float32),
                   jax.ShapeDtypeStruct((SUBLANES, LANES), jnp.float32)],
        scratch_shapes=[pltpu.VMEM((HALO + tm, POOL_WIDTH), jnp.float32),
                        pltpu.VMEM((SUBLANES, LANES), jnp.float32)],
        compiler_params=pltpu.CompilerParams(vmem_limit_bytes=VMEM_LIMIT,
                                             dimension_semantics=("arbitrary",)),
        name="mix_ln1",
    )(o_sb, p, p, w_pool_bf16, pool_scale, w_out_bf16, x2d, mod, ln1_g, ln1_b, w_router, b_router,
      jnp.tri(tm, tm, -1, dtype=jnp.bfloat16))


def _dispatch_kernel(tile_end_ref, pos_ref, u_ref, sorted_hbm, zeros_ref, sem):
    tm = u_ref.shape[0] // ROW_CHUNKS
    tile = MOE_ROW_TILE * ROW_CHUNKS
    stored = lambda r: pl.ds(pl.multiple_of(r * ROW_CHUNKS, ROW_CHUNKS), ROW_CHUNKS)

    @pl.when(pl.program_id(0) == 0)
    def _():
        zeros_ref[...] = jnp.zeros_like(zeros_ref)

        def tile_fill(i):
            return pltpu.make_async_copy(zeros_ref, sorted_hbm.at[pl.ds(pl.multiple_of(i * tile, tile), tile), :],
                                         sem)

        def last_tile_fills(act):
            for e in range(N_EXPERTS):
                first_tile = tile_end_ref[e - 1] if e else 0

                @pl.when(tile_end_ref[e] > first_tile)
                def _():
                    act(tile_fill(tile_end_ref[e] - 1))

        unused = (tile_end_ref[N_EXPERTS - 1], sorted_hbm.shape[0] // tile)
        last_tile_fills(lambda fill: fill.start())
        lax.fori_loop(*unused, lambda i, _: (tile_fill(i).start(), 0)[1], 0)
        last_tile_fills(lambda fill: fill.wait())
        lax.fori_loop(*unused, lambda i, _: (tile_fill(i).wait(), 0)[1], 0)

    def body(r, _):
        row = u_ref.at[stored(r), :]
        pltpu.make_async_copy(row, sorted_hbm.at[stored(pos_ref[0, 0, r]), :], sem).start()
        pltpu.make_async_copy(row, sorted_hbm.at[stored(pos_ref[0, 0, tm + r]), :], sem).start()
        return 0

    lax.fori_loop(0, tm, body, 0, unroll=GATHER_UNROLL // 2)
    for _ in range(2):
        pltpu.make_async_copy(u_ref, sorted_hbm.at[pl.ds(0, tm * ROW_CHUNKS), :], sem).wait()


def _dispatch(tile_end, pos, u2, n_rows):
    tm = pos.shape[2] // 2
    t = u2.shape[0] // ROW_CHUNKS
    grid_spec = pltpu.PrefetchScalarGridSpec(
        num_scalar_prefetch=1,
        grid=(t // tm,),
        in_specs=[pl.BlockSpec((1, 1, 2 * tm), lambda i, te: (i, 0, 0), memory_space=pltpu.SMEM),
                  pl.BlockSpec((tm * ROW_CHUNKS, LANES), lambda i, te: (i, 0))],
        out_specs=pl.BlockSpec(memory_space=pl.ANY),
        scratch_shapes=[pltpu.VMEM((MOE_ROW_TILE * ROW_CHUNKS, LANES), jnp.float32),
                        pltpu.SemaphoreType.DMA(())],
    )
    return pl.pallas_call(
        _dispatch_kernel,
        grid_spec=grid_spec,
        out_shape=jax.ShapeDtypeStruct((n_rows * ROW_CHUNKS, LANES), jnp.float32),
        compiler_params=pltpu.CompilerParams(vmem_limit_bytes=VMEM_LIMIT,
                                             dimension_semantics=("arbitrary",)),
        name="dispatch",
    )(tile_end, pos, u2)


def _moe_grouped_kernel(tile_expert_ref, n_tiles_ref, u_ref, wg_ref, wu_ref, wd_ref, y_ref,
                        wg_bf16_ref, wu_bf16_ref, wd_bf16_ref):
    i = pl.program_id(0)
    bf16 = jnp.bfloat16
    valid = i < n_tiles_ref[0]

    @pl.when(valid & ((i == 0) | (tile_expert_ref[i] != tile_expert_ref[jnp.maximum(i - 1, 0)])))
    def _():
        wg_bf16_ref[...] = wg_ref[0].astype(bf16)
        wu_bf16_ref[...] = wu_ref[0].astype(bf16)
        wd_bf16_ref[...] = wd_ref[0].astype(bf16)

    @pl.when(valid)
    def _():
        u = _load_tile_rows(u_ref, 0, MOE_ROW_TILE).astype(bf16)
        gate = jnp.dot(u, wg_bf16_ref[...], preferred_element_type=jnp.float32)
        up = jnp.dot(u, wu_bf16_ref[...], preferred_element_type=jnp.float32)
        h = gate * jax.nn.sigmoid(gate) * up
        _store_tile_rows(y_ref, 0, jnp.dot(h.astype(bf16), wd_bf16_ref[...], preferred_element_type=jnp.float32))

    @pl.when(jnp.logical_not(valid))
    def _():
        y_ref[...] = jnp.zeros_like(y_ref)


def _moe_grouped(tile_expert, n_tiles, u_sorted, wg, wu, wd):
    d = D_MODEL
    tm = MOE_ROW_TILE
    max_tiles = tile_expert.shape[0]
    expert = lambda i, te, nt: (te[i], 0, 0)
    grid_spec = pltpu.PrefetchScalarGridSpec(
        num_scalar_prefetch=2,
        grid=(max_tiles,),
        in_specs=[pl.BlockSpec((tm * ROW_CHUNKS, LANES), lambda i, te, nt: (jnp.minimum(i, nt[0] - 1), 0)),
                  pl.BlockSpec((1, d, EXPERT_HIDDEN), expert),
                  pl.BlockSpec((1, d, EXPERT_HIDDEN), expert),
                  pl.BlockSpec((1, EXPERT_HIDDEN, d), expert)],
        out_specs=pl.BlockSpec((tm * ROW_CHUNKS, LANES), lambda i, te, nt: (i, 0)),
        scratch_shapes=[pltpu.VMEM((d, EXPERT_HIDDEN), jnp.bfloat16),
                        pltpu.VMEM((d, EXPERT_HIDDEN), jnp.bfloat16),
                        pltpu.VMEM((EXPERT_HIDDEN, d), jnp.bfloat16)],
    )
    return pl.pallas_call(
        _moe_grouped_kernel,
        grid_spec=grid_spec,
        out_shape=jax.ShapeDtypeStruct((max_tiles * tm * ROW_CHUNKS, LANES), jnp.float32),
        compiler_params=pltpu.CompilerParams(vmem_limit_bytes=VMEM_LIMIT,
                                             dimension_semantics=("arbitrary",)),
        name="moe_grouped",
    )(tile_expert, n_tiles, u_sorted, wg, wu, wd)


def _combine_ln2_kernel(pos_ref, pos_next_ref, y_hbm, route_ref, x1_ref, mod_ref, g2_ref, b2_ref, o_ref,
                        buf_ref, sem_ref):
    i = pl.program_id(0)
    n = pl.num_programs(0)
    tm = o_ref.shape[0]
    rows = 2 * tm
    stored = lambda r: pl.ds(pl.multiple_of(r * ROW_CHUNKS, ROW_CHUNKS), ROW_CHUNKS)

    def start_gather(rows_ref, slot):
        def body(r, _):
            pltpu.make_async_copy(y_hbm.at[stored(rows_ref[0, 0, r]), :], buf_ref.at[slot, stored(r), :],
                                  sem_ref.at[slot]).start()
            return 0
        lax.fori_loop(0, rows, body, 0, unroll=GATHER_UNROLL)

    @pl.when(i == 0)
    def _():
        start_gather(pos_ref, 0)

    @pl.when(i + 1 < n)
    def _():
        start_gather(pos_next_ref, (i + 1) % 2)

    slot = i % 2
    pltpu.make_async_copy(y_hbm.at[pl.ds(0, rows * ROW_CHUNKS), :], buf_ref.at[slot], sem_ref.at[slot]).wait()
    route = route_ref[...]
    w1 = route[:, ROUTE_W1:ROUTE_W1 + 1]
    w2 = route[:, ROUTE_W2:ROUTE_W2 + 1]
    gathered = buf_ref.at[slot]
    y = w1 * _load_tile_rows(gathered, 0, tm) + w2 * _load_tile_rows(gathered, tm * ROW_CHUNKS, tm)
    gate2 = mod_ref[0, 5:6, :]
    o_ref[...] = _ln(DEEPNORM_ALPHA * x1_ref[...] + (1.0 + gate2) * y) * g2_ref[...] + b2_ref[...]


def _combine_ln2(pos, y_sorted, route, x1, mod, ln2_g, ln2_b, seq):
    t, d = x1.shape
    tm = COMBINE_ROW_TILE
    n = t // tm
    tiles_per_seq = seq // tm
    row = lambda i: (i, 0)
    const2 = lambda i: (0, 0)
    return pl.pallas_call(
        _combine_ln2_kernel,
        grid=(n,),
        in_specs=[pl.BlockSpec((1, 1, 2 * tm), lambda i: (i, 0, 0), memory_space=pltpu.SMEM),
                  pl.BlockSpec((1, 1, 2 * tm), lambda i: (jnp.minimum(i + 1, n - 1), 0, 0),
                               memory_space=pltpu.SMEM),
                  pl.BlockSpec(memory_space=pl.ANY),
                  pl.BlockSpec((tm, LANES), row),
                  pl.BlockSpec((tm, d), row),
                  pl.BlockSpec((1, N_MOD, d), lambda i: (i // tiles_per_seq, 0, 0)),
                  pl.BlockSpec((1, d), const2),
                  pl.BlockSpec((1, d), const2)],
        out_specs=pl.BlockSpec((tm, d), row),
        out_shape=jax.ShapeDtypeStruct((t, d), jnp.float32),
        scratch_shapes=[pltpu.VMEM((2, 2 * tm * ROW_CHUNKS, LANES), jnp.float32),
                        pltpu.SemaphoreType.DMA((2,))],
        compiler_params=pltpu.CompilerParams(vmem_limit_bytes=VMEM_LIMIT,
                                             dimension_semantics=("arbitrary",)),
        name="combine_ln2",
    )(pos, pos, y_sorted, route, x1, mod, ln2_g, ln2_b)


def _dispatch_plan(route, counts):
    t = route.shape[0]
    tm = MOE_ROW_TILE
    max_tiles = 2 * t // tm + N_EXPERTS
    counts = counts[0, :N_EXPERTS].astype(jnp.int32)
    tiles = (counts + tm - 1) // tm
    tile_end = jnp.cumsum(tiles)
    row_start = (tile_end - tiles) * tm
    record = route[:, :SUBLANES].T
    experts = record[ROUTE_E1:ROUTE_E2 + 1].astype(jnp.int32)
    ranks = record[ROUTE_RANK1:ROUTE_RANK2 + 1].astype(jnp.int32)
    expert_ids = jnp.arange(N_EXPERTS, dtype=jnp.int32)[:, None, None]
    pos = jnp.sum(jnp.where(experts[None] == expert_ids, row_start[:, None, None], 0), axis=0) + ranks
    tile_ids = jnp.arange(max_tiles, dtype=jnp.int32)
    tile_expert = jnp.minimum(jnp.sum(tile_end[None, :] <= tile_ids[:, None], axis=1), N_EXPERTS - 1)
    by_tile = lambda tc: pos.reshape(2, t // tc, tc).transpose(1, 0, 2).reshape(t // tc, 1, 2 * tc)
    return (tile_expert.astype(jnp.int32), tile_end[-1:].astype(jnp.int32), tile_end.astype(jnp.int32),
            by_tile(DISPATCH_ROW_TILE), by_tile(COMBINE_ROW_TILE), max_tiles * tm)


def kernel(x, c, w_ada, b_ada, w_in, w_pool, pool_scale, w_out, ln1_g, ln1_b, w_router_group, b_router_group,
           w_router_expert, b_router_expert, w_gate, w_up, w_down, ln2_g, ln2_b):
    b, s, d = x.shape
    bf16 = jnp.bfloat16
    for layer in range(DEPTH):
        c_pad = jnp.pad(c, ((0, SUBLANES - b), (0, 0)))
        mod = _adaln(c_pad, w_ada[layer], b_ada[layer][None, :])[:b].reshape(b, N_MOD, d)
        x2d = x.reshape(b * s, d)
        qkv, p = _ln_inproj(x2d, mod, w_in[layer].astype(bf16), s)
        o_sb = _sb_attn(qkv, b, s).reshape(b * s, SB_WIDTH)
        pad = LANES - N_EXPERTS - N_EXPERT_GROUPS
        w_router = jnp.pad(jnp.concatenate([w_router_expert[layer], w_router_group[layer]], axis=1),
                           ((0, 0), (0, pad)))
        b_router = jnp.pad(jnp.concatenate([b_router_expert[layer], b_router_group[layer]]), (0, pad))[None, :]
        w_router_hi = w_router.astype(bf16)
        w_router_lo = (w_router - w_router_hi.astype(jnp.float32)).astype(bf16)
        w_router = jnp.concatenate([w_router_hi, w_router_hi, w_router_lo], axis=0)
        x1, u2, route, counts = _mix_ln1(o_sb, p, w_pool[layer].astype(bf16), pool_scale[layer][None, :],
                                         w_out[layer].astype(bf16), x2d, mod, ln1_g[layer][None, :],
                                         ln1_b[layer][None, :], w_router, b_router, s)
        tile_expert, n_tiles, tile_end, pos_dispatch, pos, n_rows = _dispatch_plan(route, counts)
        u_sorted = _dispatch(tile_end, pos_dispatch, u2, n_rows)
        y_sorted = _moe_grouped(tile_expert, n_tiles, u_sorted, w_gate[layer], w_up[layer], w_down[layer])
        x2 = _combine_ln2(pos, y_sorted, route, x1, mod, ln2_g[layer][None, :], ln2_b[layer][None, :], s)
        x = x2.reshape(b, s, d)
    return x
```

```python
import functools

import jax
import jax.numpy as jnp
from jax import lax
from jax.experimental import pallas as pl
from jax.experimental.pallas import tpu as pltpu

D_MODEL = 1024
N_SB_HEADS = 8
SB_HEAD_DIM = 64
SB_WIDTH = N_SB_HEADS * SB_HEAD_DIM
POOL_WINDOWS = (2, 4, 8, 16)
POOL_GROUP_DIM = 128
POOL_WIDTH = len(POOL_WINDOWS) * POOL_GROUP_DIM
N_EXPERT_GROUPS = 4
EXPERTS_PER_GROUP = 4
N_EXPERTS = N_EXPERT_GROUPS * EXPERTS_PER_GROUP
EXPERT_HIDDEN = 512
DEPTH = 1
DEEPNORM_ALPHA = (2.0 * DEPTH) ** 0.25
LN_EPS = 1e-5
N_MOD = 6

LANES = 128
SUBLANES = 8
HALO = max(POOL_WINDOWS)
VMEM_LIMIT = 56 * 1024 * 1024

ROW_TILE = 512
ATTN_TILE = 256
ATTN_QUERY_BLOCKS = 2
STICK_GONE_LOG2 = -180.0
MOE_ROW_TILE = 512
DISPATCH_ROW_TILE = 1024
COMBINE_ROW_TILE = 256
GATHER_UNROLL = 8


ROW_CHUNKS = D_MODEL // LANES


def _store_tile_rows(ref, base, value):
    n = value.shape[0]
    for c in range(ROW_CHUNKS):
        ref[pl.ds(base + c, n, stride=ROW_CHUNKS), :] = value[:, c * LANES:(c + 1) * LANES]


def _load_tile_rows(ref, base, n):
    return jnp.concatenate([ref[pl.ds(base + c, n, stride=ROW_CHUNKS), :] for c in range(ROW_CHUNKS)], axis=1)


def _ln(x):
    mu = jnp.mean(x, axis=-1, keepdims=True)
    xc = x - mu
    var = jnp.mean(xc * xc, axis=-1, keepdims=True)
    return xc * lax.rsqrt(var + LN_EPS)


def _adaln_kernel(c_ref, w_ref, b_ref, o_ref):
    c = c_ref[...]
    a = c * jax.nn.sigmoid(c)
    o_ref[...] = jnp.dot(a, w_ref[...], preferred_element_type=jnp.float32,
                         precision=lax.Precision.HIGHEST) + b_ref[...]


def _adaln(c_pad, w_ada, b_ada):
    rows, d = c_pad.shape
    n = w_ada.shape[1]
    tn = 1024
    return pl.pallas_call(
        _adaln_kernel,
        grid=(n // tn,),
        in_specs=[pl.BlockSpec((rows, d), lambda j: (0, 0)),
                  pl.BlockSpec((d, tn), lambda j: (0, j)),
                  pl.BlockSpec((1, tn), lambda j: (0, j))],
        out_specs=pl.BlockSpec((rows, tn), lambda j: (0, j)),
        out_shape=jax.ShapeDtypeStruct((rows, n), jnp.float32),
        compiler_params=pltpu.CompilerParams(vmem_limit_bytes=VMEM_LIMIT),
        name="adaln",
    )(c_pad, w_ada, b_ada)


def _ln_inproj_kernel(x_ref, mod_ref, w_ref, qkv_ref, p_ref):
    shift = mod_ref[0, 0:1, :]
    scale = mod_ref[0, 1:2, :]
    u = (_ln(x_ref[...]) * (1.0 + scale) + shift).astype(jnp.bfloat16)
    qk_scale = SB_HEAD_DIM ** -0.5 * 1.4426950408889634
    q = jnp.dot(u, w_ref[:, 0:SB_WIDTH], preferred_element_type=jnp.float32) * qk_scale
    qkv_ref[:, 0:SB_WIDTH] = q.astype(jnp.bfloat16)
    for j in (1, 2):
        kv = jnp.dot(u, w_ref[:, j * SB_WIDTH:(j + 1) * SB_WIDTH], preferred_element_type=jnp.float32)
        qkv_ref[:, j * SB_WIDTH:(j + 1) * SB_WIDTH] = kv.astype(jnp.bfloat16)
    p_ref[...] = jnp.dot(u, w_ref[:, 3 * SB_WIDTH:], preferred_element_type=jnp.float32)


def _ln_inproj(x2d, mod, w_in_bf16, seq):
    t, d = x2d.shape
    tm = ROW_TILE
    tiles_per_seq = seq // tm
    return pl.pallas_call(
        _ln_inproj_kernel,
        grid=(t // tm,),
        in_specs=[pl.BlockSpec((tm, d), lambda i: (i, 0)),
                  pl.BlockSpec((1, N_MOD, d), lambda i: (i // tiles_per_seq, 0, 0)),
                  pl.BlockSpec(w_in_bf16.shape, lambda i: (0, 0))],
        out_specs=[pl.BlockSpec((tm, 3 * SB_WIDTH), lambda i: (i, 0)),
                   pl.BlockSpec((tm, POOL_WIDTH), lambda i: (i, 0))],
        out_shape=[jax.ShapeDtypeStruct((t, 3 * SB_WIDTH), jnp.bfloat16),
                   jax.ShapeDtypeStruct((t, POOL_WIDTH), jnp.float32)],
        compiler_params=pltpu.CompilerParams(vmem_limit_bytes=VMEM_LIMIT),
        name="ln_inproj",
    )(x2d, mod, w_in_bf16)


def _sb_attn_kernel(q_ref, k_ref, v_ref, o_ref, acc_ref, carry_ref):
    tq = tk = ATTN_TILE
    heads = (0, 1)
    subs = tuple(range(ATTN_QUERY_BLOCKS))
    qis = [pl.program_id(2) * ATTN_QUERY_BLOCKS + s for s in subs]
    lane = lax.broadcasted_iota(jnp.int32, (tq, LANES), 1)
    first_head = lane < SB_HEAD_DIM
    q_heads = {}
    for s in subs:
        q2 = q_ref[0, s * tq:(s + 1) * tq, :]
        zero = jnp.zeros_like(q2)
        q_heads[s, 0], q_heads[s, 1] = jnp.where(first_head, q2, zero), jnp.where(first_head, zero, q2)

    row = lax.broadcasted_iota(jnp.int32, (tk, tk), 0)
    col = lax.broadcasted_iota(jnp.int32, (tk, tk), 1)
    neg_suffix = jnp.where(row >= col, -1.0, 0.0).astype(jnp.bfloat16)
    causal = col < row

    def visit(blocks, carries):
        k_blks, v_blks = [], []
        for s, j, _ in blocks:
            start = pl.multiple_of(jnp.maximum(qis[s] - j, 0) * tk, tk)
            k_blks.append(k_ref[0, pl.ds(start, tk), :])
            v_blks.append(v_ref[0, pl.ds(start, tk), :])
        pairs = [(b, h) for b in range(len(blocks)) for h in heads]
        zs = {(b, h): lax.dot_general(q_heads[blocks[b][0], h], k_blks[b], (((1,), (1,)), ((), ())),
                                      preferred_element_type=jnp.float32) for b, h in pairs}
        fixed, rounded = {}, {}
        for b, h in pairs:
            z = zs[b, h]
            n = jnp.maximum(z, 0.0) + jnp.log2(1.0 + jnp.exp2(-jnp.abs(z)))
            if blocks[b][2]:
                n = jnp.where(causal, n, 0.0)
            n_bf16 = n.astype(jnp.bfloat16)
            rounded[b, h] = n_bf16
            fixed[b, h] = z - (n - n_bf16.astype(jnp.float32))
        incls = {bh: jnp.dot(rounded[bh], neg_suffix, preferred_element_type=jnp.float32) for bh in pairs}
        atts = {}
        carries = dict(carries)
        for b, h in pairs:
            s = blocks[b][0]
            carry = carries[s, h]
            block_sum = jnp.broadcast_to(incls[b, h][:, 0:1], (tq, LANES))
            expo = fixed[b, h] + incls[b, h]
            if carry is None:
                carries[s, h] = block_sum
            else:
                expo = expo + jnp.concatenate([carry] * (tk // LANES), axis=1)
                carries[s, h] = carry + block_sum
            att = jnp.exp2(expo)
            if blocks[b][2]:
                att = jnp.where(causal, att, 0.0)
            atts[b, h] = att.astype(jnp.bfloat16)
        outs = {(b, h): jnp.dot(atts[b, h], v_blks[b], preferred_element_type=jnp.float32) for b, h in pairs}
        return outs, carries

    head_blocks = [(s, 0, True) for s in subs] + [(s, 1, False) for s in subs]
    outs, carries = visit(head_blocks, {(s, h): None for s in subs for h in heads})
    for s in subs:
        has_second = qis[s] >= 1
        for h in heads:
            acc_ref[s, h] = outs[s, h] + jnp.where(has_second, outs[len(subs) + s, h], 0.0)
            carry_ref[s, h] = carries[s, h]

    for s in subs:
        def more(state, s=s):
            j, stick_gone = state
            return (j <= qis[s]) & jnp.logical_not(stick_gone)

        def visit_next(state, s=s):
            j, _ = state
            outs, new_carries = visit([(s, j, False)], {(s, h): carry_ref[s, h] for h in heads})
            for h in heads:
                acc_ref[s, h] += outs[0, h]
                carry_ref[s, h] = new_carries[s, h]
            return j + 1, jnp.max(carry_ref[s]) < STICK_GONE_LOG2

        lax.while_loop(more, visit_next, (jnp.int32(2), jnp.max(carry_ref[s]) < STICK_GONE_LOG2))
        o_ref[0, s * tq:(s + 1) * tq, :] = jnp.where(first_head, acc_ref[s, 0], acc_ref[s, 1]).astype(o_ref.dtype)


def _sb_attn(qkv, batch, seq):
    qkv3 = qkv.reshape(batch, seq, 3 * SB_WIDTH)
    tq = ATTN_TILE
    rows = ATTN_QUERY_BLOCKS * tq
    pairs = SB_WIDTH // LANES
    return pl.pallas_call(
        _sb_attn_kernel,
        grid=(batch, pairs, seq // rows),
        in_specs=[pl.BlockSpec((1, rows, LANES), lambda b, h, i: (b, i, h)),
                  pl.BlockSpec((1, seq, LANES), lambda b, h, i: (b, 0, pairs + h)),
                  pl.BlockSpec((1, seq, LANES), lambda b, h, i: (b, 0, 2 * pairs + h))],
        out_specs=pl.BlockSpec((1, rows, LANES), lambda b, h, i: (b, i, h)),
        out_shape=jax.ShapeDtypeStruct((batch, seq, SB_WIDTH), jnp.bfloat16),
        scratch_shapes=[pltpu.VMEM((ATTN_QUERY_BLOCKS, 2, tq, LANES), jnp.float32),
                        pltpu.VMEM((ATTN_QUERY_BLOCKS, 2, tq, LANES), jnp.float32)],
        compiler_params=pltpu.CompilerParams(vmem_limit_bytes=VMEM_LIMIT),
        name="sb_attn",
    )(qkv3, qkv3, qkv3)


def _route(logits):
    lane = lax.broadcasted_iota(jnp.int32, logits.shape, 1).astype(jnp.float32)
    neg = jnp.float32(-jnp.inf)
    big = jnp.float32(LANES)
    is_group = (lane >= N_EXPERTS) & (lane < N_EXPERTS + N_EXPERT_GROUPS)
    gl = jnp.where(is_group, logits, neg)
    g_max = jnp.max(gl, axis=-1, keepdims=True)
    g_sum = jnp.sum(jnp.exp(gl - g_max), axis=-1, keepdims=True)
    g_p = 1.0 / g_sum
    g_idx = jnp.min(jnp.where(gl == g_max, lane, big), axis=-1, keepdims=True) - N_EXPERTS
    in_group = (lane >= g_idx * EXPERTS_PER_GROUP) & (lane < (g_idx + 1) * EXPERTS_PER_GROUP)
    el = jnp.where(in_group, logits, neg)
    e1 = jnp.max(el, axis=-1, keepdims=True)
    i1 = jnp.min(jnp.where(el == e1, lane, big), axis=-1, keepdims=True)
    el2 = jnp.where(lane == i1, neg, el)
    e2 = jnp.max(el2, axis=-1, keepdims=True)
    i2 = jnp.min(jnp.where(el2 == e2, lane, big), axis=-1, keepdims=True)
    r = jnp.exp(e2 - e1)
    w1 = g_p / (1.0 + r)
    w2 = w1 * r
    return i1, i2, w1, w2


ROUTE_W1, ROUTE_W2, ROUTE_E1, ROUTE_E2, ROUTE_RANK1, ROUTE_RANK2 = range(6)


def _mix_ln1_kernel(tiles_per_seq, osb_ref, p_ref, halo_ref, wpool_ref, pscale_ref, wout_ref, x_ref, mod_ref,
                    g1_ref, b1_ref, wr_ref, br_ref, earlier_ref, x1_ref, u2_ref, route_ref, count_ref, pext_ref,
                    running_ref):
    tm = p_ref.shape[0]
    tile_in_seq = pl.program_id(0) % tiles_per_seq
    p = p_ref[...]
    pext_ref[0:HALO, :] = jnp.where(tile_in_seq == 0, 0.0, halo_ref[...])
    pext_ref[HALO:, :] = p
    pos = tile_in_seq * tm + lax.broadcasted_iota(jnp.int32, (tm, 1), 0)

    mixer_out = [osb_ref[...]]
    for g, w in enumerate(POOL_WINDOWS):
        cols = slice(g * POOL_GROUP_DIM, (g + 1) * POOL_GROUP_DIM)
        win = p[:, cols]
        for i in range(1, w):
            win = win + pext_ref[HALO - i:HALO - i + tm, cols]
        count = jnp.minimum(pos + 1, w).astype(jnp.float32)
        pooled = win / count - p[:, cols]
        o_pool = jnp.dot(pooled.astype(jnp.bfloat16), wpool_ref[g], preferred_element_type=jnp.float32)
        mixer_out.append((o_pool * pscale_ref[:, cols]).astype(jnp.bfloat16))
    mixed = jnp.dot(jnp.concatenate(mixer_out, axis=1), wout_ref[...], preferred_element_type=jnp.float32)

    gate1 = mod_ref[0, 2:3, :]
    shift2 = mod_ref[0, 3:4, :]
    scale2 = mod_ref[0, 4:5, :]
    x1 = _ln(DEEPNORM_ALPHA * x_ref[...] + (1.0 + gate1) * mixed) * g1_ref[...] + b1_ref[...]
    x1_ref[...] = x1
    u2 = _ln(x1) * (1.0 + scale2) + shift2
    _store_tile_rows(u2_ref, 0, u2)
    u2_hi = u2.astype(jnp.bfloat16)
    u2_lo = (u2 - u2_hi.astype(jnp.float32)).astype(jnp.bfloat16)
    logits = jnp.dot(jnp.concatenate([u2_hi, u2_lo, u2_hi], axis=1), wr_ref[...],
                     preferred_element_type=jnp.float32) + br_ref[...]
    i1, i2, w1, w2 = _route(logits)

    @pl.when(pl.program_id(0) == 0)
    def _():
        running_ref[...] = jnp.zeros_like(running_ref)

    lane = lax.broadcasted_iota(jnp.int32, (tm, LANES), 1)
    lane_f = lane.astype(jnp.float32)
    first, second = lane_f == i1, lane_f == i2
    uses = jnp.where(first | second, 1.0, 0.0)
    before = jnp.dot(earlier_ref[...], uses.astype(jnp.bfloat16), preferred_element_type=jnp.float32)
    before = before + running_ref[0:1, :]
    rank1 = jnp.sum(jnp.where(first, before, 0.0), axis=-1, keepdims=True)
    rank2 = jnp.sum(jnp.where(second, before, 0.0), axis=-1, keepdims=True)
    running = running_ref[0:1, :] + jnp.sum(uses, axis=0, keepdims=True)
    running_ref[...] = jnp.broadcast_to(running, running_ref.shape)
    count_ref[...] = jnp.broadcast_to(running, count_ref.shape)

    record = jnp.zeros((tm, LANES), jnp.float32)
    for slot, value in ((ROUTE_W1, w1), (ROUTE_W2, w2), (ROUTE_E1, i1), (ROUTE_E2, i2),
                        (ROUTE_RANK1, rank1), (ROUTE_RANK2, rank2)):
        record = jnp.where(lane == slot, value, record)
    route_ref[...] = record


def _mix_ln1(o_sb, p, w_pool_bf16, pool_scale, w_out_bf16, x2d, mod, ln1_g, ln1_b, w_router, b_router, seq):
    t, d = x2d.shape
    tm = ROW_TILE
    tiles_per_seq = seq // tm
    halo_blocks_per_tile = tm // HALO
    row = lambda i: (i, 0)
    const2 = lambda i: (0, 0)
    return pl.pallas_call(
        functools.partial(_mix_ln1_kernel, tiles_per_seq),
        grid=(t // tm,),
        in_specs=[pl.BlockSpec((tm, SB_WIDTH), row),
                  pl.BlockSpec((tm, POOL_WIDTH), row),
                  pl.BlockSpec((HALO, POOL_WIDTH), lambda i: (jnp.maximum(i * halo_blocks_per_tile - 1, 0), 0)),
                  pl.BlockSpec(w_pool_bf16.shape, lambda i: (0, 0, 0)),
                  pl.BlockSpec((1, POOL_WIDTH), const2),
                  pl.BlockSpec(w_out_bf16.shape, const2),
                  pl.BlockSpec((tm, d), row),
                  pl.BlockSpec((1, N_MOD, d), lambda i: (i // tiles_per_seq, 0, 0)),
                  pl.BlockSpec((1, d), const2),
                  pl.BlockSpec((1, d), const2),
                  pl.BlockSpec(w_router.shape, const2),
                  pl.BlockSpec((1, LANES), const2),
                  pl.BlockSpec((tm, tm), const2)],
        out_specs=[pl.BlockSpec((tm, d), row),
                   pl.BlockSpec((tm * ROW_CHUNKS, LANES), row),
                   pl.BlockSpec((tm, LANES), row),
                   pl.BlockSpec((SUBLANES, LANES), const2)],
        out_shape=[jax.ShapeDtypeStruct((t, d), jnp.float32),
                   jax.ShapeDtypeStruct((t * ROW_CHUNKS, LANES), jnp.float32),
                   jax.ShapeDtypeStruct((t, LANES), jnp.float32),
                   jax.ShapeDtypeStruct((SUBLANES, LANES), jnp.float32)],
        scratch_shapes=[pltpu.VMEM((HALO + tm, POOL_WIDTH), jnp.float32),
                        pltpu.VMEM((SUBLANES, LANES), jnp.float32)],
        compiler_params=pltpu.CompilerParams(vmem_limit_bytes=VMEM_LIMIT,
                                             dimension_semantics=("arbitrary",)),
        name="mix_ln1",
    )(o_sb, p, p, w_pool_bf16, pool_scale, w_out_bf16, x2d, mod, ln1_g, ln1_b, w_router, b_router,
      jnp.tri(tm, tm, -1, dtype=jnp.bfloat16))


def _dispatch_kernel(tile_end_ref, pos_ref, u_ref, sorted_hbm, zeros_ref, sem):
    tm = u_ref.shape[0] // ROW_CHUNKS
    tile = MOE_ROW_TILE * ROW_CHUNKS
    stored = lambda r: pl.ds(pl.multiple_of(r * ROW_CHUNKS, ROW_CHUNKS), ROW_CHUNKS)

    @pl.when(pl.program_id(0) == 0)
    def _():
        zeros_ref[...] = jnp.zeros_like(zeros_ref)

        def tile_fill(i):
            return pltpu.make_async_copy(zeros_ref, sorted_hbm.at[pl.ds(pl.multiple_of(i * tile, tile), tile), :],
                                         sem)

        def last_tile_fills(act):
            for e in range(N_EXPERTS):
                first_tile = tile_end_ref[e - 1] if e else 0

                @pl.when(tile_end_ref[e] > first_tile)
                def _():
                    act(tile_fill(tile_end_ref[e] - 1))

        unused = (tile_end_ref[N_EXPERTS - 1], sorted_hbm.shape[0] // tile)
        last_tile_fills(lambda fill: fill.start())
        lax.fori_loop(*unused, lambda i, _: (tile_fill(i).start(), 0)[1], 0)
        last_tile_fills(lambda fill: fill.wait())
        lax.fori_loop(*unused, lambda i, _: (tile_fill(i).wait(), 0)[1], 0)

    def body(r, _):
        row = u_ref.at[stored(r), :]
        pltpu.make_async_copy(row, sorted_hbm.at[stored(pos_ref[0, 0, r]), :], sem).start()
        pltpu.make_async_copy(row, sorted_hbm.at[stored(pos_ref[0, 0, tm + r]), :], sem).start()
        return 0

    lax.fori_loop(0, tm, body, 0, unroll=GATHER_UNROLL // 2)
    for _ in range(2):
        pltpu.make_async_copy(u_ref, sorted_hbm.at[pl.ds(0, tm * ROW_CHUNKS), :], sem).wait()


def _dispatch(tile_end, pos, u2, n_rows):
    tm = pos.shape[2] // 2
    t = u2.shape[0] // ROW_CHUNKS
    grid_spec = pltpu.PrefetchScalarGridSpec(
        num_scalar_prefetch=1,
        grid=(t // tm,),
        in_specs=[pl.BlockSpec((1, 1, 2 * tm), lambda i, te: (i, 0, 0), memory_space=pltpu.SMEM),
                  pl.BlockSpec((tm * ROW_CHUNKS, LANES), lambda i, te: (i, 0))],
        out_specs=pl.BlockSpec(memory_space=pl.ANY),
        scratch_shapes=[pltpu.VMEM((MOE_ROW_TILE * ROW_CHUNKS, LANES), jnp.float32),
                        pltpu.SemaphoreType.DMA(())],
    )
    return pl.pallas_call(
        _dispatch_kernel,
        grid_spec=grid_spec,
        out_shape=jax.ShapeDtypeStruct((n_rows * ROW_CHUNKS, LANES), jnp.float32),
        compiler_params=pltpu.CompilerParams(vmem_limit_bytes=VMEM_LIMIT,
                                             dimension_semantics=("arbitrary",)),
        name="dispatch",
    )(tile_end, pos, u2)


def _moe_grouped_kernel(tile_expert_ref, n_tiles_ref, u_ref, wg_ref, wu_ref, wd_ref, y_ref,
                        wg_bf16_ref, wu_bf16_ref, wd_bf16_ref):
    i = pl.program_id(0)
    bf16 = jnp.bfloat16
    valid = i < n_tiles_ref[0]

    @pl.when(valid & ((i == 0) | (tile_expert_ref[i] != tile_expert_ref[jnp.maximum(i - 1, 0)])))
    def _():
        wg_bf16_ref[...] = wg_ref[0].astype(bf16)
        wu_bf16_ref[...] = wu_ref[0].astype(bf16)
        wd_bf16_ref[...] = wd_ref[0].astype(bf16)

    @pl.when(valid)
    def _():
        u = _load_tile_rows(u_ref, 0, MOE_ROW_TILE).astype(bf16)
        gate = jnp.dot(u, wg_bf16_ref[...], preferred_element_type=jnp.float32)
        up = jnp.dot(u, wu_bf16_ref[...], preferred_element_type=jnp.float32)
        h = gate * jax.nn.sigmoid(gate) * up
        _store_tile_rows(y_ref, 0, jnp.dot(h.astype(bf16), wd_bf16_ref[...], preferred_element_type=jnp.float32))

    @pl.when(jnp.logical_not(valid))
    def _():
        y_ref[...] = jnp.zeros_like(y_ref)


def _moe_grouped(tile_expert, n_tiles, u_sorted, wg, wu, wd):
    d = D_MODEL
    tm = MOE_ROW_TILE
    max_tiles = tile_expert.shape[0]
    expert = lambda i, te, nt: (te[i], 0, 0)
    grid_spec = pltpu.PrefetchScalarGridSpec(
        num_scalar_prefetch=2,
        grid=(max_tiles,),
        in_specs=[pl.BlockSpec((tm * ROW_CHUNKS, LANES), lambda i, te, nt: (jnp.minimum(i, nt[0] - 1), 0)),
                  pl.BlockSpec((1, d, EXPERT_HIDDEN), expert),
                  pl.BlockSpec((1, d, EXPERT_HIDDEN), expert),
                  pl.BlockSpec((1, EXPERT_HIDDEN, d), expert)],
        out_specs=pl.BlockSpec((tm * ROW_CHUNKS, LANES), lambda i, te, nt: (i, 0)),
        scratch_shapes=[pltpu.VMEM((d, EXPERT_HIDDEN), jnp.bfloat16),
                        pltpu.VMEM((d, EXPERT_HIDDEN), jnp.bfloat16),
                        pltpu.VMEM((EXPERT_HIDDEN, d), jnp.bfloat16)],
    )
    return pl.pallas_call(
        _moe_grouped_kernel,
        grid_spec=grid_spec,
        out_shape=jax.ShapeDtypeStruct((max_tiles * tm * ROW_CHUNKS, LANES), jnp.float32),
        compiler_params=pltpu.CompilerParams(vmem_limit_bytes=VMEM_LIMIT,
                                             dimension_semantics=("arbitrary",)),
        name="moe_grouped",
    )(tile_expert, n_tiles, u_sorted, wg, wu, wd)


def _combine_ln2_kernel(pos_ref, pos_next_ref, y_hbm, route_ref, x1_ref, mod_ref, g2_ref, b2_ref, o_ref,
                        buf_ref, sem_ref):
    i = pl.program_id(0)
    n = pl.num_programs(0)
    tm = o_ref.shape[0]
    rows = 2 * tm
    stored = lambda r: pl.ds(pl.multiple_of(r * ROW_CHUNKS, ROW_CHUNKS), ROW_CHUNKS)

    def start_gather(rows_ref, slot):
        def body(r, _):
            pltpu.make_async_copy(y_hbm.at[stored(rows_ref[0, 0, r]), :], buf_ref.at[slot, stored(r), :],
                                  sem_ref.at[slot]).start()
            return 0
        lax.fori_loop(0, rows, body, 0, unroll=GATHER_UNROLL)

    @pl.when(i == 0)
    def _():
        start_gather(pos_ref, 0)

    @pl.when(i + 1 < n)
    def _():
        start_gather(pos_next_ref, (i + 1) % 2)

    slot = i % 2
    pltpu.make_async_copy(y_hbm.at[pl.ds(0, rows * ROW_CHUNKS), :], buf_ref.at[slot], sem_ref.at[slot]).wait()
    route = route_ref[...]
    w1 = route[:, ROUTE_W1:ROUTE_W1 + 1]
    w2 = route[:, ROUTE_W2:ROUTE_W2 + 1]
    gathered = buf_ref.at[slot]
    y = w1 * _load_tile_rows(gathered, 0, tm) + w2 * _load_tile_rows(gathered, tm * ROW_CHUNKS, tm)
    gate2 = mod_ref[0, 5:6, :]
    o_ref[...] = _ln(DEEPNORM_ALPHA * x1_ref[...] + (1.0 + gate2) * y) * g2_ref[...] + b2_ref[...]


def _combine_ln2(pos, y_sorted, route, x1, mod, ln2_g, ln2_b, seq):
    t, d = x1.shape
    tm = COMBINE_ROW_TILE
    n = t // tm
    tiles_per_seq = seq // tm
    row = lambda i: (i, 0)
    const2 = lambda i: (0, 0)
    return pl.pallas_call(
        _combine_ln2_kernel,
        grid=(n,),
        in_specs=[pl.BlockSpec((1, 1, 2 * tm), lambda i: (i, 0, 0), memory_space=pltpu.SMEM),
                  pl.BlockSpec((1, 1, 2 * tm), lambda i: (jnp.minimum(i + 1, n - 1), 0, 0),
                               memory_space=pltpu.SMEM),
                  pl.BlockSpec(memory_space=pl.ANY),
                  pl.BlockSpec((tm, LANES), row),
                  pl.BlockSpec((tm, d), row),
                  pl.BlockSpec((1, N_MOD, d), lambda i: (i // tiles_per_seq, 0, 0)),
                  pl.BlockSpec((1, d), const2),
                  pl.BlockSpec((1, d), const2)],
        out_specs=pl.BlockSpec((tm, d), row),
        out_shape=jax.ShapeDtypeStruct((t, d), jnp.float32),
        scratch_shapes=[pltpu.VMEM((2, 2 * tm * ROW_CHUNKS, LANES), jnp.float32),
                        pltpu.SemaphoreType.DMA((2,))],
        compiler_params=pltpu.CompilerParams(vmem_limit_bytes=VMEM_LIMIT,
                                             dimension_semantics=("arbitrary",)),
        name="combine_ln2",
    )(pos, pos, y_sorted, route, x1, mod, ln2_g, ln2_b)


def _dispatch_plan(route, counts):
    t = route.shape[0]
    tm = MOE_ROW_TILE
    max_tiles = 2 * t // tm + N_EXPERTS
    counts = counts[0, :N_EXPERTS].astype(jnp.int32)
    tiles = (counts + tm - 1) // tm
    tile_end = jnp.cumsum(tiles)
    row_start = (tile_end - tiles) * tm
    experts = route[:, ROUTE_E1:ROUTE_E2 + 1].astype(jnp.int32)
    ranks = route[:, ROUTE_RANK1:ROUTE_RANK2 + 1].astype(jnp.int32)
    is_expert = experts[:, :, None] == jnp.arange(N_EXPERTS, dtype=jnp.int32)
    pos = jnp.sum(jnp.where(is_expert, row_start, 0), axis=-1) + ranks
    tile_ids = jnp.arange(max_tiles, dtype=jnp.int32)
    tile_expert = jnp.minimum(jnp.sum(tile_end[None, :] <= tile_ids[:, None], axis=1), N_EXPERTS - 1)
    by_tile = lambda tc: pos.reshape(t // tc, tc, 2).transpose(0, 2, 1).reshape(t // tc, 1, 2 * tc)
    return (tile_expert.astype(jnp.int32), tile_end[-1:].astype(jnp.int32), tile_end.astype(jnp.int32),
            by_tile(DISPATCH_ROW_TILE), by_tile(COMBINE_ROW_TILE), max_tiles * tm)


def kernel(x, c, w_ada, b_ada, w_in, w_pool, pool_scale, w_out, ln1_g, ln1_b, w_router_group, b_router_group,
           w_router_expert, b_router_expert, w_gate, w_up, w_down, ln2_g, ln2_b):
    b, s, d = x.shape
    bf16 = jnp.bfloat16
    for layer in range(DEPTH):
        c_pad = jnp.pad(c, ((0, SUBLANES - b), (0, 0)))
        mod = _adaln(c_pad, w_ada[layer], b_ada[layer][None, :])[:b].reshape(b, N_MOD, d)
        x2d = x.reshape(b * s, d)
        qkv, p = _ln_inproj(x2d, mod, w_in[layer].astype(bf16), s)
        o_sb = _sb_attn(qkv, b, s).reshape(b * s, SB_WIDTH)
        pad = LANES - N_EXPERTS - N_EXPERT_GROUPS
        w_router = jnp.pad(jnp.concatenate([w_router_expert[layer], w_router_group[layer]], axis=1),
                           ((0, 0), (0, pad)))
        b_router = jnp.pad(jnp.concatenate([b_router_expert[layer], b_router_group[layer]]), (0, pad))[None, :]
        w_router_hi = w_router.astype(bf16)
        w_router_lo = (w_router - w_router_hi.astype(jnp.float32)).astype(bf16)
        w_router = jnp.concatenate([w_router_hi, w_router_hi, w_router_lo], axis=0)
        x1, u2, route, counts = _mix_ln1(o_sb, p, w_pool[layer].astype(bf16), pool_scale[layer][None, :],
                                         w_out[layer].astype(bf16), x2d, mod, ln1_g[layer][None, :],
                                         ln1_b[layer][None, :], w_router, b_router, s)
        tile_expert, n_tiles, tile_end, pos_dispatch, pos, n_rows = _dispatch_plan(route, counts)
        u_sorted = _dispatch(tile_end, pos_dispatch, u2, n_rows)
        y_sorted = _moe_grouped(tile_expert, n_tiles, u_sorted, w_gate[layer], w_up[layer], w_down[layer])
        x2 = _combine_ln2(pos, y_sorted, route, x1, mod, ln2_g[layer][None, :], ln2_b[layer][None, :], s)
        x = x2.reshape(b, s, d)
    return x
```

```python
import functools

import jax
import jax.numpy as jnp
from jax import lax
from jax.experimental import pallas as pl
from jax.experimental.pallas import tpu as pltpu

D_MODEL = 1024
N_SB_HEADS = 8
SB_HEAD_DIM = 64
SB_WIDTH = N_SB_HEADS * SB_HEAD_DIM
POOL_WINDOWS = (2, 4, 8, 16)
POOL_GROUP_DIM = 128
POOL_WIDTH = len(POOL_WINDOWS) * POOL_GROUP_DIM
N_EXPERT_GROUPS = 4
EXPERTS_PER_GROUP = 4
N_EXPERTS = N_EXPERT_GROUPS * EXPERTS_PER_GROUP
EXPERT_HIDDEN = 512
DEPTH = 1
DEEPNORM_ALPHA = (2.0 * DEPTH) ** 0.25
LN_EPS = 1e-5
N_MOD = 6

LANES = 128
SUBLANES = 8
HALO = max(POOL_WINDOWS)
VMEM_LIMIT = 56 * 1024 * 1024

ROW_TILE = 512
MIX_STREAMS = 2
ATTN_TILE = 256
ATTN_QUERY_BLOCKS = 2
STICK_GONE_LOG2 = -180.0
MOE_ROW_TILE = 512
DISPATCH_ROW_TILE = 1024
COMBINE_ROW_TILE = 256
GATHER_UNROLL = 8


ROW_CHUNKS = D_MODEL // LANES


def _store_tile_rows(ref, base, value):
    n = value.shape[0]
    for c in range(ROW_CHUNKS):
        ref[pl.ds(base + c, n, stride=ROW_CHUNKS), :] = value[:, c * LANES:(c + 1) * LANES]


def _load_tile_rows(ref, base, n):
    return jnp.concatenate([ref[pl.ds(base + c, n, stride=ROW_CHUNKS), :] for c in range(ROW_CHUNKS)], axis=1)


def _ln(x):
    mu = jnp.mean(x, axis=-1, keepdims=True)
    xc = x - mu
    var = jnp.mean(xc * xc, axis=-1, keepdims=True)
    return xc * lax.rsqrt(var + LN_EPS)


def _adaln_kernel(c_ref, w_ref, b_ref, o_ref):
    c = c_ref[...]
    a = c * jax.nn.sigmoid(c)
    o_ref[...] = jnp.dot(a, w_ref[...], preferred_element_type=jnp.float32,
                         precision=lax.Precision.HIGHEST) + b_ref[...]


def _adaln(c_pad, w_ada, b_ada):
    rows, d = c_pad.shape
    n = w_ada.shape[1]
    tn = 1024
    return pl.pallas_call(
        _adaln_kernel,
        grid=(n // tn,),
        in_specs=[pl.BlockSpec((rows, d), lambda j: (0, 0)),
                  pl.BlockSpec((d, tn), lambda j: (0, j)),
                  pl.BlockSpec((1, tn), lambda j: (0, j))],
        out_specs=pl.BlockSpec((rows, tn), lambda j: (0, j)),
        out_shape=jax.ShapeDtypeStruct((rows, n), jnp.float32),
        compiler_params=pltpu.CompilerParams(vmem_limit_bytes=VMEM_LIMIT),
        name="adaln",
    )(c_pad, w_ada, b_ada)


def _ln_inproj_kernel(x_ref, mod_ref, w_ref, qkv_ref, p_ref):
    shift = mod_ref[0, 0:1, :]
    scale = mod_ref[0, 1:2, :]
    u = (_ln(x_ref[...]) * (1.0 + scale) + shift).astype(jnp.bfloat16)
    qk_scale = SB_HEAD_DIM ** -0.5 * 1.4426950408889634
    q = jnp.dot(u, w_ref[:, 0:SB_WIDTH], preferred_element_type=jnp.float32) * qk_scale
    qkv_ref[:, 0:SB_WIDTH] = q.astype(jnp.bfloat16)
    for j in (1, 2):
        kv = jnp.dot(u, w_ref[:, j * SB_WIDTH:(j + 1) * SB_WIDTH], preferred_element_type=jnp.float32)
        qkv_ref[:, j * SB_WIDTH:(j + 1) * SB_WIDTH] = kv.astype(jnp.bfloat16)
    p_ref[...] = jnp.dot(u, w_ref[:, 3 * SB_WIDTH:], preferred_element_type=jnp.float32)


def _ln_inproj(x2d, mod, w_in_bf16, seq):
    t, d = x2d.shape
    tm = ROW_TILE
    tiles_per_seq = seq // tm
    return pl.pallas_call(
        _ln_inproj_kernel,
        grid=(t // tm,),
        in_specs=[pl.BlockSpec((tm, d), lambda i: (i, 0)),
                  pl.BlockSpec((1, N_MOD, d), lambda i: (i // tiles_per_seq, 0, 0)),
                  pl.BlockSpec(w_in_bf16.shape, lambda i: (0, 0))],
        out_specs=[pl.BlockSpec((tm, 3 * SB_WIDTH), lambda i: (i, 0)),
                   pl.BlockSpec((tm, POOL_WIDTH), lambda i: (i, 0))],
        out_shape=[jax.ShapeDtypeStruct((t, 3 * SB_WIDTH), jnp.bfloat16),
                   jax.ShapeDtypeStruct((t, POOL_WIDTH), jnp.float32)],
        compiler_params=pltpu.CompilerParams(vmem_limit_bytes=VMEM_LIMIT),
        name="ln_inproj",
    )(x2d, mod, w_in_bf16)


def _sb_attn_kernel(q_ref, k_ref, v_ref, o_ref, acc_ref, carry_ref):
    tq = tk = ATTN_TILE
    heads = (0, 1)
    subs = tuple(range(ATTN_QUERY_BLOCKS))
    qis = [pl.program_id(2) * ATTN_QUERY_BLOCKS + s for s in subs]
    lane = lax.broadcasted_iota(jnp.int32, (tq, LANES), 1)
    first_head = lane < SB_HEAD_DIM
    q_heads = {}
    for s in subs:
        q2 = q_ref[0, s * tq:(s + 1) * tq, :]
        zero = jnp.zeros_like(q2)
        q_heads[s, 0], q_heads[s, 1] = jnp.where(first_head, q2, zero), jnp.where(first_head, zero, q2)

    row = lax.broadcasted_iota(jnp.int32, (tk, tk), 0)
    col = lax.broadcasted_iota(jnp.int32, (tk, tk), 1)
    neg_suffix = jnp.where(row >= col, -1.0, 0.0).astype(jnp.bfloat16)
    causal = col < row

    def visit(blocks, carries):
        k_blks, v_blks = [], []
        for s, j, _ in blocks:
            start = pl.multiple_of(jnp.maximum(qis[s] - j, 0) * tk, tk)
            k_blks.append(k_ref[0, pl.ds(start, tk), :])
            v_blks.append(v_ref[0, pl.ds(start, tk), :])
        pairs = [(b, h) for b in range(len(blocks)) for h in heads]
        zs = {(b, h): lax.dot_general(q_heads[blocks[b][0], h], k_blks[b], (((1,), (1,)), ((), ())),
                                      preferred_element_type=jnp.float32) for b, h in pairs}
        fixed, rounded = {}, {}
        for b, h in pairs:
            z = zs[b, h]
            n = jnp.maximum(z, 0.0) + jnp.log2(1.0 + jnp.exp2(-jnp.abs(z)))
            if blocks[b][2]:
                n = jnp.where(causal, n, 0.0)
            n_bf16 = n.astype(jnp.bfloat16)
            rounded[b, h] = n_bf16
            fixed[b, h] = z - (n - n_bf16.astype(jnp.float32))
        incls = {bh: jnp.dot(rounded[bh], neg_suffix, preferred_element_type=jnp.float32) for bh in pairs}
        atts = {}
        carries = dict(carries)
        for b, h in pairs:
            s = blocks[b][0]
            carry = carries[s, h]
            block_sum = jnp.broadcast_to(incls[b, h][:, 0:1], (tq, LANES))
            expo = fixed[b, h] + incls[b, h]
            if carry is None:
                carries[s, h] = block_sum
            else:
                expo = expo + jnp.concatenate([carry] * (tk // LANES), axis=1)
                carries[s, h] = carry + block_sum
            att = jnp.exp2(expo)
            if blocks[b][2]:
                att = jnp.where(causal, att, 0.0)
            atts[b, h] = att.astype(jnp.bfloat16)
        outs = {(b, h): jnp.dot(atts[b, h], v_blks[b], preferred_element_type=jnp.float32) for b, h in pairs}
        return outs, carries

    head_blocks = [(s, 0, True) for s in subs] + [(s, 1, False) for s in subs]
    outs, carries = visit(head_blocks, {(s, h): None for s in subs for h in heads})
    for s in subs:
        has_second = qis[s] >= 1
        for h in heads:
            acc_ref[s, h] = outs[s, h] + jnp.where(has_second, outs[len(subs) + s, h], 0.0)
            carry_ref[s, h] = carries[s, h]

    for s in subs:
        def more(state, s=s):
            j, stick_gone = state
            return (j <= qis[s]) & jnp.logical_not(stick_gone)

        def visit_next(state, s=s):
            j, _ = state
            outs, new_carries = visit([(s, j, False)], {(s, h): carry_ref[s, h] for h in heads})
            for h in heads:
                acc_ref[s, h] += outs[0, h]
                carry_ref[s, h] = new_carries[s, h]
            return j + 1, jnp.max(carry_ref[s]) < STICK_GONE_LOG2

        lax.while_loop(more, visit_next, (jnp.int32(2), jnp.max(carry_ref[s]) < STICK_GONE_LOG2))
        o_ref[0, s * tq:(s + 1) * tq, :] = jnp.where(first_head, acc_ref[s, 0], acc_ref[s, 1]).astype(o_ref.dtype)


def _sb_attn(qkv, batch, seq):
    qkv3 = qkv.reshape(batch, seq, 3 * SB_WIDTH)
    tq = ATTN_TILE
    rows = ATTN_QUERY_BLOCKS * tq
    pairs = SB_WIDTH // LANES
    return pl.pallas_call(
        _sb_attn_kernel,
        grid=(batch, pairs, seq // rows),
        in_specs=[pl.BlockSpec((1, rows, LANES), lambda b, h, i: (b, i, h)),
                  pl.BlockSpec((1, seq, LANES), lambda b, h, i: (b, 0, pairs + h)),
                  pl.BlockSpec((1, seq, LANES), lambda b, h, i: (b, 0, 2 * pairs + h))],
        out_specs=pl.BlockSpec((1, rows, LANES), lambda b, h, i: (b, i, h)),
        out_shape=jax.ShapeDtypeStruct((batch, seq, SB_WIDTH), jnp.bfloat16),
        scratch_shapes=[pltpu.VMEM((ATTN_QUERY_BLOCKS, 2, tq, LANES), jnp.float32),
                        pltpu.VMEM((ATTN_QUERY_BLOCKS, 2, tq, LANES), jnp.float32)],
        compiler_params=pltpu.CompilerParams(vmem_limit_bytes=VMEM_LIMIT),
        name="sb_attn",
    )(qkv3, qkv3, qkv3)


def _route(logits):
    lane = lax.broadcasted_iota(jnp.int32, logits.shape, 1).astype(jnp.float32)
    neg = jnp.float32(-jnp.inf)
    big = jnp.float32(LANES)
    is_group = (lane >= N_EXPERTS) & (lane < N_EXPERTS + N_EXPERT_GROUPS)
    gl = jnp.where(is_group, logits, neg)
    g_max = jnp.max(gl, axis=-1, keepdims=True)
    g_sum = jnp.sum(jnp.exp(gl - g_max), axis=-1, keepdims=True)
    g_p = 1.0 / g_sum
    g_idx = jnp.min(jnp.where(gl == g_max, lane, big), axis=-1, keepdims=True) - N_EXPERTS
    in_group = (lane >= g_idx * EXPERTS_PER_GROUP) & (lane < (g_idx + 1) * EXPERTS_PER_GROUP)
    el = jnp.where(in_group, logits, neg)
    e1 = jnp.max(el, axis=-1, keepdims=True)
    i1 = jnp.min(jnp.where(el == e1, lane, big), axis=-1, keepdims=True)
    el2 = jnp.where(lane == i1, neg, el)
    e2 = jnp.max(el2, axis=-1, keepdims=True)
    i2 = jnp.min(jnp.where(el2 == e2, lane, big), axis=-1, keepdims=True)
    r = jnp.exp(e2 - e1)
    w1 = g_p / (1.0 + r)
    w2 = w1 * r
    return i1, i2, w1, w2


ROUTE_W1, ROUTE_W2, ROUTE_E1, ROUTE_E2, ROUTE_RANK1, ROUTE_RANK2 = range(6)


def _mix_ln1_kernel(steps_per_seq, osb_ref, p_ref, halo_ref, wpool_ref, pscale_ref, wout_ref, x_ref, mod_ref,
                    g1_ref, b1_ref, wr_ref, br_ref, earlier_ref, x1_ref, u2_ref, route_ref, count_ref, pext_ref,
                    running_ref):
    tm = ROW_TILE
    streams = range(MIX_STREAMS)
    rows = lambda s: slice(s * tm, (s + 1) * tm)
    step_in_seq = pl.program_id(0) % steps_per_seq
    pext_ref[0:HALO, :] = jnp.where(step_in_seq == 0, 0.0, halo_ref[...])
    pext_ref[HALO:, :] = p_ref[...]

    mixer_outs = []
    for s in streams:
        pos = (step_in_seq * MIX_STREAMS + s) * tm + lax.broadcasted_iota(jnp.int32, (tm, 1), 0)
        first_row = HALO + s * tm
        mixer_out = [osb_ref[rows(s), :]]
        for g, w in enumerate(POOL_WINDOWS):
            cols = slice(g * POOL_GROUP_DIM, (g + 1) * POOL_GROUP_DIM)
            own = p_ref[rows(s), cols]
            win = own
            for i in range(1, w):
                win = win + pext_ref[first_row - i:first_row - i + tm, cols]
            count = jnp.minimum(pos + 1, w).astype(jnp.float32)
            pooled = win / count - own
            o_pool = jnp.dot(pooled.astype(jnp.bfloat16), wpool_ref[g], preferred_element_type=jnp.float32)
            mixer_out.append((o_pool * pscale_ref[:, cols]).astype(jnp.bfloat16))
        mixer_outs.append(jnp.concatenate(mixer_out, axis=1))
    mixed = [jnp.dot(m, wout_ref[...], preferred_element_type=jnp.float32) for m in mixer_outs]

    gate1 = mod_ref[0, 2:3, :]
    shift2 = mod_ref[0, 3:4, :]
    scale2 = mod_ref[0, 4:5, :]
    router_in = []
    for s in streams:
        x1 = _ln(DEEPNORM_ALPHA * x_ref[rows(s), :] + (1.0 + gate1) * mixed[s]) * g1_ref[...] + b1_ref[...]
        x1_ref[rows(s), :] = x1
        u2 = _ln(x1) * (1.0 + scale2) + shift2
        _store_tile_rows(u2_ref, s * tm * ROW_CHUNKS, u2)
        u2_hi = u2.astype(jnp.bfloat16)
        u2_lo = (u2 - u2_hi.astype(jnp.float32)).astype(jnp.bfloat16)
        router_in.append(jnp.concatenate([u2_hi, u2_lo, u2_hi], axis=1))
    logits = [jnp.dot(r, wr_ref[...], preferred_element_type=jnp.float32) + br_ref[...] for r in router_in]
    routed = [_route(l) for l in logits]

    @pl.when(pl.program_id(0) == 0)
    def _():
        running_ref[...] = jnp.zeros_like(running_ref)

    lane = lax.broadcasted_iota(jnp.int32, (tm, LANES), 1)
    lane_f = lane.astype(jnp.float32)
    picked = [(lane_f == i1, lane_f == i2) for i1, i2, _, _ in routed]
    uses = [jnp.where(first | second, 1.0, 0.0) for first, second in picked]
    within = [jnp.dot(earlier_ref[...], u.astype(jnp.bfloat16), preferred_element_type=jnp.float32) for u in uses]
    running = running_ref[0:1, :]
    for s in streams:
        i1, i2, w1, w2 = routed[s]
        first, second = picked[s]
        before = within[s] + running
        rank1 = jnp.sum(jnp.where(first, before, 0.0), axis=-1, keepdims=True)
        rank2 = jnp.sum(jnp.where(second, before, 0.0), axis=-1, keepdims=True)
        running = running + jnp.sum(uses[s], axis=0, keepdims=True)
        record = jnp.zeros((tm, LANES), jnp.float32)
        for slot, value in ((ROUTE_W1, w1), (ROUTE_W2, w2), (ROUTE_E1, i1), (ROUTE_E2, i2),
                            (ROUTE_RANK1, rank1), (ROUTE_RANK2, rank2)):
            record = jnp.where(lane == slot, value, record)
        route_ref[rows(s), :] = record
    running_ref[...] = jnp.broadcast_to(running, running_ref.shape)
    count_ref[...] = jnp.broadcast_to(running, count_ref.shape)


def _mix_ln1(o_sb, p, w_pool_bf16, pool_scale, w_out_bf16, x2d, mod, ln1_g, ln1_b, w_router, b_router, seq):
    t, d = x2d.shape
    tm = MIX_STREAMS * ROW_TILE
    steps_per_seq = seq // tm
    halo_blocks_per_step = tm // HALO
    row = lambda i: (i, 0)
    const2 = lambda i: (0, 0)
    return pl.pallas_call(
        functools.partial(_mix_ln1_kernel, steps_per_seq),
        grid=(t // tm,),
        in_specs=[pl.BlockSpec((tm, SB_WIDTH), row),
                  pl.BlockSpec((tm, POOL_WIDTH), row),
                  pl.BlockSpec((HALO, POOL_WIDTH), lambda i: (jnp.maximum(i * halo_blocks_per_step - 1, 0), 0)),
                  pl.BlockSpec(w_pool_bf16.shape, lambda i: (0, 0, 0)),
                  pl.BlockSpec((1, POOL_WIDTH), const2),
                  pl.BlockSpec(w_out_bf16.shape, const2),
                  pl.BlockSpec((tm, d), row),
                  pl.BlockSpec((1, N_MOD, d), lambda i: (i // steps_per_seq, 0, 0)),
                  pl.BlockSpec((1, d), const2),
                  pl.BlockSpec((1, d), const2),
                  pl.BlockSpec(w_router.shape, const2),
                  pl.BlockSpec((1, LANES), const2),
                  pl.BlockSpec((ROW_TILE, ROW_TILE), const2)],
        out_specs=[pl.BlockSpec((tm, d), row),
                   pl.BlockSpec((tm * ROW_CHUNKS, LANES), row),
                   pl.BlockSpec((tm, LANES), row),
                   pl.BlockSpec((SUBLANES, LANES), const2)],
        out_shape=[jax.ShapeDtypeStruct((t, d), jnp.float32),
                   jax.ShapeDtypeStruct((t * ROW_CHUNKS, LANES), jnp.float32),
                   jax.ShapeDtypeStruct((t, LANES), jnp.float32),
                   jax.ShapeDtypeStruct((SUBLANES, LANES), jnp.float32)],
        scratch_shapes=[pltpu.VMEM((HALO + tm, POOL_WIDTH), jnp.float32),
                        pltpu.VMEM((SUBLANES, LANES), jnp.float32)],
        compiler_params=pltpu.CompilerParams(vmem_limit_bytes=VMEM_LIMIT,
                                             dimension_semantics=("arbitrary",)),
        name="mix_ln1",
    )(o_sb, p, p, w_pool_bf16, pool_scale, w_out_bf16, x2d, mod, ln1_g, ln1_b, w_router, b_router,
      jnp.tri(ROW_TILE, ROW_TILE, -1, dtype=jnp.bfloat16))


def _dispatch_kernel(tile_end_ref, pos_ref, u_ref, sorted_hbm, zeros_ref, sem):
    tm = u_ref.shape[0] // ROW_CHUNKS
    tile = MOE_ROW_TILE * ROW_CHUNKS
    stored = lambda r: pl.ds(pl.multiple_of(r * ROW_CHUNKS, ROW_CHUNKS), ROW_CHUNKS)

    @pl.when(pl.program_id(0) == 0)
    def _():
        zeros_ref[...] = jnp.zeros_like(zeros_ref)

        def tile_fill(i):
            return pltpu.make_async_copy(zeros_ref, sorted_hbm.at[pl.ds(pl.multiple_of(i * tile, tile), tile), :],
                                         sem)

        def last_tile_fills(act):
            for e in range(N_EXPERTS):
                first_tile = tile_end_ref[e - 1] if e else 0

                @pl.when(tile_end_ref[e] > first_tile)
                def _():
                    act(tile_fill(tile_end_ref[e] - 1))

        unused = (tile_end_ref[N_EXPERTS - 1], sorted_hbm.shape[0] // tile)
        last_tile_fills(lambda fill: fill.start())
        lax.fori_loop(*unused, lambda i, _: (tile_fill(i).start(), 0)[1], 0)
        last_tile_fills(lambda fill: fill.wait())
        lax.fori_loop(*unused, lambda i, _: (tile_fill(i).wait(), 0)[1], 0)

    def body(r, _):
        row = u_ref.at[stored(r), :]
        pltpu.make_async_copy(row, sorted_hbm.at[stored(pos_ref[0, 0, r]), :], sem).start()
        pltpu.make_async_copy(row, sorted_hbm.at[stored(pos_ref[0, 0, tm + r]), :], sem).start()
        return 0

    lax.fori_loop(0, tm, body, 0, unroll=GATHER_UNROLL // 2)
    for _ in range(2):
        pltpu.make_async_copy(u_ref, sorted_hbm.at[pl.ds(0, tm * ROW_CHUNKS), :], sem).wait()


def _dispatch(tile_end, pos, u2, n_rows):
    tm = pos.shape[2] // 2
    t = u2.shape[0] // ROW_CHUNKS
    grid_spec = pltpu.PrefetchScalarGridSpec(
        num_scalar_prefetch=1,
        grid=(t // tm,),
        in_specs=[pl.BlockSpec((1, 1, 2 * tm), lambda i, te: (i, 0, 0), memory_space=pltpu.SMEM),
                  pl.BlockSpec((tm * ROW_CHUNKS, LANES), lambda i, te: (i, 0))],
        out_specs=pl.BlockSpec(memory_space=pl.ANY),
        scratch_shapes=[pltpu.VMEM((MOE_ROW_TILE * ROW_CHUNKS, LANES), jnp.float32),
                        pltpu.SemaphoreType.DMA(())],
    )
    return pl.pallas_call(
        _dispatch_kernel,
        grid_spec=grid_spec,
        out_shape=jax.ShapeDtypeStruct((n_rows * ROW_CHUNKS, LANES), jnp.float32),
        compiler_params=pltpu.CompilerParams(vmem_limit_bytes=VMEM_LIMIT,
                                             dimension_semantics=("arbitrary",)),
        name="dispatch",
    )(tile_end, pos, u2)


def _moe_grouped_kernel(tile_expert_ref, n_tiles_ref, u_ref, wg_ref, wu_ref, wd_ref, y_ref,
                        wg_bf16_ref, wu_bf16_ref, wd_bf16_ref):
    i = pl.program_id(0)
    bf16 = jnp.bfloat16
    valid = i < n_tiles_ref[0]

    @pl.when(valid & ((i == 0) | (tile_expert_ref[i] != tile_expert_ref[jnp.maximum(i - 1, 0)])))
    def _():
        wg_bf16_ref[...] = wg_ref[0].astype(bf16)
        wu_bf16_ref[...] = wu_ref[0].astype(bf16)
        wd_bf16_ref[...] = wd_ref[0].astype(bf16)

    @pl.when(valid)
    def _():
        u = _load_tile_rows(u_ref, 0, MOE_ROW_TILE).astype(bf16)
        gate = jnp.dot(u, wg_bf16_ref[...], preferred_element_type=jnp.float32)
        up = jnp.dot(u, wu_bf16_ref[...], preferred_element_type=jnp.float32)
        h = gate * jax.nn.sigmoid(gate) * up
        _store_tile_rows(y_ref, 0, jnp.dot(h.astype(bf16), wd_bf16_ref[...], preferred_element_type=jnp.float32))

    @pl.when(jnp.logical_not(valid))
    def _():
        y_ref[...] = jnp.zeros_like(y_ref)


def _moe_grouped(tile_expert, n_tiles, u_sorted, wg, wu, wd):
    d = D_MODEL
    tm = MOE_ROW_TILE
    max_tiles = tile_expert.shape[0]
    expert = lambda i, te, nt: (te[i], 0, 0)
    grid_spec = pltpu.PrefetchScalarGridSpec(
        num_scalar_prefetch=2,
        grid=(max_tiles,),
        in_specs=[pl.BlockSpec((tm * ROW_CHUNKS, LANES), lambda i, te, nt: (jnp.minimum(i, nt[0] - 1), 0)),
                  pl.BlockSpec((1, d, EXPERT_HIDDEN), expert),
                  pl.BlockSpec((1, d, EXPERT_HIDDEN), expert),
                  pl.BlockSpec((1, EXPERT_HIDDEN, d), expert)],
        out_specs=pl.BlockSpec((tm * ROW_CHUNKS, LANES), lambda i, te, nt: (i, 0)),
        scratch_shapes=[pltpu.VMEM((d, EXPERT_HIDDEN), jnp.bfloat16),
                        pltpu.VMEM((d, EXPERT_HIDDEN), jnp.bfloat16),
                        pltpu.VMEM((EXPERT_HIDDEN, d), jnp.bfloat16)],
    )
    return pl.pallas_call(
        _moe_grouped_kernel,
        grid_spec=grid_spec,
        out_shape=jax.ShapeDtypeStruct((max_tiles * tm * ROW_CHUNKS, LANES), jnp.float32),
        compiler_params=pltpu.CompilerParams(vmem_limit_bytes=VMEM_LIMIT,
                                             dimension_semantics=("arbitrary",)),
        name="moe_grouped",
    )(tile_expert, n_tiles, u_sorted, wg, wu, wd)


def _combine_ln2_kernel(pos_ref, pos_next_ref, y_hbm, route_ref, x1_ref, mod_ref, g2_ref, b2_ref, o_ref,
                        buf_ref, sem_ref):
    i = pl.program_id(0)
    n = pl.num_programs(0)
    tm = o_ref.shape[0]
    rows = 2 * tm
    stored = lambda r: pl.ds(pl.multiple_of(r * ROW_CHUNKS, ROW_CHUNKS), ROW_CHUNKS)

    def start_gather(rows_ref, slot):
        def body(r, _):
            pltpu.make_async_copy(y_hbm.at[stored(rows_ref[0, 0, r]), :], buf_ref.at[slot, stored(r), :],
                                  sem_ref.at[slot]).start()
            return 0
        lax.fori_loop(0, rows, body, 0, unroll=GATHER_UNROLL)

    @pl.when(i == 0)
    def _():
        start_gather(pos_ref, 0)

    @pl.when(i + 1 < n)
    def _():
        start_gather(pos_next_ref, (i + 1) % 2)

    slot = i % 2
    pltpu.make_async_copy(y_hbm.at[pl.ds(0, rows * ROW_CHUNKS), :], buf_ref.at[slot], sem_ref.at[slot]).wait()
    route = route_ref[...]
    w1 = route[:, ROUTE_W1:ROUTE_W1 + 1]
    w2 = route[:, ROUTE_W2:ROUTE_W2 + 1]
    gathered = buf_ref.at[slot]
    y = w1 * _load_tile_rows(gathered, 0, tm) + w2 * _load_tile_rows(gathered, tm * ROW_CHUNKS, tm)
    gate2 = mod_ref[0, 5:6, :]
    o_ref[...] = _ln(DEEPNORM_ALPHA * x1_ref[...] + (1.0 + gate2) * y) * g2_ref[...] + b2_ref[...]


def _combine_ln2(pos, y_sorted, route, x1, mod, ln2_g, ln2_b, seq):
    t, d = x1.shape
    tm = COMBINE_ROW_TILE
    n = t // tm
    tiles_per_seq = seq // tm
    row = lambda i: (i, 0)
    const2 = lambda i: (0, 0)
    return pl.pallas_call(
        _combine_ln2_kernel,
        grid=(n,),
        in_specs=[pl.BlockSpec((1, 1, 2 * tm), lambda i: (i, 0, 0), memory_space=pltpu.SMEM),
                  pl.BlockSpec((1, 1, 2 * tm), lambda i: (jnp.minimum(i + 1, n - 1), 0, 0),
                               memory_space=pltpu.SMEM),
                  pl.BlockSpec(memory_space=pl.ANY),
                  pl.BlockSpec((tm, LANES), row),
                  pl.BlockSpec((tm, d), row),
                  pl.BlockSpec((1, N_MOD, d), lambda i: (i // tiles_per_seq, 0, 0)),
                  pl.BlockSpec((1, d), const2),
                  pl.BlockSpec((1, d), const2)],
        out_specs=pl.BlockSpec((tm, d), row),
        out_shape=jax.ShapeDtypeStruct((t, d), jnp.float32),
        scratch_shapes=[pltpu.VMEM((2, 2 * tm * ROW_CHUNKS, LANES), jnp.float32),
                        pltpu.SemaphoreType.DMA((2,))],
        compiler_params=pltpu.CompilerParams(vmem_limit_bytes=VMEM_LIMIT,
                                             dimension_semantics=("arbitrary",)),
        name="combine_ln2",
    )(pos, pos, y_sorted, route, x1, mod, ln2_g, ln2_b)


def _dispatch_plan(route, counts):
    t = route.shape[0]
    tm = MOE_ROW_TILE
    max_tiles = 2 * t // tm + N_EXPERTS
    counts = counts[0, :N_EXPERTS].astype(jnp.int32)
    tiles = (counts + tm - 1) // tm
    tile_end = jnp.cumsum(tiles)
    row_start = (tile_end - tiles) * tm
    experts = route[:, ROUTE_E1:ROUTE_E2 + 1].astype(jnp.int32)
    ranks = route[:, ROUTE_RANK1:ROUTE_RANK2 + 1].astype(jnp.int32)
    is_expert = experts[:, :, None] == jnp.arange(N_EXPERTS, dtype=jnp.int32)
    pos = jnp.sum(jnp.where(is_expert, row_start, 0), axis=-1) + ranks
    tile_ids = jnp.arange(max_tiles, dtype=jnp.int32)
    tile_expert = jnp.minimum(jnp.sum(tile_end[None, :] <= tile_ids[:, None], axis=1), N_EXPERTS - 1)
    by_tile = lambda tc: pos.reshape(t // tc, tc, 2).transpose(0, 2, 1).reshape(t // tc, 1, 2 * tc)
    return (tile_expert.astype(jnp.int32), tile_end[-1:].astype(jnp.int32), tile_end.astype(jnp.int32),
            by_tile(DISPATCH_ROW_TILE), by_tile(COMBINE_ROW_TILE), max_tiles * tm)


def kernel(x, c, w_ada, b_ada, w_in, w_pool, pool_scale, w_out, ln1_g, ln1_b, w_router_group, b_router_group,
           w_router_expert, b_router_expert, w_gate, w_up, w_down, ln2_g, ln2_b):
    b, s, d = x.shape
    bf16 = jnp.bfloat16
    for layer in range(DEPTH):
        c_pad = jnp.pad(c, ((0, SUBLANES - b), (0, 0)))
        mod = _adaln(c_pad, w_ada[layer], b_ada[layer][None, :])[:b].reshape(b, N_MOD, d)
        x2d = x.reshape(b * s, d)
        qkv, p = _ln_inproj(x2d, mod, w_in[layer].astype(bf16), s)
        o_sb = _sb_attn(qkv, b, s).reshape(b * s, SB_WIDTH)
        pad = LANES - N_EXPERTS - N_EXPERT_GROUPS
        w_router = jnp.pad(jnp.concatenate([w_router_expert[layer], w_router_group[layer]], axis=1),
                           ((0, 0), (0, pad)))
        b_router = jnp.pad(jnp.concatenate([b_router_expert[layer], b_router_group[layer]]), (0, pad))[None, :]
        w_router_hi = w_router.astype(bf16)
        w_router_lo = (w_router - w_router_hi.astype(jnp.float32)).astype(bf16)
        w_router = jnp.concatenate([w_router_hi, w_router_hi, w_router_lo], axis=0)
        x1, u2, route, counts = _mix_ln1(o_sb, p, w_pool[layer].astype(bf16), pool_scale[layer][None, :],
                                         w_out[layer].astype(bf16), x2d, mod, ln1_g[layer][None, :],
                                         ln1_b[layer][None, :], w_router, b_router, s)
        tile_expert, n_tiles, tile_end, pos_dispatch, pos, n_rows = _dispatch_plan(route, counts)
        u_sorted = _dispatch(tile_end, pos_dispatch, u2, n_rows)
        y_sorted = _moe_grouped(tile_expert, n_tiles, u_sorted, w_gate[layer], w_up[layer], w_down[layer])
        x2 = _combine_ln2(pos, y_sorted, route, x1, mod, ln2_g[layer][None, :], ln2_b[layer][None, :], s)
        x = x2.reshape(b, s, d)
    return x
```

```python
import functools

import jax
import jax.numpy as jnp
from jax import lax
from jax.experimental import pallas as pl
from jax.experimental.pallas import tpu as pltpu

D_MODEL = 1024
N_SB_HEADS = 8
SB_HEAD_DIM = 64
SB_WIDTH = N_SB_HEADS * SB_HEAD_DIM
POOL_WINDOWS = (2, 4, 8, 16)
POOL_GROUP_DIM = 128
POOL_WIDTH = len(POOL_WINDOWS) * POOL_GROUP_DIM
N_EXPERT_GROUPS = 4
EXPERTS_PER_GROUP = 4
N_EXPERTS = N_EXPERT_GROUPS * EXPERTS_PER_GROUP
EXPERT_HIDDEN = 512
DEPTH = 1
DEEPNORM_ALPHA = (2.0 * DEPTH) ** 0.25
LN_EPS = 1e-5
N_MOD = 6

LANES = 128
SUBLANES = 8
HALO = max(POOL_WINDOWS)
VMEM_LIMIT = 56 * 1024 * 1024

ROW_TILE = 512
MIX_STREAMS = 2
ATTN_TILE = 256
ATTN_QUERY_BLOCKS = 2
STICK_GONE_LOG2 = -180.0
MOE_ROW_TILE = 512
DISPATCH_ROW_TILE = 1024
COMBINE_ROW_TILE = 256
COMBINE_SLOTS = 3
GATHER_UNROLL = 8


ROW_CHUNKS = D_MODEL // LANES


def _store_tile_rows(ref, base, value):
    n = value.shape[0]
    for c in range(ROW_CHUNKS):
        ref[pl.ds(base + c, n, stride=ROW_CHUNKS), :] = value[:, c * LANES:(c + 1) * LANES]


def _load_tile_rows(ref, base, n):
    return jnp.concatenate([ref[pl.ds(base + c, n, stride=ROW_CHUNKS), :] for c in range(ROW_CHUNKS)], axis=1)


def _ln(x):
    mu = jnp.mean(x, axis=-1, keepdims=True)
    xc = x - mu
    var = jnp.mean(xc * xc, axis=-1, keepdims=True)
    return xc * lax.rsqrt(var + LN_EPS)


def _adaln_kernel(c_ref, w_ref, b_ref, o_ref):
    c = c_ref[...]
    a = c * jax.nn.sigmoid(c)
    o_ref[...] = jnp.dot(a, w_ref[...], preferred_element_type=jnp.float32,
                         precision=lax.Precision.HIGHEST) + b_ref[...]


def _adaln(c_pad, w_ada, b_ada):
    rows, d = c_pad.shape
    n = w_ada.shape[1]
    tn = 1024
    return pl.pallas_call(
        _adaln_kernel,
        grid=(n // tn,),
        in_specs=[pl.BlockSpec((rows, d), lambda j: (0, 0)),
                  pl.BlockSpec((d, tn), lambda j: (0, j)),
                  pl.BlockSpec((1, tn), lambda j: (0, j))],
        out_specs=pl.BlockSpec((rows, tn), lambda j: (0, j)),
        out_shape=jax.ShapeDtypeStruct((rows, n), jnp.float32),
        compiler_params=pltpu.CompilerParams(vmem_limit_bytes=VMEM_LIMIT),
        name="adaln",
    )(c_pad, w_ada, b_ada)


def _ln_inproj_kernel(x_ref, mod_ref, w_ref, qkv_ref, p_ref):
    shift = mod_ref[0, 0:1, :]
    scale = mod_ref[0, 1:2, :]
    u = (_ln(x_ref[...]) * (1.0 + scale) + shift).astype(jnp.bfloat16)
    qk_scale = SB_HEAD_DIM ** -0.5 * 1.4426950408889634
    q = jnp.dot(u, w_ref[:, 0:SB_WIDTH], preferred_element_type=jnp.float32) * qk_scale
    qkv_ref[:, 0:SB_WIDTH] = q.astype(jnp.bfloat16)
    for j in (1, 2):
        kv = jnp.dot(u, w_ref[:, j * SB_WIDTH:(j + 1) * SB_WIDTH], preferred_element_type=jnp.float32)
        qkv_ref[:, j * SB_WIDTH:(j + 1) * SB_WIDTH] = kv.astype(jnp.bfloat16)
    p_ref[...] = jnp.dot(u, w_ref[:, 3 * SB_WIDTH:], preferred_element_type=jnp.float32)


def _ln_inproj(x2d, mod, w_in_bf16, seq):
    t, d = x2d.shape
    tm = ROW_TILE
    tiles_per_seq = seq // tm
    return pl.pallas_call(
        _ln_inproj_kernel,
        grid=(t // tm,),
        in_specs=[pl.BlockSpec((tm, d), lambda i: (i, 0)),
                  pl.BlockSpec((1, N_MOD, d), lambda i: (i // tiles_per_seq, 0, 0)),
                  pl.BlockSpec(w_in_bf16.shape, lambda i: (0, 0))],
        out_specs=[pl.BlockSpec((tm, 3 * SB_WIDTH), lambda i: (i, 0)),
                   pl.BlockSpec((tm, POOL_WIDTH), lambda i: (i, 0))],
        out_shape=[jax.ShapeDtypeStruct((t, 3 * SB_WIDTH), jnp.bfloat16),
                   jax.ShapeDtypeStruct((t, POOL_WIDTH), jnp.float32)],
        compiler_params=pltpu.CompilerParams(vmem_limit_bytes=VMEM_LIMIT),
        name="ln_inproj",
    )(x2d, mod, w_in_bf16)


def _sb_attn_kernel(q_ref, k_ref, v_ref, o_ref, acc_ref, carry_ref):
    tq = tk = ATTN_TILE
    heads = (0, 1)
    subs = tuple(range(ATTN_QUERY_BLOCKS))
    qis = [pl.program_id(2) * ATTN_QUERY_BLOCKS + s for s in subs]
    lane = lax.broadcasted_iota(jnp.int32, (tq, LANES), 1)
    first_head = lane < SB_HEAD_DIM
    q_heads = {}
    for s in subs:
        q2 = q_ref[0, s * tq:(s + 1) * tq, :]
        zero = jnp.zeros_like(q2)
        q_heads[s, 0], q_heads[s, 1] = jnp.where(first_head, q2, zero), jnp.where(first_head, zero, q2)

    row = lax.broadcasted_iota(jnp.int32, (tk, tk), 0)
    col = lax.broadcasted_iota(jnp.int32, (tk, tk), 1)
    neg_suffix = jnp.where(row >= col, -1.0, 0.0).astype(jnp.bfloat16)
    causal = col < row

    def visit(blocks, carries):
        k_blks, v_blks = [], []
        for s, j, _ in blocks:
            start = pl.multiple_of(jnp.maximum(qis[s] - j, 0) * tk, tk)
            k_blks.append(k_ref[0, pl.ds(start, tk), :])
            v_blks.append(v_ref[0, pl.ds(start, tk), :])
        pairs = [(b, h) for b in range(len(blocks)) for h in heads]
        zs = {(b, h): lax.dot_general(q_heads[blocks[b][0], h], k_blks[b], (((1,), (1,)), ((), ())),
                                      preferred_element_type=jnp.float32) for b, h in pairs}
        fixed, rounded = {}, {}
        for b, h in pairs:
            z = zs[b, h]
            n = jnp.maximum(z, 0.0) + jnp.log2(1.0 + jnp.exp2(-jnp.abs(z)))
            if blocks[b][2]:
                n = jnp.where(causal, n, 0.0)
            n_bf16 = n.astype(jnp.bfloat16)
            rounded[b, h] = n_bf16
            fixed[b, h] = z - (n - n_bf16.astype(jnp.float32))
        incls = {bh: jnp.dot(rounded[bh], neg_suffix, preferred_element_type=jnp.float32) for bh in pairs}
        atts = {}
        carries = dict(carries)
        for b, h in pairs:
            s = blocks[b][0]
            carry = carries[s, h]
            block_sum = jnp.broadcast_to(incls[b, h][:, 0:1], (tq, LANES))
            expo = fixed[b, h] + incls[b, h]
            if carry is None:
                carries[s, h] = block_sum
            else:
                expo = expo + jnp.concatenate([carry] * (tk // LANES), axis=1)
                carries[s, h] = carry + block_sum
            att = jnp.exp2(expo)
            if blocks[b][2]:
                att = jnp.where(causal, att, 0.0)
            atts[b, h] = att.astype(jnp.bfloat16)
        outs = {(b, h): jnp.dot(atts[b, h], v_blks[b], preferred_element_type=jnp.float32) for b, h in pairs}
        return outs, carries

    head_blocks = [(s, 0, True) for s in subs] + [(s, 1, False) for s in subs]
    outs, carries = visit(head_blocks, {(s, h): None for s in subs for h in heads})
    for s in subs:
        has_second = qis[s] >= 1
        for h in heads:
            acc_ref[s, h] = outs[s, h] + jnp.where(has_second, outs[len(subs) + s, h], 0.0)
            carry_ref[s, h] = carries[s, h]

    for s in subs:
        def more(state, s=s):
            j, stick_gone = state
            return (j <= qis[s]) & jnp.logical_not(stick_gone)

        def visit_next(state, s=s):
            j, _ = state
            outs, new_carries = visit([(s, j, False)], {(s, h): carry_ref[s, h] for h in heads})
            for h in heads:
                acc_ref[s, h] += outs[0, h]
                carry_ref[s, h] = new_carries[s, h]
            return j + 1, jnp.max(carry_ref[s]) < STICK_GONE_LOG2

        lax.while_loop(more, visit_next, (jnp.int32(2), jnp.max(carry_ref[s]) < STICK_GONE_LOG2))
        o_ref[0, s * tq:(s + 1) * tq, :] = jnp.where(first_head, acc_ref[s, 0], acc_ref[s, 1]).astype(o_ref.dtype)


def _sb_attn(qkv, batch, seq):
    qkv3 = qkv.reshape(batch, seq, 3 * SB_WIDTH)
    tq = ATTN_TILE
    rows = ATTN_QUERY_BLOCKS * tq
    pairs = SB_WIDTH // LANES
    return pl.pallas_call(
        _sb_attn_kernel,
        grid=(batch, pairs, seq // rows),
        in_specs=[pl.BlockSpec((1, rows, LANES), lambda b, h, i: (b, i, h)),
                  pl.BlockSpec((1, seq, LANES), lambda b, h, i: (b, 0, pairs + h)),
                  pl.BlockSpec((1, seq, LANES), lambda b, h, i: (b, 0, 2 * pairs + h))],
        out_specs=pl.BlockSpec((1, rows, LANES), lambda b, h, i: (b, i, h)),
        out_shape=jax.ShapeDtypeStruct((batch, seq, SB_WIDTH), jnp.bfloat16),
        scratch_shapes=[pltpu.VMEM((ATTN_QUERY_BLOCKS, 2, tq, LANES), jnp.float32),
                        pltpu.VMEM((ATTN_QUERY_BLOCKS, 2, tq, LANES), jnp.float32)],
        compiler_params=pltpu.CompilerParams(vmem_limit_bytes=VMEM_LIMIT),
        name="sb_attn",
    )(qkv3, qkv3, qkv3)


def _route(logits):
    lane = lax.broadcasted_iota(jnp.int32, logits.shape, 1).astype(jnp.float32)
    neg = jnp.float32(-jnp.inf)
    big = jnp.float32(LANES)
    is_group = (lane >= N_EXPERTS) & (lane < N_EXPERTS + N_EXPERT_GROUPS)
    gl = jnp.where(is_group, logits, neg)
    g_max = jnp.max(gl, axis=-1, keepdims=True)
    g_sum = jnp.sum(jnp.exp(gl - g_max), axis=-1, keepdims=True)
    g_p = 1.0 / g_sum
    g_idx = jnp.min(jnp.where(gl == g_max, lane, big), axis=-1, keepdims=True) - N_EXPERTS
    in_group = (lane >= g_idx * EXPERTS_PER_GROUP) & (lane < (g_idx + 1) * EXPERTS_PER_GROUP)
    el = jnp.where(in_group, logits, neg)
    e1 = jnp.max(el, axis=-1, keepdims=True)
    i1 = jnp.min(jnp.where(el == e1, lane, big), axis=-1, keepdims=True)
    el2 = jnp.where(lane == i1, neg, el)
    e2 = jnp.max(el2, axis=-1, keepdims=True)
    i2 = jnp.min(jnp.where(el2 == e2, lane, big), axis=-1, keepdims=True)
    r = jnp.exp(e2 - e1)
    w1 = g_p / (1.0 + r)
    w2 = w1 * r
    return i1, i2, w1, w2


ROUTE_W1, ROUTE_W2, ROUTE_E1, ROUTE_E2, ROUTE_RANK1, ROUTE_RANK2 = range(6)


def _mix_ln1_kernel(steps_per_seq, osb_ref, p_ref, halo_ref, wpool_ref, pscale_ref, wout_ref, x_ref, mod_ref,
                    g1_ref, b1_ref, wr_ref, br_ref, earlier_ref, x1_ref, u2_ref, route_ref, count_ref, pext_ref,
                    running_ref):
    tm = ROW_TILE
    streams = range(MIX_STREAMS)
    rows = lambda s: slice(s * tm, (s + 1) * tm)
    step_in_seq = pl.program_id(0) % steps_per_seq
    pext_ref[0:HALO, :] = jnp.where(step_in_seq == 0, 0.0, halo_ref[...])
    pext_ref[HALO:, :] = p_ref[...]

    mixer_outs = []
    for s in streams:
        pos = (step_in_seq * MIX_STREAMS + s) * tm + lax.broadcasted_iota(jnp.int32, (tm, 1), 0)
        first_row = HALO + s * tm
        mixer_out = [osb_ref[rows(s), :]]
        for g, w in enumerate(POOL_WINDOWS):
            cols = slice(g * POOL_GROUP_DIM, (g + 1) * POOL_GROUP_DIM)
            own = p_ref[rows(s), cols]
            win = own
            for i in range(1, w):
                win = win + pext_ref[first_row - i:first_row - i + tm, cols]
            count = jnp.minimum(pos + 1, w).astype(jnp.float32)
            pooled = win / count - own
            o_pool = jnp.dot(pooled.astype(jnp.bfloat16), wpool_ref[g], preferred_element_type=jnp.float32)
            mixer_out.append((o_pool * pscale_ref[:, cols]).astype(jnp.bfloat16))
        mixer_outs.append(jnp.concatenate(mixer_out, axis=1))
    mixed = [jnp.dot(m, wout_ref[...], preferred_element_type=jnp.float32) for m in mixer_outs]

    gate1 = mod_ref[0, 2:3, :]
    shift2 = mod_ref[0, 3:4, :]
    scale2 = mod_ref[0, 4:5, :]
    router_in = []
    for s in streams:
        x1 = _ln(DEEPNORM_ALPHA * x_ref[rows(s), :] + (1.0 + gate1) * mixed[s]) * g1_ref[...] + b1_ref[...]
        x1_ref[rows(s), :] = x1
        u2 = _ln(x1) * (1.0 + scale2) + shift2
        _store_tile_rows(u2_ref, s * tm * ROW_CHUNKS, u2)
        u2_hi = u2.astype(jnp.bfloat16)
        u2_lo = (u2 - u2_hi.astype(jnp.float32)).astype(jnp.bfloat16)
        router_in.append(jnp.concatenate([u2_hi, u2_lo, u2_hi], axis=1))
    logits = [jnp.dot(r, wr_ref[...], preferred_element_type=jnp.float32) + br_ref[...] for r in router_in]
    routed = [_route(l) for l in logits]

    @pl.when(pl.program_id(0) == 0)
    def _():
        running_ref[...] = jnp.zeros_like(running_ref)

    lane = lax.broadcasted_iota(jnp.int32, (tm, LANES), 1)
    lane_f = lane.astype(jnp.float32)
    picked = [(lane_f == i1, lane_f == i2) for i1, i2, _, _ in routed]
    uses = [jnp.where(first | second, 1.0, 0.0) for first, second in picked]
    within = [jnp.dot(earlier_ref[...], u.astype(jnp.bfloat16), preferred_element_type=jnp.float32) for u in uses]
    running = running_ref[0:1, :]
    for s in streams:
        i1, i2, w1, w2 = routed[s]
        first, second = picked[s]
        before = within[s] + running
        rank1 = jnp.sum(jnp.where(first, before, 0.0), axis=-1, keepdims=True)
        rank2 = jnp.sum(jnp.where(second, before, 0.0), axis=-1, keepdims=True)
        running = running + jnp.sum(uses[s], axis=0, keepdims=True)
        record = jnp.zeros((tm, LANES), jnp.float32)
        for slot, value in ((ROUTE_W1, w1), (ROUTE_W2, w2), (ROUTE_E1, i1), (ROUTE_E2, i2),
                            (ROUTE_RANK1, rank1), (ROUTE_RANK2, rank2)):
            record = jnp.where(lane == slot, value, record)
        route_ref[rows(s), :] = record
    running_ref[...] = jnp.broadcast_to(running, running_ref.shape)
    count_ref[...] = jnp.broadcast_to(running, count_ref.shape)


def _mix_ln1(o_sb, p, w_pool_bf16, pool_scale, w_out_bf16, x2d, mod, ln1_g, ln1_b, w_router, b_router, seq):
    t, d = x2d.shape
    tm = MIX_STREAMS * ROW_TILE
    steps_per_seq = seq // tm
    halo_blocks_per_step = tm // HALO
    row = lambda i: (i, 0)
    const2 = lambda i: (0, 0)
    return pl.pallas_call(
        functools.partial(_mix_ln1_kernel, steps_per_seq),
        grid=(t // tm,),
        in_specs=[pl.BlockSpec((tm, SB_WIDTH), row),
                  pl.BlockSpec((tm, POOL_WIDTH), row),
                  pl.BlockSpec((HALO, POOL_WIDTH), lambda i: (jnp.maximum(i * halo_blocks_per_step - 1, 0), 0)),
                  pl.BlockSpec(w_pool_bf16.shape, lambda i: (0, 0, 0)),
                  pl.BlockSpec((1, POOL_WIDTH), const2),
                  pl.BlockSpec(w_out_bf16.shape, const2),
                  pl.BlockSpec((tm, d), row),
                  pl.BlockSpec((1, N_MOD, d), lambda i: (i // steps_per_seq, 0, 0)),
                  pl.BlockSpec((1, d), const2),
                  pl.BlockSpec((1, d), const2),
                  pl.BlockSpec(w_router.shape, const2),
                  pl.BlockSpec((1, LANES), const2),
                  pl.BlockSpec((ROW_TILE, ROW_TILE), const2)],
        out_specs=[pl.BlockSpec((tm, d), row),
                   pl.BlockSpec((tm * ROW_CHUNKS, LANES), row),
                   pl.BlockSpec((tm, LANES), row),
                   pl.BlockSpec((SUBLANES, LANES), const2)],
        out_shape=[jax.ShapeDtypeStruct((t, d), jnp.float32),
                   jax.ShapeDtypeStruct((t * ROW_CHUNKS, LANES), jnp.float32),
                   jax.ShapeDtypeStruct((t, LANES), jnp.float32),
                   jax.ShapeDtypeStruct((SUBLANES, LANES), jnp.float32)],
        scratch_shapes=[pltpu.VMEM((HALO + tm, POOL_WIDTH), jnp.float32),
                        pltpu.VMEM((SUBLANES, LANES), jnp.float32)],
        compiler_params=pltpu.CompilerParams(vmem_limit_bytes=VMEM_LIMIT,
                                             dimension_semantics=("arbitrary",)),
        name="mix_ln1",
    )(o_sb, p, p, w_pool_bf16, pool_scale, w_out_bf16, x2d, mod, ln1_g, ln1_b, w_router, b_router,
      jnp.tri(ROW_TILE, ROW_TILE, -1, dtype=jnp.bfloat16))


def _dispatch_kernel(tile_end_ref, pos_ref, u_ref, sorted_hbm, zeros_ref, sem):
    tm = u_ref.shape[0] // ROW_CHUNKS
    tile = MOE_ROW_TILE * ROW_CHUNKS
    stored = lambda r: pl.ds(pl.multiple_of(r * ROW_CHUNKS, ROW_CHUNKS), ROW_CHUNKS)

    @pl.when(pl.program_id(0) == 0)
    def _():
        zeros_ref[...] = jnp.zeros_like(zeros_ref)

        def tile_fill(i):
            return pltpu.make_async_copy(zeros_ref, sorted_hbm.at[pl.ds(pl.multiple_of(i * tile, tile), tile), :],
                                         sem)

        def last_tile_fills(act):
            for e in range(N_EXPERTS):
                first_tile = tile_end_ref[e - 1] if e else 0

                @pl.when(tile_end_ref[e] > first_tile)
                def _():
                    act(tile_fill(tile_end_ref[e] - 1))

        unused = (tile_end_ref[N_EXPERTS - 1], sorted_hbm.shape[0] // tile)
        last_tile_fills(lambda fill: fill.start())
        lax.fori_loop(*unused, lambda i, _: (tile_fill(i).start(), 0)[1], 0)
        last_tile_fills(lambda fill: fill.wait())
        lax.fori_loop(*unused, lambda i, _: (tile_fill(i).wait(), 0)[1], 0)

    def body(r, _):
        row = u_ref.at[stored(r), :]
        pltpu.make_async_copy(row, sorted_hbm.at[stored(pos_ref[0, 0, r]), :], sem).start()
        pltpu.make_async_copy(row, sorted_hbm.at[stored(pos_ref[0, 0, tm + r]), :], sem).start()
        return 0

    lax.fori_loop(0, tm, body, 0, unroll=GATHER_UNROLL // 2)
    for _ in range(2):
        pltpu.make_async_copy(u_ref, sorted_hbm.at[pl.ds(0, tm * ROW_CHUNKS), :], sem).wait()


def _dispatch(tile_end, pos, u2, n_rows):
    tm = pos.shape[2] // 2
    t = u2.shape[0] // ROW_CHUNKS
    grid_spec = pltpu.PrefetchScalarGridSpec(
        num_scalar_prefetch=1,
        grid=(t // tm,),
        in_specs=[pl.BlockSpec((1, 1, 2 * tm), lambda i, te: (i, 0, 0), memory_space=pltpu.SMEM),
                  pl.BlockSpec((tm * ROW_CHUNKS, LANES), lambda i, te: (i, 0))],
        out_specs=pl.BlockSpec(memory_space=pl.ANY),
        scratch_shapes=[pltpu.VMEM((MOE_ROW_TILE * ROW_CHUNKS, LANES), jnp.float32),
                        pltpu.SemaphoreType.DMA(())],
    )
    return pl.pallas_call(
        _dispatch_kernel,
        grid_spec=grid_spec,
        out_shape=jax.ShapeDtypeStruct((n_rows * ROW_CHUNKS, LANES), jnp.float32),
        compiler_params=pltpu.CompilerParams(vmem_limit_bytes=VMEM_LIMIT,
                                             dimension_semantics=("arbitrary",)),
        name="dispatch",
    )(tile_end, pos, u2)


def _moe_grouped_kernel(tile_expert_ref, n_tiles_ref, u_ref, wg_ref, wu_ref, wd_ref, y_ref,
                        wg_bf16_ref, wu_bf16_ref, wd_bf16_ref):
    i = pl.program_id(0)
    bf16 = jnp.bfloat16
    valid = i < n_tiles_ref[0]

    @pl.when(valid & ((i == 0) | (tile_expert_ref[i] != tile_expert_ref[jnp.maximum(i - 1, 0)])))
    def _():
        wg_bf16_ref[...] = wg_ref[0].astype(bf16)
        wu_bf16_ref[...] = wu_ref[0].astype(bf16)
        wd_bf16_ref[...] = wd_ref[0].astype(bf16)

    @pl.when(valid)
    def _():
        u = _load_tile_rows(u_ref, 0, MOE_ROW_TILE).astype(bf16)
        gate = jnp.dot(u, wg_bf16_ref[...], preferred_element_type=jnp.float32)
        up = jnp.dot(u, wu_bf16_ref[...], preferred_element_type=jnp.float32)
        h = gate * jax.nn.sigmoid(gate) * up
        _store_tile_rows(y_ref, 0, jnp.dot(h.astype(bf16), wd_bf16_ref[...], preferred_element_type=jnp.float32))

    @pl.when(jnp.logical_not(valid))
    def _():
        y_ref[...] = jnp.zeros_like(y_ref)


def _moe_grouped(tile_expert, n_tiles, u_sorted, wg, wu, wd):
    d = D_MODEL
    tm = MOE_ROW_TILE
    max_tiles = tile_expert.shape[0]
    expert = lambda i, te, nt: (te[i], 0, 0)
    grid_spec = pltpu.PrefetchScalarGridSpec(
        num_scalar_prefetch=2,
        grid=(max_tiles,),
        in_specs=[pl.BlockSpec((tm * ROW_CHUNKS, LANES), lambda i, te, nt: (jnp.minimum(i, nt[0] - 1), 0)),
                  pl.BlockSpec((1, d, EXPERT_HIDDEN), expert),
                  pl.BlockSpec((1, d, EXPERT_HIDDEN), expert),
                  pl.BlockSpec((1, EXPERT_HIDDEN, d), expert)],
        out_specs=pl.BlockSpec((tm * ROW_CHUNKS, LANES), lambda i, te, nt: (i, 0)),
        scratch_shapes=[pltpu.VMEM((d, EXPERT_HIDDEN), jnp.bfloat16),
                        pltpu.VMEM((d, EXPERT_HIDDEN), jnp.bfloat16),
                        pltpu.VMEM((EXPERT_HIDDEN, d), jnp.bfloat16)],
    )
    return pl.pallas_call(
        _moe_grouped_kernel,
        grid_spec=grid_spec,
        out_shape=jax.ShapeDtypeStruct((max_tiles * tm * ROW_CHUNKS, LANES), jnp.float32),
        compiler_params=pltpu.CompilerParams(vmem_limit_bytes=VMEM_LIMIT,
                                             dimension_semantics=("arbitrary",)),
        name="moe_grouped",
    )(tile_expert, n_tiles, u_sorted, wg, wu, wd)


def _combine_ln2_kernel(pos_ref, pos_next_ref, pos_ahead_ref, y_hbm, route_ref, x1_ref, mod_ref, g2_ref, b2_ref,
                        o_ref, buf_ref, sem_ref):
    i = pl.program_id(0)
    n = pl.num_programs(0)
    tm = o_ref.shape[0]
    rows = 2 * tm
    slots = buf_ref.shape[0]
    stored = lambda r: pl.ds(pl.multiple_of(r * ROW_CHUNKS, ROW_CHUNKS), ROW_CHUNKS)

    def row_copy(rows_ref, r, slot):
        return pltpu.make_async_copy(y_hbm.at[stored(rows_ref[0, 0, r]), :], buf_ref.at[slot, stored(r), :],
                                     sem_ref.at[slot])

    def start_gather(rows_ref, slot):
        def body(r, _):
            row_copy(rows_ref, r, slot).start()
            return 0
        lax.fori_loop(0, rows, body, 0, unroll=GATHER_UNROLL)

    @pl.when(i == 0)
    def _():
        start_gather(pos_ref, 0)
        start_gather(pos_next_ref, 1)

    slot = i % slots
    pltpu.make_async_copy(y_hbm.at[pl.ds(0, rows * ROW_CHUNKS), :], buf_ref.at[slot], sem_ref.at[slot]).wait()

    def finish_tile():
        route = route_ref[...]
        w1 = route[:, ROUTE_W1:ROUTE_W1 + 1]
        w2 = route[:, ROUTE_W2:ROUTE_W2 + 1]
        gathered = buf_ref.at[slot]
        y = w1 * _load_tile_rows(gathered, 0, tm) + w2 * _load_tile_rows(gathered, tm * ROW_CHUNKS, tm)
        gate2 = mod_ref[0, 5:6, :]
        o_ref[...] = _ln(DEEPNORM_ALPHA * x1_ref[...] + (1.0 + gate2) * y) * g2_ref[...] + b2_ref[...]

    @pl.when(i + 2 < n)
    def _():
        ahead_slot = (i + 2) % slots
        for r in range(rows):
            row_copy(pos_ahead_ref, r, ahead_slot).start()
        finish_tile()

    @pl.when(i + 2 >= n)
    def _():
        finish_tile()


def _combine_ln2(pos, y_sorted, route, x1, mod, ln2_g, ln2_b, seq):
    t, d = x1.shape
    tm = COMBINE_ROW_TILE
    n = t // tm
    assert n >= 2
    tiles_per_seq = seq // tm
    row = lambda i: (i, 0)
    const2 = lambda i: (0, 0)
    ahead = lambda k: pl.BlockSpec((1, 1, 2 * tm), lambda i: (jnp.minimum(i + k, n - 1), 0, 0),
                                   memory_space=pltpu.SMEM)
    return pl.pallas_call(
        _combine_ln2_kernel,
        grid=(n,),
        in_specs=[ahead(0), ahead(1), ahead(2),
                  pl.BlockSpec(memory_space=pl.ANY),
                  pl.BlockSpec((tm, LANES), row),
                  pl.BlockSpec((tm, d), row),
                  pl.BlockSpec((1, N_MOD, d), lambda i: (i // tiles_per_seq, 0, 0)),
                  pl.BlockSpec((1, d), const2),
                  pl.BlockSpec((1, d), const2)],
        out_specs=pl.BlockSpec((tm, d), row),
        out_shape=jax.ShapeDtypeStruct((t, d), jnp.float32),
        scratch_shapes=[pltpu.VMEM((COMBINE_SLOTS, 2 * tm * ROW_CHUNKS, LANES), jnp.float32),
                        pltpu.SemaphoreType.DMA((COMBINE_SLOTS,))],
        compiler_params=pltpu.CompilerParams(vmem_limit_bytes=VMEM_LIMIT,
                                             dimension_semantics=("arbitrary",)),
        name="combine_ln2",
    )(pos, pos, pos, y_sorted, route, x1, mod, ln2_g, ln2_b)


def _dispatch_plan(route, counts):
    t = route.shape[0]
    tm = MOE_ROW_TILE
    max_tiles = 2 * t // tm + N_EXPERTS
    counts = counts[0, :N_EXPERTS].astype(jnp.int32)
    tiles = (counts + tm - 1) // tm
    tile_end = jnp.cumsum(tiles)
    row_start = (tile_end - tiles) * tm
    experts = route[:, ROUTE_E1:ROUTE_E2 + 1].astype(jnp.int32)
    ranks = route[:, ROUTE_RANK1:ROUTE_RANK2 + 1].astype(jnp.int32)
    is_expert = experts[:, :, None] == jnp.arange(N_EXPERTS, dtype=jnp.int32)
    pos = jnp.sum(jnp.where(is_expert, row_start, 0), axis=-1) + ranks
    tile_ids = jnp.arange(max_tiles, dtype=jnp.int32)
    tile_expert = jnp.minimum(jnp.sum(tile_end[None, :] <= tile_ids[:, None], axis=1), N_EXPERTS - 1)
    by_tile = lambda tc: pos.reshape(t // tc, tc, 2).transpose(0, 2, 1).reshape(t // tc, 1, 2 * tc)
    return (tile_expert.astype(jnp.int32), tile_end[-1:].astype(jnp.int32), tile_end.astype(jnp.int32),
            by_tile(DISPATCH_ROW_TILE), by_tile(COMBINE_ROW_TILE), max_tiles * tm)


def kernel(x, c, w_ada, b_ada, w_in, w_pool, pool_scale, w_out, ln1_g, ln1_b, w_router_group, b_router_group,
           w_router_expert, b_router_expert, w_gate, w_up, w_down, ln2_g, ln2_b):
    b, s, d = x.shape
    bf16 = jnp.bfloat16
    for layer in range(DEPTH):
        c_pad = jnp.pad(c, ((0, SUBLANES - b), (0, 0)))
        mod = _adaln(c_pad, w_ada[layer], b_ada[layer][None, :])[:b].reshape(b, N_MOD, d)
        x2d = x.reshape(b * s, d)
        qkv, p = _ln_inproj(x2d, mod, w_in[layer].astype(bf16), s)
        o_sb = _sb_attn(qkv, b, s).reshape(b * s, SB_WIDTH)
        pad = LANES - N_EXPERTS - N_EXPERT_GROUPS
        w_router = jnp.pad(jnp.concatenate([w_router_expert[layer], w_router_group[layer]], axis=1),
                           ((0, 0), (0, pad)))
        b_router = jnp.pad(jnp.concatenate([b_router_expert[layer], b_router_group[layer]]), (0, pad))[None, :]
        w_router_hi = w_router.astype(bf16)
        w_router_lo = (w_router - w_router_hi.astype(jnp.float32)).astype(bf16)
        w_router = jnp.concatenate([w_router_hi, w_router_hi, w_router_lo], axis=0)
        x1, u2, route, counts = _mix_ln1(o_sb, p, w_pool[layer].astype(bf16), pool_scale[layer][None, :],
                                         w_out[layer].astype(bf16), x2d, mod, ln1_g[layer][None, :],
                                         ln1_b[layer][None, :], w_router, b_router, s)
        tile_expert, n_tiles, tile_end, pos_dispatch, pos, n_rows = _dispatch_plan(route, counts)
        u_sorted = _dispatch(tile_end, pos_dispatch, u2, n_rows)
        y_sorted = _moe_grouped(tile_expert, n_tiles, u_sorted, w_gate[layer], w_up[layer], w_down[layer])
        x2 = _combine_ln2(pos, y_sorted, route, x1, mod, ln2_g[layer][None, :], ln2_b[layer][None, :], s)
        x = x2.reshape(b, s, d)
    return x
```

```python
import functools

import jax
import jax.numpy as jnp
from jax import lax
from jax.experimental import pallas as pl
from jax.experimental.pallas import tpu as pltpu

D_MODEL = 1024
N_SB_HEADS = 8
SB_HEAD_DIM = 64
SB_WIDTH = N_SB_HEADS * SB_HEAD_DIM
POOL_WINDOWS = (2, 4, 8, 16)
POOL_GROUP_DIM = 128
POOL_WIDTH = len(POOL_WINDOWS) * POOL_GROUP_DIM
N_EXPERT_GROUPS = 4
EXPERTS_PER_GROUP = 4
N_EXPERTS = N_EXPERT_GROUPS * EXPERTS_PER_GROUP
EXPERT_HIDDEN = 512
DEPTH = 1
DEEPNORM_ALPHA = (2.0 * DEPTH) ** 0.25
LN_EPS = 1e-5
N_MOD = 6

LANES = 128
SUBLANES = 8
HALO = max(POOL_WINDOWS)
VMEM_LIMIT = 56 * 1024 * 1024

ROW_TILE = 512
MIX_STREAMS = 2
ATTN_TILE = 256
ATTN_QUERY_BLOCKS = 2
STICK_GONE_LOG2 = -180.0
MOE_ROW_TILE = 512
DISPATCH_ROW_TILE = 1024
COMBINE_ROW_TILE = 256
GATHER_UNROLL = 8


ROW_CHUNKS = D_MODEL // LANES


def _store_tile_rows(ref, base, value):
    n = value.shape[0]
    for c in range(ROW_CHUNKS):
        ref[pl.ds(base + c, n, stride=ROW_CHUNKS), :] = value[:, c * LANES:(c + 1) * LANES]


def _load_tile_rows(ref, base, n):
    return jnp.concatenate([ref[pl.ds(base + c, n, stride=ROW_CHUNKS), :] for c in range(ROW_CHUNKS)], axis=1)


def _ln(x):
    mu = jnp.mean(x, axis=-1, keepdims=True)
    xc = x - mu
    var = jnp.mean(xc * xc, axis=-1, keepdims=True)
    return xc * lax.rsqrt(var + LN_EPS)


def _adaln_kernel(c_ref, w_ref, b_ref, o_ref):
    c = c_ref[...]
    a = c * jax.nn.sigmoid(c)
    o_ref[...] = jnp.dot(a, w_ref[...], preferred_element_type=jnp.float32,
                         precision=lax.Precision.HIGHEST) + b_ref[...]


def _adaln(c_pad, w_ada, b_ada):
    rows, d = c_pad.shape
    n = w_ada.shape[1]
    tn = 1024
    return pl.pallas_call(
        _adaln_kernel,
        grid=(n // tn,),
        in_specs=[pl.BlockSpec((rows, d), lambda j: (0, 0)),
                  pl.BlockSpec((d, tn), lambda j: (0, j)),
                  pl.BlockSpec((1, tn), lambda j: (0, j))],
        out_specs=pl.BlockSpec((rows, tn), lambda j: (0, j)),
        out_shape=jax.ShapeDtypeStruct((rows, n), jnp.float32),
        compiler_params=pltpu.CompilerParams(vmem_limit_bytes=VMEM_LIMIT),
        name="adaln",
    )(c_pad, w_ada, b_ada)


def _ln_inproj_kernel(x_ref, mod_ref, w_ref, qkv_ref, p_ref):
    shift = mod_ref[0, 0:1, :]
    scale = mod_ref[0, 1:2, :]
    u = (_ln(x_ref[...]) * (1.0 + scale) + shift).astype(jnp.bfloat16)
    qk_scale = SB_HEAD_DIM ** -0.5 * 1.4426950408889634
    q = jnp.dot(u, w_ref[:, 0:SB_WIDTH], preferred_element_type=jnp.float32) * qk_scale
    qkv_ref[:, 0:SB_WIDTH] = q.astype(jnp.bfloat16)
    for j in (1, 2):
        kv = jnp.dot(u, w_ref[:, j * SB_WIDTH:(j + 1) * SB_WIDTH], preferred_element_type=jnp.float32)
        qkv_ref[:, j * SB_WIDTH:(j + 1) * SB_WIDTH] = kv.astype(jnp.bfloat16)
    p_ref[...] = jnp.dot(u, w_ref[:, 3 * SB_WIDTH:], preferred_element_type=jnp.float32)


def _ln_inproj(x2d, mod, w_in_bf16, seq):
    t, d = x2d.shape
    tm = ROW_TILE
    tiles_per_seq = seq // tm
    return pl.pallas_call(
        _ln_inproj_kernel,
        grid=(t // tm,),
        in_specs=[pl.BlockSpec((tm, d), lambda i: (i, 0)),
                  pl.BlockSpec((1, N_MOD, d), lambda i: (i // tiles_per_seq, 0, 0)),
                  pl.BlockSpec(w_in_bf16.shape, lambda i: (0, 0))],
        out_specs=[pl.BlockSpec((tm, 3 * SB_WIDTH), lambda i: (i, 0)),
                   pl.BlockSpec((tm, POOL_WIDTH), lambda i: (i, 0))],
        out_shape=[jax.ShapeDtypeStruct((t, 3 * SB_WIDTH), jnp.bfloat16),
                   jax.ShapeDtypeStruct((t, POOL_WIDTH), jnp.float32)],
        compiler_params=pltpu.CompilerParams(vmem_limit_bytes=VMEM_LIMIT),
        name="ln_inproj",
    )(x2d, mod, w_in_bf16)


def _sb_attn_kernel(q_ref, k_ref, v_ref, o_ref, acc_ref, carry_ref):
    tq = tk = ATTN_TILE
    heads = (0, 1)
    subs = tuple(range(ATTN_QUERY_BLOCKS))
    qis = [pl.program_id(2) * ATTN_QUERY_BLOCKS + s for s in subs]
    lane = lax.broadcasted_iota(jnp.int32, (tq, LANES), 1)
    first_head = lane < SB_HEAD_DIM
    q_heads = {}
    for s in subs:
        q2 = q_ref[0, s * tq:(s + 1) * tq, :]
        zero = jnp.zeros_like(q2)
        q_heads[s, 0], q_heads[s, 1] = jnp.where(first_head, q2, zero), jnp.where(first_head, zero, q2)

    row = lax.broadcasted_iota(jnp.int32, (tk, tk), 0)
    col = lax.broadcasted_iota(jnp.int32, (tk, tk), 1)
    neg_suffix = jnp.where(row >= col, -1.0, 0.0).astype(jnp.bfloat16)
    causal = col < row

    def visit(blocks, carries):
        k_blks, v_blks = [], []
        for s, j, _ in blocks:
            start = pl.multiple_of(jnp.maximum(qis[s] - j, 0) * tk, tk)
            k_blks.append(k_ref[0, pl.ds(start, tk), :])
            v_blks.append(v_ref[0, pl.ds(start, tk), :])
        pairs = [(b, h) for b in range(len(blocks)) for h in heads]
        zs = {(b, h): lax.dot_general(q_heads[blocks[b][0], h], k_blks[b], (((1,), (1,)), ((), ())),
                                      preferred_element_type=jnp.float32) for b, h in pairs}
        fixed, rounded = {}, {}
        for b, h in pairs:
            z = zs[b, h]
            n = jnp.maximum(z, 0.0) + jnp.log2(1.0 + jnp.exp2(-jnp.abs(z)))
            if blocks[b][2]:
                n = jnp.where(causal, n, 0.0)
            n_bf16 = n.astype(jnp.bfloat16)
            rounded[b, h] = n_bf16
            fixed[b, h] = z - (n - n_bf16.astype(jnp.float32))
        incls = {bh: jnp.dot(rounded[bh], neg_suffix, preferred_element_type=jnp.float32) for bh in pairs}
        atts = {}
        carries = dict(carries)
        for b, h in pairs:
            s = blocks[b][0]
            carry = carries[s, h]
            block_sum = jnp.broadcast_to(incls[b, h][:, 0:1], (tq, LANES))
            expo = fixed[b, h] + incls[b, h]
            if carry is None:
                carries[s, h] = block_sum
            else:
                expo = expo + jnp.concatenate([carry] * (tk // LANES), axis=1)
                carries[s, h] = carry + block_sum
            att = jnp.exp2(expo)
            if blocks[b][2]:
                att = jnp.where(causal, att, 0.0)
            atts[b, h] = att.astype(jnp.bfloat16)
        outs = {(b, h): jnp.dot(atts[b, h], v_blks[b], preferred_element_type=jnp.float32) for b, h in pairs}
        return outs, carries

    head_blocks = [(s, 0, True) for s in subs] + [(s, 1, False) for s in subs]
    outs, carries = visit(head_blocks, {(s, h): None for s in subs for h in heads})
    for s in subs:
        has_second = qis[s] >= 1
        for h in heads:
            acc_ref[s, h] = outs[s, h] + jnp.where(has_second, outs[len(subs) + s, h], 0.0)
            carry_ref[s, h] = carries[s, h]

    for s in subs:
        def more(state, s=s):
            j, stick_gone = state
            return (j <= qis[s]) & jnp.logical_not(stick_gone)

        def visit_next(state, s=s):
            j, _ = state
            outs, new_carries = visit([(s, j, False)], {(s, h): carry_ref[s, h] for h in heads})
            for h in heads:
                acc_ref[s, h] += outs[0, h]
                carry_ref[s, h] = new_carries[s, h]
            return j + 1, jnp.max(carry_ref[s]) < STICK_GONE_LOG2

        lax.while_loop(more, visit_next, (jnp.int32(2), jnp.max(carry_ref[s]) < STICK_GONE_LOG2))
        o_ref[0, s * tq:(s + 1) * tq, :] = jnp.where(first_head, acc_ref[s, 0], acc_ref[s, 1]).astype(o_ref.dtype)


def _sb_attn(qkv, batch, seq):
    qkv3 = qkv.reshape(batch, seq, 3 * SB_WIDTH)
    tq = ATTN_TILE
    rows = ATTN_QUERY_BLOCKS * tq
    pairs = SB_WIDTH // LANES
    return pl.pallas_call(
        _sb_attn_kernel,
        grid=(batch, pairs, seq // rows),
        in_specs=[pl.BlockSpec((1, rows, LANES), lambda b, h, i: (b, i, h)),
                  pl.BlockSpec((1, seq, LANES), lambda b, h, i: (b, 0, pairs + h)),
                  pl.BlockSpec((1, seq, LANES), lambda b, h, i: (b, 0, 2 * pairs + h))],
        out_specs=pl.BlockSpec((1, rows, LANES), lambda b, h, i: (b, i, h)),
        out_shape=jax.ShapeDtypeStruct((batch, seq, SB_WIDTH), jnp.bfloat16),
        scratch_shapes=[pltpu.VMEM((ATTN_QUERY_BLOCKS, 2, tq, LANES), jnp.float32),
                        pltpu.VMEM((ATTN_QUERY_BLOCKS, 2, tq, LANES), jnp.float32)],
        compiler_params=pltpu.CompilerParams(vmem_limit_bytes=VMEM_LIMIT),
        name="sb_attn",
    )(qkv3, qkv3, qkv3)


def _route(logits):
    lane = lax.broadcasted_iota(jnp.int32, logits.shape, 1).astype(jnp.float32)
    neg = jnp.float32(-jnp.inf)
    big = jnp.float32(LANES)
    is_group = (lane >= N_EXPERTS) & (lane < N_EXPERTS + N_EXPERT_GROUPS)
    gl = jnp.where(is_group, logits, neg)
    g_max = jnp.max(gl, axis=-1, keepdims=True)
    g_sum = jnp.sum(jnp.exp(gl - g_max), axis=-1, keepdims=True)
    g_p = 1.0 / g_sum
    g_idx = jnp.min(jnp.where(gl == g_max, lane, big), axis=-1, keepdims=True) - N_EXPERTS
    in_group = (lane >= g_idx * EXPERTS_PER_GROUP) & (lane < (g_idx + 1) * EXPERTS_PER_GROUP)
    el = jnp.where(in_group, logits, neg)
    e1 = jnp.max(el, axis=-1, keepdims=True)
    i1 = jnp.min(jnp.where(el == e1, lane, big), axis=-1, keepdims=True)
    el2 = jnp.where(lane == i1, neg, el)
    e2 = jnp.max(el2, axis=-1, keepdims=True)
    i2 = jnp.min(jnp.where(el2 == e2, lane, big), axis=-1, keepdims=True)
    r = jnp.exp(e2 - e1)
    w1 = g_p / (1.0 + r)
    w2 = w1 * r
    return i1, i2, w1, w2


ROUTE_W1, ROUTE_W2, ROUTE_E1, ROUTE_E2, ROUTE_RANK1, ROUTE_RANK2 = range(6)


def _mix_ln1_kernel(steps_per_seq, osb_ref, p_ref, halo_ref, wpool_ref, pscale_ref, wout_ref, x_ref, mod_ref,
                    g1_ref, b1_ref, wr_ref, br_ref, earlier_ref, x1_ref, u2_ref, route_ref, route_t_ref, count_ref,
                    pext_ref, running_ref):
    tm = ROW_TILE
    streams = range(MIX_STREAMS)
    rows = lambda s: slice(s * tm, (s + 1) * tm)
    step_in_seq = pl.program_id(0) % steps_per_seq
    pext_ref[0:HALO, :] = jnp.where(step_in_seq == 0, 0.0, halo_ref[...])
    pext_ref[HALO:, :] = p_ref[...]

    mixer_outs = []
    for s in streams:
        pos = (step_in_seq * MIX_STREAMS + s) * tm + lax.broadcasted_iota(jnp.int32, (tm, 1), 0)
        first_row = HALO + s * tm
        mixer_out = [osb_ref[rows(s), :]]
        for g, w in enumerate(POOL_WINDOWS):
            cols = slice(g * POOL_GROUP_DIM, (g + 1) * POOL_GROUP_DIM)
            own = p_ref[rows(s), cols]
            win = own
            for i in range(1, w):
                win = win + pext_ref[first_row - i:first_row - i + tm, cols]
            count = jnp.minimum(pos + 1, w).astype(jnp.float32)
            pooled = win / count - own
            o_pool = jnp.dot(pooled.astype(jnp.bfloat16), wpool_ref[g], preferred_element_type=jnp.float32)
            mixer_out.append((o_pool * pscale_ref[:, cols]).astype(jnp.bfloat16))
        mixer_outs.append(jnp.concatenate(mixer_out, axis=1))
    mixed = [jnp.dot(m, wout_ref[...], preferred_element_type=jnp.float32) for m in mixer_outs]

    gate1 = mod_ref[0, 2:3, :]
    shift2 = mod_ref[0, 3:4, :]
    scale2 = mod_ref[0, 4:5, :]
    router_in = []
    for s in streams:
        x1 = _ln(DEEPNORM_ALPHA * x_ref[rows(s), :] + (1.0 + gate1) * mixed[s]) * g1_ref[...] + b1_ref[...]
        x1_ref[rows(s), :] = x1
        u2 = _ln(x1) * (1.0 + scale2) + shift2
        _store_tile_rows(u2_ref, s * tm * ROW_CHUNKS, u2)
        u2_hi = u2.astype(jnp.bfloat16)
        u2_lo = (u2 - u2_hi.astype(jnp.float32)).astype(jnp.bfloat16)
        router_in.append(jnp.concatenate([u2_hi, u2_lo, u2_hi], axis=1))
    logits = [jnp.dot(r, wr_ref[...], preferred_element_type=jnp.float32) + br_ref[...] for r in router_in]
    routed = [_route(l) for l in logits]

    @pl.when(pl.program_id(0) == 0)
    def _():
        running_ref[...] = jnp.zeros_like(running_ref)

    lane = lax.broadcasted_iota(jnp.int32, (tm, LANES), 1)
    lane_f = lane.astype(jnp.float32)
    picked = [(lane_f == i1, lane_f == i2) for i1, i2, _, _ in routed]
    uses = [jnp.where(first | second, 1.0, 0.0) for first, second in picked]
    within = [jnp.dot(earlier_ref[...], u.astype(jnp.bfloat16), preferred_element_type=jnp.float32) for u in uses]
    running = running_ref[0:1, :]
    for s in streams:
        i1, i2, w1, w2 = routed[s]
        first, second = picked[s]
        before = within[s] + running
        rank1 = jnp.sum(jnp.where(first, before, 0.0), axis=-1, keepdims=True)
        rank2 = jnp.sum(jnp.where(second, before, 0.0), axis=-1, keepdims=True)
        running = running + jnp.sum(uses[s], axis=0, keepdims=True)
        record = jnp.zeros((tm, LANES), jnp.float32)
        for slot, value in ((ROUTE_W1, w1), (ROUTE_W2, w2), (ROUTE_E1, i1), (ROUTE_E2, i2),
                            (ROUTE_RANK1, rank1), (ROUTE_RANK2, rank2)):
            record = jnp.where(lane == slot, value, record)
        route_ref[rows(s), :] = record
        route_t_ref[:, rows(s)] = record.T[0:SUBLANES, :]
    running_ref[...] = jnp.broadcast_to(running, running_ref.shape)
    count_ref[...] = jnp.broadcast_to(running, count_ref.shape)


def _mix_ln1(o_sb, p, w_pool_bf16, pool_scale, w_out_bf16, x2d, mod, ln1_g, ln1_b, w_router, b_router, seq):
    t, d = x2d.shape
    tm = MIX_STREAMS * ROW_TILE
    steps_per_seq = seq // tm
    halo_blocks_per_step = tm // HALO
    row = lambda i: (i, 0)
    const2 = lambda i: (0, 0)
    return pl.pallas_call(
        functools.partial(_mix_ln1_kernel, steps_per_seq),
        grid=(t // tm,),
        in_specs=[pl.BlockSpec((tm, SB_WIDTH), row),
                  pl.BlockSpec((tm, POOL_WIDTH), row),
                  pl.BlockSpec((HALO, POOL_WIDTH), lambda i: (jnp.maximum(i * halo_blocks_per_step - 1, 0), 0)),
                  pl.BlockSpec(w_pool_bf16.shape, lambda i: (0, 0, 0)),
                  pl.BlockSpec((1, POOL_WIDTH), const2),
                  pl.BlockSpec(w_out_bf16.shape, const2),
                  pl.BlockSpec((tm, d), row),
                  pl.BlockSpec((1, N_MOD, d), lambda i: (i // steps_per_seq, 0, 0)),
                  pl.BlockSpec((1, d), const2),
                  pl.BlockSpec((1, d), const2),
                  pl.BlockSpec(w_router.shape, const2),
                  pl.BlockSpec((1, LANES), const2),
                  pl.BlockSpec((ROW_TILE, ROW_TILE), const2)],
        out_specs=[pl.BlockSpec((tm, d), row),
                   pl.BlockSpec((tm * ROW_CHUNKS, LANES), row),
                   pl.BlockSpec((tm, LANES), row),
                   pl.BlockSpec((SUBLANES, tm), lambda i: (0, i)),
                   pl.BlockSpec((SUBLANES, LANES), const2)],
        out_shape=[jax.ShapeDtypeStruct((t, d), jnp.float32),
                   jax.ShapeDtypeStruct((t * ROW_CHUNKS, LANES), jnp.float32),
                   jax.ShapeDtypeStruct((t, LANES), jnp.float32),
                   jax.ShapeDtypeStruct((SUBLANES, t), jnp.float32),
                   jax.ShapeDtypeStruct((SUBLANES, LANES), jnp.float32)],
        scratch_shapes=[pltpu.VMEM((HALO + tm, POOL_WIDTH), jnp.float32),
                        pltpu.VMEM((SUBLANES, LANES), jnp.float32)],
        compiler_params=pltpu.CompilerParams(vmem_limit_bytes=VMEM_LIMIT,
                                             dimension_semantics=("arbitrary",)),
        name="mix_ln1",
    )(o_sb, p, p, w_pool_bf16, pool_scale, w_out_bf16, x2d, mod, ln1_g, ln1_b, w_router, b_router,
      jnp.tri(ROW_TILE, ROW_TILE, -1, dtype=jnp.bfloat16))


def _dispatch_kernel(tile_end_ref, pos_ref, u_ref, sorted_hbm, zeros_ref, sem):
    tm = u_ref.shape[0] // ROW_CHUNKS
    tile = MOE_ROW_TILE * ROW_CHUNKS
    stored = lambda r: pl.ds(pl.multiple_of(r * ROW_CHUNKS, ROW_CHUNKS), ROW_CHUNKS)

    @pl.when(pl.program_id(0) == 0)
    def _():
        zeros_ref[...] = jnp.zeros_like(zeros_ref)

        def tile_fill(i):
            return pltpu.make_async_copy(zeros_ref, sorted_hbm.at[pl.ds(pl.multiple_of(i * tile, tile), tile), :],
                                         sem)

        def last_tile_fills(act):
            for e in range(N_EXPERTS):
                first_tile = tile_end_ref[e - 1] if e else 0

                @pl.when(tile_end_ref[e] > first_tile)
                def _():
                    act(tile_fill(tile_end_ref[e] - 1))

        unused = (tile_end_ref[N_EXPERTS - 1], sorted_hbm.shape[0] // tile)
        last_tile_fills(lambda fill: fill.start())
        lax.fori_loop(*unused, lambda i, _: (tile_fill(i).start(), 0)[1], 0)
        last_tile_fills(lambda fill: fill.wait())
        lax.fori_loop(*unused, lambda i, _: (tile_fill(i).wait(), 0)[1], 0)

    def body(r, _):
        row = u_ref.at[stored(r), :]
        pltpu.make_async_copy(row, sorted_hbm.at[stored(pos_ref[0, 0, r]), :], sem).start()
        pltpu.make_async_copy(row, sorted_hbm.at[stored(pos_ref[0, 0, tm + r]), :], sem).start()
        return 0

    lax.fori_loop(0, tm, body, 0, unroll=GATHER_UNROLL // 2)
    for _ in range(2):
        pltpu.make_async_copy(u_ref, sorted_hbm.at[pl.ds(0, tm * ROW_CHUNKS), :], sem).wait()


def _dispatch(tile_end, pos, u2, n_rows):
    tm = pos.shape[2] // 2
    t = u2.shape[0] // ROW_CHUNKS
    grid_spec = pltpu.PrefetchScalarGridSpec(
        num_scalar_prefetch=1,
        grid=(t // tm,),
        in_specs=[pl.BlockSpec((1, 1, 2 * tm), lambda i, te: (i, 0, 0), memory_space=pltpu.SMEM),
                  pl.BlockSpec((tm * ROW_CHUNKS, LANES), lambda i, te: (i, 0))],
        out_specs=pl.BlockSpec(memory_space=pl.ANY),
        scratch_shapes=[pltpu.VMEM((MOE_ROW_TILE * ROW_CHUNKS, LANES), jnp.float32),
                        pltpu.SemaphoreType.DMA(())],
    )
    return pl.pallas_call(
        _dispatch_kernel,
        grid_spec=grid_spec,
        out_shape=jax.ShapeDtypeStruct((n_rows * ROW_CHUNKS, LANES), jnp.float32),
        compiler_params=pltpu.CompilerParams(vmem_limit_bytes=VMEM_LIMIT,
                                             dimension_semantics=("arbitrary",)),
        name="dispatch",
    )(tile_end, pos, u2)


def _moe_grouped_kernel(tile_expert_ref, n_tiles_ref, u_ref, wg_ref, wu_ref, wd_ref, y_ref,
                        wg_bf16_ref, wu_bf16_ref, wd_bf16_ref):
    i = pl.program_id(0)
    bf16 = jnp.bfloat16
    valid = i < n_tiles_ref[0]

    @pl.when(valid & ((i == 0) | (tile_expert_ref[i] != tile_expert_ref[jnp.maximum(i - 1, 0)])))
    def _():
        wg_bf16_ref[...] = wg_ref[0].astype(bf16)
        wu_bf16_ref[...] = wu_ref[0].astype(bf16)
        wd_bf16_ref[...] = wd_ref[0].astype(bf16)

    @pl.when(valid)
    def _():
        u = _load_tile_rows(u_ref, 0, MOE_ROW_TILE).astype(bf16)
        gate = jnp.dot(u, wg_bf16_ref[...], preferred_element_type=jnp.float32)
        up = jnp.dot(u, wu_bf16_ref[...], preferred_element_type=jnp.float32)
        h = gate * jax.nn.sigmoid(gate) * up
        _store_tile_rows(y_ref, 0, jnp.dot(h.astype(bf16), wd_bf16_ref[...], preferred_element_type=jnp.float32))

    @pl.when(jnp.logical_not(valid))
    def _():
        y_ref[...] = jnp.zeros_like(y_ref)


def _moe_grouped(tile_expert, n_tiles, u_sorted, wg, wu, wd):
    d = D_MODEL
    tm = MOE_ROW_TILE
    max_tiles = tile_expert.shape[0]
    expert = lambda i, te, nt: (te[i], 0, 0)
    grid_spec = pltpu.PrefetchScalarGridSpec(
        num_scalar_prefetch=2,
        grid=(max_tiles,),
        in_specs=[pl.BlockSpec((tm * ROW_CHUNKS, LANES), lambda i, te, nt: (jnp.minimum(i, nt[0] - 1), 0)),
                  pl.BlockSpec((1, d, EXPERT_HIDDEN), expert),
                  pl.BlockSpec((1, d, EXPERT_HIDDEN), expert),
                  pl.BlockSpec((1, EXPERT_HIDDEN, d), expert)],
        out_specs=pl.BlockSpec((tm * ROW_CHUNKS, LANES), lambda i, te, nt: (i, 0)),
        scratch_shapes=[pltpu.VMEM((d, EXPERT_HIDDEN), jnp.bfloat16),
                        pltpu.VMEM((d, EXPERT_HIDDEN), jnp.bfloat16),
                        pltpu.VMEM((EXPERT_HIDDEN, d), jnp.bfloat16)],
    )
    return pl.pallas_call(
        _moe_grouped_kernel,
        grid_spec=grid_spec,
        out_shape=jax.ShapeDtypeStruct((max_tiles * tm * ROW_CHUNKS, LANES), jnp.float32),
        compiler_params=pltpu.CompilerParams(vmem_limit_bytes=VMEM_LIMIT,
                                             dimension_semantics=("arbitrary",)),
        name="moe_grouped",
    )(tile_expert, n_tiles, u_sorted, wg, wu, wd)


def _combine_ln2_kernel(pos_ref, pos_next_ref, y_hbm, route_ref, x1_ref, mod_ref, g2_ref, b2_ref, o_ref,
                        buf_ref, sem_ref):
    i = pl.program_id(0)
    n = pl.num_programs(0)
    tm = o_ref.shape[0]
    rows = 2 * tm
    stored = lambda r: pl.ds(pl.multiple_of(r * ROW_CHUNKS, ROW_CHUNKS), ROW_CHUNKS)

    def start_gather(rows_ref, slot):
        def body(r, _):
            pltpu.make_async_copy(y_hbm.at[stored(rows_ref[0, 0, r]), :], buf_ref.at[slot, stored(r), :],
                                  sem_ref.at[slot]).start()
            return 0
        lax.fori_loop(0, rows, body, 0, unroll=GATHER_UNROLL)

    @pl.when(i == 0)
    def _():
        start_gather(pos_ref, 0)

    @pl.when(i + 1 < n)
    def _():
        start_gather(pos_next_ref, (i + 1) % 2)

    slot = i % 2
    pltpu.make_async_copy(y_hbm.at[pl.ds(0, rows * ROW_CHUNKS), :], buf_ref.at[slot], sem_ref.at[slot]).wait()
    route = route_ref[...]
    w1 = route[:, ROUTE_W1:ROUTE_W1 + 1]
    w2 = route[:, ROUTE_W2:ROUTE_W2 + 1]
    gathered = buf_ref.at[slot]
    y = w1 * _load_tile_rows(gathered, 0, tm) + w2 * _load_tile_rows(gathered, tm * ROW_CHUNKS, tm)
    gate2 = mod_ref[0, 5:6, :]
    o_ref[...] = _ln(DEEPNORM_ALPHA * x1_ref[...] + (1.0 + gate2) * y) * g2_ref[...] + b2_ref[...]


def _combine_ln2(pos, y_sorted, route, x1, mod, ln2_g, ln2_b, seq):
    t, d = x1.shape
    tm = COMBINE_ROW_TILE
    n = t // tm
    tiles_per_seq = seq // tm
    row = lambda i: (i, 0)
    const2 = lambda i: (0, 0)
    return pl.pallas_call(
        _combine_ln2_kernel,
        grid=(n,),
        in_specs=[pl.BlockSpec((1, 1, 2 * tm), lambda i: (i, 0, 0), memory_space=pltpu.SMEM),
                  pl.BlockSpec((1, 1, 2 * tm), lambda i: (jnp.minimum(i + 1, n - 1), 0, 0),
                               memory_space=pltpu.SMEM),
                  pl.BlockSpec(memory_space=pl.ANY),
                  pl.BlockSpec((tm, LANES), row),
                  pl.BlockSpec((tm, d), row),
                  pl.BlockSpec((1, N_MOD, d), lambda i: (i // tiles_per_seq, 0, 0)),
                  pl.BlockSpec((1, d), const2),
                  pl.BlockSpec((1, d), const2)],
        out_specs=pl.BlockSpec((tm, d), row),
        out_shape=jax.ShapeDtypeStruct((t, d), jnp.float32),
        scratch_shapes=[pltpu.VMEM((2, 2 * tm * ROW_CHUNKS, LANES), jnp.float32),
                        pltpu.SemaphoreType.DMA((2,))],
        compiler_params=pltpu.CompilerParams(vmem_limit_bytes=VMEM_LIMIT,
                                             dimension_semantics=("arbitrary",)),
        name="combine_ln2",
    )(pos, pos, y_sorted, route, x1, mod, ln2_g, ln2_b)


def _dispatch_plan(route_t, counts):
    t = route_t.shape[1]
    tm = MOE_ROW_TILE
    max_tiles = 2 * t // tm + N_EXPERTS
    counts = counts[0, :N_EXPERTS].astype(jnp.int32)
    tiles = (counts + tm - 1) // tm
    tile_end = jnp.cumsum(tiles)
    row_start = (tile_end - tiles) * tm
    experts = route_t[ROUTE_E1:ROUTE_E2 + 1].astype(jnp.int32)
    ranks = route_t[ROUTE_RANK1:ROUTE_RANK2 + 1].astype(jnp.int32)
    expert_ids = jnp.arange(N_EXPERTS, dtype=jnp.int32)[:, None, None]
    pos = jnp.sum(jnp.where(experts[None] == expert_ids, row_start[:, None, None], 0), axis=0) + ranks
    tile_ids = jnp.arange(max_tiles, dtype=jnp.int32)
    tile_expert = jnp.minimum(jnp.sum(tile_end[None, :] <= tile_ids[:, None], axis=1), N_EXPERTS - 1)
    by_tile = lambda tc: pos.reshape(2, t // tc, tc).transpose(1, 0, 2).reshape(t // tc, 1, 2 * tc)
    return (tile_expert.astype(jnp.int32), tile_end[-1:].astype(jnp.int32), tile_end.astype(jnp.int32),
            by_tile(DISPATCH_ROW_TILE), by_tile(COMBINE_ROW_TILE), max_tiles * tm)


def kernel(x, c, w_ada, b_ada, w_in, w_pool, pool_scale, w_out, ln1_g, ln1_b, w_router_group, b_router_group,
           w_router_expert, b_router_expert, w_gate, w_up, w_down, ln2_g, ln2_b):
    b, s, d = x.shape
    bf16 = jnp.bfloat16
    for layer in range(DEPTH):
        c_pad = jnp.pad(c, ((0, SUBLANES - b), (0, 0)))
        mod = _adaln(c_pad, w_ada[layer], b_ada[layer][None, :])[:b].reshape(b, N_MOD, d)
        x2d = x.reshape(b * s, d)
        qkv, p = _ln_inproj(x2d, mod, w_in[layer].astype(bf16), s)
        o_sb = _sb_attn(qkv, b, s).reshape(b * s, SB_WIDTH)
        pad = LANES - N_EXPERTS - N_EXPERT_GROUPS
        w_router = jnp.pad(jnp.concatenate([w_router_expert[layer], w_router_group[layer]], axis=1),
                           ((0, 0), (0, pad)))
        b_router = jnp.pad(jnp.concatenate([b_router_expert[layer], b_router_group[layer]]), (0, pad))[None, :]
        w_router_hi = w_router.astype(bf16)
        w_router_lo = (w_router - w_router_hi.astype(jnp.float32)).astype(bf16)
        w_router = jnp.concatenate([w_router_hi, w_router_hi, w_router_lo], axis=0)
        x1, u2, route, route_t, counts = _mix_ln1(o_sb, p, w_pool[layer].astype(bf16), pool_scale[layer][None, :],
                                         w_out[layer].astype(bf16), x2d, mod, ln1_g[layer][None, :],
                                         ln1_b[layer][None, :], w_router, b_router, s)
        tile_expert, n_tiles, tile_end, pos_dispatch, pos, n_rows = _dispatch_plan(route_t, counts)
        u_sorted = _dispatch(tile_end, pos_dispatch, u2, n_rows)
        y_sorted = _moe_grouped(tile_expert, n_tiles, u_sorted, w_gate[layer], w_up[layer], w_down[layer])
        x2 = _combine_ln2(pos, y_sorted, route, x1, mod, ln2_g[layer][None, :], ln2_b[layer][None, :], s)
        x = x2.reshape(b, s, d)
    return x
```

```python
import functools

import jax
import jax.numpy as jnp
from jax import lax
from jax.experimental import pallas as pl
from jax.experimental.pallas import tpu as pltpu

D_MODEL = 1024
N_SB_HEADS = 8
SB_HEAD_DIM = 64
SB_WIDTH = N_SB_HEADS * SB_HEAD_DIM
POOL_WINDOWS = (2, 4, 8, 16)
POOL_GROUP_DIM = 128
POOL_WIDTH = len(POOL_WINDOWS) * POOL_GROUP_DIM
N_EXPERT_GROUPS = 4
EXPERTS_PER_GROUP = 4
N_EXPERTS = N_EXPERT_GROUPS * EXPERTS_PER_GROUP
EXPERT_HIDDEN = 512
DEPTH = 1
DEEPNORM_ALPHA = (2.0 * DEPTH) ** 0.25
LN_EPS = 1e-5
N_MOD = 6

LANES = 128
SUBLANES = 8
HALO = max(POOL_WINDOWS)
VMEM_LIMIT = 56 * 1024 * 1024

ROW_TILE = 512
MIX_STREAMS = 2
ATTN_TILE = 256
ATTN_QUERY_BLOCKS = 2
STICK_GONE_LOG2 = -180.0
MOE_ROW_TILE = 512
DISPATCH_ROW_TILE = 1024
COMBINE_ROW_TILE = 256
GATHER_UNROLL = 8


ROW_CHUNKS = D_MODEL // LANES


def _store_tile_rows(ref, base, value):
    n = value.shape[0]
    for c in range(ROW_CHUNKS):
        ref[pl.ds(base + c, n, stride=ROW_CHUNKS), :] = value[:, c * LANES:(c + 1) * LANES]


def _load_tile_rows(ref, base, n):
    return jnp.concatenate([ref[pl.ds(base + c, n, stride=ROW_CHUNKS), :] for c in range(ROW_CHUNKS)], axis=1)


def _ln(x):
    mu = jnp.mean(x, axis=-1, keepdims=True)
    xc = x - mu
    var = jnp.mean(xc * xc, axis=-1, keepdims=True)
    return xc * lax.rsqrt(var + LN_EPS)


def _adaln_kernel(c_ref, w_ref, b_ref, o_ref):
    c = c_ref[...]
    a = c * jax.nn.sigmoid(c)
    o_ref[...] = jnp.dot(a, w_ref[...], preferred_element_type=jnp.float32,
                         precision=lax.Precision.HIGHEST) + b_ref[...]


def _adaln(c_pad, w_ada, b_ada):
    rows, d = c_pad.shape
    n = w_ada.shape[1]
    tn = 1024
    return pl.pallas_call(
        _adaln_kernel,
        grid=(n // tn,),
        in_specs=[pl.BlockSpec((rows, d), lambda j: (0, 0)),
                  pl.BlockSpec((d, tn), lambda j: (0, j)),
                  pl.BlockSpec((1, tn), lambda j: (0, j))],
        out_specs=pl.BlockSpec((rows, tn), lambda j: (0, j)),
        out_shape=jax.ShapeDtypeStruct((rows, n), jnp.float32),
        compiler_params=pltpu.CompilerParams(vmem_limit_bytes=VMEM_LIMIT),
        name="adaln",
    )(c_pad, w_ada, b_ada)


def _ln_inproj_kernel(x_ref, mod_ref, w_ref, qkv_ref, p_ref, w_bf16_ref):
    @pl.when(pl.program_id(0) == 0)
    def _():
        w_bf16_ref[...] = w_ref[...].astype(jnp.bfloat16)

    tm = ROW_TILE
    rows = lambda s: slice(s * tm, (s + 1) * tm)
    shift = mod_ref[0, 0:1, :]
    scale = mod_ref[0, 1:2, :]
    us = [(_ln(x_ref[rows(s), :]) * (1.0 + scale) + shift).astype(jnp.bfloat16) for s in range(MIX_STREAMS)]
    qk_scale = SB_HEAD_DIM ** -0.5 * 1.4426950408889634
    for s, u in enumerate(us):
        q = jnp.dot(u, w_bf16_ref[:, 0:SB_WIDTH], preferred_element_type=jnp.float32) * qk_scale
        qkv_ref[rows(s), 0:SB_WIDTH] = q.astype(jnp.bfloat16)
        for j in (1, 2):
            kv = jnp.dot(u, w_bf16_ref[:, j * SB_WIDTH:(j + 1) * SB_WIDTH], preferred_element_type=jnp.float32)
            qkv_ref[rows(s), j * SB_WIDTH:(j + 1) * SB_WIDTH] = kv.astype(jnp.bfloat16)
        p_ref[rows(s), :] = jnp.dot(u, w_bf16_ref[:, 3 * SB_WIDTH:], preferred_element_type=jnp.float32)


def _ln_inproj(x2d, mod, w_in, seq):
    t, d = x2d.shape
    tm = MIX_STREAMS * ROW_TILE
    steps_per_seq = seq // tm
    return pl.pallas_call(
        _ln_inproj_kernel,
        grid=(t // tm,),
        in_specs=[pl.BlockSpec((tm, d), lambda i: (i, 0)),
                  pl.BlockSpec((1, N_MOD, d), lambda i: (i // steps_per_seq, 0, 0)),
                  pl.BlockSpec(w_in.shape, lambda i: (0, 0))],
        out_specs=[pl.BlockSpec((tm, 3 * SB_WIDTH), lambda i: (i, 0)),
                   pl.BlockSpec((tm, POOL_WIDTH), lambda i: (i, 0))],
        out_shape=[jax.ShapeDtypeStruct((t, 3 * SB_WIDTH), jnp.bfloat16),
                   jax.ShapeDtypeStruct((t, POOL_WIDTH), jnp.float32)],
        scratch_shapes=[pltpu.VMEM(w_in.shape, jnp.bfloat16)],
        compiler_params=pltpu.CompilerParams(vmem_limit_bytes=VMEM_LIMIT,
                                             dimension_semantics=("arbitrary",)),
        name="ln_inproj",
    )(x2d, mod, w_in)


def _sb_attn_kernel(q_ref, k_ref, v_ref, o_ref, acc_ref, carry_ref):
    tq = tk = ATTN_TILE
    heads = (0, 1)
    subs = tuple(range(ATTN_QUERY_BLOCKS))
    qis = [pl.program_id(2) * ATTN_QUERY_BLOCKS + s for s in subs]
    lane = lax.broadcasted_iota(jnp.int32, (tq, LANES), 1)
    first_head = lane < SB_HEAD_DIM
    q_heads = {}
    for s in subs:
        q2 = q_ref[0, s * tq:(s + 1) * tq, :]
        zero = jnp.zeros_like(q2)
        q_heads[s, 0], q_heads[s, 1] = jnp.where(first_head, q2, zero), jnp.where(first_head, zero, q2)

    row = lax.broadcasted_iota(jnp.int32, (tk, tk), 0)
    col = lax.broadcasted_iota(jnp.int32, (tk, tk), 1)
    neg_suffix = jnp.where(row >= col, -1.0, 0.0).astype(jnp.bfloat16)
    causal = col < row

    def visit(blocks, carries):
        k_blks, v_blks = [], []
        for s, j, _ in blocks:
            start = pl.multiple_of(jnp.maximum(qis[s] - j, 0) * tk, tk)
            k_blks.append(k_ref[0, pl.ds(start, tk), :])
            v_blks.append(v_ref[0, pl.ds(start, tk), :])
        pairs = [(b, h) for b in range(len(blocks)) for h in heads]
        zs = {(b, h): lax.dot_general(q_heads[blocks[b][0], h], k_blks[b], (((1,), (1,)), ((), ())),
                                      preferred_element_type=jnp.float32) for b, h in pairs}
        fixed, rounded = {}, {}
        for b, h in pairs:
            z = zs[b, h]
            n = jnp.maximum(z, 0.0) + jnp.log2(1.0 + jnp.exp2(-jnp.abs(z)))
            if blocks[b][2]:
                n = jnp.where(causal, n, 0.0)
            n_bf16 = n.astype(jnp.bfloat16)
            rounded[b, h] = n_bf16
            fixed[b, h] = z - (n - n_bf16.astype(jnp.float32))
        incls = {bh: jnp.dot(rounded[bh], neg_suffix, preferred_element_type=jnp.float32) for bh in pairs}
        atts = {}
        carries = dict(carries)
        for b, h in pairs:
            s = blocks[b][0]
            carry = carries[s, h]
            block_sum = jnp.broadcast_to(incls[b, h][:, 0:1], (tq, LANES))
            expo = fixed[b, h] + incls[b, h]
            if carry is None:
                carries[s, h] = block_sum
            else:
                expo = expo + jnp.concatenate([carry] * (tk // LANES), axis=1)
                carries[s, h] = carry + block_sum
            att = jnp.exp2(expo)
            if blocks[b][2]:
                att = jnp.where(causal, att, 0.0)
            atts[b, h] = att.astype(jnp.bfloat16)
        outs = {(b, h): jnp.dot(atts[b, h], v_blks[b], preferred_element_type=jnp.float32) for b, h in pairs}
        return outs, carries

    head_blocks = [(s, 0, True) for s in subs] + [(s, 1, False) for s in subs]
    outs, carries = visit(head_blocks, {(s, h): None for s in subs for h in heads})
    for s in subs:
        has_second = qis[s] >= 1
        for h in heads:
            acc_ref[s, h] = outs[s, h] + jnp.where(has_second, outs[len(subs) + s, h], 0.0)
            carry_ref[s, h] = carries[s, h]

    for s in subs:
        def more(state, s=s):
            j, stick_gone = state
            return (j <= qis[s]) & jnp.logical_not(stick_gone)

        def visit_next(state, s=s):
            j, _ = state
            outs, new_carries = visit([(s, j, False)], {(s, h): carry_ref[s, h] for h in heads})
            for h in heads:
                acc_ref[s, h] += outs[0, h]
                carry_ref[s, h] = new_carries[s, h]
            return j + 1, jnp.max(carry_ref[s]) < STICK_GONE_LOG2

        lax.while_loop(more, visit_next, (jnp.int32(2), jnp.max(carry_ref[s]) < STICK_GONE_LOG2))
        o_ref[0, s * tq:(s + 1) * tq, :] = jnp.where(first_head, acc_ref[s, 0], acc_ref[s, 1]).astype(o_ref.dtype)


def _sb_attn(qkv, batch, seq):
    qkv3 = qkv.reshape(batch, seq, 3 * SB_WIDTH)
    tq = ATTN_TILE
    rows = ATTN_QUERY_BLOCKS * tq
    pairs = SB_WIDTH // LANES
    return pl.pallas_call(
        _sb_attn_kernel,
        grid=(batch, pairs, seq // rows),
        in_specs=[pl.BlockSpec((1, rows, LANES), lambda b, h, i: (b, i, h)),
                  pl.BlockSpec((1, seq, LANES), lambda b, h, i: (b, 0, pairs + h)),
                  pl.BlockSpec((1, seq, LANES), lambda b, h, i: (b, 0, 2 * pairs + h))],
        out_specs=pl.BlockSpec((1, rows, LANES), lambda b, h, i: (b, i, h)),
        out_shape=jax.ShapeDtypeStruct((batch, seq, SB_WIDTH), jnp.bfloat16),
        scratch_shapes=[pltpu.VMEM((ATTN_QUERY_BLOCKS, 2, tq, LANES), jnp.float32),
                        pltpu.VMEM((ATTN_QUERY_BLOCKS, 2, tq, LANES), jnp.float32)],
        compiler_params=pltpu.CompilerParams(vmem_limit_bytes=VMEM_LIMIT),
        name="sb_attn",
    )(qkv3, qkv3, qkv3)


def _route(logits):
    lane = lax.broadcasted_iota(jnp.int32, logits.shape, 1).astype(jnp.float32)
    neg = jnp.float32(-jnp.inf)
    big = jnp.float32(LANES)
    is_group = (lane >= N_EXPERTS) & (lane < N_EXPERTS + N_EXPERT_GROUPS)
    gl = jnp.where(is_group, logits, neg)
    g_max = jnp.max(gl, axis=-1, keepdims=True)
    g_sum = jnp.sum(jnp.exp(gl - g_max), axis=-1, keepdims=True)
    g_p = 1.0 / g_sum
    g_idx = jnp.min(jnp.where(gl == g_max, lane, big), axis=-1, keepdims=True) - N_EXPERTS
    in_group = (lane >= g_idx * EXPERTS_PER_GROUP) & (lane < (g_idx + 1) * EXPERTS_PER_GROUP)
    el = jnp.where(in_group, logits, neg)
    e1 = jnp.max(el, axis=-1, keepdims=True)
    i1 = jnp.min(jnp.where(el == e1, lane, big), axis=-1, keepdims=True)
    el2 = jnp.where(lane == i1, neg, el)
    e2 = jnp.max(el2, axis=-1, keepdims=True)
    i2 = jnp.min(jnp.where(el2 == e2, lane, big), axis=-1, keepdims=True)
    r = jnp.exp(e2 - e1)
    w1 = g_p / (1.0 + r)
    w2 = w1 * r
    return i1, i2, w1, w2


ROUTE_W1, ROUTE_W2, ROUTE_E1, ROUTE_E2, ROUTE_RANK1, ROUTE_RANK2 = range(6)


def _mix_ln1_kernel(steps_per_seq, osb_ref, p_ref, halo_ref, wpool_ref, pscale_ref, wout_ref, x_ref, mod_ref,
                    g1_ref, b1_ref, wr_ref, br_ref, earlier_ref, x1_ref, u2_ref, route_ref, route_t_ref, count_ref,
                    pext_ref, running_ref):
    tm = ROW_TILE
    streams = range(MIX_STREAMS)
    rows = lambda s: slice(s * tm, (s + 1) * tm)
    step_in_seq = pl.program_id(0) % steps_per_seq
    pext_ref[0:HALO, :] = jnp.where(step_in_seq == 0, 0.0, halo_ref[...])
    pext_ref[HALO:, :] = p_ref[...]

    mixer_outs = []
    for s in streams:
        pos = (step_in_seq * MIX_STREAMS + s) * tm + lax.broadcasted_iota(jnp.int32, (tm, 1), 0)
        first_row = HALO + s * tm
        mixer_out = [osb_ref[rows(s), :]]
        for g, w in enumerate(POOL_WINDOWS):
            cols = slice(g * POOL_GROUP_DIM, (g + 1) * POOL_GROUP_DIM)
            own = p_ref[rows(s), cols]
            win = own
            for i in range(1, w):
                win = win + pext_ref[first_row - i:first_row - i + tm, cols]
            count = jnp.minimum(pos + 1, w).astype(jnp.float32)
            pooled = win / count - own
            o_pool = jnp.dot(pooled.astype(jnp.bfloat16), wpool_ref[g], preferred_element_type=jnp.float32)
            mixer_out.append((o_pool * pscale_ref[:, cols]).astype(jnp.bfloat16))
        mixer_outs.append(jnp.concatenate(mixer_out, axis=1))
    mixed = [jnp.dot(m, wout_ref[...], preferred_element_type=jnp.float32) for m in mixer_outs]

    gate1 = mod_ref[0, 2:3, :]
    shift2 = mod_ref[0, 3:4, :]
    scale2 = mod_ref[0, 4:5, :]
    router_in = []
    for s in streams:
        x1 = _ln(DEEPNORM_ALPHA * x_ref[rows(s), :] + (1.0 + gate1) * mixed[s]) * g1_ref[...] + b1_ref[...]
        x1_ref[rows(s), :] = x1
        u2 = _ln(x1) * (1.0 + scale2) + shift2
        _store_tile_rows(u2_ref, s * tm * ROW_CHUNKS, u2)
        u2_hi = u2.astype(jnp.bfloat16)
        u2_lo = (u2 - u2_hi.astype(jnp.float32)).astype(jnp.bfloat16)
        router_in.append(jnp.concatenate([u2_hi, u2_lo, u2_hi], axis=1))
    logits = [jnp.dot(r, wr_ref[...], preferred_element_type=jnp.float32) + br_ref[...] for r in router_in]
    routed = [_route(l) for l in logits]

    @pl.when(pl.program_id(0) == 0)
    def _():
        running_ref[...] = jnp.zeros_like(running_ref)

    lane = lax.broadcasted_iota(jnp.int32, (tm, LANES), 1)
    lane_f = lane.astype(jnp.float32)
    picked = [(lane_f == i1, lane_f == i2) for i1, i2, _, _ in routed]
    uses = [jnp.where(first | second, 1.0, 0.0) for first, second in picked]
    within = [jnp.dot(earlier_ref[...], u.astype(jnp.bfloat16), preferred_element_type=jnp.float32) for u in uses]
    running = running_ref[0:1, :]
    for s in streams:
        i1, i2, w1, w2 = routed[s]
        first, second = picked[s]
        before = within[s] + running
        rank1 = jnp.sum(jnp.where(first, before, 0.0), axis=-1, keepdims=True)
        rank2 = jnp.sum(jnp.where(second, before, 0.0), axis=-1, keepdims=True)
        running = running + jnp.sum(uses[s], axis=0, keepdims=True)
        record = jnp.zeros((tm, LANES), jnp.float32)
        for slot, value in ((ROUTE_W1, w1), (ROUTE_W2, w2), (ROUTE_E1, i1), (ROUTE_E2, i2),
                            (ROUTE_RANK1, rank1), (ROUTE_RANK2, rank2)):
            record = jnp.where(lane == slot, value, record)
        route_ref[rows(s), :] = record
        route_t_ref[:, rows(s)] = record.T[0:SUBLANES, :]
    running_ref[...] = jnp.broadcast_to(running, running_ref.shape)
    count_ref[...] = jnp.broadcast_to(running, count_ref.shape)


def _mix_ln1(o_sb, p, w_pool_bf16, pool_scale, w_out_bf16, x2d, mod, ln1_g, ln1_b, w_router, b_router, seq):
    t, d = x2d.shape
    tm = MIX_STREAMS * ROW_TILE
    steps_per_seq = seq // tm
    halo_blocks_per_step = tm // HALO
    row = lambda i: (i, 0)
    const2 = lambda i: (0, 0)
    return pl.pallas_call(
        functools.partial(_mix_ln1_kernel, steps_per_seq),
        grid=(t // tm,),
        in_specs=[pl.BlockSpec((tm, SB_WIDTH), row),
                  pl.BlockSpec((tm, POOL_WIDTH), row),
                  pl.BlockSpec((HALO, POOL_WIDTH), lambda i: (jnp.maximum(i * halo_blocks_per_step - 1, 0), 0)),
                  pl.BlockSpec(w_pool_bf16.shape, lambda i: (0, 0, 0)),
                  pl.BlockSpec((1, POOL_WIDTH), const2),
                  pl.BlockSpec(w_out_bf16.shape, const2),
                  pl.BlockSpec((tm, d), row),
                  pl.BlockSpec((1, N_MOD, d), lambda i: (i // steps_per_seq, 0, 0)),
                  pl.BlockSpec((1, d), const2),
                  pl.BlockSpec((1, d), const2),
                  pl.BlockSpec(w_router.shape, const2),
                  pl.BlockSpec((1, LANES), const2),
                  pl.BlockSpec((ROW_TILE, ROW_TILE), const2)],
        out_specs=[pl.BlockSpec((tm, d), row),
                   pl.BlockSpec((tm * ROW_CHUNKS, LANES), row),
                   pl.BlockSpec((tm, LANES), row),
                   pl.BlockSpec((SUBLANES, tm), lambda i: (0, i)),
                   pl.BlockSpec((SUBLANES, LANES), const2)],
        out_shape=[jax.ShapeDtypeStruct((t, d), jnp.float32),
                   jax.ShapeDtypeStruct((t * ROW_CHUNKS, LANES), jnp.float32),
                   jax.ShapeDtypeStruct((t, LANES), jnp.float32),
                   jax.ShapeDtypeStruct((SUBLANES, t), jnp.float32),
                   jax.ShapeDtypeStruct((SUBLANES, LANES), jnp.float32)],
        scratch_shapes=[pltpu.VMEM((HALO + tm, POOL_WIDTH), jnp.float32),
                        pltpu.VMEM((SUBLANES, LANES), jnp.float32)],
        compiler_params=pltpu.CompilerParams(vmem_limit_bytes=VMEM_LIMIT,
                                             dimension_semantics=("arbitrary",)),
        name="mix_ln1",
    )(o_sb, p, p, w_pool_bf16, pool_scale, w_out_bf16, x2d, mod, ln1_g, ln1_b, w_router, b_router,
      jnp.tri(ROW_TILE, ROW_TILE, -1, dtype=jnp.bfloat16))


def _dispatch_kernel(tile_end_ref, pos_ref, u_ref, sorted_hbm, zeros_ref, sem):
    tm = u_ref.shape[0] // ROW_CHUNKS
    tile = MOE_ROW_TILE * ROW_CHUNKS
    stored = lambda r: pl.ds(pl.multiple_of(r * ROW_CHUNKS, ROW_CHUNKS), ROW_CHUNKS)

    @pl.when(pl.program_id(0) == 0)
    def _():
        zeros_ref[...] = jnp.zeros_like(zeros_ref)

        def tile_fill(i):
            return pltpu.make_async_copy(zeros_ref, sorted_hbm.at[pl.ds(pl.multiple_of(i * tile, tile), tile), :],
                                         sem)

        def last_tile_fills(act):
            for e in range(N_EXPERTS):
                first_tile = tile_end_ref[e - 1] if e else 0

                @pl.when(tile_end_ref[e] > first_tile)
                def _():
                    act(tile_fill(tile_end_ref[e] - 1))

        unused = (tile_end_ref[N_EXPERTS - 1], sorted_hbm.shape[0] // tile)
        last_tile_fills(lambda fill: fill.start())
        lax.fori_loop(*unused, lambda i, _: (tile_fill(i).start(), 0)[1], 0)
        last_tile_fills(lambda fill: fill.wait())
        lax.fori_loop(*unused, lambda i, _: (tile_fill(i).wait(), 0)[1], 0)

    def body(r, _):
        row = u_ref.at[stored(r), :]
        pltpu.make_async_copy(row, sorted_hbm.at[stored(pos_ref[0, 0, r]), :], sem).start()
        pltpu.make_async_copy(row, sorted_hbm.at[stored(pos_ref[0, 0, tm + r]), :], sem).start()
        return 0

    lax.fori_loop(0, tm, body, 0, unroll=GATHER_UNROLL // 2)
    for _ in range(2):
        pltpu.make_async_copy(u_ref, sorted_hbm.at[pl.ds(0, tm * ROW_CHUNKS), :], sem).wait()


def _dispatch(tile_end, pos, u2, n_rows):
    tm = pos.shape[2] // 2
    t = u2.shape[0] // ROW_CHUNKS
    grid_spec = pltpu.PrefetchScalarGridSpec(
        num_scalar_prefetch=1,
        grid=(t // tm,),
        in_specs=[pl.BlockSpec((1, 1, 2 * tm), lambda i, te: (i, 0, 0), memory_space=pltpu.SMEM),
                  pl.BlockSpec((tm * ROW_CHUNKS, LANES), lambda i, te: (i, 0))],
        out_specs=pl.BlockSpec(memory_space=pl.ANY),
        scratch_shapes=[pltpu.VMEM((MOE_ROW_TILE * ROW_CHUNKS, LANES), jnp.float32),
                        pltpu.SemaphoreType.DMA(())],
    )
    return pl.pallas_call(
        _dispatch_kernel,
        grid_spec=grid_spec,
        out_shape=jax.ShapeDtypeStruct((n_rows * ROW_CHUNKS, LANES), jnp.float32),
        compiler_params=pltpu.CompilerParams(vmem_limit_bytes=VMEM_LIMIT,
                                             dimension_semantics=("arbitrary",)),
        name="dispatch",
    )(tile_end, pos, u2)


def _moe_grouped_kernel(tile_expert_ref, n_tiles_ref, u_ref, wg_ref, wu_ref, wd_ref, y_ref,
                        wg_bf16_ref, wu_bf16_ref, wd_bf16_ref):
    i = pl.program_id(0)
    bf16 = jnp.bfloat16
    valid = i < n_tiles_ref[0]

    @pl.when(valid & ((i == 0) | (tile_expert_ref[i] != tile_expert_ref[jnp.maximum(i - 1, 0)])))
    def _():
        wg_bf16_ref[...] = wg_ref[0].astype(bf16)
        wu_bf16_ref[...] = wu_ref[0].astype(bf16)
        wd_bf16_ref[...] = wd_ref[0].astype(bf16)

    @pl.when(valid)
    def _():
        u = _load_tile_rows(u_ref, 0, MOE_ROW_TILE).astype(bf16)
        gate = jnp.dot(u, wg_bf16_ref[...], preferred_element_type=jnp.float32)
        up = jnp.dot(u, wu_bf16_ref[...], preferred_element_type=jnp.float32)
        h = gate * jax.nn.sigmoid(gate) * up
        _store_tile_rows(y_ref, 0, jnp.dot(h.astype(bf16), wd_bf16_ref[...], preferred_element_type=jnp.float32))

    @pl.when(jnp.logical_not(valid))
    def _():
        y_ref[...] = jnp.zeros_like(y_ref)


def _moe_grouped(tile_expert, n_tiles, u_sorted, wg, wu, wd):
    d = D_MODEL
    tm = MOE_ROW_TILE
    max_tiles = tile_expert.shape[0]
    expert = lambda i, te, nt: (te[i], 0, 0)
    grid_spec = pltpu.PrefetchScalarGridSpec(
        num_scalar_prefetch=2,
        grid=(max_tiles,),
        in_specs=[pl.BlockSpec((tm * ROW_CHUNKS, LANES), lambda i, te, nt: (jnp.minimum(i, nt[0] - 1), 0)),
                  pl.BlockSpec((1, d, EXPERT_HIDDEN), expert),
                  pl.BlockSpec((1, d, EXPERT_HIDDEN), expert),
                  pl.BlockSpec((1, EXPERT_HIDDEN, d), expert)],
        out_specs=pl.BlockSpec((tm * ROW_CHUNKS, LANES), lambda i, te, nt: (i, 0)),
        scratch_shapes=[pltpu.VMEM((d, EXPERT_HIDDEN), jnp.bfloat16),
                        pltpu.VMEM((d, EXPERT_HIDDEN), jnp.bfloat16),
                        pltpu.VMEM((EXPERT_HIDDEN, d), jnp.bfloat16)],
    )
    return pl.pallas_call(
        _moe_grouped_kernel,
        grid_spec=grid_spec,
        out_shape=jax.ShapeDtypeStruct((max_tiles * tm * ROW_CHUNKS, LANES), jnp.float32),
        compiler_params=pltpu.CompilerParams(vmem_limit_bytes=VMEM_LIMIT,
                                             dimension_semantics=("arbitrary",)),
        name="moe_grouped",
    )(tile_expert, n_tiles, u_sorted, wg, wu, wd)


def _combine_ln2_kernel(pos_ref, pos_next_ref, y_hbm, route_ref, x1_ref, mod_ref, g2_ref, b2_ref, o_ref,
                        buf_ref, sem_ref):
    i = pl.program_id(0)
    n = pl.num_programs(0)
    tm = o_ref.shape[0]
    rows = 2 * tm
    stored = lambda r: pl.ds(pl.multiple_of(r * ROW_CHUNKS, ROW_CHUNKS), ROW_CHUNKS)

    def start_gather(rows_ref, slot):
        def body(r, _):
            pltpu.make_async_copy(y_hbm.at[stored(rows_ref[0, 0, r]), :], buf_ref.at[slot, stored(r), :],
                                  sem_ref.at[slot]).start()
            return 0
        lax.fori_loop(0, rows, body, 0, unroll=GATHER_UNROLL)

    @pl.when(i == 0)
    def _():
        start_gather(pos_ref, 0)

    @pl.when(i + 1 < n)
    def _():
        start_gather(pos_next_ref, (i + 1) % 2)

    slot = i % 2
    pltpu.make_async_copy(y_hbm.at[pl.ds(0, rows * ROW_CHUNKS), :], buf_ref.at[slot], sem_ref.at[slot]).wait()
    route = route_ref[...]
    w1 = route[:, ROUTE_W1:ROUTE_W1 + 1]
    w2 = route[:, ROUTE_W2:ROUTE_W2 + 1]
    gathered = buf_ref.at[slot]
    y = w1 * _load_tile_rows(gathered, 0, tm) + w2 * _load_tile_rows(gathered, tm * ROW_CHUNKS, tm)
    gate2 = mod_ref[0, 5:6, :]
    o_ref[...] = _ln(DEEPNORM_ALPHA * x1_ref[...] + (1.0 + gate2) * y) * g2_ref[...] + b2_ref[...]


def _combine_ln2(pos, y_sorted, route, x1, mod, ln2_g, ln2_b, seq):
    t, d = x1.shape
    tm = COMBINE_ROW_TILE
    n = t // tm
    tiles_per_seq = seq // tm
    row = lambda i: (i, 0)
    const2 = lambda i: (0, 0)
    return pl.pallas_call(
        _combine_ln2_kernel,
        grid=(n,),
        in_specs=[pl.BlockSpec((1, 1, 2 * tm), lambda i: (i, 0, 0), memory_space=pltpu.SMEM),
                  pl.BlockSpec((1, 1, 2 * tm), lambda i: (jnp.minimum(i + 1, n - 1), 0, 0),
                               memory_space=pltpu.SMEM),
                  pl.BlockSpec(memory_space=pl.ANY),
                  pl.BlockSpec((tm, LANES), row),
                  pl.BlockSpec((tm, d), row),
                  pl.BlockSpec((1, N_MOD, d), lambda i: (i // tiles_per_seq, 0, 0)),
                  pl.BlockSpec((1, d), const2),
                  pl.BlockSpec((1, d), const2)],
        out_specs=pl.BlockSpec((tm, d), row),
        out_shape=jax.ShapeDtypeStruct((t, d), jnp.float32),
        scratch_shapes=[pltpu.VMEM((2, 2 * tm * ROW_CHUNKS, LANES), jnp.float32),
                        pltpu.SemaphoreType.DMA((2,))],
        compiler_params=pltpu.CompilerParams(vmem_limit_bytes=VMEM_LIMIT,
                                             dimension_semantics=("arbitrary",)),
        name="combine_ln2",
    )(pos, pos, y_sorted, route, x1, mod, ln2_g, ln2_b)


def _dispatch_plan(route_t, counts):
    t = route_t.shape[1]
    tm = MOE_ROW_TILE
    max_tiles = 2 * t // tm + N_EXPERTS
    counts = counts[0, :N_EXPERTS].astype(jnp.int32)
    tiles = (counts + tm - 1) // tm
    tile_end = jnp.cumsum(tiles)
    row_start = (tile_end - tiles) * tm
    experts = route_t[ROUTE_E1:ROUTE_E2 + 1].astype(jnp.int32)
    ranks = route_t[ROUTE_RANK1:ROUTE_RANK2 + 1].astype(jnp.int32)
    expert_ids = jnp.arange(N_EXPERTS, dtype=jnp.int32)[:, None, None]
    pos = jnp.sum(jnp.where(experts[None] == expert_ids, row_start[:, None, None], 0), axis=0) + ranks
    tile_ids = jnp.arange(max_tiles, dtype=jnp.int32)
    tile_expert = jnp.minimum(jnp.sum(tile_end[None, :] <= tile_ids[:, None], axis=1), N_EXPERTS - 1)
    by_tile = lambda tc: pos.reshape(2, t // tc, tc).transpose(1, 0, 2).reshape(t // tc, 1, 2 * tc)
    return (tile_expert.astype(jnp.int32), tile_end[-1:].astype(jnp.int32), tile_end.astype(jnp.int32),
            by_tile(DISPATCH_ROW_TILE), by_tile(COMBINE_ROW_TILE), max_tiles * tm)


def kernel(x, c, w_ada, b_ada, w_in, w_pool, pool_scale, w_out, ln1_g, ln1_b, w_router_group, b_router_group,
           w_router_expert, b_router_expert, w_gate, w_up, w_down, ln2_g, ln2_b):
    b, s, d = x.shape
    bf16 = jnp.bfloat16
    for layer in range(DEPTH):
        c_pad = jnp.pad(c, ((0, SUBLANES - b), (0, 0)))
        mod = _adaln(c_pad, w_ada[layer], b_ada[layer][None, :])[:b].reshape(b, N_MOD, d)
        x2d = x.reshape(b * s, d)
        qkv, p = _ln_inproj(x2d, mod, w_in[layer], s)
        o_sb = _sb_attn(qkv, b, s).reshape(b * s, SB_WIDTH)
        pad = LANES - N_EXPERTS - N_EXPERT_GROUPS
        w_router = jnp.pad(jnp.concatenate([w_router_expert[layer], w_router_group[layer]], axis=1),
                           ((0, 0), (0, pad)))
        b_router = jnp.pad(jnp.concatenate([b_router_expert[layer], b_router_group[layer]]), (0, pad))[None, :]
        w_router_hi = w_router.astype(bf16)
        w_router_lo = (w_router - w_router_hi.astype(jnp.float32)).astype(bf16)
        w_router = jnp.concatenate([w_router_hi, w_router_hi, w_router_lo], axis=0)
        x1, u2, route, route_t, counts = _mix_ln1(o_sb, p, w_pool[layer].astype(bf16), pool_scale[layer][None, :],
                                         w_out[layer].astype(bf16), x2d, mod, ln1_g[layer][None, :],
                                         ln1_b[layer][None, :], w_router, b_router, s)
        tile_expert, n_tiles, tile_end, pos_dispatch, pos, n_rows = _dispatch_plan(route_t, counts)
        u_sorted = _dispatch(tile_end, pos_dispatch, u2, n_rows)
        y_sorted = _moe_grouped(tile_expert, n_tiles, u_sorted, w_gate[layer], w_up[layer], w_down[layer])
        x2 = _combine_ln2(pos, y_sorted, route, x1, mod, ln2_g[layer][None, :], ln2_b[layer][None, :], s)
        x = x2.reshape(b, s, d)
    return x
```

```python
import functools

import jax
import jax.numpy as jnp
from jax import lax
from jax.experimental import pallas as pl
from jax.experimental.pallas import tpu as pltpu

D_MODEL = 1024
N_SB_HEADS = 8
SB_HEAD_DIM = 64
SB_WIDTH = N_SB_HEADS * SB_HEAD_DIM
POOL_WINDOWS = (2, 4, 8, 16)
POOL_GROUP_DIM = 128
POOL_WIDTH = len(POOL_WINDOWS) * POOL_GROUP_DIM
N_EXPERT_GROUPS = 4
EXPERTS_PER_GROUP = 4
N_EXPERTS = N_EXPERT_GROUPS * EXPERTS_PER_GROUP
EXPERT_HIDDEN = 512
DEPTH = 1
DEEPNORM_ALPHA = (2.0 * DEPTH) ** 0.25
LN_EPS = 1e-5
N_MOD = 6

LANES = 128
SUBLANES = 8
HALO = max(POOL_WINDOWS)
VMEM_LIMIT = 56 * 1024 * 1024

ROW_TILE = 512
MIX_STREAMS = 2
ATTN_TILE = 256
ATTN_QUERY_BLOCKS = 2
STICK_GONE_LOG2 = -180.0
MOE_ROW_TILE = 512
DISPATCH_ROW_TILE = 2048
COMBINE_ROW_TILE = 256
GATHER_UNROLL = 8


ROW_CHUNKS = D_MODEL // LANES


def _store_tile_rows(ref, base, value):
    n = value.shape[0]
    for c in range(ROW_CHUNKS):
        ref[pl.ds(base + c, n, stride=ROW_CHUNKS), :] = value[:, c * LANES:(c + 1) * LANES]


def _load_tile_rows(ref, base, n):
    return jnp.concatenate([ref[pl.ds(base + c, n, stride=ROW_CHUNKS), :] for c in range(ROW_CHUNKS)], axis=1)


def _ln(x):
    mu = jnp.mean(x, axis=-1, keepdims=True)
    xc = x - mu
    var = jnp.mean(xc * xc, axis=-1, keepdims=True)
    return xc * lax.rsqrt(var + LN_EPS)


def _adaln_kernel(c_ref, w_ref, b_ref, o_ref):
    c = c_ref[...]
    a = c * jax.nn.sigmoid(c)
    o_ref[...] = jnp.dot(a, w_ref[...], preferred_element_type=jnp.float32,
                         precision=lax.Precision.HIGHEST) + b_ref[...]


def _adaln(c_pad, w_ada, b_ada):
    rows, d = c_pad.shape
    n = w_ada.shape[1]
    tn = 1024
    return pl.pallas_call(
        _adaln_kernel,
        grid=(n // tn,),
        in_specs=[pl.BlockSpec((rows, d), lambda j: (0, 0)),
                  pl.BlockSpec((d, tn), lambda j: (0, j)),
                  pl.BlockSpec((1, tn), lambda j: (0, j))],
        out_specs=pl.BlockSpec((rows, tn), lambda j: (0, j)),
        out_shape=jax.ShapeDtypeStruct((rows, n), jnp.float32),
        compiler_params=pltpu.CompilerParams(vmem_limit_bytes=VMEM_LIMIT),
        name="adaln",
    )(c_pad, w_ada, b_ada)


def _ln_inproj_kernel(x_ref, mod_ref, w_ref, qkv_ref, p_ref, w_bf16_ref):
    @pl.when(pl.program_id(0) == 0)
    def _():
        w_bf16_ref[...] = w_ref[...].astype(jnp.bfloat16)

    tm = ROW_TILE
    rows = lambda s: slice(s * tm, (s + 1) * tm)
    shift = mod_ref[0, 0:1, :]
    scale = mod_ref[0, 1:2, :]
    us = [(_ln(x_ref[rows(s), :]) * (1.0 + scale) + shift).astype(jnp.bfloat16) for s in range(MIX_STREAMS)]
    qk_scale = SB_HEAD_DIM ** -0.5 * 1.4426950408889634
    for s, u in enumerate(us):
        q = jnp.dot(u, w_bf16_ref[:, 0:SB_WIDTH], preferred_element_type=jnp.float32) * qk_scale
        qkv_ref[rows(s), 0:SB_WIDTH] = q.astype(jnp.bfloat16)
        for j in (1, 2):
            kv = jnp.dot(u, w_bf16_ref[:, j * SB_WIDTH:(j + 1) * SB_WIDTH], preferred_element_type=jnp.float32)
            qkv_ref[rows(s), j * SB_WIDTH:(j + 1) * SB_WIDTH] = kv.astype(jnp.bfloat16)
        p_ref[rows(s), :] = jnp.dot(u, w_bf16_ref[:, 3 * SB_WIDTH:], preferred_element_type=jnp.float32)


def _ln_inproj(x2d, mod, w_in, seq):
    t, d = x2d.shape
    tm = MIX_STREAMS * ROW_TILE
    steps_per_seq = seq // tm
    return pl.pallas_call(
        _ln_inproj_kernel,
        grid=(t // tm,),
        in_specs=[pl.BlockSpec((tm, d), lambda i: (i, 0)),
                  pl.BlockSpec((1, N_MOD, d), lambda i: (i // steps_per_seq, 0, 0)),
                  pl.BlockSpec(w_in.shape, lambda i: (0, 0))],
        out_specs=[pl.BlockSpec((tm, 3 * SB_WIDTH), lambda i: (i, 0)),
                   pl.BlockSpec((tm, POOL_WIDTH), lambda i: (i, 0))],
        out_shape=[jax.ShapeDtypeStruct((t, 3 * SB_WIDTH), jnp.bfloat16),
                   jax.ShapeDtypeStruct((t, POOL_WIDTH), jnp.float32)],
        scratch_shapes=[pltpu.VMEM(w_in.shape, jnp.bfloat16)],
        compiler_params=pltpu.CompilerParams(vmem_limit_bytes=VMEM_LIMIT,
                                             dimension_semantics=("arbitrary",)),
        name="ln_inproj",
    )(x2d, mod, w_in)


def _sb_attn_kernel(q_ref, k_ref, v_ref, o_ref, acc_ref, carry_ref):
    tq = tk = ATTN_TILE
    heads = (0, 1)
    subs = tuple(range(ATTN_QUERY_BLOCKS))
    qis = [pl.program_id(2) * ATTN_QUERY_BLOCKS + s for s in subs]
    lane = lax.broadcasted_iota(jnp.int32, (tq, LANES), 1)
    first_head = lane < SB_HEAD_DIM
    q_heads = {}
    for s in subs:
        q2 = q_ref[0, s * tq:(s + 1) * tq, :]
        zero = jnp.zeros_like(q2)
        q_heads[s, 0], q_heads[s, 1] = jnp.where(first_head, q2, zero), jnp.where(first_head, zero, q2)

    row = lax.broadcasted_iota(jnp.int32, (tk, tk), 0)
    col = lax.broadcasted_iota(jnp.int32, (tk, tk), 1)
    neg_suffix = jnp.where(row >= col, -1.0, 0.0).astype(jnp.bfloat16)
    causal = col < row

    def visit(blocks, carries):
        k_blks, v_blks = [], []
        for s, j, _ in blocks:
            start = pl.multiple_of(jnp.maximum(qis[s] - j, 0) * tk, tk)
            k_blks.append(k_ref[0, pl.ds(start, tk), :])
            v_blks.append(v_ref[0, pl.ds(start, tk), :])
        pairs = [(b, h) for b in range(len(blocks)) for h in heads]
        zs = {(b, h): lax.dot_general(q_heads[blocks[b][0], h], k_blks[b], (((1,), (1,)), ((), ())),
                                      preferred_element_type=jnp.float32) for b, h in pairs}
        fixed, rounded = {}, {}
        for b, h in pairs:
            z = zs[b, h]
            n = jnp.maximum(z, 0.0) + jnp.log2(1.0 + jnp.exp2(-jnp.abs(z)))
            if blocks[b][2]:
                n = jnp.where(causal, n, 0.0)
            n_bf16 = n.astype(jnp.bfloat16)
            rounded[b, h] = n_bf16
            fixed[b, h] = z - (n - n_bf16.astype(jnp.float32))
        incls = {bh: jnp.dot(rounded[bh], neg_suffix, preferred_element_type=jnp.float32) for bh in pairs}
        atts = {}
        carries = dict(carries)
        for b, h in pairs:
            s = blocks[b][0]
            carry = carries[s, h]
            block_sum = jnp.broadcast_to(incls[b, h][:, 0:1], (tq, LANES))
            expo = fixed[b, h] + incls[b, h]
            if carry is None:
                carries[s, h] = block_sum
            else:
                expo = expo + jnp.concatenate([carry] * (tk // LANES), axis=1)
                carries[s, h] = carry + block_sum
            att = jnp.exp2(expo)
            if blocks[b][2]:
                att = jnp.where(causal, att, 0.0)
            atts[b, h] = att.astype(jnp.bfloat16)
        outs = {(b, h): jnp.dot(atts[b, h], v_blks[b], preferred_element_type=jnp.float32) for b, h in pairs}
        return outs, carries

    head_blocks = [(s, 0, True) for s in subs] + [(s, 1, False) for s in subs]
    outs, carries = visit(head_blocks, {(s, h): None for s in subs for h in heads})
    for s in subs:
        has_second = qis[s] >= 1
        for h in heads:
            acc_ref[s, h] = outs[s, h] + jnp.where(has_second, outs[len(subs) + s, h], 0.0)
            carry_ref[s, h] = carries[s, h]

    for s in subs:
        def more(state, s=s):
            j, stick_gone = state
            return (j <= qis[s]) & jnp.logical_not(stick_gone)

        def visit_next(state, s=s):
            j, _ = state
            outs, new_carries = visit([(s, j, False)], {(s, h): carry_ref[s, h] for h in heads})
            for h in heads:
                acc_ref[s, h] += outs[0, h]
                carry_ref[s, h] = new_carries[s, h]
            return j + 1, jnp.max(carry_ref[s]) < STICK_GONE_LOG2

        lax.while_loop(more, visit_next, (jnp.int32(2), jnp.max(carry_ref[s]) < STICK_GONE_LOG2))
        o_ref[0, s * tq:(s + 1) * tq, :] = jnp.where(first_head, acc_ref[s, 0], acc_ref[s, 1]).astype(o_ref.dtype)


def _sb_attn(qkv, batch, seq):
    qkv3 = qkv.reshape(batch, seq, 3 * SB_WIDTH)
    tq = ATTN_TILE
    rows = ATTN_QUERY_BLOCKS * tq
    pairs = SB_WIDTH // LANES
    return pl.pallas_call(
        _sb_attn_kernel,
        grid=(batch, pairs, seq // rows),
        in_specs=[pl.BlockSpec((1, rows, LANES), lambda b, h, i: (b, i, h)),
                  pl.BlockSpec((1, seq, LANES), lambda b, h, i: (b, 0, pairs + h)),
                  pl.BlockSpec((1, seq, LANES), lambda b, h, i: (b, 0, 2 * pairs + h))],
        out_specs=pl.BlockSpec((1, rows, LANES), lambda b, h, i: (b, i, h)),
        out_shape=jax.ShapeDtypeStruct((batch, seq, SB_WIDTH), jnp.bfloat16),
        scratch_shapes=[pltpu.VMEM((ATTN_QUERY_BLOCKS, 2, tq, LANES), jnp.float32),
                        pltpu.VMEM((ATTN_QUERY_BLOCKS, 2, tq, LANES), jnp.float32)],
        compiler_params=pltpu.CompilerParams(vmem_limit_bytes=VMEM_LIMIT),
        name="sb_attn",
    )(qkv3, qkv3, qkv3)


def _route(logits):
    lane = lax.broadcasted_iota(jnp.int32, logits.shape, 1).astype(jnp.float32)
    neg = jnp.float32(-jnp.inf)
    big = jnp.float32(LANES)
    is_group = (lane >= N_EXPERTS) & (lane < N_EXPERTS + N_EXPERT_GROUPS)
    gl = jnp.where(is_group, logits, neg)
    g_max = jnp.max(gl, axis=-1, keepdims=True)
    g_sum = jnp.sum(jnp.exp(gl - g_max), axis=-1, keepdims=True)
    g_p = 1.0 / g_sum
    g_idx = jnp.min(jnp.where(gl == g_max, lane, big), axis=-1, keepdims=True) - N_EXPERTS
    in_group = (lane >= g_idx * EXPERTS_PER_GROUP) & (lane < (g_idx + 1) * EXPERTS_PER_GROUP)
    el = jnp.where(in_group, logits, neg)
    e1 = jnp.max(el, axis=-1, keepdims=True)
    i1 = jnp.min(jnp.where(el == e1, lane, big), axis=-1, keepdims=True)
    el2 = jnp.where(lane == i1, neg, el)
    e2 = jnp.max(el2, axis=-1, keepdims=True)
    i2 = jnp.min(jnp.where(el2 == e2, lane, big), axis=-1, keepdims=True)
    r = jnp.exp(e2 - e1)
    w1 = g_p / (1.0 + r)
    w2 = w1 * r
    return i1, i2, w1, w2


ROUTE_W1, ROUTE_W2, ROUTE_E1, ROUTE_E2, ROUTE_RANK1, ROUTE_RANK2 = range(6)


def _mix_ln1_kernel(steps_per_seq, osb_ref, p_ref, halo_ref, wpool_ref, pscale_ref, wout_ref, x_ref, mod_ref,
                    g1_ref, b1_ref, wr_ref, br_ref, earlier_ref, x1_ref, u2_ref, route_ref, route_t_ref, count_ref,
                    pext_ref, running_ref):
    tm = ROW_TILE
    streams = range(MIX_STREAMS)
    rows = lambda s: slice(s * tm, (s + 1) * tm)
    step_in_seq = pl.program_id(0) % steps_per_seq
    pext_ref[0:HALO, :] = jnp.where(step_in_seq == 0, 0.0, halo_ref[...])
    pext_ref[HALO:, :] = p_ref[...]

    mixer_outs = []
    for s in streams:
        pos = (step_in_seq * MIX_STREAMS + s) * tm + lax.broadcasted_iota(jnp.int32, (tm, 1), 0)
        first_row = HALO + s * tm
        mixer_out = [osb_ref[rows(s), :]]
        for g, w in enumerate(POOL_WINDOWS):
            cols = slice(g * POOL_GROUP_DIM, (g + 1) * POOL_GROUP_DIM)
            own = p_ref[rows(s), cols]
            win = own
            for i in range(1, w):
                win = win + pext_ref[first_row - i:first_row - i + tm, cols]
            count = jnp.minimum(pos + 1, w).astype(jnp.float32)
            pooled = win / count - own
            o_pool = jnp.dot(pooled.astype(jnp.bfloat16), wpool_ref[g], preferred_element_type=jnp.float32)
            mixer_out.append((o_pool * pscale_ref[:, cols]).astype(jnp.bfloat16))
        mixer_outs.append(jnp.concatenate(mixer_out, axis=1))
    mixed = [jnp.dot(m, wout_ref[...], preferred_element_type=jnp.float32) for m in mixer_outs]

    gate1 = mod_ref[0, 2:3, :]
    shift2 = mod_ref[0, 3:4, :]
    scale2 = mod_ref[0, 4:5, :]
    router_in = []
    for s in streams:
        x1 = _ln(DEEPNORM_ALPHA * x_ref[rows(s), :] + (1.0 + gate1) * mixed[s]) * g1_ref[...] + b1_ref[...]
        x1_ref[rows(s), :] = x1
        u2 = _ln(x1) * (1.0 + scale2) + shift2
        _store_tile_rows(u2_ref, s * tm * ROW_CHUNKS, u2)
        u2_hi = u2.astype(jnp.bfloat16)
        u2_lo = (u2 - u2_hi.astype(jnp.float32)).astype(jnp.bfloat16)
        router_in.append(jnp.concatenate([u2_hi, u2_lo, u2_hi], axis=1))
    logits = [jnp.dot(r, wr_ref[...], preferred_element_type=jnp.float32) + br_ref[...] for r in router_in]
    routed = [_route(l) for l in logits]

    @pl.when(pl.program_id(0) == 0)
    def _():
        running_ref[...] = jnp.zeros_like(running_ref)

    lane = lax.broadcasted_iota(jnp.int32, (tm, LANES), 1)
    lane_f = lane.astype(jnp.float32)
    picked = [(lane_f == i1, lane_f == i2) for i1, i2, _, _ in routed]
    uses = [jnp.where(first | second, 1.0, 0.0) for first, second in picked]
    within = [jnp.dot(earlier_ref[...], u.astype(jnp.bfloat16), preferred_element_type=jnp.float32) for u in uses]
    running = running_ref[0:1, :]
    for s in streams:
        i1, i2, w1, w2 = routed[s]
        first, second = picked[s]
        before = within[s] + running
        rank1 = jnp.sum(jnp.where(first, before, 0.0), axis=-1, keepdims=True)
        rank2 = jnp.sum(jnp.where(second, before, 0.0), axis=-1, keepdims=True)
        running = running + jnp.sum(uses[s], axis=0, keepdims=True)
        record = jnp.zeros((tm, LANES), jnp.float32)
        for slot, value in ((ROUTE_W1, w1), (ROUTE_W2, w2), (ROUTE_E1, i1), (ROUTE_E2, i2),
                            (ROUTE_RANK1, rank1), (ROUTE_RANK2, rank2)):
            record = jnp.where(lane == slot, value, record)
        route_ref[rows(s), :] = record
        route_t_ref[:, rows(s)] = record.T[0:SUBLANES, :]
    running_ref[...] = jnp.broadcast_to(running, running_ref.shape)
    count_ref[...] = jnp.broadcast_to(running, count_ref.shape)


def _mix_ln1(o_sb, p, w_pool_bf16, pool_scale, w_out_bf16, x2d, mod, ln1_g, ln1_b, w_router, b_router, seq):
    t, d = x2d.shape
    tm = MIX_STREAMS * ROW_TILE
    steps_per_seq = seq // tm
    halo_blocks_per_step = tm // HALO
    row = lambda i: (i, 0)
    const2 = lambda i: (0, 0)
    return pl.pallas_call(
        functools.partial(_mix_ln1_kernel, steps_per_seq),
        grid=(t // tm,),
        in_specs=[pl.BlockSpec((tm, SB_WIDTH), row),
                  pl.BlockSpec((tm, POOL_WIDTH), row),
                  pl.BlockSpec((HALO, POOL_WIDTH), lambda i: (jnp.maximum(i * halo_blocks_per_step - 1, 0), 0)),
                  pl.BlockSpec(w_pool_bf16.shape, lambda i: (0, 0, 0)),
                  pl.BlockSpec((1, POOL_WIDTH), const2),
                  pl.BlockSpec(w_out_bf16.shape, const2),
                  pl.BlockSpec((tm, d), row),
                  pl.BlockSpec((1, N_MOD, d), lambda i: (i // steps_per_seq, 0, 0)),
                  pl.BlockSpec((1, d), const2),
                  pl.BlockSpec((1, d), const2),
                  pl.BlockSpec(w_router.shape, const2),
                  pl.BlockSpec((1, LANES), const2),
                  pl.BlockSpec((ROW_TILE, ROW_TILE), const2)],
        out_specs=[pl.BlockSpec((tm, d), row),
                   pl.BlockSpec((tm * ROW_CHUNKS, LANES), row),
                   pl.BlockSpec((tm, LANES), row),
                   pl.BlockSpec((SUBLANES, tm), lambda i: (0, i)),
                   pl.BlockSpec((SUBLANES, LANES), const2)],
        out_shape=[jax.ShapeDtypeStruct((t, d), jnp.float32),
                   jax.ShapeDtypeStruct((t * ROW_CHUNKS, LANES), jnp.float32),
                   jax.ShapeDtypeStruct((t, LANES), jnp.float32),
                   jax.ShapeDtypeStruct((SUBLANES, t), jnp.float32),
                   jax.ShapeDtypeStruct((SUBLANES, LANES), jnp.float32)],
        scratch_shapes=[pltpu.VMEM((HALO + tm, POOL_WIDTH), jnp.float32),
                        pltpu.VMEM((SUBLANES, LANES), jnp.float32)],
        compiler_params=pltpu.CompilerParams(vmem_limit_bytes=VMEM_LIMIT,
                                             dimension_semantics=("arbitrary",)),
        name="mix_ln1",
    )(o_sb, p, p, w_pool_bf16, pool_scale, w_out_bf16, x2d, mod, ln1_g, ln1_b, w_router, b_router,
      jnp.tri(ROW_TILE, ROW_TILE, -1, dtype=jnp.bfloat16))


def _dispatch_kernel(tile_end_ref, pos_ref, u_ref, sorted_hbm, zeros_ref, sem):
    tm = u_ref.shape[0] // ROW_CHUNKS
    tile = MOE_ROW_TILE * ROW_CHUNKS
    stored = lambda r: pl.ds(pl.multiple_of(r * ROW_CHUNKS, ROW_CHUNKS), ROW_CHUNKS)

    @pl.when(pl.program_id(0) == 0)
    def _():
        zeros_ref[...] = jnp.zeros_like(zeros_ref)

        def tile_fill(i):
            return pltpu.make_async_copy(zeros_ref, sorted_hbm.at[pl.ds(pl.multiple_of(i * tile, tile), tile), :],
                                         sem)

        def last_tile_fills(act):
            for e in range(N_EXPERTS):
                first_tile = tile_end_ref[e - 1] if e else 0

                @pl.when(tile_end_ref[e] > first_tile)
                def _():
                    act(tile_fill(tile_end_ref[e] - 1))

        unused = (tile_end_ref[N_EXPERTS - 1], sorted_hbm.shape[0] // tile)
        last_tile_fills(lambda fill: fill.start())
        lax.fori_loop(*unused, lambda i, _: (tile_fill(i).start(), 0)[1], 0)
        last_tile_fills(lambda fill: fill.wait())
        lax.fori_loop(*unused, lambda i, _: (tile_fill(i).wait(), 0)[1], 0)

    def body(r, _):
        row = u_ref.at[stored(r), :]
        pltpu.make_async_copy(row, sorted_hbm.at[stored(pos_ref[0, 0, r]), :], sem).start()
        pltpu.make_async_copy(row, sorted_hbm.at[stored(pos_ref[0, 0, tm + r]), :], sem).start()
        return 0

    lax.fori_loop(0, tm, body, 0, unroll=GATHER_UNROLL // 2)
    for _ in range(2):
        pltpu.make_async_copy(u_ref, sorted_hbm.at[pl.ds(0, tm * ROW_CHUNKS), :], sem).wait()


def _dispatch(tile_end, pos, u2, n_rows):
    tm = pos.shape[2] // 2
    t = u2.shape[0] // ROW_CHUNKS
    grid_spec = pltpu.PrefetchScalarGridSpec(
        num_scalar_prefetch=1,
        grid=(t // tm,),
        in_specs=[pl.BlockSpec((1, 1, 2 * tm), lambda i, te: (i, 0, 0), memory_space=pltpu.SMEM),
                  pl.BlockSpec((tm * ROW_CHUNKS, LANES), lambda i, te: (i, 0))],
        out_specs=pl.BlockSpec(memory_space=pl.ANY),
        scratch_shapes=[pltpu.VMEM((MOE_ROW_TILE * ROW_CHUNKS, LANES), jnp.float32),
                        pltpu.SemaphoreType.DMA(())],
    )
    return pl.pallas_call(
        _dispatch_kernel,
        grid_spec=grid_spec,
        out_shape=jax.ShapeDtypeStruct((n_rows * ROW_CHUNKS, LANES), jnp.float32),
        compiler_params=pltpu.CompilerParams(vmem_limit_bytes=VMEM_LIMIT,
                                             dimension_semantics=("arbitrary",)),
        name="dispatch",
    )(tile_end, pos, u2)


def _moe_grouped_kernel(tile_expert_ref, n_tiles_ref, u_ref, wg_ref, wu_ref, wd_ref, y_ref,
                        wg_bf16_ref, wu_bf16_ref, wd_bf16_ref):
    i = pl.program_id(0)
    bf16 = jnp.bfloat16
    valid = i < n_tiles_ref[0]

    @pl.when(valid & ((i == 0) | (tile_expert_ref[i] != tile_expert_ref[jnp.maximum(i - 1, 0)])))
    def _():
        wg_bf16_ref[...] = wg_ref[0].astype(bf16)
        wu_bf16_ref[...] = wu_ref[0].astype(bf16)
        wd_bf16_ref[...] = wd_ref[0].astype(bf16)

    @pl.when(valid)
    def _():
        u = _load_tile_rows(u_ref, 0, MOE_ROW_TILE).astype(bf16)
        gate = jnp.dot(u, wg_bf16_ref[...], preferred_element_type=jnp.float32)
        up = jnp.dot(u, wu_bf16_ref[...], preferred_element_type=jnp.float32)
        h = gate * jax.nn.sigmoid(gate) * up
        _store_tile_rows(y_ref, 0, jnp.dot(h.astype(bf16), wd_bf16_ref[...], preferred_element_type=jnp.float32))

    @pl.when(jnp.logical_not(valid))
    def _():
        y_ref[...] = jnp.zeros_like(y_ref)


def _moe_grouped(tile_expert, n_tiles, u_sorted, wg, wu, wd):
    d = D_MODEL
    tm = MOE_ROW_TILE
    max_tiles = tile_expert.shape[0]
    expert = lambda i, te, nt: (te[i], 0, 0)
    grid_spec = pltpu.PrefetchScalarGridSpec(
        num_scalar_prefetch=2,
        grid=(max_tiles,),
        in_specs=[pl.BlockSpec((tm * ROW_CHUNKS, LANES), lambda i, te, nt: (jnp.minimum(i, nt[0] - 1), 0)),
                  pl.BlockSpec((1, d, EXPERT_HIDDEN), expert),
                  pl.BlockSpec((1, d, EXPERT_HIDDEN), expert),
                  pl.BlockSpec((1, EXPERT_HIDDEN, d), expert)],
        out_specs=pl.BlockSpec((tm * ROW_CHUNKS, LANES), lambda i, te, nt: (i, 0)),
        scratch_shapes=[pltpu.VMEM((d, EXPERT_HIDDEN), jnp.bfloat16),
                        pltpu.VMEM((d, EXPERT_HIDDEN), jnp.bfloat16),
                        pltpu.VMEM((EXPERT_HIDDEN, d), jnp.bfloat16)],
    )
    return pl.pallas_call(
        _moe_grouped_kernel,
        grid_spec=grid_spec,
        out_shape=jax.ShapeDtypeStruct((max_tiles * tm * ROW_CHUNKS, LANES), jnp.float32),
        compiler_params=pltpu.CompilerParams(vmem_limit_bytes=VMEM_LIMIT,
                                             dimension_semantics=("arbitrary",)),
        name="moe_grouped",
    )(tile_expert, n_tiles, u_sorted, wg, wu, wd)


def _combine_ln2_kernel(pos_ref, pos_next_ref, y_hbm, route_ref, x1_ref, mod_ref, g2_ref, b2_ref, o_ref,
                        buf_ref, sem_ref):
    i = pl.program_id(0)
    n = pl.num_programs(0)
    tm = o_ref.shape[0]
    rows = 2 * tm
    stored = lambda r: pl.ds(pl.multiple_of(r * ROW_CHUNKS, ROW_CHUNKS), ROW_CHUNKS)

    def start_gather(rows_ref, slot):
        def body(r, _):
            pltpu.make_async_copy(y_hbm.at[stored(rows_ref[0, 0, r]), :], buf_ref.at[slot, stored(r), :],
                                  sem_ref.at[slot]).start()
            return 0
        lax.fori_loop(0, rows, body, 0, unroll=GATHER_UNROLL)

    @pl.when(i == 0)
    def _():
        start_gather(pos_ref, 0)

    @pl.when(i + 1 < n)
    def _():
        start_gather(pos_next_ref, (i + 1) % 2)

    slot = i % 2
    pltpu.make_async_copy(y_hbm.at[pl.ds(0, rows * ROW_CHUNKS), :], buf_ref.at[slot], sem_ref.at[slot]).wait()
    route = route_ref[...]
    w1 = route[:, ROUTE_W1:ROUTE_W1 + 1]
    w2 = route[:, ROUTE_W2:ROUTE_W2 + 1]
    gathered = buf_ref.at[slot]
    y = w1 * _load_tile_rows(gathered, 0, tm) + w2 * _load_tile_rows(gathered, tm * ROW_CHUNKS, tm)
    gate2 = mod_ref[0, 5:6, :]
    o_ref[...] = _ln(DEEPNORM_ALPHA * x1_ref[...] + (1.0 + gate2) * y) * g2_ref[...] + b2_ref[...]


def _combine_ln2(pos, y_sorted, route, x1, mod, ln2_g, ln2_b, seq):
    t, d = x1.shape
    tm = COMBINE_ROW_TILE
    n = t // tm
    tiles_per_seq = seq // tm
    row = lambda i: (i, 0)
    const2 = lambda i: (0, 0)
    return pl.pallas_call(
        _combine_ln2_kernel,
        grid=(n,),
        in_specs=[pl.BlockSpec((1, 1, 2 * tm), lambda i: (i, 0, 0), memory_space=pltpu.SMEM),
                  pl.BlockSpec((1, 1, 2 * tm), lambda i: (jnp.minimum(i + 1, n - 1), 0, 0),
                               memory_space=pltpu.SMEM),
                  pl.BlockSpec(memory_space=pl.ANY),
                  pl.BlockSpec((tm, LANES), row),
                  pl.BlockSpec((tm, d), row),
                  pl.BlockSpec((1, N_MOD, d), lambda i: (i // tiles_per_seq, 0, 0)),
                  pl.BlockSpec((1, d), const2),
                  pl.BlockSpec((1, d), const2)],
        out_specs=pl.BlockSpec((tm, d), row),
        out_shape=jax.ShapeDtypeStruct((t, d), jnp.float32),
        scratch_shapes=[pltpu.VMEM((2, 2 * tm * ROW_CHUNKS, LANES), jnp.float32),
                        pltpu.SemaphoreType.DMA((2,))],
        compiler_params=pltpu.CompilerParams(vmem_limit_bytes=VMEM_LIMIT,
                                             dimension_semantics=("arbitrary",)),
        name="combine_ln2",
    )(pos, pos, y_sorted, route, x1, mod, ln2_g, ln2_b)


def _dispatch_plan(route_t, counts):
    t = route_t.shape[1]
    tm = MOE_ROW_TILE
    max_tiles = 2 * t // tm + N_EXPERTS
    counts = counts[0, :N_EXPERTS].astype(jnp.int32)
    tiles = (counts + tm - 1) // tm
    tile_end = jnp.cumsum(tiles)
    row_start = (tile_end - tiles) * tm
    experts = route_t[ROUTE_E1:ROUTE_E2 + 1].astype(jnp.int32)
    ranks = route_t[ROUTE_RANK1:ROUTE_RANK2 + 1].astype(jnp.int32)
    expert_ids = jnp.arange(N_EXPERTS, dtype=jnp.int32)[:, None, None]
    pos = jnp.sum(jnp.where(experts[None] == expert_ids, row_start[:, None, None], 0), axis=0) + ranks
    tile_ids = jnp.arange(max_tiles, dtype=jnp.int32)
    tile_expert = jnp.minimum(jnp.sum(tile_end[None, :] <= tile_ids[:, None], axis=1), N_EXPERTS - 1)
    by_tile = lambda tc: pos.reshape(2, t // tc, tc).transpose(1, 0, 2).reshape(t // tc, 1, 2 * tc)
    return (tile_expert.astype(jnp.int32), tile_end[-1:].astype(jnp.int32), tile_end.astype(jnp.int32),
            by_tile(DISPATCH_ROW_TILE), by_tile(COMBINE_ROW_TILE), max_tiles * tm)


def kernel(x, c, w_ada, b_ada, w_in, w_pool, pool_scale, w_out, ln1_g, ln1_b, w_router_group, b_router_group,
           w_router_expert, b_router_expert, w_gate, w_up, w_down, ln2_g, ln2_b):
    b, s, d = x.shape
    bf16 = jnp.bfloat16
    for layer in range(DEPTH):
        c_pad = jnp.pad(c, ((0, SUBLANES - b), (0, 0)))
        mod = _adaln(c_pad, w_ada[layer], b_ada[layer][None, :])[:b].reshape(b, N_MOD, d)
        x2d = x.reshape(b * s, d)
        qkv, p = _ln_inproj(x2d, mod, w_in[layer], s)
        o_sb = _sb_attn(qkv, b, s).reshape(b * s, SB_WIDTH)
        pad = LANES - N_EXPERTS - N_EXPERT_GROUPS
        w_router = jnp.pad(jnp.concatenate([w_router_expert[layer], w_router_group[layer]], axis=1),
                           ((0, 0), (0, pad)))
        b_router = jnp.pad(jnp.concatenate([b_router_expert[layer], b_router_group[layer]]), (0, pad))[None, :]
        w_router_hi = w_router.astype(bf16)
        w_router_lo = (w_router - w_router_hi.astype(jnp.float32)).astype(bf16)
        w_router = jnp.concatenate([w_router_hi, w_router_hi, w_router_lo], axis=0)
        x1, u2, route, route_t, counts = _mix_ln1(o_sb, p, w_pool[layer].astype(bf16), pool_scale[layer][None, :],
                                         w_out[layer].astype(bf16), x2d, mod, ln1_g[layer][None, :],
                                         ln1_b[layer][None, :], w_router, b_router, s)
        tile_expert, n_tiles, tile_end, pos_dispatch, pos, n_rows = _dispatch_plan(route_t, counts)
        u_sorted = _dispatch(tile_end, pos_dispatch, u2, n_rows)
        y_sorted = _moe_grouped(tile_expert, n_tiles, u_sorted, w_gate[layer], w_up[layer], w_down[layer])
        x2 = _combine_ln2(pos, y_sorted, route, x1, mod, ln2_g[layer][None, :], ln2_b[layer][None, :], s)
        x = x2.reshape(b, s, d)
    return x
```

```python
import functools

import jax
import jax.numpy as jnp
from jax import lax
from jax.experimental import pallas as pl
from jax.experimental.pallas import tpu as pltpu

D_MODEL = 1024
N_SB_HEADS = 8
SB_HEAD_DIM = 64
SB_WIDTH = N_SB_HEADS * SB_HEAD_DIM
POOL_WINDOWS = (2, 4, 8, 16)
POOL_GROUP_DIM = 128
POOL_WIDTH = len(POOL_WINDOWS) * POOL_GROUP_DIM
N_EXPERT_GROUPS = 4
EXPERTS_PER_GROUP = 4
N_EXPERTS = N_EXPERT_GROUPS * EXPERTS_PER_GROUP
EXPERT_HIDDEN = 512
DEPTH = 1
DEEPNORM_ALPHA = (2.0 * DEPTH) ** 0.25
LN_EPS = 1e-5
N_MOD = 6

LANES = 128
SUBLANES = 8
HALO = max(POOL_WINDOWS)
VMEM_LIMIT = 56 * 1024 * 1024

ROW_TILE = 512
MIX_STREAMS = 2
ATTN_TILE = 256
ATTN_QUERY_BLOCKS = 2
STICK_GONE_LOG2 = -180.0
MOE_ROW_TILE = 512
DISPATCH_ROW_TILE = 1024
COMBINE_ROW_TILE = 512
GATHER_UNROLL = 8


ROW_CHUNKS = D_MODEL // LANES


def _store_tile_rows(ref, base, value):
    n = value.shape[0]
    for c in range(ROW_CHUNKS):
        ref[pl.ds(base + c, n, stride=ROW_CHUNKS), :] = value[:, c * LANES:(c + 1) * LANES]


def _load_tile_rows(ref, base, n):
    return jnp.concatenate([ref[pl.ds(base + c, n, stride=ROW_CHUNKS), :] for c in range(ROW_CHUNKS)], axis=1)


def _ln(x):
    mu = jnp.mean(x, axis=-1, keepdims=True)
    xc = x - mu
    var = jnp.mean(xc * xc, axis=-1, keepdims=True)
    return xc * lax.rsqrt(var + LN_EPS)


def _adaln_kernel(c_ref, w_ref, b_ref, o_ref):
    c = c_ref[...]
    a = c * jax.nn.sigmoid(c)
    o_ref[...] = jnp.dot(a, w_ref[...], preferred_element_type=jnp.float32,
                         precision=lax.Precision.HIGHEST) + b_ref[...]


def _adaln(c_pad, w_ada, b_ada):
    rows, d = c_pad.shape
    n = w_ada.shape[1]
    tn = 1024
    return pl.pallas_call(
        _adaln_kernel,
        grid=(n // tn,),
        in_specs=[pl.BlockSpec((rows, d), lambda j: (0, 0)),
                  pl.BlockSpec((d, tn), lambda j: (0, j)),
                  pl.BlockSpec((1, tn), lambda j: (0, j))],
        out_specs=pl.BlockSpec((rows, tn), lambda j: (0, j)),
        out_shape=jax.ShapeDtypeStruct((rows, n), jnp.float32),
        compiler_params=pltpu.CompilerParams(vmem_limit_bytes=VMEM_LIMIT),
        name="adaln",
    )(c_pad, w_ada, b_ada)


def _ln_inproj_kernel(x_ref, mod_ref, w_ref, qkv_ref, p_ref, w_bf16_ref):
    @pl.when(pl.program_id(0) == 0)
    def _():
        w_bf16_ref[...] = w_ref[...].astype(jnp.bfloat16)

    tm = ROW_TILE
    rows = lambda s: slice(s * tm, (s + 1) * tm)
    shift = mod_ref[0, 0:1, :]
    scale = mod_ref[0, 1:2, :]
    us = [(_ln(x_ref[rows(s), :]) * (1.0 + scale) + shift).astype(jnp.bfloat16) for s in range(MIX_STREAMS)]
    qk_scale = SB_HEAD_DIM ** -0.5 * 1.4426950408889634
    for s, u in enumerate(us):
        q = jnp.dot(u, w_bf16_ref[:, 0:SB_WIDTH], preferred_element_type=jnp.float32) * qk_scale
        qkv_ref[rows(s), 0:SB_WIDTH] = q.astype(jnp.bfloat16)
        for j in (1, 2):
            kv = jnp.dot(u, w_bf16_ref[:, j * SB_WIDTH:(j + 1) * SB_WIDTH], preferred_element_type=jnp.float32)
            qkv_ref[rows(s), j * SB_WIDTH:(j + 1) * SB_WIDTH] = kv.astype(jnp.bfloat16)
        p_ref[rows(s), :] = jnp.dot(u, w_bf16_ref[:, 3 * SB_WIDTH:], preferred_element_type=jnp.float32)


def _ln_inproj(x2d, mod, w_in, seq):
    t, d = x2d.shape
    tm = MIX_STREAMS * ROW_TILE
    steps_per_seq = seq // tm
    return pl.pallas_call(
        _ln_inproj_kernel,
        grid=(t // tm,),
        in_specs=[pl.BlockSpec((tm, d), lambda i: (i, 0)),
                  pl.BlockSpec((1, N_MOD, d), lambda i: (i // steps_per_seq, 0, 0)),
                  pl.BlockSpec(w_in.shape, lambda i: (0, 0))],
        out_specs=[pl.BlockSpec((tm, 3 * SB_WIDTH), lambda i: (i, 0)),
                   pl.BlockSpec((tm, POOL_WIDTH), lambda i: (i, 0))],
        out_shape=[jax.ShapeDtypeStruct((t, 3 * SB_WIDTH), jnp.bfloat16),
                   jax.ShapeDtypeStruct((t, POOL_WIDTH), jnp.float32)],
        scratch_shapes=[pltpu.VMEM(w_in.shape, jnp.bfloat16)],
        compiler_params=pltpu.CompilerParams(vmem_limit_bytes=VMEM_LIMIT,
                                             dimension_semantics=("arbitrary",)),
        name="ln_inproj",
    )(x2d, mod, w_in)


def _sb_attn_kernel(q_ref, k_ref, v_ref, o_ref, acc_ref, carry_ref):
    tq = tk = ATTN_TILE
    heads = (0, 1)
    subs = tuple(range(ATTN_QUERY_BLOCKS))
    qis = [pl.program_id(2) * ATTN_QUERY_BLOCKS + s for s in subs]
    lane = lax.broadcasted_iota(jnp.int32, (tq, LANES), 1)
    first_head = lane < SB_HEAD_DIM
    q_heads = {}
    for s in subs:
        q2 = q_ref[0, s * tq:(s + 1) * tq, :]
        zero = jnp.zeros_like(q2)
        q_heads[s, 0], q_heads[s, 1] = jnp.where(first_head, q2, zero), jnp.where(first_head, zero, q2)

    row = lax.broadcasted_iota(jnp.int32, (tk, tk), 0)
    col = lax.broadcasted_iota(jnp.int32, (tk, tk), 1)
    neg_suffix = jnp.where(row >= col, -1.0, 0.0).astype(jnp.bfloat16)
    causal = col < row

    def visit(blocks, carries):
        k_blks, v_blks = [], []
        for s, j, _ in blocks:
            start = pl.multiple_of(jnp.maximum(qis[s] - j, 0) * tk, tk)
            k_blks.append(k_ref[0, pl.ds(start, tk), :])
            v_blks.append(v_ref[0, pl.ds(start, tk), :])
        pairs = [(b, h) for b in range(len(blocks)) for h in heads]
        zs = {(b, h): lax.dot_general(q_heads[blocks[b][0], h], k_blks[b], (((1,), (1,)), ((), ())),
                                      preferred_element_type=jnp.float32) for b, h in pairs}
        fixed, rounded = {}, {}
        for b, h in pairs:
            z = zs[b, h]
            n = jnp.maximum(z, 0.0) + jnp.log2(1.0 + jnp.exp2(-jnp.abs(z)))
            if blocks[b][2]:
                n = jnp.where(causal, n, 0.0)
            n_bf16 = n.astype(jnp.bfloat16)
            rounded[b, h] = n_bf16
            fixed[b, h] = z - (n - n_bf16.astype(jnp.float32))
        incls = {bh: jnp.dot(rounded[bh], neg_suffix, preferred_element_type=jnp.float32) for bh in pairs}
        atts = {}
        carries = dict(carries)
        for b, h in pairs:
            s = blocks[b][0]
            carry = carries[s, h]
            block_sum = jnp.broadcast_to(incls[b, h][:, 0:1], (tq, LANES))
            expo = fixed[b, h] + incls[b, h]
            if carry is None:
                carries[s, h] = block_sum
            else:
                expo = expo + jnp.concatenate([carry] * (tk // LANES), axis=1)
                carries[s, h] = carry + block_sum
            att = jnp.exp2(expo)
            if blocks[b][2]:
                att = jnp.where(causal, att, 0.0)
            atts[b, h] = att.astype(jnp.bfloat16)
        outs = {(b, h): jnp.dot(atts[b, h], v_blks[b], preferred_element_type=jnp.float32) for b, h in pairs}
        return outs, carries

    head_blocks = [(s, 0, True) for s in subs] + [(s, 1, False) for s in subs]
    outs, carries = visit(head_blocks, {(s, h): None for s in subs for h in heads})
    for s in subs:
        has_second = qis[s] >= 1
        for h in heads:
            acc_ref[s, h] = outs[s, h] + jnp.where(has_second, outs[len(subs) + s, h], 0.0)
            carry_ref[s, h] = carries[s, h]

    for s in subs:
        def more(state, s=s):
            j, stick_gone = state
            return (j <= qis[s]) & jnp.logical_not(stick_gone)

        def visit_next(state, s=s):
            j, _ = state
            outs, new_carries = visit([(s, j, False)], {(s, h): carry_ref[s, h] for h in heads})
            for h in heads:
                acc_ref[s, h] += outs[0, h]
                carry_ref[s, h] = new_carries[s, h]
            return j + 1, jnp.max(carry_ref[s]) < STICK_GONE_LOG2

        lax.while_loop(more, visit_next, (jnp.int32(2), jnp.max(carry_ref[s]) < STICK_GONE_LOG2))
        o_ref[0, s * tq:(s + 1) * tq, :] = jnp.where(first_head, acc_ref[s, 0], acc_ref[s, 1]).astype(o_ref.dtype)


def _sb_attn(qkv, batch, seq):
    qkv3 = qkv.reshape(batch, seq, 3 * SB_WIDTH)
    tq = ATTN_TILE
    rows = ATTN_QUERY_BLOCKS * tq
    pairs = SB_WIDTH // LANES
    return pl.pallas_call(
        _sb_attn_kernel,
        grid=(batch, pairs, seq // rows),
        in_specs=[pl.BlockSpec((1, rows, LANES), lambda b, h, i: (b, i, h)),
                  pl.BlockSpec((1, seq, LANES), lambda b, h, i: (b, 0, pairs + h)),
                  pl.BlockSpec((1, seq, LANES), lambda b, h, i: (b, 0, 2 * pairs + h))],
        out_specs=pl.BlockSpec((1, rows, LANES), lambda b, h, i: (b, i, h)),
        out_shape=jax.ShapeDtypeStruct((batch, seq, SB_WIDTH), jnp.bfloat16),
        scratch_shapes=[pltpu.VMEM((ATTN_QUERY_BLOCKS, 2, tq, LANES), jnp.float32),
                        pltpu.VMEM((ATTN_QUERY_BLOCKS, 2, tq, LANES), jnp.float32)],
        compiler_params=pltpu.CompilerParams(vmem_limit_bytes=VMEM_LIMIT),
        name="sb_attn",
    )(qkv3, qkv3, qkv3)


def _route(logits):
    lane = lax.broadcasted_iota(jnp.int32, logits.shape, 1).astype(jnp.float32)
    neg = jnp.float32(-jnp.inf)
    big = jnp.float32(LANES)
    is_group = (lane >= N_EXPERTS) & (lane < N_EXPERTS + N_EXPERT_GROUPS)
    gl = jnp.where(is_group, logits, neg)
    g_max = jnp.max(gl, axis=-1, keepdims=True)
    g_sum = jnp.sum(jnp.exp(gl - g_max), axis=-1, keepdims=True)
    g_p = 1.0 / g_sum
    g_idx = jnp.min(jnp.where(gl == g_max, lane, big), axis=-1, keepdims=True) - N_EXPERTS
    in_group = (lane >= g_idx * EXPERTS_PER_GROUP) & (lane < (g_idx + 1) * EXPERTS_PER_GROUP)
    el = jnp.where(in_group, logits, neg)
    e1 = jnp.max(el, axis=-1, keepdims=True)
    i1 = jnp.min(jnp.where(el == e1, lane, big), axis=-1, keepdims=True)
    el2 = jnp.where(lane == i1, neg, el)
    e2 = jnp.max(el2, axis=-1, keepdims=True)
    i2 = jnp.min(jnp.where(el2 == e2, lane, big), axis=-1, keepdims=True)
    r = jnp.exp(e2 - e1)
    w1 = g_p / (1.0 + r)
    w2 = w1 * r
    return i1, i2, w1, w2


ROUTE_W1, ROUTE_W2, ROUTE_E1, ROUTE_E2, ROUTE_RANK1, ROUTE_RANK2 = range(6)


def _mix_ln1_kernel(steps_per_seq, osb_ref, p_ref, halo_ref, wpool_ref, pscale_ref, wout_ref, x_ref, mod_ref,
                    g1_ref, b1_ref, wr_ref, br_ref, earlier_ref, x1_ref, u2_ref, route_ref, route_t_ref, count_ref,
                    pext_ref, running_ref):
    tm = ROW_TILE
    streams = range(MIX_STREAMS)
    rows = lambda s: slice(s * tm, (s + 1) * tm)
    step_in_seq = pl.program_id(0) % steps_per_seq
    pext_ref[0:HALO, :] = jnp.where(step_in_seq == 0, 0.0, halo_ref[...])
    pext_ref[HALO:, :] = p_ref[...]

    mixer_outs = []
    for s in streams:
        pos = (step_in_seq * MIX_STREAMS + s) * tm + lax.broadcasted_iota(jnp.int32, (tm, 1), 0)
        first_row = HALO + s * tm
        mixer_out = [osb_ref[rows(s), :]]
        for g, w in enumerate(POOL_WINDOWS):
            cols = slice(g * POOL_GROUP_DIM, (g + 1) * POOL_GROUP_DIM)
            own = p_ref[rows(s), cols]
            win = own
            for i in range(1, w):
                win = win + pext_ref[first_row - i:first_row - i + tm, cols]
            count = jnp.minimum(pos + 1, w).astype(jnp.float32)
            pooled = win / count - own
            o_pool = jnp.dot(pooled.astype(jnp.bfloat16), wpool_ref[g], preferred_element_type=jnp.float32)
            mixer_out.append((o_pool * pscale_ref[:, cols]).astype(jnp.bfloat16))
        mixer_outs.append(jnp.concatenate(mixer_out, axis=1))
    mixed = [jnp.dot(m, wout_ref[...], preferred_element_type=jnp.float32) for m in mixer_outs]

    gate1 = mod_ref[0, 2:3, :]
    shift2 = mod_ref[0, 3:4, :]
    scale2 = mod_ref[0, 4:5, :]
    router_in = []
    for s in streams:
        x1 = _ln(DEEPNORM_ALPHA * x_ref[rows(s), :] + (1.0 + gate1) * mixed[s]) * g1_ref[...] + b1_ref[...]
        x1_ref[rows(s), :] = x1
        u2 = _ln(x1) * (1.0 + scale2) + shift2
        _store_tile_rows(u2_ref, s * tm * ROW_CHUNKS, u2)
        u2_hi = u2.astype(jnp.bfloat16)
        u2_lo = (u2 - u2_hi.astype(jnp.float32)).astype(jnp.bfloat16)
        router_in.append(jnp.concatenate([u2_hi, u2_lo, u2_hi], axis=1))
    logits = [jnp.dot(r, wr_ref[...], preferred_element_type=jnp.float32) + br_ref[...] for r in router_in]
    routed = [_route(l) for l in logits]

    @pl.when(pl.program_id(0) == 0)
    def _():
        running_ref[...] = jnp.zeros_like(running_ref)

    lane = lax.broadcasted_iota(jnp.int32, (tm, LANES), 1)
    lane_f = lane.astype(jnp.float32)
    picked = [(lane_f == i1, lane_f == i2) for i1, i2, _, _ in routed]
    uses = [jnp.where(first | second, 1.0, 0.0) for first, second in picked]
    within = [jnp.dot(earlier_ref[...], u.astype(jnp.bfloat16), preferred_element_type=jnp.float32) for u in uses]
    running = running_ref[0:1, :]
    for s in streams:
        i1, i2, w1, w2 = routed[s]
        first, second = picked[s]
        before = within[s] + running
        rank1 = jnp.sum(jnp.where(first, before, 0.0), axis=-1, keepdims=True)
        rank2 = jnp.sum(jnp.where(second, before, 0.0), axis=-1, keepdims=True)
        running = running + jnp.sum(uses[s], axis=0, keepdims=True)
        record = jnp.zeros((tm, LANES), jnp.float32)
        for slot, value in ((ROUTE_W1, w1), (ROUTE_W2, w2), (ROUTE_E1, i1), (ROUTE_E2, i2),
                            (ROUTE_RANK1, rank1), (ROUTE_RANK2, rank2)):
            record = jnp.where(lane == slot, value, record)
        route_ref[rows(s), :] = record
        route_t_ref[:, rows(s)] = record.T[0:SUBLANES, :]
    running_ref[...] = jnp.broadcast_to(running, running_ref.shape)
    count_ref[...] = jnp.broadcast_to(running, count_ref.shape)


def _mix_ln1(o_sb, p, w_pool_bf16, pool_scale, w_out_bf16, x2d, mod, ln1_g, ln1_b, w_router, b_router, seq):
    t, d = x2d.shape
    tm = MIX_STREAMS * ROW_TILE
    steps_per_seq = seq // tm
    halo_blocks_per_step = tm // HALO
    row = lambda i: (i, 0)
    const2 = lambda i: (0, 0)
    return pl.pallas_call(
        functools.partial(_mix_ln1_kernel, steps_per_seq),
        grid=(t // tm,),
        in_specs=[pl.BlockSpec((tm, SB_WIDTH), row),
                  pl.BlockSpec((tm, POOL_WIDTH), row),
                  pl.BlockSpec((HALO, POOL_WIDTH), lambda i: (jnp.maximum(i * halo_blocks_per_step - 1, 0), 0)),
                  pl.BlockSpec(w_pool_bf16.shape, lambda i: (0, 0, 0)),
                  pl.BlockSpec((1, POOL_WIDTH), const2),
                  pl.BlockSpec(w_out_bf16.shape, const2),
                  pl.BlockSpec((tm, d), row),
                  pl.BlockSpec((1, N_MOD, d), lambda i: (i // steps_per_seq, 0, 0)),
                  pl.BlockSpec((1, d), const2),
                  pl.BlockSpec((1, d), const2),
                  pl.BlockSpec(w_router.shape, const2),
                  pl.BlockSpec((1, LANES), const2),
                  pl.BlockSpec((ROW_TILE, ROW_TILE), const2)],
        out_specs=[pl.BlockSpec((tm, d), row),
                   pl.BlockSpec((tm * ROW_CHUNKS, LANES), row),
                   pl.BlockSpec((tm, LANES), row),
                   pl.BlockSpec((SUBLANES, tm), lambda i: (0, i)),
                   pl.BlockSpec((SUBLANES, LANES), const2)],
        out_shape=[jax.ShapeDtypeStruct((t, d), jnp.float32),
                   jax.ShapeDtypeStruct((t * ROW_CHUNKS, LANES), jnp.float32),
                   jax.ShapeDtypeStruct((t, LANES), jnp.float32),
                   jax.ShapeDtypeStruct((SUBLANES, t), jnp.float32),
                   jax.ShapeDtypeStruct((SUBLANES, LANES), jnp.float32)],
        scratch_shapes=[pltpu.VMEM((HALO + tm, POOL_WIDTH), jnp.float32),
                        pltpu.VMEM((SUBLANES, LANES), jnp.float32)],
        compiler_params=pltpu.CompilerParams(vmem_limit_bytes=VMEM_LIMIT,
                                             dimension_semantics=("arbitrary",)),
        name="mix_ln1",
    )(o_sb, p, p, w_pool_bf16, pool_scale, w_out_bf16, x2d, mod, ln1_g, ln1_b, w_router, b_router,
      jnp.tri(ROW_TILE, ROW_TILE, -1, dtype=jnp.bfloat16))


def _dispatch_kernel(tile_end_ref, pos_ref, u_ref, sorted_hbm, zeros_ref, sem):
    tm = u_ref.shape[0] // ROW_CHUNKS
    tile = MOE_ROW_TILE * ROW_CHUNKS
    stored = lambda r: pl.ds(pl.multiple_of(r * ROW_CHUNKS, ROW_CHUNKS), ROW_CHUNKS)

    @pl.when(pl.program_id(0) == 0)
    def _():
        zeros_ref[...] = jnp.zeros_like(zeros_ref)

        def tile_fill(i):
            return pltpu.make_async_copy(zeros_ref, sorted_hbm.at[pl.ds(pl.multiple_of(i * tile, tile), tile), :],
                                         sem)

        def last_tile_fills(act):
            for e in range(N_EXPERTS):
                first_tile = tile_end_ref[e - 1] if e else 0

                @pl.when(tile_end_ref[e] > first_tile)
                def _():
                    act(tile_fill(tile_end_ref[e] - 1))

        unused = (tile_end_ref[N_EXPERTS - 1], sorted_hbm.shape[0] // tile)
        last_tile_fills(lambda fill: fill.start())
        lax.fori_loop(*unused, lambda i, _: (tile_fill(i).start(), 0)[1], 0)
        last_tile_fills(lambda fill: fill.wait())
        lax.fori_loop(*unused, lambda i, _: (tile_fill(i).wait(), 0)[1], 0)

    def body(r, _):
        row = u_ref.at[stored(r), :]
        pltpu.make_async_copy(row, sorted_hbm.at[stored(pos_ref[0, 0, r]), :], sem).start()
        pltpu.make_async_copy(row, sorted_hbm.at[stored(pos_ref[0, 0, tm + r]), :], sem).start()
        return 0

    lax.fori_loop(0, tm, body, 0, unroll=GATHER_UNROLL // 2)
    for _ in range(2):
        pltpu.make_async_copy(u_ref, sorted_hbm.at[pl.ds(0, tm * ROW_CHUNKS), :], sem).wait()


def _dispatch(tile_end, pos, u2, n_rows):
    tm = pos.shape[2] // 2
    t = u2.shape[0] // ROW_CHUNKS
    grid_spec = pltpu.PrefetchScalarGridSpec(
        num_scalar_prefetch=1,
        grid=(t // tm,),
        in_specs=[pl.BlockSpec((1, 1, 2 * tm), lambda i, te: (i, 0, 0), memory_space=pltpu.SMEM),
                  pl.BlockSpec((tm * ROW_CHUNKS, LANES), lambda i, te: (i, 0))],
        out_specs=pl.BlockSpec(memory_space=pl.ANY),
        scratch_shapes=[pltpu.VMEM((MOE_ROW_TILE * ROW_CHUNKS, LANES), jnp.float32),
                        pltpu.SemaphoreType.DMA(())],
    )
    return pl.pallas_call(
        _dispatch_kernel,
        grid_spec=grid_spec,
        out_shape=jax.ShapeDtypeStruct((n_rows * ROW_CHUNKS, LANES), jnp.float32),
        compiler_params=pltpu.CompilerParams(vmem_limit_bytes=VMEM_LIMIT,
                                             dimension_semantics=("arbitrary",)),
        name="dispatch",
    )(tile_end, pos, u2)


def _moe_grouped_kernel(tile_expert_ref, n_tiles_ref, u_ref, wg_ref, wu_ref, wd_ref, y_ref,
                        wg_bf16_ref, wu_bf16_ref, wd_bf16_ref):
    i = pl.program_id(0)
    bf16 = jnp.bfloat16
    valid = i < n_tiles_ref[0]

    @pl.when(valid & ((i == 0) | (tile_expert_ref[i] != tile_expert_ref[jnp.maximum(i - 1, 0)])))
    def _():
        wg_bf16_ref[...] = wg_ref[0].astype(bf16)
        wu_bf16_ref[...] = wu_ref[0].astype(bf16)
        wd_bf16_ref[...] = wd_ref[0].astype(bf16)

    @pl.when(valid)
    def _():
        u = _load_tile_rows(u_ref, 0, MOE_ROW_TILE).astype(bf16)
        gate = jnp.dot(u, wg_bf16_ref[...], preferred_element_type=jnp.float32)
        up = jnp.dot(u, wu_bf16_ref[...], preferred_element_type=jnp.float32)
        h = gate * jax.nn.sigmoid(gate) * up
        _store_tile_rows(y_ref, 0, jnp.dot(h.astype(bf16), wd_bf16_ref[...], preferred_element_type=jnp.float32))

    @pl.when(jnp.logical_not(valid))
    def _():
        y_ref[...] = jnp.zeros_like(y_ref)


def _moe_grouped(tile_expert, n_tiles, u_sorted, wg, wu, wd):
    d = D_MODEL
    tm = MOE_ROW_TILE
    max_tiles = tile_expert.shape[0]
    expert = lambda i, te, nt: (te[i], 0, 0)
    grid_spec = pltpu.PrefetchScalarGridSpec(
        num_scalar_prefetch=2,
        grid=(max_tiles,),
        in_specs=[pl.BlockSpec((tm * ROW_CHUNKS, LANES), lambda i, te, nt: (jnp.minimum(i, nt[0] - 1), 0)),
                  pl.BlockSpec((1, d, EXPERT_HIDDEN), expert),
                  pl.BlockSpec((1, d, EXPERT_HIDDEN), expert),
                  pl.BlockSpec((1, EXPERT_HIDDEN, d), expert)],
        out_specs=pl.BlockSpec((tm * ROW_CHUNKS, LANES), lambda i, te, nt: (i, 0)),
        scratch_shapes=[pltpu.VMEM((d, EXPERT_HIDDEN), jnp.bfloat16),
                        pltpu.VMEM((d, EXPERT_HIDDEN), jnp.bfloat16),
                        pltpu.VMEM((EXPERT_HIDDEN, d), jnp.bfloat16)],
    )
    return pl.pallas_call(
        _moe_grouped_kernel,
        grid_spec=grid_spec,
        out_shape=jax.ShapeDtypeStruct((max_tiles * tm * ROW_CHUNKS, LANES), jnp.float32),
        compiler_params=pltpu.CompilerParams(vmem_limit_bytes=VMEM_LIMIT,
                                             dimension_semantics=("arbitrary",)),
        name="moe_grouped",
    )(tile_expert, n_tiles, u_sorted, wg, wu, wd)


def _combine_ln2_kernel(pos_ref, pos_next_ref, y_hbm, route_ref, x1_ref, mod_ref, g2_ref, b2_ref, o_ref,
                        buf_ref, sem_ref):
    i = pl.program_id(0)
    n = pl.num_programs(0)
    tm = o_ref.shape[0]
    rows = 2 * tm
    stored = lambda r: pl.ds(pl.multiple_of(r * ROW_CHUNKS, ROW_CHUNKS), ROW_CHUNKS)

    def start_gather(rows_ref, slot):
        def body(r, _):
            pltpu.make_async_copy(y_hbm.at[stored(rows_ref[0, 0, r]), :], buf_ref.at[slot, stored(r), :],
                                  sem_ref.at[slot]).start()
            return 0
        lax.fori_loop(0, rows, body, 0, unroll=GATHER_UNROLL)

    @pl.when(i == 0)
    def _():
        start_gather(pos_ref, 0)

    @pl.when(i + 1 < n)
    def _():
        start_gather(pos_next_ref, (i + 1) % 2)

    slot = i % 2
    pltpu.make_async_copy(y_hbm.at[pl.ds(0, rows * ROW_CHUNKS), :], buf_ref.at[slot], sem_ref.at[slot]).wait()
    route = route_ref[...]
    w1 = route[:, ROUTE_W1:ROUTE_W1 + 1]
    w2 = route[:, ROUTE_W2:ROUTE_W2 + 1]
    gathered = buf_ref.at[slot]
    y = w1 * _load_tile_rows(gathered, 0, tm) + w2 * _load_tile_rows(gathered, tm * ROW_CHUNKS, tm)
    gate2 = mod_ref[0, 5:6, :]
    o_ref[...] = _ln(DEEPNORM_ALPHA * x1_ref[...] + (1.0 + gate2) * y) * g2_ref[...] + b2_ref[...]


def _combine_ln2(pos, y_sorted, route, x1, mod, ln2_g, ln2_b, seq):
    t, d = x1.shape
    tm = COMBINE_ROW_TILE
    n = t // tm
    tiles_per_seq = seq // tm
    row = lambda i: (i, 0)
    const2 = lambda i: (0, 0)
    return pl.pallas_call(
        _combine_ln2_kernel,
        grid=(n,),
        in_specs=[pl.BlockSpec((1, 1, 2 * tm), lambda i: (i, 0, 0), memory_space=pltpu.SMEM),
                  pl.BlockSpec((1, 1, 2 * tm), lambda i: (jnp.minimum(i + 1, n - 1), 0, 0),
                               memory_space=pltpu.SMEM),
                  pl.BlockSpec(memory_space=pl.ANY),
                  pl.BlockSpec((tm, LANES), row),
                  pl.BlockSpec((tm, d), row),
                  pl.BlockSpec((1, N_MOD, d), lambda i: (i // tiles_per_seq, 0, 0)),
                  pl.BlockSpec((1, d), const2),
                  pl.BlockSpec((1, d), const2)],
        out_specs=pl.BlockSpec((tm, d), row),
        out_shape=jax.ShapeDtypeStruct((t, d), jnp.float32),
        scratch_shapes=[pltpu.VMEM((2, 2 * tm * ROW_CHUNKS, LANES), jnp.float32),
                        pltpu.SemaphoreType.DMA((2,))],
        compiler_params=pltpu.CompilerParams(vmem_limit_bytes=VMEM_LIMIT,
                                             dimension_semantics=("arbitrary",)),
        name="combine_ln2",
    )(pos, pos, y_sorted, route, x1, mod, ln2_g, ln2_b)


def _dispatch_plan(route_t, counts):
    t = route_t.shape[1]
    tm = MOE_ROW_TILE
    max_tiles = 2 * t // tm + N_EXPERTS
    counts = counts[0, :N_EXPERTS].astype(jnp.int32)
    tiles = (counts + tm - 1) // tm
    tile_end = jnp.cumsum(tiles)
    row_start = (tile_end - tiles) * tm
    experts = route_t[ROUTE_E1:ROUTE_E2 + 1].astype(jnp.int32)
    ranks = route_t[ROUTE_RANK1:ROUTE_RANK2 + 1].astype(jnp.int32)
    expert_ids = jnp.arange(N_EXPERTS, dtype=jnp.int32)[:, None, None]
    pos = jnp.sum(jnp.where(experts[None] == expert_ids, row_start[:, None, None], 0), axis=0) + ranks
    tile_ids = jnp.arange(max_tiles, dtype=jnp.int32)
    tile_expert = jnp.minimum(jnp.sum(tile_end[None, :] <= tile_ids[:, None], axis=1), N_EXPERTS - 1)
    by_tile = lambda tc: pos.reshape(2, t // tc, tc).transpose(1, 0, 2).reshape(t // tc, 1, 2 * tc)
    return (tile_expert.astype(jnp.int32), tile_end[-1:].astype(jnp.int32), tile_end.astype(jnp.int32),
            by_tile(DISPATCH_ROW_TILE), by_tile(COMBINE_ROW_TILE), max_tiles * tm)


def kernel(x, c, w_ada, b_ada, w_in, w_pool, pool_scale, w_out, ln1_g, ln1_b, w_router_group, b_router_group,
           w_router_expert, b_router_expert, w_gate, w_up, w_down, ln2_g, ln2_b):
    b, s, d = x.shape
    bf16 = jnp.bfloat16
    for layer in range(DEPTH):
        c_pad = jnp.pad(c, ((0, SUBLANES - b), (0, 0)))
        mod = _adaln(c_pad, w_ada[layer], b_ada[layer][None, :])[:b].reshape(b, N_MOD, d)
        x2d = x.reshape(b * s, d)
        qkv, p = _ln_inproj(x2d, mod, w_in[layer], s)
        o_sb = _sb_attn(qkv, b, s).reshape(b * s, SB_WIDTH)
        pad = LANES - N_EXPERTS - N_EXPERT_GROUPS
        w_router = jnp.pad(jnp.concatenate([w_router_expert[layer], w_router_group[layer]], axis=1),
                           ((0, 0), (0, pad)))
        b_router = jnp.pad(jnp.concatenate([b_router_expert[layer], b_router_group[layer]]), (0, pad))[None, :]
        w_router_hi = w_router.astype(bf16)
        w_router_lo = (w_router - w_router_hi.astype(jnp.float32)).astype(bf16)
        w_router = jnp.concatenate([w_router_hi, w_router_hi, w_router_lo], axis=0)
        x1, u2, route, route_t, counts = _mix_ln1(o_sb, p, w_pool[layer].astype(bf16), pool_scale[layer][None, :],
                                         w_out[layer].astype(bf16), x2d, mod, ln1_g[layer][None, :],
                                         ln1_b[layer][None, :], w_router, b_router, s)
        tile_expert, n_tiles, tile_end, pos_dispatch, pos, n_rows = _dispatch_plan(route_t, counts)
        u_sorted = _dispatch(tile_end, pos_dispatch, u2, n_rows)
        y_sorted = _moe_grouped(tile_expert, n_tiles, u_sorted, w_gate[layer], w_up[layer], w_down[layer])
        x2 = _combine_ln2(pos, y_sorted, route, x1, mod, ln2_g[layer][None, :], ln2_b[layer][None, :], s)
        x = x2.reshape(b, s, d)
    return x
```

```python
import functools

import jax
import jax.numpy as jnp
from jax import lax
from jax.experimental import pallas as pl
from jax.experimental.pallas import tpu as pltpu

D_MODEL = 1024
N_SB_HEADS = 8
SB_HEAD_DIM = 64
SB_WIDTH = N_SB_HEADS * SB_HEAD_DIM
POOL_WINDOWS = (2, 4, 8, 16)
POOL_GROUP_DIM = 128
POOL_WIDTH = len(POOL_WINDOWS) * POOL_GROUP_DIM
N_EXPERT_GROUPS = 4
EXPERTS_PER_GROUP = 4
N_EXPERTS = N_EXPERT_GROUPS * EXPERTS_PER_GROUP
EXPERT_HIDDEN = 512
DEPTH = 1
DEEPNORM_ALPHA = (2.0 * DEPTH) ** 0.25
LN_EPS = 1e-5
N_MOD = 6

LANES = 128
SUBLANES = 8
HALO = max(POOL_WINDOWS)
VMEM_LIMIT = 56 * 1024 * 1024

ROW_TILE = 512
MIX_STREAMS = 2
ATTN_TILE = 256
ATTN_QUERY_BLOCKS = 2
STICK_GONE_LOG2 = -180.0
MOE_ROW_TILE = 512
DISPATCH_ROW_TILE = 1024
COMBINE_ROW_TILE = 256
GATHER_UNROLL = 8


ROW_CHUNKS = D_MODEL // LANES


def _store_tile_rows(ref, base, value):
    n = value.shape[0]
    for c in range(ROW_CHUNKS):
        ref[pl.ds(base + c, n, stride=ROW_CHUNKS), :] = value[:, c * LANES:(c + 1) * LANES]


def _load_tile_rows(ref, base, n):
    return jnp.concatenate([ref[pl.ds(base + c, n, stride=ROW_CHUNKS), :] for c in range(ROW_CHUNKS)], axis=1)


def _ln(x):
    mu = jnp.mean(x, axis=-1, keepdims=True)
    xc = x - mu
    var = jnp.mean(xc * xc, axis=-1, keepdims=True)
    return xc * lax.rsqrt(var + LN_EPS)


def _adaln_kernel(c_ref, w_ref, b_ref, o_ref):
    c = c_ref[...]
    a = c * jax.nn.sigmoid(c)
    o_ref[...] = jnp.dot(a, w_ref[...], preferred_element_type=jnp.float32,
                         precision=lax.Precision.HIGHEST) + b_ref[...]


def _adaln(c_pad, w_ada, b_ada):
    rows, d = c_pad.shape
    n = w_ada.shape[1]
    tn = 1024
    return pl.pallas_call(
        _adaln_kernel,
        grid=(n // tn,),
        in_specs=[pl.BlockSpec((rows, d), lambda j: (0, 0)),
                  pl.BlockSpec((d, tn), lambda j: (0, j)),
                  pl.BlockSpec((1, tn), lambda j: (0, j))],
        out_specs=pl.BlockSpec((rows, tn), lambda j: (0, j)),
        out_shape=jax.ShapeDtypeStruct((rows, n), jnp.float32),
        compiler_params=pltpu.CompilerParams(vmem_limit_bytes=VMEM_LIMIT),
        name="adaln",
    )(c_pad, w_ada, b_ada)


def _ln_inproj_kernel(x_ref, mod_ref, w_ref, qkv_ref, p_ref, w_bf16_ref):
    @pl.when(pl.program_id(0) == 0)
    def _():
        w_bf16_ref[...] = w_ref[...].astype(jnp.bfloat16)

    tm = ROW_TILE
    rows = lambda s: slice(s * tm, (s + 1) * tm)
    shift = mod_ref[0, 0:1, :]
    scale = mod_ref[0, 1:2, :]
    us = [(_ln(x_ref[rows(s), :]) * (1.0 + scale) + shift).astype(jnp.bfloat16) for s in range(MIX_STREAMS)]
    qk_scale = SB_HEAD_DIM ** -0.5 * 1.4426950408889634
    for s, u in enumerate(us):
        q = jnp.dot(u, w_bf16_ref[:, 0:SB_WIDTH], preferred_element_type=jnp.float32) * qk_scale
        qkv_ref[rows(s), 0:SB_WIDTH] = q.astype(jnp.bfloat16)
        for j in (1, 2):
            kv = jnp.dot(u, w_bf16_ref[:, j * SB_WIDTH:(j + 1) * SB_WIDTH], preferred_element_type=jnp.float32)
            qkv_ref[rows(s), j * SB_WIDTH:(j + 1) * SB_WIDTH] = kv.astype(jnp.bfloat16)
        p_ref[rows(s), :] = jnp.dot(u, w_bf16_ref[:, 3 * SB_WIDTH:], preferred_element_type=jnp.float32)


def _ln_inproj(x2d, mod, w_in, seq):
    t, d = x2d.shape
    tm = MIX_STREAMS * ROW_TILE
    steps_per_seq = seq // tm
    return pl.pallas_call(
        _ln_inproj_kernel,
        grid=(t // tm,),
        in_specs=[pl.BlockSpec((tm, d), lambda i: (i, 0)),
                  pl.BlockSpec((1, N_MOD, d), lambda i: (i // steps_per_seq, 0, 0)),
                  pl.BlockSpec(w_in.shape, lambda i: (0, 0))],
        out_specs=[pl.BlockSpec((tm, 3 * SB_WIDTH), lambda i: (i, 0)),
                   pl.BlockSpec((tm, POOL_WIDTH), lambda i: (i, 0))],
        out_shape=[jax.ShapeDtypeStruct((t, 3 * SB_WIDTH), jnp.bfloat16),
                   jax.ShapeDtypeStruct((t, POOL_WIDTH), jnp.float32)],
        scratch_shapes=[pltpu.VMEM(w_in.shape, jnp.bfloat16)],
        compiler_params=pltpu.CompilerParams(vmem_limit_bytes=VMEM_LIMIT,
                                             dimension_semantics=("arbitrary",)),
        name="ln_inproj",
    )(x2d, mod, w_in)


def _sb_attn_kernel(q_ref, k_ref, v_ref, o_ref, acc_ref, carry_ref):
    tq = tk = ATTN_TILE
    heads = (0, 1)
    subs = tuple(range(ATTN_QUERY_BLOCKS))
    qis = [pl.program_id(2) * ATTN_QUERY_BLOCKS + s for s in subs]
    lane = lax.broadcasted_iota(jnp.int32, (tq, LANES), 1)
    first_head = lane < SB_HEAD_DIM
    q_heads = {}
    for s in subs:
        q2 = q_ref[0, s * tq:(s + 1) * tq, :]
        zero = jnp.zeros_like(q2)
        q_heads[s, 0], q_heads[s, 1] = jnp.where(first_head, q2, zero), jnp.where(first_head, zero, q2)

    row = lax.broadcasted_iota(jnp.int32, (tk, tk), 0)
    col = lax.broadcasted_iota(jnp.int32, (tk, tk), 1)
    neg_suffix = jnp.where(row >= col, -1.0, 0.0).astype(jnp.bfloat16)
    causal = col < row

    def visit(blocks, carries):
        k_blks, v_blks = [], []
        for s, j, _ in blocks:
            start = pl.multiple_of(jnp.maximum(qis[s] - j, 0) * tk, tk)
            k_blks.append(k_ref[0, pl.ds(start, tk), :])
            v_blks.append(v_ref[0, pl.ds(start, tk), :])
        pairs = [(b, h) for b in range(len(blocks)) for h in heads]
        zs = {(b, h): lax.dot_general(q_heads[blocks[b][0], h], k_blks[b], (((1,), (1,)), ((), ())),
                                      preferred_element_type=jnp.float32) for b, h in pairs}
        fixed, rounded = {}, {}
        for b, h in pairs:
            z = zs[b, h]
            n = jnp.maximum(z, 0.0) + jnp.log2(1.0 + jnp.exp2(-jnp.abs(z)))
            if blocks[b][2]:
                n = jnp.where(causal, n, 0.0)
            n_bf16 = n.astype(jnp.bfloat16)
            rounded[b, h] = n_bf16
            fixed[b, h] = z - (n - n_bf16.astype(jnp.float32))
        incls = {bh: jnp.dot(rounded[bh], neg_suffix, preferred_element_type=jnp.float32) for bh in pairs}
        atts = {}
        carries = dict(carries)
        for b, h in pairs:
            s = blocks[b][0]
            carry = carries[s, h]
            block_sum = jnp.broadcast_to(incls[b, h][:, 0:1], (tq, LANES))
            expo = fixed[b, h] + incls[b, h]
            if carry is None:
                carries[s, h] = block_sum
            else:
                expo = expo + jnp.concatenate([carry] * (tk // LANES), axis=1)
                carries[s, h] = carry + block_sum
            att = jnp.exp2(expo)
            if blocks[b][2]:
                att = jnp.where(causal, att, 0.0)
            atts[b, h] = att.astype(jnp.bfloat16)
        outs = {(b, h): jnp.dot(atts[b, h], v_blks[b], preferred_element_type=jnp.float32) for b, h in pairs}
        return outs, carries

    head_blocks = [(s, 0, True) for s in subs] + [(s, 1, False) for s in subs]
    outs, carries = visit(head_blocks, {(s, h): None for s in subs for h in heads})
    for s in subs:
        has_second = qis[s] >= 1
        for h in heads:
            acc_ref[s, h] = outs[s, h] + jnp.where(has_second, outs[len(subs) + s, h], 0.0)
            carry_ref[s, h] = carries[s, h]

    for s in subs:
        def more(state, s=s):
            j, stick_gone = state
            return (j <= qis[s]) & jnp.logical_not(stick_gone)

        def visit_next(state, s=s):
            j, _ = state
            outs, new_carries = visit([(s, j, False)], {(s, h): carry_ref[s, h] for h in heads})
            for h in heads:
                acc_ref[s, h] += outs[0, h]
                carry_ref[s, h] = new_carries[s, h]
            return j + 1, jnp.max(carry_ref[s]) < STICK_GONE_LOG2

        lax.while_loop(more, visit_next, (jnp.int32(2), jnp.max(carry_ref[s]) < STICK_GONE_LOG2))
        o_ref[0, s * tq:(s + 1) * tq, :] = jnp.where(first_head, acc_ref[s, 0], acc_ref[s, 1]).astype(o_ref.dtype)


def _sb_attn(qkv, batch, seq):
    qkv3 = qkv.reshape(batch, seq, 3 * SB_WIDTH)
    tq = ATTN_TILE
    rows = ATTN_QUERY_BLOCKS * tq
    pairs = SB_WIDTH // LANES
    return pl.pallas_call(
        _sb_attn_kernel,
        grid=(batch, pairs, seq // rows),
        in_specs=[pl.BlockSpec((1, rows, LANES), lambda b, h, i: (b, i, h)),
                  pl.BlockSpec((1, seq, LANES), lambda b, h, i: (b, 0, pairs + h)),
                  pl.BlockSpec((1, seq, LANES), lambda b, h, i: (b, 0, 2 * pairs + h))],
        out_specs=pl.BlockSpec((1, rows, LANES), lambda b, h, i: (b, i, h)),
        out_shape=jax.ShapeDtypeStruct((batch, seq, SB_WIDTH), jnp.bfloat16),
        scratch_shapes=[pltpu.VMEM((ATTN_QUERY_BLOCKS, 2, tq, LANES), jnp.float32),
                        pltpu.VMEM((ATTN_QUERY_BLOCKS, 2, tq, LANES), jnp.float32)],
        compiler_params=pltpu.CompilerParams(vmem_limit_bytes=VMEM_LIMIT),
        name="sb_attn",
    )(qkv3, qkv3, qkv3)


def _route(logits):
    lane = lax.broadcasted_iota(jnp.int32, logits.shape, 1).astype(jnp.float32)
    neg = jnp.float32(-jnp.inf)
    big = jnp.float32(LANES)
    is_group = (lane >= N_EXPERTS) & (lane < N_EXPERTS + N_EXPERT_GROUPS)
    gl = jnp.where(is_group, logits, neg)
    g_max = jnp.max(gl, axis=-1, keepdims=True)
    g_sum = jnp.sum(jnp.exp(gl - g_max), axis=-1, keepdims=True)
    g_p = 1.0 / g_sum
    g_idx = jnp.min(jnp.where(gl == g_max, lane, big), axis=-1, keepdims=True) - N_EXPERTS
    in_group = (lane >= g_idx * EXPERTS_PER_GROUP) & (lane < (g_idx + 1) * EXPERTS_PER_GROUP)
    el = jnp.where(in_group, logits, neg)
    e1 = jnp.max(el, axis=-1, keepdims=True)
    i1 = jnp.min(jnp.where(el == e1, lane, big), axis=-1, keepdims=True)
    el2 = jnp.where(lane == i1, neg, el)
    e2 = jnp.max(el2, axis=-1, keepdims=True)
    i2 = jnp.min(jnp.where(el2 == e2, lane, big), axis=-1, keepdims=True)
    r = jnp.exp(e2 - e1)
    w1 = g_p / (1.0 + r)
    w2 = w1 * r
    return i1, i2, w1, w2


ROUTE_W1, ROUTE_W2, ROUTE_E1, ROUTE_E2, ROUTE_RANK1, ROUTE_RANK2 = range(6)


def _mix_ln1_kernel(steps_per_seq, osb_ref, p_ref, halo_ref, wpool_ref, pscale_ref, wout_ref, x_ref, mod_ref,
                    g1_ref, b1_ref, wr_ref, br_ref, earlier_ref, x1_ref, u2_ref, route_ref, route_t_ref, count_ref,
                    pext_ref, running_ref):
    tm = ROW_TILE
    streams = range(MIX_STREAMS)
    rows = lambda s: slice(s * tm, (s + 1) * tm)
    step_in_seq = pl.program_id(0) % steps_per_seq
    pext_ref[0:HALO, :] = jnp.where(step_in_seq == 0, 0.0, halo_ref[...])
    pext_ref[HALO:, :] = p_ref[...]

    mixer_outs = []
    for s in streams:
        pos = (step_in_seq * MIX_STREAMS + s) * tm + lax.broadcasted_iota(jnp.int32, (tm, 1), 0)
        first_row = HALO + s * tm
        mixer_out = [osb_ref[rows(s), :]]
        for g, w in enumerate(POOL_WINDOWS):
            cols = slice(g * POOL_GROUP_DIM, (g + 1) * POOL_GROUP_DIM)
            own = p_ref[rows(s), cols]
            win = own
            for i in range(1, w):
                win = win + pext_ref[first_row - i:first_row - i + tm, cols]
            count = jnp.minimum(pos + 1, w).astype(jnp.float32)
            pooled = win / count - own
            o_pool = jnp.dot(pooled.astype(jnp.bfloat16), wpool_ref[g], preferred_element_type=jnp.float32)
            mixer_out.append((o_pool * pscale_ref[:, cols]).astype(jnp.bfloat16))
        mixer_outs.append(jnp.concatenate(mixer_out, axis=1))
    mixed = [jnp.dot(m, wout_ref[...], preferred_element_type=jnp.float32) for m in mixer_outs]

    gate1 = mod_ref[0, 2:3, :]
    shift2 = mod_ref[0, 3:4, :]
    scale2 = mod_ref[0, 4:5, :]
    router_in = []
    for s in streams:
        x1 = _ln(DEEPNORM_ALPHA * x_ref[rows(s), :] + (1.0 + gate1) * mixed[s]) * g1_ref[...] + b1_ref[...]
        x1_ref[rows(s), :] = x1
        u2 = _ln(x1) * (1.0 + scale2) + shift2
        _store_tile_rows(u2_ref, s * tm * ROW_CHUNKS, u2)
        u2_hi = u2.astype(jnp.bfloat16)
        u2_lo = (u2 - u2_hi.astype(jnp.float32)).astype(jnp.bfloat16)
        router_in.append(jnp.concatenate([u2_hi, u2_lo, u2_hi], axis=1))
    logits = [jnp.dot(r, wr_ref[...], preferred_element_type=jnp.float32) + br_ref[...] for r in router_in]
    routed = [_route(l) for l in logits]

    @pl.when(pl.program_id(0) == 0)
    def _():
        running_ref[...] = jnp.zeros_like(running_ref)

    lane = lax.broadcasted_iota(jnp.int32, (tm, LANES), 1)
    lane_f = lane.astype(jnp.float32)
    picked = [(lane_f == i1, lane_f == i2) for i1, i2, _, _ in routed]
    uses = [jnp.where(first | second, 1.0, 0.0) for first, second in picked]
    within = [jnp.dot(earlier_ref[...], u.astype(jnp.bfloat16), preferred_element_type=jnp.float32) for u in uses]
    running = running_ref[0:1, :]
    for s in streams:
        i1, i2, w1, w2 = routed[s]
        first, second = picked[s]
        before = within[s] + running
        rank1 = jnp.sum(jnp.where(first, before, 0.0), axis=-1, keepdims=True)
        rank2 = jnp.sum(jnp.where(second, before, 0.0), axis=-1, keepdims=True)
        running = running + jnp.sum(uses[s], axis=0, keepdims=True)
        record = jnp.zeros((tm, LANES), jnp.float32)
        for slot, value in ((ROUTE_W1, w1), (ROUTE_W2, w2), (ROUTE_E1, i1), (ROUTE_E2, i2),
                            (ROUTE_RANK1, rank1), (ROUTE_RANK2, rank2)):
            record = jnp.where(lane == slot, value, record)
        route_ref[rows(s), :] = record
        route_t_ref[:, rows(s)] = record.T[0:SUBLANES, :]
    running_ref[...] = jnp.broadcast_to(running, running_ref.shape)
    count_ref[...] = jnp.broadcast_to(running, count_ref.shape)


def _mix_ln1(o_sb, p, w_pool_bf16, pool_scale, w_out_bf16, x2d, mod, ln1_g, ln1_b, w_router, b_router, seq):
    t, d = x2d.shape
    tm = MIX_STREAMS * ROW_TILE
    steps_per_seq = seq // tm
    halo_blocks_per_step = tm // HALO
    row = lambda i: (i, 0)
    const2 = lambda i: (0, 0)
    return pl.pallas_call(
        functools.partial(_mix_ln1_kernel, steps_per_seq),
        grid=(t // tm,),
        in_specs=[pl.BlockSpec((tm, SB_WIDTH), row),
                  pl.BlockSpec((tm, POOL_WIDTH), row),
                  pl.BlockSpec((HALO, POOL_WIDTH), lambda i: (jnp.maximum(i * halo_blocks_per_step - 1, 0), 0)),
                  pl.BlockSpec(w_pool_bf16.shape, lambda i: (0, 0, 0)),
                  pl.BlockSpec((1, POOL_WIDTH), const2),
                  pl.BlockSpec(w_out_bf16.shape, const2),
                  pl.BlockSpec((tm, d), row),
                  pl.BlockSpec((1, N_MOD, d), lambda i: (i // steps_per_seq, 0, 0)),
                  pl.BlockSpec((1, d), const2),
                  pl.BlockSpec((1, d), const2),
                  pl.BlockSpec(w_router.shape, const2),
                  pl.BlockSpec((1, LANES), const2),
                  pl.BlockSpec((ROW_TILE, ROW_TILE), const2)],
        out_specs=[pl.BlockSpec((tm, d), row),
                   pl.BlockSpec((tm * ROW_CHUNKS, LANES), row),
                   pl.BlockSpec((tm, LANES), row),
                   pl.BlockSpec((SUBLANES, tm), lambda i: (0, i)),
                   pl.BlockSpec((SUBLANES, LANES), const2)],
        out_shape=[jax.ShapeDtypeStruct((t, d), jnp.float32),
                   jax.ShapeDtypeStruct((t * ROW_CHUNKS, LANES), jnp.float32),
                   jax.ShapeDtypeStruct((t, LANES), jnp.float32),
                   jax.ShapeDtypeStruct((SUBLANES, t), jnp.float32),
                   jax.ShapeDtypeStruct((SUBLANES, LANES), jnp.float32)],
        scratch_shapes=[pltpu.VMEM((HALO + tm, POOL_WIDTH), jnp.float32),
                        pltpu.VMEM((SUBLANES, LANES), jnp.float32)],
        compiler_params=pltpu.CompilerParams(vmem_limit_bytes=VMEM_LIMIT,
                                             dimension_semantics=("arbitrary",)),
        name="mix_ln1",
    )(o_sb, p, p, w_pool_bf16, pool_scale, w_out_bf16, x2d, mod, ln1_g, ln1_b, w_router, b_router,
      jnp.tri(ROW_TILE, ROW_TILE, -1, dtype=jnp.bfloat16))


def _dispatch_kernel(tile_end_ref, pos_ref, u_ref, sorted_hbm, zeros_ref, sem):
    tm = u_ref.shape[0] // ROW_CHUNKS
    tile = MOE_ROW_TILE * ROW_CHUNKS
    stored = lambda r: pl.ds(pl.multiple_of(r * ROW_CHUNKS, ROW_CHUNKS), ROW_CHUNKS)

    @pl.when(pl.program_id(0) == 0)
    def _():
        zeros_ref[...] = jnp.zeros_like(zeros_ref)

        def tile_fill(i):
            return pltpu.make_async_copy(zeros_ref, sorted_hbm.at[pl.ds(pl.multiple_of(i * tile, tile), tile), :],
                                         sem)

        def last_tile_fills(act):
            for e in range(N_EXPERTS):
                first_tile = tile_end_ref[e - 1] if e else 0

                @pl.when(tile_end_ref[e] > first_tile)
                def _():
                    act(tile_fill(tile_end_ref[e] - 1))

        unused = (tile_end_ref[N_EXPERTS - 1], sorted_hbm.shape[0] // tile)
        last_tile_fills(lambda fill: fill.start())
        lax.fori_loop(*unused, lambda i, _: (tile_fill(i).start(), 0)[1], 0)
        last_tile_fills(lambda fill: fill.wait())
        lax.fori_loop(*unused, lambda i, _: (tile_fill(i).wait(), 0)[1], 0)

    def body(r, _):
        row = u_ref.at[stored(r), :]
        pltpu.make_async_copy(row, sorted_hbm.at[stored(pos_ref[0, 0, r]), :], sem).start()
        pltpu.make_async_copy(row, sorted_hbm.at[stored(pos_ref[0, 0, tm + r]), :], sem).start()
        return 0

    lax.fori_loop(0, tm, body, 0, unroll=GATHER_UNROLL // 2)
    for _ in range(2):
        pltpu.make_async_copy(u_ref, sorted_hbm.at[pl.ds(0, tm * ROW_CHUNKS), :], sem).wait()


def _dispatch(tile_end, pos, u2, n_rows):
    tm = pos.shape[2] // 2
    t = u2.shape[0] // ROW_CHUNKS
    grid_spec = pltpu.PrefetchScalarGridSpec(
        num_scalar_prefetch=1,
        grid=(t // tm,),
        in_specs=[pl.BlockSpec((1, 1, 2 * tm), lambda i, te: (i, 0, 0), memory_space=pltpu.SMEM),
                  pl.BlockSpec((tm * ROW_CHUNKS, LANES), lambda i, te: (i, 0))],
        out_specs=pl.BlockSpec(memory_space=pl.ANY),
        scratch_shapes=[pltpu.VMEM((MOE_ROW_TILE * ROW_CHUNKS, LANES), jnp.float32),
                        pltpu.SemaphoreType.DMA(())],
    )
    return pl.pallas_call(
        _dispatch_kernel,
        grid_spec=grid_spec,
        out_shape=jax.ShapeDtypeStruct((n_rows * ROW_CHUNKS, LANES), jnp.float32),
        compiler_params=pltpu.CompilerParams(vmem_limit_bytes=VMEM_LIMIT,
                                             dimension_semantics=("arbitrary",)),
        name="dispatch",
    )(tile_end, pos, u2)


def _moe_grouped_kernel(tile_expert_ref, n_tiles_ref, u_ref, wg_ref, wu_ref, wd_ref, y_ref,
                        wg_bf16_ref, wu_bf16_ref, wd_bf16_ref):
    i = pl.program_id(0)
    bf16 = jnp.bfloat16
    valid = i < n_tiles_ref[0]

    @pl.when(valid & ((i == 0) | (tile_expert_ref[i] != tile_expert_ref[jnp.maximum(i - 1, 0)])))
    def _():
        wg_bf16_ref[...] = wg_ref[0].astype(bf16)
        wu_bf16_ref[...] = wu_ref[0].astype(bf16)
        wd_bf16_ref[...] = wd_ref[0].astype(bf16)

    @pl.when(valid)
    def _():
        u = _load_tile_rows(u_ref, 0, MOE_ROW_TILE).astype(bf16)
        gate = jnp.dot(u, wg_bf16_ref[...], preferred_element_type=jnp.float32)
        up = jnp.dot(u, wu_bf16_ref[...], preferred_element_type=jnp.float32)
        h = gate * jax.nn.sigmoid(gate) * up
        _store_tile_rows(y_ref, 0, jnp.dot(h.astype(bf16), wd_bf16_ref[...], preferred_element_type=jnp.float32))

    @pl.when(jnp.logical_not(valid))
    def _():
        y_ref[...] = jnp.zeros_like(y_ref)


def _moe_grouped(tile_expert, n_tiles, u_sorted, wg, wu, wd):
    d = D_MODEL
    tm = MOE_ROW_TILE
    max_tiles = tile_expert.shape[0]
    expert = lambda i, te, nt: (te[i], 0, 0)
    grid_spec = pltpu.PrefetchScalarGridSpec(
        num_scalar_prefetch=2,
        grid=(max_tiles,),
        in_specs=[pl.BlockSpec((tm * ROW_CHUNKS, LANES), lambda i, te, nt: (jnp.minimum(i, nt[0] - 1), 0)),
                  pl.BlockSpec((1, d, EXPERT_HIDDEN), expert),
                  pl.BlockSpec((1, d, EXPERT_HIDDEN), expert),
                  pl.BlockSpec((1, EXPERT_HIDDEN, d), expert)],
        out_specs=pl.BlockSpec((tm * ROW_CHUNKS, LANES), lambda i, te, nt: (i, 0)),
        scratch_shapes=[pltpu.VMEM((d, EXPERT_HIDDEN), jnp.bfloat16),
                        pltpu.VMEM((d, EXPERT_HIDDEN), jnp.bfloat16),
                        pltpu.VMEM((EXPERT_HIDDEN, d), jnp.bfloat16)],
    )
    return pl.pallas_call(
        _moe_grouped_kernel,
        grid_spec=grid_spec,
        out_shape=jax.ShapeDtypeStruct((max_tiles * tm * ROW_CHUNKS, LANES), jnp.float32),
        compiler_params=pltpu.CompilerParams(vmem_limit_bytes=VMEM_LIMIT,
                                             dimension_semantics=("arbitrary",)),
        name="moe_grouped",
    )(tile_expert, n_tiles, u_sorted, wg, wu, wd)


def _combine_ln2_kernel(pos_ref, pos_next_ref, y_hbm, route_ref, x1_ref, mod_ref, g2_ref, b2_ref, o_ref,
                        buf_ref, sem_ref):
    i = pl.program_id(0)
    n = pl.num_programs(0)
    tm = o_ref.shape[0]
    rows = 2 * tm
    stored = lambda r: pl.ds(pl.multiple_of(r * ROW_CHUNKS, ROW_CHUNKS), ROW_CHUNKS)

    def start_gather(rows_ref, slot):
        def body(r, _):
            pltpu.make_async_copy(y_hbm.at[stored(rows_ref[0, 0, r]), :], buf_ref.at[slot, stored(r), :],
                                  sem_ref.at[slot]).start()
            return 0
        lax.fori_loop(0, rows, body, 0, unroll=GATHER_UNROLL)

    @pl.when(i == 0)
    def _():
        start_gather(pos_ref, 0)

    @pl.when(i + 1 < n)
    def _():
        start_gather(pos_next_ref, (i + 1) % 2)

    slot = i % 2
    pltpu.make_async_copy(y_hbm.at[pl.ds(0, rows * ROW_CHUNKS), :], buf_ref.at[slot], sem_ref.at[slot]).wait()
    route = route_ref[...]
    w1 = route[:, ROUTE_W1:ROUTE_W1 + 1]
    w2 = route[:, ROUTE_W2:ROUTE_W2 + 1]
    gathered = buf_ref.at[slot]
    y = w1 * _load_tile_rows(gathered, 0, tm) + w2 * _load_tile_rows(gathered, tm * ROW_CHUNKS, tm)
    gate2 = mod_ref[0, 5:6, :]
    o_ref[...] = _ln(DEEPNORM_ALPHA * x1_ref[...] + (1.0 + gate2) * y) * g2_ref[...] + b2_ref[...]


def _combine_ln2(pos, y_sorted, route, x1, mod, ln2_g, ln2_b, seq):
    t, d = x1.shape
    tm = COMBINE_ROW_TILE
    n = t // tm
    tiles_per_seq = seq // tm
    row = lambda i: (i, 0)
    const2 = lambda i: (0, 0)
    return pl.pallas_call(
        _combine_ln2_kernel,
        grid=(n,),
        in_specs=[pl.BlockSpec((1, 1, 2 * tm), lambda i: (i, 0, 0), memory_space=pltpu.SMEM),
                  pl.BlockSpec((1, 1, 2 * tm), lambda i: (jnp.minimum(i + 1, n - 1), 0, 0),
                               memory_space=pltpu.SMEM),
                  pl.BlockSpec(memory_space=pl.ANY),
                  pl.BlockSpec((tm, LANES), row),
                  pl.BlockSpec((tm, d), row),
                  pl.BlockSpec((1, N_MOD, d), lambda i: (i // tiles_per_seq, 0, 0)),
                  pl.BlockSpec((1, d), const2),
                  pl.BlockSpec((1, d), const2)],
        out_specs=pl.BlockSpec((tm, d), row),
        out_shape=jax.ShapeDtypeStruct((t, d), jnp.float32),
        scratch_shapes=[pltpu.VMEM((2, 2 * tm * ROW_CHUNKS, LANES), jnp.float32),
                        pltpu.SemaphoreType.DMA((2,))],
        compiler_params=pltpu.CompilerParams(vmem_limit_bytes=VMEM_LIMIT,
                                             dimension_semantics=("arbitrary",)),
        name="combine_ln2",
    )(pos, pos, y_sorted, route, x1, mod, ln2_g, ln2_b)


def _dispatch_plan(route_t, counts):
    t = route_t.shape[1]
    tm = MOE_ROW_TILE
    max_tiles = 2 * t // tm + N_EXPERTS
    counts = counts[0, :N_EXPERTS].astype(jnp.int32)
    tiles = (counts + tm - 1) // tm
    tile_end = jnp.cumsum(tiles)
    row_start = (tile_end - tiles) * tm
    experts = route_t[ROUTE_E1:ROUTE_E2 + 1].astype(jnp.int32)
    ranks = route_t[ROUTE_RANK1:ROUTE_RANK2 + 1].astype(jnp.int32)
    expert_ids = jnp.arange(N_EXPERTS, dtype=jnp.int32)[:, None, None]
    pos = jnp.sum(jnp.where(experts[None] == expert_ids, row_start[:, None, None], 0), axis=0) + ranks
    tile_ids = jnp.arange(max_tiles, dtype=jnp.int32)
    tile_expert = jnp.minimum(jnp.sum(tile_end[None, :] <= tile_ids[:, None], axis=1), N_EXPERTS - 1)
    by_tile = lambda tc: pos.reshape(2, t // tc, tc).transpose(1, 0, 2).reshape(t // tc, 1, 2 * tc)
    return (tile_expert.astype(jnp.int32), tile_end[-1:].astype(jnp.int32), tile_end.astype(jnp.int32),
            by_tile(DISPATCH_ROW_TILE), by_tile(COMBINE_ROW_TILE), max_tiles * tm)


def kernel(x, c, w_ada, b_ada, w_in, w_pool, pool_scale, w_out, ln1_g, ln1_b, w_router_group, b_router_group,
           w_router_expert, b_router_expert, w_gate, w_up, w_down, ln2_g, ln2_b):
    b, s, d = x.shape
    bf16 = jnp.bfloat16
    for layer in range(DEPTH):
        c_pad = jnp.pad(c, ((0, SUBLANES - b), (0, 0)))
        mod = _adaln(c_pad, w_ada[layer], b_ada[layer][None, :])[:b].reshape(b, N_MOD, d)
        x2d = x.reshape(b * s, d)
        qkv, p = _ln_inproj(x2d, mod, w_in[layer], s)
        o_sb = _sb_attn(qkv, b, s).reshape(b * s, SB_WIDTH)
        pad = LANES - N_EXPERTS - N_EXPERT_GROUPS
        w_router = jnp.pad(jnp.concatenate([w_router_expert[layer], w_router_group[layer]], axis=1),
                           ((0, 0), (0, pad)))
        b_router = jnp.pad(jnp.concatenate([b_router_expert[layer], b_router_group[layer]]), (0, pad))[None, :]
        w_router_hi = w_router.astype(bf16)
        w_router_lo = (w_router - w_router_hi.astype(jnp.float32)).astype(bf16)
        w_router = jnp.concatenate([w_router_hi, w_router_hi, w_router_lo], axis=0)
        x1, u2, route, route_t, counts = _mix_ln1(o_sb, p, w_pool[layer].astype(bf16), pool_scale[layer][None, :],
                                         w_out[layer].astype(bf16), x2d, mod, ln1_g[layer][None, :],
                                         ln1_b[layer][None, :], w_router, b_router, s)
        tile_expert, n_tiles, tile_end, pos_dispatch, pos, n_rows = _dispatch_plan(route_t, counts)
        u_sorted = _dispatch(tile_end, pos_dispatch, u2, n_rows)
        y_sorted = _moe_grouped(tile_expert, n_tiles, u_sorted, w_gate[layer], w_up[layer], w_down[layer])
        x2 = _combine_ln2(pos, y_sorted, route, x1, mod, ln2_g[layer][None, :], ln2_b[layer][None, :], s)
        x = x2.reshape(b, s, d)
    return x
```

```python
import functools

import jax
import jax.numpy as jnp
from jax import lax
from jax.experimental import pallas as pl
from jax.experimental.pallas import tpu as pltpu

D_MODEL = 1024
N_SB_HEADS = 8
SB_HEAD_DIM = 64
SB_WIDTH = N_SB_HEADS * SB_HEAD_DIM
POOL_WINDOWS = (2, 4, 8, 16)
POOL_GROUP_DIM = 128
POOL_WIDTH = len(POOL_WINDOWS) * POOL_GROUP_DIM
N_EXPERT_GROUPS = 4
EXPERTS_PER_GROUP = 4
N_EXPERTS = N_EXPERT_GROUPS * EXPERTS_PER_GROUP
EXPERT_HIDDEN = 512
DEPTH = 1
DEEPNORM_ALPHA = (2.0 * DEPTH) ** 0.25
LN_EPS = 1e-5
N_MOD = 6

LANES = 128
SUBLANES = 8
HALO = max(POOL_WINDOWS)
VMEM_LIMIT = 56 * 1024 * 1024

ROW_TILE = 512
MIX_STREAMS = 2
ATTN_TILE = 256
ATTN_QUERY_BLOCKS = 2
STICK_GONE_LOG2 = -180.0
MOE_ROW_TILE = 512
DISPATCH_ROW_TILE = 1024
COMBINE_ROW_TILE = 256
GATHER_UNROLL = 8


ROW_CHUNKS = D_MODEL // LANES


def _store_tile_rows(ref, base, value):
    n = value.shape[0]
    for c in range(ROW_CHUNKS):
        ref[pl.ds(base + c, n, stride=ROW_CHUNKS), :] = value[:, c * LANES:(c + 1) * LANES]


def _load_tile_rows(ref, base, n):
    return jnp.concatenate([ref[pl.ds(base + c, n, stride=ROW_CHUNKS), :] for c in range(ROW_CHUNKS)], axis=1)


def _ln(x):
    mu = jnp.mean(x, axis=-1, keepdims=True)
    xc = x - mu
    var = jnp.mean(xc * xc, axis=-1, keepdims=True)
    return xc * lax.rsqrt(var + LN_EPS)


def _adaln_kernel(c_ref, w_ref, b_ref, o_ref):
    c = c_ref[...]
    a = c * jax.nn.sigmoid(c)
    o_ref[...] = jnp.dot(a, w_ref[...], preferred_element_type=jnp.float32,
                         precision=lax.Precision.HIGHEST) + b_ref[...]


def _adaln(c_pad, w_ada, b_ada):
    rows, d = c_pad.shape
    n = w_ada.shape[1]
    tn = 1024
    return pl.pallas_call(
        _adaln_kernel,
        grid=(n // tn,),
        in_specs=[pl.BlockSpec((rows, d), lambda j: (0, 0)),
                  pl.BlockSpec((d, tn), lambda j: (0, j)),
                  pl.BlockSpec((1, tn), lambda j: (0, j))],
        out_specs=pl.BlockSpec((rows, tn), lambda j: (0, j)),
        out_shape=jax.ShapeDtypeStruct((rows, n), jnp.float32),
        compiler_params=pltpu.CompilerParams(vmem_limit_bytes=VMEM_LIMIT),
        name="adaln",
    )(c_pad, w_ada, b_ada)


def _ln_inproj_kernel(x_ref, mod_ref, w_ref, qkv_ref, p_ref, w_bf16_ref):
    @pl.when(pl.program_id(0) == 0)
    def _():
        w_bf16_ref[...] = w_ref[...].astype(jnp.bfloat16)

    tm = ROW_TILE
    rows = lambda s: slice(s * tm, (s + 1) * tm)
    shift = mod_ref[0, 0:1, :]
    scale = mod_ref[0, 1:2, :]
    us = [(_ln(x_ref[rows(s), :]) * (1.0 + scale) + shift).astype(jnp.bfloat16) for s in range(MIX_STREAMS)]
    qk_scale = SB_HEAD_DIM ** -0.5 * 1.4426950408889634
    for s, u in enumerate(us):
        q = jnp.dot(u, w_bf16_ref[:, 0:SB_WIDTH], preferred_element_type=jnp.float32) * qk_scale
        qkv_ref[rows(s), 0:SB_WIDTH] = q.astype(jnp.bfloat16)
        for j in (1, 2):
            kv = jnp.dot(u, w_bf16_ref[:, j * SB_WIDTH:(j + 1) * SB_WIDTH], preferred_element_type=jnp.float32)
            qkv_ref[rows(s), j * SB_WIDTH:(j + 1) * SB_WIDTH] = kv.astype(jnp.bfloat16)
        p_ref[rows(s), :] = jnp.dot(u, w_bf16_ref[:, 3 * SB_WIDTH:], preferred_element_type=jnp.float32)


def _ln_inproj(x2d, mod, w_in, seq):
    t, d = x2d.shape
    tm = MIX_STREAMS * ROW_TILE
    steps_per_seq = seq // tm
    return pl.pallas_call(
        _ln_inproj_kernel,
        grid=(t // tm,),
        in_specs=[pl.BlockSpec((tm, d), lambda i: (i, 0)),
                  pl.BlockSpec((1, N_MOD, d), lambda i: (i // steps_per_seq, 0, 0)),
                  pl.BlockSpec(w_in.shape, lambda i: (0, 0))],
        out_specs=[pl.BlockSpec((tm, 3 * SB_WIDTH), lambda i: (i, 0)),
                   pl.BlockSpec((tm, POOL_WIDTH), lambda i: (i, 0))],
        out_shape=[jax.ShapeDtypeStruct((t, 3 * SB_WIDTH), jnp.bfloat16),
                   jax.ShapeDtypeStruct((t, POOL_WIDTH), jnp.float32)],
        scratch_shapes=[pltpu.VMEM(w_in.shape, jnp.bfloat16)],
        compiler_params=pltpu.CompilerParams(vmem_limit_bytes=VMEM_LIMIT,
                                             dimension_semantics=("arbitrary",)),
        name="ln_inproj",
    )(x2d, mod, w_in)


def _sb_attn_kernel(q_ref, k_ref, v_ref, o_ref, acc_ref, carry_ref):
    tq = tk = ATTN_TILE
    heads = (0, 1)
    subs = tuple(range(ATTN_QUERY_BLOCKS))
    qis = [pl.program_id(2) * ATTN_QUERY_BLOCKS + s for s in subs]
    lane = lax.broadcasted_iota(jnp.int32, (tq, LANES), 1)
    first_head = lane < SB_HEAD_DIM
    q_heads = {}
    for s in subs:
        q2 = q_ref[0, s * tq:(s + 1) * tq, :]
        zero = jnp.zeros_like(q2)
        q_heads[s, 0], q_heads[s, 1] = jnp.where(first_head, q2, zero), jnp.where(first_head, zero, q2)

    row = lax.broadcasted_iota(jnp.int32, (tk, tk), 0)
    col = lax.broadcasted_iota(jnp.int32, (tk, tk), 1)
    neg_suffix = jnp.where(row >= col, -1.0, 0.0).astype(jnp.bfloat16)
    causal = col < row

    def visit(blocks, carries):
        k_blks, v_blks = [], []
        for s, j, _ in blocks:
            start = pl.multiple_of(jnp.maximum(qis[s] - j, 0) * tk, tk)
            k_blks.append(k_ref[0, pl.ds(start, tk), :])
            v_blks.append(v_ref[0, pl.ds(start, tk), :])
        pairs = [(b, h) for b in range(len(blocks)) for h in heads]
        zs = {(b, h): lax.dot_general(q_heads[blocks[b][0], h], k_blks[b], (((1,), (1,)), ((), ())),
                                      preferred_element_type=jnp.float32) for b, h in pairs}
        fixed, rounded = {}, {}
        for b, h in pairs:
            z = zs[b, h]
            n = jnp.maximum(z, 0.0) + jnp.log2(1.0 + jnp.exp2(-jnp.abs(z)))
            if blocks[b][2]:
                n = jnp.where(causal, n, 0.0)
            n_bf16 = n.astype(jnp.bfloat16)
            rounded[b, h] = n_bf16
            fixed[b, h] = z - (n - n_bf16.astype(jnp.float32))
        incls = {bh: jnp.dot(rounded[bh], neg_suffix, preferred_element_type=jnp.float32) for bh in pairs}
        atts = {}
        carries = dict(carries)
        for b, h in pairs:
            s = blocks[b][0]
            carry = carries[s, h]
            block_sum = jnp.broadcast_to(incls[b, h][:, 0:1], (tq, LANES))
            expo = fixed[b, h] + incls[b, h]
            if carry is None:
                carries[s, h] = block_sum
            else:
                expo = expo + jnp.concatenate([carry] * (tk // LANES), axis=1)
                carries[s, h] = carry + block_sum
            att = jnp.exp2(expo)
            if blocks[b][2]:
                att = jnp.where(causal, att, 0.0)
            atts[b, h] = att.astype(jnp.bfloat16)
        outs = {(b, h): jnp.dot(atts[b, h], v_blks[b], preferred_element_type=jnp.float32) for b, h in pairs}
        return outs, carries

    head_blocks = [(s, 0, True) for s in subs] + [(s, 1, False) for s in subs]
    outs, carries = visit(head_blocks, {(s, h): None for s in subs for h in heads})
    for s in subs:
        has_second = qis[s] >= 1
        for h in heads:
            acc_ref[s, h] = outs[s, h] + jnp.where(has_second, outs[len(subs) + s, h], 0.0)
            carry_ref[s, h] = carries[s, h]

    for s in subs:
        def more(state, s=s):
            j, stick_gone = state
            return (j <= qis[s]) & jnp.logical_not(stick_gone)

        def visit_next(state, s=s):
            j, _ = state
            outs, new_carries = visit([(s, j, False)], {(s, h): carry_ref[s, h] for h in heads})
            for h in heads:
                acc_ref[s, h] += outs[0, h]
                carry_ref[s, h] = new_carries[s, h]
            return j + 1, jnp.max(carry_ref[s]) < STICK_GONE_LOG2

        lax.while_loop(more, visit_next, (jnp.int32(2), jnp.max(carry_ref[s]) < STICK_GONE_LOG2))
        o_ref[0, s * tq:(s + 1) * tq, :] = jnp.where(first_head, acc_ref[s, 0], acc_ref[s, 1]).astype(o_ref.dtype)


def _sb_attn(qkv, batch, seq):
    qkv3 = qkv.reshape(batch, seq, 3 * SB_WIDTH)
    tq = ATTN_TILE
    rows = ATTN_QUERY_BLOCKS * tq
    pairs = SB_WIDTH // LANES
    return pl.pallas_call(
        _sb_attn_kernel,
        grid=(batch, pairs, seq // rows),
        in_specs=[pl.BlockSpec((1, rows, LANES), lambda b, h, i: (b, i, h)),
                  pl.BlockSpec((1, seq, LANES), lambda b, h, i: (b, 0, pairs + h)),
                  pl.BlockSpec((1, seq, LANES), lambda b, h, i: (b, 0, 2 * pairs + h))],
        out_specs=pl.BlockSpec((1, rows, LANES), lambda b, h, i: (b, i, h)),
        out_shape=jax.ShapeDtypeStruct((batch, seq, SB_WIDTH), jnp.bfloat16),
        scratch_shapes=[pltpu.VMEM((ATTN_QUERY_BLOCKS, 2, tq, LANES), jnp.float32),
                        pltpu.VMEM((ATTN_QUERY_BLOCKS, 2, tq, LANES), jnp.float32)],
        compiler_params=pltpu.CompilerParams(vmem_limit_bytes=VMEM_LIMIT),
        name="sb_attn",
    )(qkv3, qkv3, qkv3)


def _route(logits):
    lane = lax.broadcasted_iota(jnp.int32, logits.shape, 1).astype(jnp.float32)
    neg = jnp.float32(-jnp.inf)
    big = jnp.float32(LANES)
    is_group = (lane >= N_EXPERTS) & (lane < N_EXPERTS + N_EXPERT_GROUPS)
    gl = jnp.where(is_group, logits, neg)
    g_max = jnp.max(gl, axis=-1, keepdims=True)
    g_sum = jnp.sum(jnp.exp(gl - g_max), axis=-1, keepdims=True)
    g_p = 1.0 / g_sum
    g_idx = jnp.min(jnp.where(gl == g_max, lane, big), axis=-1, keepdims=True) - N_EXPERTS
    in_group = (lane >= g_idx * EXPERTS_PER_GROUP) & (lane < (g_idx + 1) * EXPERTS_PER_GROUP)
    el = jnp.where(in_group, logits, neg)
    e1 = jnp.max(el, axis=-1, keepdims=True)
    i1 = jnp.min(jnp.where(el == e1, lane, big), axis=-1, keepdims=True)
    el2 = jnp.where(lane == i1, neg, el)
    e2 = jnp.max(el2, axis=-1, keepdims=True)
    i2 = jnp.min(jnp.where(el2 == e2, lane, big), axis=-1, keepdims=True)
    r = jnp.exp(e2 - e1)
    w1 = g_p / (1.0 + r)
    w2 = w1 * r
    return i1, i2, w1, w2


ROUTE_W1, ROUTE_W2, ROUTE_E1, ROUTE_E2, ROUTE_RANK1, ROUTE_RANK2 = range(6)


def _mix_ln1_kernel(steps_per_seq, osb_ref, p_ref, halo_ref, wpool_ref, pscale_ref, wout_ref, x_ref, mod_ref,
                    g1_ref, b1_ref, wr_ref, br_ref, earlier_ref, x1_ref, u2_ref, route_ref, route_t_ref, count_ref,
                    pext_ref, running_ref):
    tm = ROW_TILE
    streams = range(MIX_STREAMS)
    rows = lambda s: slice(s * tm, (s + 1) * tm)
    step_in_seq = pl.program_id(0) % steps_per_seq
    pext_ref[0:HALO, :] = jnp.where(step_in_seq == 0, 0.0, halo_ref[...])
    pext_ref[HALO:, :] = p_ref[...]

    mixer_outs = []
    for s in streams:
        pos = (step_in_seq * MIX_STREAMS + s) * tm + lax.broadcasted_iota(jnp.int32, (tm, 1), 0)
        first_row = HALO + s * tm
        mixer_out = [osb_ref[rows(s), :]]
        for g, w in enumerate(POOL_WINDOWS):
            cols = slice(g * POOL_GROUP_DIM, (g + 1) * POOL_GROUP_DIM)
            own = p_ref[rows(s), cols]
            win = own
            for i in range(1, w):
                win = win + pext_ref[first_row - i:first_row - i + tm, cols]
            count = jnp.minimum(pos + 1, w).astype(jnp.float32)
            pooled = win / count - own
            o_pool = jnp.dot(pooled.astype(jnp.bfloat16), wpool_ref[g], preferred_element_type=jnp.float32)
            mixer_out.append((o_pool * pscale_ref[:, cols]).astype(jnp.bfloat16))
        mixer_outs.append(jnp.concatenate(mixer_out, axis=1))
    mixed = [jnp.dot(m, wout_ref[...], preferred_element_type=jnp.float32) for m in mixer_outs]

    gate1 = mod_ref[0, 2:3, :]
    shift2 = mod_ref[0, 3:4, :]
    scale2 = mod_ref[0, 4:5, :]
    router_in = []
    for s in streams:
        x1 = _ln(DEEPNORM_ALPHA * x_ref[rows(s), :] + (1.0 + gate1) * mixed[s]) * g1_ref[...] + b1_ref[...]
        x1_ref[rows(s), :] = x1
        u2 = _ln(x1) * (1.0 + scale2) + shift2
        _store_tile_rows(u2_ref, s * tm * ROW_CHUNKS, u2)
        u2_hi = u2.astype(jnp.bfloat16)
        u2_lo = (u2 - u2_hi.astype(jnp.float32)).astype(jnp.bfloat16)
        router_in.append(jnp.concatenate([u2_hi, u2_lo, u2_hi], axis=1))
    logits = [jnp.dot(r, wr_ref[...], preferred_element_type=jnp.float32) + br_ref[...] for r in router_in]
    routed = [_route(l) for l in logits]

    @pl.when(pl.program_id(0) == 0)
    def _():
        running_ref[...] = jnp.zeros_like(running_ref)

    lane = lax.broadcasted_iota(jnp.int32, (tm, LANES), 1)
    lane_f = lane.astype(jnp.float32)
    picked = [(lane_f == i1, lane_f == i2) for i1, i2, _, _ in routed]
    uses = [jnp.where(first | second, 1.0, 0.0) for first, second in picked]
    within = [jnp.dot(earlier_ref[...], u.astype(jnp.bfloat16), preferred_element_type=jnp.float32) for u in uses]
    running = running_ref[0:1, :]
    for s in streams:
        i1, i2, w1, w2 = routed[s]
        first, second = picked[s]
        before = within[s] + running
        rank1 = jnp.sum(jnp.where(first, before, 0.0), axis=-1, keepdims=True)
        rank2 = jnp.sum(jnp.where(second, before, 0.0), axis=-1, keepdims=True)
        running = running + jnp.sum(uses[s], axis=0, keepdims=True)
        record = jnp.zeros((tm, LANES), jnp.float32)
        for slot, value in ((ROUTE_W1, w1), (ROUTE_W2, w2), (ROUTE_E1, i1), (ROUTE_E2, i2),
                            (ROUTE_RANK1, rank1), (ROUTE_RANK2, rank2)):
            record = jnp.where(lane == slot, value, record)
        route_ref[rows(s), :] = record
        route_t_ref[:, rows(s)] = record.T[0:SUBLANES, :]
    running_ref[...] = jnp.broadcast_to(running, running_ref.shape)
    count_ref[...] = jnp.broadcast_to(running, count_ref.shape)


def _mix_ln1(o_sb, p, w_pool_bf16, pool_scale, w_out_bf16, x2d, mod, ln1_g, ln1_b, w_router, b_router, seq):
    t, d = x2d.shape
    tm = MIX_STREAMS * ROW_TILE
    steps_per_seq = seq // tm
    halo_blocks_per_step = tm // HALO
    row = lambda i: (i, 0)
    const2 = lambda i: (0, 0)
    return pl.pallas_call(
        functools.partial(_mix_ln1_kernel, steps_per_seq),
        grid=(t // tm,),
        in_specs=[pl.BlockSpec((tm, SB_WIDTH), row),
                  pl.BlockSpec((tm, POOL_WIDTH), row),
                  pl.BlockSpec((HALO, POOL_WIDTH), lambda i: (jnp.maximum(i * halo_blocks_per_step - 1, 0), 0)),
                  pl.BlockSpec(w_pool_bf16.shape, lambda i: (0, 0, 0)),
                  pl.BlockSpec((1, POOL_WIDTH), const2),
                  pl.BlockSpec(w_out_bf16.shape, const2),
                  pl.BlockSpec((tm, d), row),
                  pl.BlockSpec((1, N_MOD, d), lambda i: (i // steps_per_seq, 0, 0)),
                  pl.BlockSpec((1, d), const2),
                  pl.BlockSpec((1, d), const2),
                  pl.BlockSpec(w_router.shape, const2),
                  pl.BlockSpec((1, LANES), const2),
                  pl.BlockSpec((ROW_TILE, ROW_TILE), const2)],
        out_specs=[pl.BlockSpec((tm, d), row),
                   pl.BlockSpec((tm * ROW_CHUNKS, LANES), row),
                   pl.BlockSpec((tm, LANES), row),
                   pl.BlockSpec((SUBLANES, tm), lambda i: (0, i)),
                   pl.BlockSpec((SUBLANES, LANES), const2)],
        out_shape=[jax.ShapeDtypeStruct((t, d), jnp.float32),
                   jax.ShapeDtypeStruct((t * ROW_CHUNKS, LANES), jnp.float32),
                   jax.ShapeDtypeStruct((t, LANES), jnp.float32),
                   jax.ShapeDtypeStruct((SUBLANES, t), jnp.float32),
                   jax.ShapeDtypeStruct((SUBLANES, LANES), jnp.float32)],
        scratch_shapes=[pltpu.VMEM((HALO + tm, POOL_WIDTH), jnp.float32),
                        pltpu.VMEM((SUBLANES, LANES), jnp.float32)],
        compiler_params=pltpu.CompilerParams(vmem_limit_bytes=VMEM_LIMIT,
                                             dimension_semantics=("arbitrary",)),
        name="mix_ln1",
    )(o_sb, p, p, w_pool_bf16, pool_scale, w_out_bf16, x2d, mod, ln1_g, ln1_b, w_router, b_router,
      jnp.tri(ROW_TILE, ROW_TILE, -1, dtype=jnp.bfloat16))


def _dispatch_kernel(tile_end_ref, pos_ref, u_ref, sorted_hbm, zeros_ref, sem):
    tm = u_ref.shape[0] // ROW_CHUNKS
    tile = MOE_ROW_TILE * ROW_CHUNKS
    stored = lambda r: pl.ds(pl.multiple_of(r * ROW_CHUNKS, ROW_CHUNKS), ROW_CHUNKS)

    @pl.when(pl.program_id(0) == 0)
    def _():
        zeros_ref[...] = jnp.zeros_like(zeros_ref)

        def tile_fill(i):
            return pltpu.make_async_copy(zeros_ref, sorted_hbm.at[pl.ds(pl.multiple_of(i * tile, tile), tile), :],
                                         sem)

        def last_tile_fills(act):
            for e in range(N_EXPERTS):
                first_tile = tile_end_ref[e - 1] if e else 0

                @pl.when(tile_end_ref[e] > first_tile)
                def _():
                    act(tile_fill(tile_end_ref[e] - 1))

        unused = (tile_end_ref[N_EXPERTS - 1], sorted_hbm.shape[0] // tile)
        last_tile_fills(lambda fill: fill.start())
        lax.fori_loop(*unused, lambda i, _: (tile_fill(i).start(), 0)[1], 0)
        last_tile_fills(lambda fill: fill.wait())
        lax.fori_loop(*unused, lambda i, _: (tile_fill(i).wait(), 0)[1], 0)

    def body(r, _):
        row = u_ref.at[stored(r), :]
        pltpu.make_async_copy(row, sorted_hbm.at[stored(pos_ref[0, 0, r]), :], sem).start(priority=0)
        pltpu.make_async_copy(row, sorted_hbm.at[stored(pos_ref[0, 0, tm + r]), :], sem).start(priority=1)
        return 0

    lax.fori_loop(0, tm, body, 0, unroll=GATHER_UNROLL // 2)
    for _ in range(2):
        pltpu.make_async_copy(u_ref, sorted_hbm.at[pl.ds(0, tm * ROW_CHUNKS), :], sem).wait()


def _dispatch(tile_end, pos, u2, n_rows):
    tm = pos.shape[2] // 2
    t = u2.shape[0] // ROW_CHUNKS
    grid_spec = pltpu.PrefetchScalarGridSpec(
        num_scalar_prefetch=1,
        grid=(t // tm,),
        in_specs=[pl.BlockSpec((1, 1, 2 * tm), lambda i, te: (i, 0, 0), memory_space=pltpu.SMEM),
                  pl.BlockSpec((tm * ROW_CHUNKS, LANES), lambda i, te: (i, 0))],
        out_specs=pl.BlockSpec(memory_space=pl.ANY),
        scratch_shapes=[pltpu.VMEM((MOE_ROW_TILE * ROW_CHUNKS, LANES), jnp.float32),
                        pltpu.SemaphoreType.DMA(())],
    )
    return pl.pallas_call(
        _dispatch_kernel,
        grid_spec=grid_spec,
        out_shape=jax.ShapeDtypeStruct((n_rows * ROW_CHUNKS, LANES), jnp.float32),
        compiler_params=pltpu.CompilerParams(vmem_limit_bytes=VMEM_LIMIT,
                                             dimension_semantics=("arbitrary",)),
        name="dispatch",
    )(tile_end, pos, u2)


def _moe_grouped_kernel(tile_expert_ref, n_tiles_ref, u_ref, wg_ref, wu_ref, wd_ref, y_ref,
                        wg_bf16_ref, wu_bf16_ref, wd_bf16_ref):
    i = pl.program_id(0)
    bf16 = jnp.bfloat16
    valid = i < n_tiles_ref[0]

    @pl.when(valid & ((i == 0) | (tile_expert_ref[i] != tile_expert_ref[jnp.maximum(i - 1, 0)])))
    def _():
        wg_bf16_ref[...] = wg_ref[0].astype(bf16)
        wu_bf16_ref[...] = wu_ref[0].astype(bf16)
        wd_bf16_ref[...] = wd_ref[0].astype(bf16)

    @pl.when(valid)
    def _():
        u = _load_tile_rows(u_ref, 0, MOE_ROW_TILE).astype(bf16)
        gate = jnp.dot(u, wg_bf16_ref[...], preferred_element_type=jnp.float32)
        up = jnp.dot(u, wu_bf16_ref[...], preferred_element_type=jnp.float32)
        h = gate * jax.nn.sigmoid(gate) * up
        _store_tile_rows(y_ref, 0, jnp.dot(h.astype(bf16), wd_bf16_ref[...], preferred_element_type=jnp.float32))

    @pl.when(jnp.logical_not(valid))
    def _():
        y_ref[...] = jnp.zeros_like(y_ref)


def _moe_grouped(tile_expert, n_tiles, u_sorted, wg, wu, wd):
    d = D_MODEL
    tm = MOE_ROW_TILE
    max_tiles = tile_expert.shape[0]
    expert = lambda i, te, nt: (te[i], 0, 0)
    grid_spec = pltpu.PrefetchScalarGridSpec(
        num_scalar_prefetch=2,
        grid=(max_tiles,),
        in_specs=[pl.BlockSpec((tm * ROW_CHUNKS, LANES), lambda i, te, nt: (jnp.minimum(i, nt[0] - 1), 0)),
                  pl.BlockSpec((1, d, EXPERT_HIDDEN), expert),
                  pl.BlockSpec((1, d, EXPERT_HIDDEN), expert),
                  pl.BlockSpec((1, EXPERT_HIDDEN, d), expert)],
        out_specs=pl.BlockSpec((tm * ROW_CHUNKS, LANES), lambda i, te, nt: (i, 0)),
        scratch_shapes=[pltpu.VMEM((d, EXPERT_HIDDEN), jnp.bfloat16),
                        pltpu.VMEM((d, EXPERT_HIDDEN), jnp.bfloat16),
                        pltpu.VMEM((EXPERT_HIDDEN, d), jnp.bfloat16)],
    )
    return pl.pallas_call(
        _moe_grouped_kernel,
        grid_spec=grid_spec,
        out_shape=jax.ShapeDtypeStruct((max_tiles * tm * ROW_CHUNKS, LANES), jnp.float32),
        compiler_params=pltpu.CompilerParams(vmem_limit_bytes=VMEM_LIMIT,
                                             dimension_semantics=("arbitrary",)),
        name="moe_grouped",
    )(tile_expert, n_tiles, u_sorted, wg, wu, wd)


def _combine_ln2_kernel(pos_ref, pos_next_ref, y_hbm, route_ref, x1_ref, mod_ref, g2_ref, b2_ref, o_ref,
                        buf_ref, sem_ref):
    i = pl.program_id(0)
    n = pl.num_programs(0)
    tm = o_ref.shape[0]
    rows = 2 * tm
    stored = lambda r: pl.ds(pl.multiple_of(r * ROW_CHUNKS, ROW_CHUNKS), ROW_CHUNKS)

    def start_gather(rows_ref, slot):
        def body(pair, _):
            for priority in (0, 1):
                r = 2 * pair + priority
                pltpu.make_async_copy(y_hbm.at[stored(rows_ref[0, 0, r]), :], buf_ref.at[slot, stored(r), :],
                                      sem_ref.at[slot]).start(priority=priority)
            return 0
        lax.fori_loop(0, rows // 2, body, 0, unroll=GATHER_UNROLL // 2)

    @pl.when(i == 0)
    def _():
        start_gather(pos_ref, 0)

    @pl.when(i + 1 < n)
    def _():
        start_gather(pos_next_ref, (i + 1) % 2)

    slot = i % 2
    pltpu.make_async_copy(y_hbm.at[pl.ds(0, rows * ROW_CHUNKS), :], buf_ref.at[slot], sem_ref.at[slot]).wait()
    route = route_ref[...]
    w1 = route[:, ROUTE_W1:ROUTE_W1 + 1]
    w2 = route[:, ROUTE_W2:ROUTE_W2 + 1]
    gathered = buf_ref.at[slot]
    y = w1 * _load_tile_rows(gathered, 0, tm) + w2 * _load_tile_rows(gathered, tm * ROW_CHUNKS, tm)
    gate2 = mod_ref[0, 5:6, :]
    o_ref[...] = _ln(DEEPNORM_ALPHA * x1_ref[...] + (1.0 + gate2) * y) * g2_ref[...] + b2_ref[...]


def _combine_ln2(pos, y_sorted, route, x1, mod, ln2_g, ln2_b, seq):
    t, d = x1.shape
    tm = COMBINE_ROW_TILE
    n = t // tm
    tiles_per_seq = seq // tm
    row = lambda i: (i, 0)
    const2 = lambda i: (0, 0)
    return pl.pallas_call(
        _combine_ln2_kernel,
        grid=(n,),
        in_specs=[pl.BlockSpec((1, 1, 2 * tm), lambda i: (i, 0, 0), memory_space=pltpu.SMEM),
                  pl.BlockSpec((1, 1, 2 * tm), lambda i: (jnp.minimum(i + 1, n - 1), 0, 0),
                               memory_space=pltpu.SMEM),
                  pl.BlockSpec(memory_space=pl.ANY),
                  pl.BlockSpec((tm, LANES), row),
                  pl.BlockSpec((tm, d), row),
                  pl.BlockSpec((1, N_MOD, d), lambda i: (i // tiles_per_seq, 0, 0)),
                  pl.BlockSpec((1, d), const2),
                  pl.BlockSpec((1, d), const2)],
        out_specs=pl.BlockSpec((tm, d), row),
        out_shape=jax.ShapeDtypeStruct((t, d), jnp.float32),
        scratch_shapes=[pltpu.VMEM((2, 2 * tm * ROW_CHUNKS, LANES), jnp.float32),
                        pltpu.SemaphoreType.DMA((2,))],
        compiler_params=pltpu.CompilerParams(vmem_limit_bytes=VMEM_LIMIT,
                                             dimension_semantics=("arbitrary",)),
        name="combine_ln2",
    )(pos, pos, y_sorted, route, x1, mod, ln2_g, ln2_b)


def _dispatch_plan(route_t, counts):
    t = route_t.shape[1]
    tm = MOE_ROW_TILE
    max_tiles = 2 * t // tm + N_EXPERTS
    counts = counts[0, :N_EXPERTS].astype(jnp.int32)
    tiles = (counts + tm - 1) // tm
    tile_end = jnp.cumsum(tiles)
    row_start = (tile_end - tiles) * tm
    experts = route_t[ROUTE_E1:ROUTE_E2 + 1].astype(jnp.int32)
    ranks = route_t[ROUTE_RANK1:ROUTE_RANK2 + 1].astype(jnp.int32)
    expert_ids = jnp.arange(N_EXPERTS, dtype=jnp.int32)[:, None, None]
    pos = jnp.sum(jnp.where(experts[None] == expert_ids, row_start[:, None, None], 0), axis=0) + ranks
    tile_ids = jnp.arange(max_tiles, dtype=jnp.int32)
    tile_expert = jnp.minimum(jnp.sum(tile_end[None, :] <= tile_ids[:, None], axis=1), N_EXPERTS - 1)
    by_tile = lambda tc: pos.reshape(2, t // tc, tc).transpose(1, 0, 2).reshape(t // tc, 1, 2 * tc)
    return (tile_expert.astype(jnp.int32), tile_end[-1:].astype(jnp.int32), tile_end.astype(jnp.int32),
            by_tile(DISPATCH_ROW_TILE), by_tile(COMBINE_ROW_TILE), max_tiles * tm)


def kernel(x, c, w_ada, b_ada, w_in, w_pool, pool_scale, w_out, ln1_g, ln1_b, w_router_group, b_router_group,
           w_router_expert, b_router_expert, w_gate, w_up, w_down, ln2_g, ln2_b):
    b, s, d = x.shape
    bf16 = jnp.bfloat16
    for layer in range(DEPTH):
        c_pad = jnp.pad(c, ((0, SUBLANES - b), (0, 0)))
        mod = _adaln(c_pad, w_ada[layer], b_ada[layer][None, :])[:b].reshape(b, N_MOD, d)
        x2d = x.reshape(b * s, d)
        qkv, p = _ln_inproj(x2d, mod, w_in[layer], s)
        o_sb = _sb_attn(qkv, b, s).reshape(b * s, SB_WIDTH)
        pad = LANES - N_EXPERTS - N_EXPERT_GROUPS
        w_router = jnp.pad(jnp.concatenate([w_router_expert[layer], w_router_group[layer]], axis=1),
                           ((0, 0), (0, pad)))
        b_router = jnp.pad(jnp.concatenate([b_router_expert[layer], b_router_group[layer]]), (0, pad))[None, :]
        w_router_hi = w_router.astype(bf16)
        w_router_lo = (w_router - w_router_hi.astype(jnp.float32)).astype(bf16)
        w_router = jnp.concatenate([w_router_hi, w_router_hi, w_router_lo], axis=0)
        x1, u2, route, route_t, counts = _mix_ln1(o_sb, p, w_pool[layer].astype(bf16), pool_scale[layer][None, :],
                                         w_out[layer].astype(bf16), x2d, mod, ln1_g[layer][None, :],
                                         ln1_b[layer][None, :], w_router, b_router, s)
        tile_expert, n_tiles, tile_end, pos_dispatch, pos, n_rows = _dispatch_plan(route_t, counts)
        u_sorted = _dispatch(tile_end, pos_dispatch, u2, n_rows)
        y_sorted = _moe_grouped(tile_expert, n_tiles, u_sorted, w_gate[layer], w_up[layer], w_down[layer])
        x2 = _combine_ln2(pos, y_sorted, route, x1, mod, ln2_g[layer][None, :], ln2_b[layer][None, :], s)
        x = x2.reshape(b, s, d)
    return x
```
